```python
import jax, jax.numpy as jnp
from jax import lax
import numpy as np

D_MODEL = 2048
BATCH = 8
SEQ = 4096
DEPTH = 4

N_MIXERS = 2
N_A = (DEPTH + 1) // 2
N_B = DEPTH // 2
HEAD_DIM = 128
N_HEADS = D_MODEL // HEAD_DIM
Q_BLOCK = 128
CHUNK = 128
D_GM = D_MODEL
GM_GROUP = 128
N_GM_GROUPS = D_GM // GM_GROUP
D_FF = 5504
CONV_W = 3
EPS = 1e-6

kernel_name = "hybrid_fox_gmlp_convffn_adaln"


def rms_norm(x, g):
    xf = x.astype(jnp.float32)
    y = xf * lax.rsqrt(jnp.mean(xf * xf, axis=-1, keepdims=True) + EPS)
    return (y * g.astype(jnp.float32)).astype(x.dtype)


def modulate(h, shift, scale):
    return h * (1 + scale[:, None, :]) + shift[:, None, :]


def fox_attention(h, w_in, b_f, w_o):
    B, S, D = h.shape
    proj = h @ w_in
    q = proj[..., :D].reshape(B, S, N_HEADS, HEAD_DIM).transpose(0, 2, 1, 3)
    k = proj[..., D:2 * D].reshape(B, S, N_HEADS, HEAD_DIM).transpose(0, 2, 1, 3)
    v = proj[..., 2 * D:3 * D].reshape(B, S, N_HEADS, HEAD_DIM).transpose(0, 2, 1, 3)
    f_logit = (proj[..., 3 * D:] + b_f).astype(jnp.float32)
    log_f = jax.nn.log_sigmoid(f_logit)
    F = jnp.cumsum(log_f, axis=1).transpose(0, 2, 1)
    scale = HEAD_DIM ** -0.5
    outs = []
    for i in range(S // Q_BLOCK):
        lo, hi = i * Q_BLOCK, (i + 1) * Q_BLOCK
        qb = q[:, :, lo:hi]
        kb = k[:, :, :hi]
        vb = v[:, :, :hi]
        s = jnp.einsum('bhqd,bhkd->bhqk', qb, kb).astype(jnp.float32) * scale
        s = s + F[:, :, lo:hi, None] - F[:, :, None, :hi]
        q_pos = lo + jnp.arange(Q_BLOCK)
        k_pos = jnp.arange(hi)
        mask = k_pos[None, :] <= q_pos[:, None]
        s = jnp.where(mask, s, -jnp.inf)
        p = jax.nn.softmax(s, axis=-1).astype(v.dtype)
        outs.append(jnp.einsum('bhqk,bhkd->bhqd', p, vb))
    o = jnp.concatenate(outs, axis=2)
    o = o.transpose(0, 2, 1, 3).reshape(B, S, D)
    return o @ w_o


def chunked_gmlp(h, w_in, v_g, w_s, b_s, w_o):
    B, S, _ = h.shape
    z = jax.nn.gelu(h @ w_in)
    u, v = z[..., :D_GM], z[..., D_GM:]
    v = rms_norm(v, v_g)
    vc = v.reshape(B, S // CHUNK, CHUNK, N_GM_GROUPS, GM_GROUP)
    causal = jnp.tril(jnp.ones((CHUNK, CHUNK), dtype=w_s.dtype))
    w = w_s * causal[None]
    sv = jnp.einsum('gts,bnsgd->bntgd', w, vc)
    sv = sv + b_s.T[None, None, :, :, None]
    gated = u * sv.reshape(B, S, D_GM)
    return gated @ w_o


def conv_ffn(h, w_in, conv_w, conv_b, w_out):
    a = h @ w_in
    S = a.shape[1]
    ap = jnp.pad(a, ((0, 0), (CONV_W - 1, 0), (0, 0)))
    a = (conv_w[0] * ap[:, 0:S] + conv_w[1] * ap[:, 1:S + 1]
         + conv_w[2] * ap[:, 2:S + 2] + conv_b)
    gate, up = a[..., :D_FF], a[..., D_FF:]
    return (jax.nn.silu(gate) * up) @ w_out


def _fwd_setup_inputs(seed: int = 0) -> dict:
    key = jax.random.key(seed)
    ks = jax.random.split(key, 24)
    f32 = jnp.float32
    nrm = lambda k, shape, s: jax.random.normal(k, shape, f32) * s
    D = D_MODEL
    return {
        "x": nrm(ks[0], (BATCH, SEQ, D), 1.0),
        "c": nrm(ks[1], (BATCH, D), 1.0),
        "mod_w": nrm(ks[2], (DEPTH, D, 6 * D), 0.5 * D ** -0.5),
        "mod_b": nrm(ks[3], (DEPTH, 6 * D), 0.02),
        "mix_norm_g": 1.0 + nrm(ks[4], (DEPTH, D), 0.02),
        "ffn_norm_g": 1.0 + nrm(ks[5], (DEPTH, D), 0.02),
        "attn_w_in": nrm(ks[6], (N_A, D, 3 * D + N_HEADS), D ** -0.5),
        "attn_b_f": jax.random.uniform(ks[7], (N_A, N_HEADS), f32, 1.0, 6.0),
        "attn_w_o": nrm(ks[8], (N_A, D, D), D ** -0.5),
        "gm_w_in": nrm(ks[9], (N_B, D, 2 * D_GM), D ** -0.5),
        "gm_v_g": 1.0 + nrm(ks[10], (N_B, D_GM), 0.02),
        "gm_w_s": nrm(ks[11], (N_B, N_GM_GROUPS, CHUNK, CHUNK), CHUNK ** -0.5),
        "gm_b_s": 1.0 + nrm(ks[12], (N_B, N_GM_GROUPS, CHUNK), 0.1),
        "gm_w_o": nrm(ks[13], (N_B, D_GM, D), D_GM ** -0.5),
        "ffn_w_in": nrm(ks[14], (DEPTH, D, 2 * D_FF), D ** -0.5),
        "ffn_conv_w": nrm(ks[15], (DEPTH, CONV_W, 2 * D_FF), CONV_W ** -0.5),
        "ffn_conv_b": nrm(ks[16], (DEPTH, 2 * D_FF), 0.02),
        "ffn_w_out": nrm(ks[17], (DEPTH, D_FF, D), D_FF ** -0.5),
        "final_g": 1.0 + nrm(ks[18], (D,), 0.02),
    }


def _fwd_reference(x, c, mod_w, mod_b, mix_norm_g, ffn_norm_g, attn_w_in, attn_b_f, attn_w_o,
              gm_w_in, gm_v_g, gm_w_s, gm_b_s, gm_w_o, ffn_w_in, ffn_conv_w, ffn_conv_b,
              ffn_w_out, final_g):
    c_act = jax.nn.silu(c)
    for i in range(DEPTH):
        mod = c_act @ mod_w[i] + mod_b[i]
        sh1, sc1, g1, sh2, sc2, g2 = jnp.split(mod, 6, axis=-1)
        h = modulate(rms_norm(x, mix_norm_g[i]), sh1, sc1)
        j = i // N_MIXERS
        if i % N_MIXERS == 0:
            y = fox_attention(h, attn_w_in[j], attn_b_f[j], attn_w_o[j])
        else:
            y = chunked_gmlp(h, gm_w_in[j], gm_v_g[j], gm_w_s[j], gm_b_s[j], gm_w_o[j])
        x = x + g1[:, None, :] * y
        h = modulate(rms_norm(x, ffn_norm_g[i]), sh2, sc2)
        x = x + g2[:, None, :] * conv_ffn(h, ffn_w_in[i], ffn_conv_w[i], ffn_conv_b[i], ffn_w_out[i])
    return rms_norm(x, final_g)


import jax as _jax
import jax.numpy as _jnp

TWIN_FORMAT = 'train_step'
FWD_PARAMS = ['x', 'c', 'mod_w', 'mod_b', 'mix_norm_g', 'ffn_norm_g', 'attn_w_in', 'attn_b_f', 'attn_w_o', 'gm_w_in', 'gm_v_g', 'gm_w_s', 'gm_b_s', 'gm_w_o', 'ffn_w_in', 'ffn_conv_w', 'ffn_conv_b', 'ffn_w_out', 'final_g']
TWIN_WEIGHTS = ['mod_w', 'mod_b', 'mix_norm_g', 'ffn_norm_g', 'attn_w_in', 'attn_b_f', 'attn_w_o', 'gm_w_in', 'gm_v_g', 'gm_w_s', 'gm_b_s', 'gm_w_o', 'ffn_w_in', 'ffn_conv_w', 'ffn_conv_b', 'ffn_w_out', 'final_g']
TWIN_DIFF_INPUT = 'x'
TWIN_INPUTS = ['x', 'c', 'mod_w', 'mod_b', 'mix_norm_g', 'ffn_norm_g', 'attn_w_in', 'attn_b_f', 'attn_w_o', 'gm_w_in', 'gm_v_g', 'gm_w_s', 'gm_b_s', 'gm_w_o', 'ffn_w_in', 'ffn_conv_w', 'ffn_conv_b', 'ffn_w_out', 'final_g', 'loss_target', 'm_mod_w', 'm_mod_b', 'm_mix_norm_g', 'm_ffn_norm_g', 'm_attn_w_in', 'm_attn_b_f', 'm_attn_w_o', 'm_gm_w_in', 'm_gm_v_g', 'm_gm_w_s', 'm_gm_b_s', 'm_gm_w_o', 'm_ffn_w_in', 'm_ffn_conv_w', 'm_ffn_conv_b', 'm_ffn_w_out', 'm_final_g', 'v_mod_w', 'v_mod_b', 'v_mix_norm_g', 'v_ffn_norm_g', 'v_attn_w_in', 'v_attn_b_f', 'v_attn_w_o', 'v_gm_w_in', 'v_gm_v_g', 'v_gm_w_s', 'v_gm_b_s', 'v_gm_w_o', 'v_ffn_w_in', 'v_ffn_conv_w', 'v_ffn_conv_b', 'v_ffn_w_out', 'v_final_g']
TWIN_OUTPUTS = ['loss', 'grad_x', 'grad_mod_w', 'grad_mod_b', 'grad_mix_norm_g', 'grad_ffn_norm_g', 'grad_attn_w_in', 'grad_attn_b_f', 'grad_attn_w_o', 'grad_gm_w_in', 'grad_gm_v_g', 'grad_gm_w_s', 'grad_gm_b_s', 'grad_gm_w_o', 'grad_ffn_w_in', 'grad_ffn_conv_w', 'grad_ffn_conv_b', 'grad_ffn_w_out', 'grad_final_g', 'delta_mod_w', 'delta_mod_b', 'delta_mix_norm_g', 'delta_ffn_norm_g', 'delta_attn_w_in', 'delta_attn_b_f', 'delta_attn_w_o', 'delta_gm_w_in', 'delta_gm_v_g', 'delta_gm_w_s', 'delta_gm_b_s', 'delta_gm_w_o', 'delta_ffn_w_in', 'delta_ffn_conv_w', 'delta_ffn_conv_b', 'delta_ffn_w_out', 'delta_final_g', 'new_m_mod_w', 'new_m_mod_b', 'new_m_mix_norm_g', 'new_m_ffn_norm_g', 'new_m_attn_w_in', 'new_m_attn_b_f', 'new_m_attn_w_o', 'new_m_gm_w_in', 'new_m_gm_v_g', 'new_m_gm_w_s', 'new_m_gm_b_s', 'new_m_gm_w_o', 'new_m_ffn_w_in', 'new_m_ffn_conv_w', 'new_m_ffn_conv_b', 'new_m_ffn_w_out', 'new_m_final_g', 'new_v_mod_w', 'new_v_mod_b', 'new_v_mix_norm_g', 'new_v_ffn_norm_g', 'new_v_attn_w_in', 'new_v_attn_b_f', 'new_v_attn_w_o', 'new_v_gm_w_in', 'new_v_gm_v_g', 'new_v_gm_w_s', 'new_v_gm_b_s', 'new_v_gm_w_o', 'new_v_ffn_w_in', 'new_v_ffn_conv_w', 'new_v_ffn_conv_b', 'new_v_ffn_w_out', 'new_v_final_g']
TWIN_LEAF_KINDS = {'loss': 'loss', 'grad_x': 'grad_x', 'grad_mod_w': 'grad_w', 'grad_mod_b': 'grad_w', 'grad_mix_norm_g': 'grad_w', 'grad_ffn_norm_g': 'grad_w', 'grad_attn_w_in': 'grad_w', 'grad_attn_b_f': 'grad_w', 'grad_attn_w_o': 'grad_w', 'grad_gm_w_in': 'grad_w', 'grad_gm_v_g': 'grad_w', 'grad_gm_w_s': 'grad_w', 'grad_gm_b_s': 'grad_w', 'grad_gm_w_o': 'grad_w', 'grad_ffn_w_in': 'grad_w', 'grad_ffn_conv_w': 'grad_w', 'grad_ffn_conv_b': 'grad_w', 'grad_ffn_w_out': 'grad_w', 'grad_final_g': 'grad_w', 'delta_mod_w': 'delta_w', 'delta_mod_b': 'delta_w', 'delta_mix_norm_g': 'delta_w', 'delta_ffn_norm_g': 'delta_w', 'delta_attn_w_in': 'delta_w', 'delta_attn_b_f': 'delta_w', 'delta_attn_w_o': 'delta_w', 'delta_gm_w_in': 'delta_w', 'delta_gm_v_g': 'delta_w', 'delta_gm_w_s': 'delta_w', 'delta_gm_b_s': 'delta_w', 'delta_gm_w_o': 'delta_w', 'delta_ffn_w_in': 'delta_w', 'delta_ffn_conv_w': 'delta_w', 'delta_ffn_conv_b': 'delta_w', 'delta_ffn_w_out': 'delta_w', 'delta_final_g': 'delta_w', 'new_m_mod_w': 'new_m', 'new_m_mod_b': 'new_m', 'new_m_mix_norm_g': 'new_m', 'new_m_ffn_norm_g': 'new_m', 'new_m_attn_w_in': 'new_m', 'new_m_attn_b_f': 'new_m', 'new_m_attn_w_o': 'new_m', 'new_m_gm_w_in': 'new_m', 'new_m_gm_v_g': 'new_m', 'new_m_gm_w_s': 'new_m', 'new_m_gm_b_s': 'new_m', 'new_m_gm_w_o': 'new_m', 'new_m_ffn_w_in': 'new_m', 'new_m_ffn_conv_w': 'new_m', 'new_m_ffn_conv_b': 'new_m', 'new_m_ffn_w_out': 'new_m', 'new_m_final_g': 'new_m', 'new_v_mod_w': 'new_v', 'new_v_mod_b': 'new_v', 'new_v_mix_norm_g': 'new_v', 'new_v_ffn_norm_g': 'new_v', 'new_v_attn_w_in': 'new_v', 'new_v_attn_b_f': 'new_v', 'new_v_attn_w_o': 'new_v', 'new_v_gm_w_in': 'new_v', 'new_v_gm_v_g': 'new_v', 'new_v_gm_w_s': 'new_v', 'new_v_gm_b_s': 'new_v', 'new_v_gm_w_o': 'new_v', 'new_v_ffn_w_in': 'new_v', 'new_v_ffn_conv_w': 'new_v', 'new_v_ffn_conv_b': 'new_v', 'new_v_ffn_w_out': 'new_v', 'new_v_final_g': 'new_v'}


def _forward(args):
    return _fwd_reference(*[args[k] for k in FWD_PARAMS])


def _output_shape():
    out = _jax.eval_shape(lambda: _forward(_fwd_setup_inputs(0)))
    return out.shape, out.dtype

N_MICROBATCH = 1
ADAM_LR = 0.001
ADAM_B1 = 0.9
ADAM_B2 = 0.999
ADAM_EPS = 1e-08
ADAM_WD = 0.01
ADAM_STEP = 10
PER_EXAMPLE_BATCH_AXIS = {'x': 0, 'c': 0, 'loss_target': 0}
SHARED_INPUTS = []
_WEIGHT_DTYPES = {'mod_w': _jnp.float32, 'mod_b': _jnp.float32, 'mix_norm_g': _jnp.float32, 'ffn_norm_g': _jnp.float32, 'attn_w_in': _jnp.float32, 'attn_b_f': _jnp.float32, 'attn_w_o': _jnp.float32, 'gm_w_in': _jnp.float32, 'gm_v_g': _jnp.float32, 'gm_w_s': _jnp.float32, 'gm_b_s': _jnp.float32, 'gm_w_o': _jnp.float32, 'ffn_w_in': _jnp.float32, 'ffn_conv_w': _jnp.float32, 'ffn_conv_b': _jnp.float32, 'ffn_w_out': _jnp.float32, 'final_g': _jnp.float32}
MOMENT_SCALE = {'mod_w': 2.547100e-02, 'mod_b': 4.331082e-02, 'mix_norm_g': 2.120239e-02, 'ffn_norm_g': 2.627395e-02, 'attn_w_in': 1.066770e-02, 'attn_b_f': 5.654100e-02, 'attn_w_o': 1.370594e-02, 'gm_w_in': 1.879105e-02, 'gm_v_g': 1.282720e-02, 'gm_w_s': 1.280282e-02, 'gm_b_s': 1.821994e-02, 'gm_w_o': 2.243856e-02, 'ffn_w_in': 1.159478e-02, 'ffn_conv_w': 1.158761e-02, 'ffn_conv_b': 1.055445e-02, 'ffn_w_out': 1.867109e-02, 'final_g': 1.603501e+01}


def _to_microbatches(a, axis):
    t = _jnp.moveaxis(a, axis, 0)
    t = t.reshape((N_MICROBATCH, t.shape[0] // N_MICROBATCH) + t.shape[1:])
    return _jnp.moveaxis(t, 1, axis + 1)


def setup_inputs(seed: int = 0) -> dict:
    inp = _fwd_setup_inputs(seed)
    key = _jax.random.fold_in(_jax.random.key(seed), 7919)
    shape, _ = _output_shape()
    out = dict(inp)
    out["loss_target"] = _jax.random.normal(_jax.random.fold_in(key, 0), shape, _jnp.float32)
    for i, name in enumerate(TWIN_WEIGHTS):
        w = inp[name].astype(_jnp.float32)
        if MOMENT_SCALE is None:
            s = _jnp.sqrt(_jnp.mean(_jnp.square(w)) + 1e-30)
        else:
            s = MOMENT_SCALE[name]
        km, kv = _jax.random.split(_jax.random.fold_in(key, i + 1))
        out[name] = w
        out["m_" + name] = s * _jax.random.normal(km, w.shape, _jnp.float32)
        out["v_" + name] = (s * s) * _jax.random.uniform(kv, w.shape, _jnp.float32, 0.5, 1.5)
    if N_MICROBATCH > 1:
        for name, axis in PER_EXAMPLE_BATCH_AXIS.items():
            out[name] = _to_microbatches(out[name], axis)
    return {'x': out['x'], 'c': out['c'], 'mod_w': out['mod_w'], 'mod_b': out['mod_b'], 'mix_norm_g': out['mix_norm_g'], 'ffn_norm_g': out['ffn_norm_g'], 'attn_w_in': out['attn_w_in'], 'attn_b_f': out['attn_b_f'], 'attn_w_o': out['attn_w_o'], 'gm_w_in': out['gm_w_in'], 'gm_v_g': out['gm_v_g'], 'gm_w_s': out['gm_w_s'], 'gm_b_s': out['gm_b_s'], 'gm_w_o': out['gm_w_o'], 'ffn_w_in': out['ffn_w_in'], 'ffn_conv_w': out['ffn_conv_w'], 'ffn_conv_b': out['ffn_conv_b'], 'ffn_w_out': out['ffn_w_out'], 'final_g': out['final_g'], 'loss_target': out['loss_target'], 'm_mod_w': out['m_mod_w'], 'm_mod_b': out['m_mod_b'], 'm_mix_norm_g': out['m_mix_norm_g'], 'm_ffn_norm_g': out['m_ffn_norm_g'], 'm_attn_w_in': out['m_attn_w_in'], 'm_attn_b_f': out['m_attn_b_f'], 'm_attn_w_o': out['m_attn_w_o'], 'm_gm_w_in': out['m_gm_w_in'], 'm_gm_v_g': out['m_gm_v_g'], 'm_gm_w_s': out['m_gm_w_s'], 'm_gm_b_s': out['m_gm_b_s'], 'm_gm_w_o': out['m_gm_w_o'], 'm_ffn_w_in': out['m_ffn_w_in'], 'm_ffn_conv_w': out['m_ffn_conv_w'], 'm_ffn_conv_b': out['m_ffn_conv_b'], 'm_ffn_w_out': out['m_ffn_w_out'], 'm_final_g': out['m_final_g'], 'v_mod_w': out['v_mod_w'], 'v_mod_b': out['v_mod_b'], 'v_mix_norm_g': out['v_mix_norm_g'], 'v_ffn_norm_g': out['v_ffn_norm_g'], 'v_attn_w_in': out['v_attn_w_in'], 'v_attn_b_f': out['v_attn_b_f'], 'v_attn_w_o': out['v_attn_w_o'], 'v_gm_w_in': out['v_gm_w_in'], 'v_gm_v_g': out['v_gm_v_g'], 'v_gm_w_s': out['v_gm_w_s'], 'v_gm_b_s': out['v_gm_b_s'], 'v_gm_w_o': out['v_gm_w_o'], 'v_ffn_w_in': out['v_ffn_w_in'], 'v_ffn_conv_w': out['v_ffn_conv_w'], 'v_ffn_conv_b': out['v_ffn_conv_b'], 'v_ffn_w_out': out['v_ffn_w_out'], 'v_final_g': out['v_final_g']}


def _loss(weights, diff, rest, loss_target):
    with _jax.named_scope("forward"):
        args = {**rest, TWIN_DIFF_INPUT: diff, **{k: w.astype(_WEIGHT_DTYPES[k]) for k, w in weights.items()}}
        y = _forward(args)
    with _jax.named_scope("loss_head"):
        err = _jnp.square(y.astype(_jnp.float32) - loss_target)
        return 0.5 * _jnp.sum(_jnp.mean(err, axis=-1)) if err.ndim else 0.5 * err


def _adamw(w, g, m, v):
    m = ADAM_B1 * m + (1.0 - ADAM_B1) * g
    v = ADAM_B2 * v + (1.0 - ADAM_B2) * _jnp.square(g)
    m_hat = m / (1.0 - ADAM_B1 ** ADAM_STEP)
    v_hat = v / (1.0 - ADAM_B2 ** ADAM_STEP)
    delta = -ADAM_LR * (m_hat / (_jnp.sqrt(v_hat) + ADAM_EPS) + ADAM_WD * w)
    return delta, m, v


def reference(x, c, mod_w, mod_b, mix_norm_g, ffn_norm_g, attn_w_in, attn_b_f, attn_w_o, gm_w_in, gm_v_g, gm_w_s, gm_b_s, gm_w_o, ffn_w_in, ffn_conv_w, ffn_conv_b, ffn_w_out, final_g, loss_target, m_mod_w, m_mod_b, m_mix_norm_g, m_ffn_norm_g, m_attn_w_in, m_attn_b_f, m_attn_w_o, m_gm_w_in, m_gm_v_g, m_gm_w_s, m_gm_b_s, m_gm_w_o, m_ffn_w_in, m_ffn_conv_w, m_ffn_conv_b, m_ffn_w_out, m_final_g, v_mod_w, v_mod_b, v_mix_norm_g, v_ffn_norm_g, v_attn_w_in, v_attn_b_f, v_attn_w_o, v_gm_w_in, v_gm_v_g, v_gm_w_s, v_gm_b_s, v_gm_w_o, v_ffn_w_in, v_ffn_conv_w, v_ffn_conv_b, v_ffn_w_out, v_final_g):
    given = dict(x=x, c=c, mod_w=mod_w, mod_b=mod_b, mix_norm_g=mix_norm_g, ffn_norm_g=ffn_norm_g, attn_w_in=attn_w_in, attn_b_f=attn_b_f, attn_w_o=attn_w_o, gm_w_in=gm_w_in, gm_v_g=gm_v_g, gm_w_s=gm_w_s, gm_b_s=gm_b_s, gm_w_o=gm_w_o, ffn_w_in=ffn_w_in, ffn_conv_w=ffn_conv_w, ffn_conv_b=ffn_conv_b, ffn_w_out=ffn_w_out, final_g=final_g, loss_target=loss_target, m_mod_w=m_mod_w, m_mod_b=m_mod_b, m_mix_norm_g=m_mix_norm_g, m_ffn_norm_g=m_ffn_norm_g, m_attn_w_in=m_attn_w_in, m_attn_b_f=m_attn_b_f, m_attn_w_o=m_attn_w_o, m_gm_w_in=m_gm_w_in, m_gm_v_g=m_gm_v_g, m_gm_w_s=m_gm_w_s, m_gm_b_s=m_gm_b_s, m_gm_w_o=m_gm_w_o, m_ffn_w_in=m_ffn_w_in, m_ffn_conv_w=m_ffn_conv_w, m_ffn_conv_b=m_ffn_conv_b, m_ffn_w_out=m_ffn_w_out, m_final_g=m_final_g, v_mod_w=v_mod_w, v_mod_b=v_mod_b, v_mix_norm_g=v_mix_norm_g, v_ffn_norm_g=v_ffn_norm_g, v_attn_w_in=v_attn_w_in, v_attn_b_f=v_attn_b_f, v_attn_w_o=v_attn_w_o, v_gm_w_in=v_gm_w_in, v_gm_v_g=v_gm_v_g, v_gm_w_s=v_gm_w_s, v_gm_b_s=v_gm_b_s, v_gm_w_o=v_gm_w_o, v_ffn_w_in=v_ffn_w_in, v_ffn_conv_w=v_ffn_conv_w, v_ffn_conv_b=v_ffn_conv_b, v_ffn_w_out=v_ffn_w_out, v_final_g=v_final_g)
    weights = {n: given[n] for n in TWIN_WEIGHTS}
    shared = {n: given[n] for n in SHARED_INPUTS}
    per_example = {n: given[n] for n in ['x', 'c']}
    grad_fn = _jax.value_and_grad(_loss, argnums=(0, 1))

    def one_microbatch(ex, loss_target):
        ex = dict(ex)
        diff = ex.pop(TWIN_DIFF_INPUT)
        return grad_fn(weights, diff, {**shared, **ex}, loss_target)

    if N_MICROBATCH == 1:
        loss, (grad_w, grad_x) = one_microbatch(per_example, given["loss_target"])
    else:
        def body(carry, xs):
            loss_sum, grad_sum = carry
            l_k, (gw_k, gx_k) = one_microbatch(xs[0], xs[1])
            with _jax.named_scope("update"):
                return (loss_sum + l_k, _jax.tree.map(_jnp.add, grad_sum, gw_k)), gx_k

        init = (_jnp.zeros((), _jnp.float32), _jax.tree.map(_jnp.zeros_like, weights))
        (loss, grad_w), grad_x = _jax.lax.scan(body, init, (per_example, given["loss_target"]))
    with _jax.named_scope("update"):
        delta_w, new_m, new_v = {}, {}, {}
        for n in TWIN_WEIGHTS:
            delta_w[n], new_m[n], new_v[n] = _adamw(weights[n], grad_w[n], given["m_" + n], given["v_" + n])
    return (loss, grad_x, *[grad_w[n] for n in TWIN_WEIGHTS], *[delta_w[n] for n in TWIN_WEIGHTS],
            *[new_m[n] for n in TWIN_WEIGHTS], *[new_v[n] for n in TWIN_WEIGHTS])
```

```python
import functools
import math

import jax
import jax.numpy as jnp
from jax import lax
from jax.experimental import pallas as pl
from jax.experimental.pallas import tpu as pltpu

F32 = jnp.float32
BF16 = jnp.bfloat16

EPS = 1e-6
HEAD_DIM = 128
CHUNK = 128
GROUP = 128
LANES = 128
BF16_ROWS = 16
VMEM_LIMIT_BYTES = 52 * 1024 * 1024
NEG_BIG = -1e30
SMALL_PACK_COLS = 1024

ADAM_LR, ADAM_B1, ADAM_B2, ADAM_EPS, ADAM_WD, ADAM_STEP = 0.001, 0.9, 0.999, 1e-08, 0.01, 10

MESH_AXES = ("x", "y", "c")
MESH = pl.DeviceIdType.MESH


def _pick(n, pref, align):
    t = (min(pref, n) // align) * align
    while t >= align:
        if n % t == 0:
            return t
        t -= align
    return n


def _params(*sem):
    return pltpu.CompilerParams(dimension_semantics=sem, vmem_limit_bytes=VMEM_LIMIT_BYTES)


def _group(axes):
    pos = {ax: lax.axis_index(ax) for ax in MESH_AXES}
    rank = 0
    for ax in axes:
        rank = rank * 2 + pos[ax]
    peers = []
    for mask in range(1, 2 ** len(axes)):
        peer = dict(pos)
        for bit, ax in enumerate(reversed(axes)):
            if (mask >> bit) & 1:
                peer[ax] = 1 - pos[ax]
        peers.append((mask, tuple(peer[ax] for ax in MESH_AXES)))
    return rank, peers


def _slot(ref, gaxis, idx):
    return ref.at[(slice(None),) * gaxis + (idx,)]


def _all_gather(x, axes, gaxis, name):
    n = 2 ** len(axes)
    out_shape = x.shape[:gaxis] + (n,) + x.shape[gaxis:]

    def body(x_ref, o_ref, send_sems, recv_sems, local_sem):
        rank, peers = _group(axes)
        mine = pltpu.make_async_copy(x_ref, _slot(o_ref, gaxis, rank), local_sem)
        mine.start()
        sends = []
        for mask, peer in peers:
            cp = pltpu.make_async_remote_copy(
                src_ref=x_ref, dst_ref=_slot(o_ref, gaxis, rank),
                send_sem=send_sems.at[mask - 1], recv_sem=recv_sems.at[mask - 1],
                device_id=peer, device_id_type=MESH)
            cp.start()
            sends.append(cp)
        for (mask, peer), cp in zip(peers, sends):
            pltpu.make_async_remote_copy(
                src_ref=x_ref, dst_ref=_slot(o_ref, gaxis, rank ^ mask),
                send_sem=send_sems.at[mask - 1], recv_sem=recv_sems.at[mask - 1],
                device_id=peer, device_id_type=MESH).wait_recv()
            cp.wait_send()
        mine.wait()

    return pl.pallas_call(
        body, name=name,
        out_shape=jax.ShapeDtypeStruct(out_shape, x.dtype),
        in_specs=[pl.BlockSpec(memory_space=pl.ANY)],
        out_specs=pl.BlockSpec(memory_space=pl.ANY),
        scratch_shapes=[pltpu.SemaphoreType.DMA((n - 1,)), pltpu.SemaphoreType.DMA((n - 1,)),
                        pltpu.SemaphoreType.DMA(())],
    )(x)


def _all_to_all(x, axes, gaxis, name):
    n = 2 ** len(axes)
    assert x.shape[gaxis] == n

    def body(x_ref, o_ref, send_sems, recv_sems, local_sem):
        rank, peers = _group(axes)
        mine = pltpu.make_async_copy(_slot(x_ref, gaxis, rank), _slot(o_ref, gaxis, rank), local_sem)
        mine.start()
        sends = []
        for mask, peer in peers:
            cp = pltpu.make_async_remote_copy(
                src_ref=_slot(x_ref, gaxis, rank ^ mask), dst_ref=_slot(o_ref, gaxis, rank),
                send_sem=send_sems.at[mask - 1], recv_sem=recv_sems.at[mask - 1],
                device_id=peer, device_id_type=MESH)
            cp.start()
            sends.append(cp)
        for (mask, peer), cp in zip(peers, sends):
            pltpu.make_async_remote_copy(
                src_ref=_slot(x_ref, gaxis, rank), dst_ref=_slot(o_ref, gaxis, rank ^ mask),
                send_sem=send_sems.at[mask - 1], recv_sem=recv_sems.at[mask - 1],
                device_id=peer, device_id_type=MESH).wait_recv()
            cp.wait_send()
        mine.wait()

    return pl.pallas_call(
        body, name=name,
        out_shape=jax.ShapeDtypeStruct(x.shape, x.dtype),
        in_specs=[pl.BlockSpec(memory_space=pl.ANY)],
        out_specs=pl.BlockSpec(memory_space=pl.ANY),
        scratch_shapes=[pltpu.SemaphoreType.DMA((n - 1,)), pltpu.SemaphoreType.DMA((n - 1,)),
                        pltpu.SemaphoreType.DMA(())],
    )(x)


def _cast_half(w, core, name):
    L, n, D = w.shape
    n2 = n // 2
    tr = _pick(n2, 256, BF16_ROWS)
    nb = n2 // tr

    def body(core_ref, w_ref, o_ref):
        o_ref[...] = w_ref[...].astype(BF16)

    return pl.pallas_call(
        body, name=name,
        grid_spec=pltpu.PrefetchScalarGridSpec(
            num_scalar_prefetch=1, grid=(L, nb),
            in_specs=[pl.BlockSpec((None, tr, D), lambda l, i, core_ref: (l, core_ref[0] * nb + i, 0))],
            out_specs=pl.BlockSpec((None, tr, D), lambda l, i, core_ref: (l, i, 0))),
        out_shape=jax.ShapeDtypeStruct((L, n2, D), BF16),
        compiler_params=_params("parallel", "parallel"),
    )(core, w)


def _sum_slots(x, out_dtype, name):
    A, G, R, C = x.shape
    budget = (2 * 1024 * 1024) // (G * C * x.dtype.itemsize)
    tr = _pick(R, max(budget, BF16_ROWS), BF16_ROWS)

    def body(x_ref, o_ref):
        acc = x_ref[0].astype(F32)
        for g in range(1, G):
            acc = acc + x_ref[g].astype(F32)
        o_ref[...] = acc.astype(out_dtype)

    return pl.pallas_call(
        body, name=name, grid=(A, R // tr),
        in_specs=[pl.BlockSpec((None, G, tr, C), lambda a, i: (a, 0, i, 0))],
        out_specs=pl.BlockSpec((None, tr, C), lambda a, i: (a, i, 0)),
        out_shape=jax.ShapeDtypeStruct((A, R, C), out_dtype),
        compiler_params=_params("parallel", "parallel"),
    )(x)


def _adamw_math(w, g, m, v):
    m = ADAM_B1 * m + (1.0 - ADAM_B1) * g
    v = ADAM_B2 * v + (1.0 - ADAM_B2) * (g * g)
    m_hat = m / (1.0 - ADAM_B1 ** ADAM_STEP)
    v_hat = v / (1.0 - ADAM_B2 ** ADAM_STEP)
    delta = -ADAM_LR * (m_hat / (jnp.sqrt(v_hat) + ADAM_EPS) + ADAM_WD * w)
    return delta, m, v


def _adamw(w, g, m, v, name):
    R, C = w.shape
    budget = (768 * 1024) // (C * 4)
    tr = _pick(R, max(budget, 8), 8)

    def body(w_ref, g_ref, m_ref, v_ref, d_ref, nm_ref, nv_ref):
        d, nm, nv = _adamw_math(w_ref[...], g_ref[...], m_ref[...], v_ref[...])
        d_ref[...] = d
        nm_ref[...] = nm
        nv_ref[...] = nv

    spec = pl.BlockSpec((tr, C), lambda i: (i, 0))
    return pl.pallas_call(
        body, name=name, grid=(R // tr,),
        in_specs=[spec] * 4, out_specs=[spec] * 3,
        out_shape=[jax.ShapeDtypeStruct((R, C), F32)] * 3,
        compiler_params=_params("parallel"),
    )(w, g, m, v)


def _modw_adamw(ct, dm, w, m, v, name):
    L, D, N = w.shape
    E = ct.shape[1]
    tr = _pick(D, 64, 8)

    def body(ct_ref, dm_ref, w_ref, m_ref, v_ref, g_ref, d_ref, nm_ref, nv_ref):
        g = ct_ref[:, 0:1] * dm_ref[0:1, :]
        for e in range(1, E):
            g = g + ct_ref[:, e:e + 1] * dm_ref[e:e + 1, :]
        d, nm, nv = _adamw_math(w_ref[...], g, m_ref[...], v_ref[...])
        g_ref[...] = g
        d_ref[...] = d
        nm_ref[...] = nm
        nv_ref[...] = nv

    spec = pl.BlockSpec((None, tr, N), lambda l, i: (l, i, 0))
    return pl.pallas_call(
        body, name=name, grid=(L, D // tr),
        in_specs=[pl.BlockSpec((tr, E), lambda l, i: (i, 0)),
                  pl.BlockSpec((None, E, N), lambda l, i: (l, 0, 0)), spec, spec, spec],
        out_specs=[spec] * 4,
        out_shape=[jax.ShapeDtypeStruct((L, D, N), F32)] * 4,
        compiler_params=_params("parallel", "parallel"),
    )(ct, dm, w, m, v)


def _mod_fwd(c_all, w, b, name):
    L, D, N = w.shape
    E = c_all.shape[0]
    tn = _pick(N, 512, LANES)

    def body(c_ref, w_ref, b_ref, act_ref, o_ref):
        cv = c_ref[...]
        act = cv / (1.0 + jnp.exp(-cv))
        act_ref[...] = act
        o_ref[...] = jnp.dot(act.astype(BF16), w_ref[...].astype(BF16),
                             preferred_element_type=F32) + b_ref[...]

    return pl.pallas_call(
        body, name=name, grid=(L, N // tn),
        in_specs=[pl.BlockSpec((E, D), lambda l, j: (0, 0)),
                  pl.BlockSpec((None, D, tn), lambda l, j: (l, 0, j)),
                  pl.BlockSpec((None, 1, tn), lambda l, j: (l, 0, j))],
        out_specs=[pl.BlockSpec((E, D), lambda l, j: (0, 0)),
                   pl.BlockSpec((None, E, tn), lambda l, j: (l, 0, j))],
        out_shape=[jax.ShapeDtypeStruct((E, D), F32), jax.ShapeDtypeStruct((L, E, N), F32)],
        compiler_params=_params("arbitrary", "arbitrary"),
    )(c_all, w, b.reshape(L, 1, N))


_DIMS = {"nn": (((1,), (0,)), ((), ())), "nt": (((1,), (1,)), ((), ())), "tn": (((0,), (0,)), ((), ()))}


def _matmul(a, b, form, name, *, out_dtype=BF16, tm=512, tn=512, tk=None, a_resident=False,
            res=None, gate=None, emit_y=False):
    if form == "nn":
        (M, K), N = a.shape, b.shape[1]
    elif form == "nt":
        (M, K), N = a.shape, b.shape[0]
    else:
        (K, M), N = a.shape, b.shape[1]
    tm = _pick(M, tm, LANES if form == "tn" else BF16_ROWS)
    tn = _pick(N, tn, LANES)
    tk = K if (tk is None or form == "tn") else _pick(K, tk, LANES)
    nk = K // tk

    def ij(g0, g1):
        return (g0, g1) if a_resident else (g1, g0)

    grid = (M // tm, N // tn, nk) if a_resident else (N // tn, M // tm, nk)
    if form == "tn":
        a_spec = pl.BlockSpec((K, tm), lambda g0, g1, k: (0, ij(g0, g1)[0]))
        b_spec = pl.BlockSpec((K, tn), lambda g0, g1, k: (0, ij(g0, g1)[1]))
    else:
        a_spec = pl.BlockSpec((tm, tk), lambda g0, g1, k: (ij(g0, g1)[0], k))
        if form == "nn":
            b_spec = pl.BlockSpec((tk, tn), lambda g0, g1, k: (k, ij(g0, g1)[1]))
        else:
            b_spec = pl.BlockSpec((tn, tk), lambda g0, g1, k: (ij(g0, g1)[1], k))
    tile = pl.BlockSpec((tm, tn), lambda g0, g1, k: ij(g0, g1))
    row = pl.BlockSpec((1, tn), lambda g0, g1, k: (0, ij(g0, g1)[1]))
    in_specs, args = [a_spec, b_spec], [a, b]
    if res is not None:
        in_specs.append(tile)
        args.append(res)
    if gate is not None:
        in_specs.append(row)
        args.append(gate)
    if emit_y:
        out_specs = [tile, tile]
        out_shape = [jax.ShapeDtypeStruct((M, N), BF16), jax.ShapeDtypeStruct((M, N), out_dtype)]
    else:
        out_specs = tile
        out_shape = jax.ShapeDtypeStruct((M, N), out_dtype)
    n_in = len(args)

    def body(*refs):
        a_ref, b_ref = refs[0], refs[1]
        res_ref = refs[2] if res is not None else None
        gate_ref = refs[n_in - 1] if gate is not None else None
        outs = refs[n_in:n_in + (2 if emit_y else 1)]

        def finish(acc):
            val = acc
            if gate_ref is not None:
                val = gate_ref[...] * val
            if res_ref is not None:
                val = res_ref[...] + val
            if emit_y:
                outs[0][...] = acc.astype(BF16)
            outs[-1][...] = val.astype(out_dtype)

        part = lax.dot_general(a_ref[...], b_ref[...], _DIMS[form], preferred_element_type=F32)
        if nk == 1:
            finish(part)
        else:
            acc_ref = refs[-1]
            k = pl.program_id(2)

            @pl.when(k == 0)
            def _():
                acc_ref[...] = part

            @pl.when(k > 0)
            def _():
                acc_ref[...] += part

            @pl.when(k == nk - 1)
            def _():
                finish(acc_ref[...])

    return pl.pallas_call(
        body, name=name, grid=grid, in_specs=in_specs, out_specs=out_specs, out_shape=out_shape,
        scratch_shapes=[pltpu.VMEM((tm, tn), F32)] if nk > 1 else [],
        compiler_params=_params("parallel", "parallel", "arbitrary"),
    )(*args)


def _norm_mod(x, g, sc, sh, name):
    T, D = x.shape
    tm = _pick(T, 256, BF16_ROWS)

    def body(x_ref, g_ref, sc_ref, sh_ref, h_ref):
        xv = x_ref[...]
        rstd = lax.rsqrt(jnp.mean(xv * xv, axis=-1, keepdims=True) + EPS)
        y = xv * rstd * g_ref[...]
        h_ref[...] = (y * (1.0 + sc_ref[...]) + sh_ref[...]).astype(BF16)

    tile = pl.BlockSpec((tm, D), lambda i: (i, 0))
    row = pl.BlockSpec((1, D), lambda i: (0, 0))
    return pl.pallas_call(
        body, name=name, grid=(T // tm,), in_specs=[tile, row, row, row], out_specs=tile,
        out_shape=jax.ShapeDtypeStruct((T, D), BF16), compiler_params=_params("parallel"),
    )(x, g, sc, sh)


def _colsum(v):
    return jnp.sum(v, axis=0, keepdims=True)


def _norm_bwd(x, g, sc, sh, dh, dres, prev, name):
    T, D = x.shape
    tm = _pick(T, 256, BF16_ROWS)
    has_prev = prev is not None

    def body(*refs):
        x_ref, g_ref, sc_ref, sh_ref, dh_ref, dres_ref = refs[:6]
        k = 6
        if has_prev:
            y_ref, gate_ref = refs[6:8]
            k = 8
        dx_ref, dsc_ref, dsh_ref, dg_ref = refs[k:k + 4]
        first = pl.program_id(0) == 0
        xv = x_ref[...]
        rstd = lax.rsqrt(jnp.mean(xv * xv, axis=-1, keepdims=True) + EPS)
        xhat = xv * rstd
        gv = g_ref[...]
        dhv = dh_ref[...].astype(F32)
        dyv = dhv * (1.0 + sc_ref[...])
        dxhat = dyv * gv
        dx = rstd * (dxhat - xhat * jnp.mean(dxhat * xhat, axis=-1, keepdims=True))
        dxt = dres_ref[...] + dx
        dx_ref[...] = dxt
        sums = [(dsc_ref, _colsum(dhv * (xhat * gv))), (dsh_ref, _colsum(dhv)), (dg_ref, _colsum(dyv * xhat))]
        if has_prev:
            dy_ref, dgate_ref = refs[k + 4:k + 6]
            dy_ref[...] = (dxt * gate_ref[...]).astype(BF16)
            sums.append((dgate_ref, _colsum(dxt * y_ref[...].astype(F32))))
        for ref, val in sums:
            @pl.when(first)
            def _(ref=ref, val=val):
                ref[...] = val

            @pl.when(jnp.logical_not(first))
            def _(ref=ref, val=val):
                ref[...] += val

    tile = pl.BlockSpec((tm, D), lambda i: (i, 0))
    row = pl.BlockSpec((1, D), lambda i: (0, 0))
    rowshape = jax.ShapeDtypeStruct((1, D), F32)
    in_specs = [tile, row, row, row, tile, tile]
    args = [x, g, sc, sh, dh, dres]
    out_specs = [tile, row, row, row]
    out_shape = [jax.ShapeDtypeStruct((T, D), F32), rowshape, rowshape, rowshape]
    if has_prev:
        in_specs += [tile, row]
        args += list(prev)
        out_specs += [tile, row]
        out_shape += [jax.ShapeDtypeStruct((T, D), BF16), rowshape]
    return pl.pallas_call(
        body, name=name, grid=(T // tm,), in_specs=in_specs, out_specs=out_specs, out_shape=out_shape,
        compiler_params=_params("arbitrary"),
    )(*args)


def _final_loss_bwd(x, g, target, y_prev, gate_prev, name):
    T, D = x.shape
    tm = _pick(T, 256, BF16_ROWS)

    def body(x_ref, g_ref, t_ref, y_ref, gate_ref, loss_ref, dx_ref, dg_ref, dy_ref, dgate_ref):
        first = pl.program_id(0) == 0
        xv = x_ref[...]
        rstd = lax.rsqrt(jnp.mean(xv * xv, axis=-1, keepdims=True) + EPS)
        xhat = xv * rstd
        gv = g_ref[...]
        err = xhat * gv - t_ref[...]
        part = 0.5 * jnp.sum(jnp.mean(err * err, axis=-1, keepdims=True))
        dyf = err * (1.0 / D)
        dxhat = dyf * gv
        dx = rstd * (dxhat - xhat * jnp.mean(dxhat * xhat, axis=-1, keepdims=True))
        dx_ref[...] = dx
        dy_ref[...] = (dx * gate_ref[...]).astype(BF16)
        sums = [(dg_ref, _colsum(dyf * xhat)), (dgate_ref, _colsum(dx * y_ref[...].astype(F32))),
                (loss_ref, jnp.full((8, LANES), part, F32))]
        for ref, val in sums:
            @pl.when(first)
            def _(ref=ref, val=val):
                ref[...] = val

            @pl.when(jnp.logical_not(first))
            def _(ref=ref, val=val):
                ref[...] += val

    tile = pl.BlockSpec((tm, D), lambda i: (i, 0))
    row = pl.BlockSpec((1, D), lambda i: (0, 0))
    rowshape = jax.ShapeDtypeStruct((1, D), F32)
    return pl.pallas_call(
        body, name=name, grid=(T // tm,),
        in_specs=[tile, row, tile, tile, row],
        out_specs=[pl.BlockSpec((8, LANES), lambda i: (0, 0)), tile, row, tile, row],
        out_shape=[jax.ShapeDtypeStruct((8, LANES), F32), jax.ShapeDtypeStruct((T, D), F32), rowshape,
                   jax.ShapeDtypeStruct((T, D), BF16), rowshape],
        compiler_params=_params("arbitrary"),
    )(x, g, target, y_prev, gate_prev)


def _split3(v):
    hi = v.astype(BF16)
    r1 = v - hi.astype(F32)
    mid = r1.astype(BF16)
    lo = (r1 - mid.astype(F32)).astype(BF16)
    return hi, mid, lo


def _tri_dot(tri, v):
    hi, mid, lo = _split3(v)
    dot = functools.partial(jnp.dot, preferred_element_type=F32)
    return dot(tri, hi) + dot(tri, mid) + dot(tri, lo)


def _fox_gate(flog, bias, name):
    T = flog.shape[0]
    tb = _pick(T, 256, BF16_ROWS)

    def body(f_ref, b_ref, o_ref, carry_ref):
        @pl.when(pl.program_id(0) == 0)
        def _():
            carry_ref[...] = jnp.zeros_like(carry_ref)

        xv = f_ref[...] + b_ref[...]
        lf = jnp.minimum(xv, 0.0) - jnp.log(1.0 + jnp.exp(-jnp.abs(xv)))
        r = lax.broadcasted_iota(jnp.int32, (tb, tb), 0)
        cidx = lax.broadcasted_iota(jnp.int32, (tb, tb), 1)
        tri = jnp.where(r >= cidx, 1.0, 0.0).astype(BF16)
        out = _tri_dot(tri, lf) + carry_ref[...]
        o_ref[...] = out
        carry_ref[...] = out[tb - 1:tb, :]

    return pl.pallas_call(
        body, name=name, grid=(T // tb,),
        in_specs=[pl.BlockSpec((tb, LANES), lambda i: (i, 0)), pl.BlockSpec((1, LANES), lambda i: (0, 0))],
        out_specs=pl.BlockSpec((tb, LANES), lambda i: (i, 0)),
        out_shape=jax.ShapeDtypeStruct((T, LANES), F32),
        scratch_shapes=[pltpu.VMEM((1, LANES), F32)],
        compiler_params=_params("arbitrary"),
    )(flog, bias)


def _fox_gate_bwd(dfq, dfk, flog, bias, name):
    T = flog.shape[0]
    tb = _pick(T, 256, BF16_ROWS)
    nb = T // tb

    def body(dq_ref, dk_ref, f_ref, b_ref, o_ref, db_ref, carry_ref):
        first = pl.program_id(0) == 0

        @pl.when(first)
        def _():
            carry_ref[...] = jnp.zeros_like(carry_ref)

        dv = dq_ref[...] + dk_ref[...]
        r = lax.broadcasted_iota(jnp.int32, (tb, tb), 0)
        cidx = lax.broadcasted_iota(jnp.int32, (tb, tb), 1)
        tri = jnp.where(cidx >= r, 1.0, 0.0).astype(BF16)
        dl = _tri_dot(tri, dv) + carry_ref[...]
        carry_ref[...] = dl[0:1, :]
        xv = f_ref[...] + b_ref[...]
        dfl = dl / (1.0 + jnp.exp(xv))
        o_ref[...] = dfl.astype(BF16)
        val = _colsum(dfl)

        @pl.when(first)
        def _():
            db_ref[...] = val

        @pl.when(jnp.logical_not(first))
        def _():
            db_ref[...] += val

    rev = pl.BlockSpec((tb, LANES), lambda i: (nb - 1 - i, 0))
    row = pl.BlockSpec((1, LANES), lambda i: (0, 0))
    return pl.pallas_call(
        body, name=name, grid=(nb,), in_specs=[rev, rev, rev, row], out_specs=[rev, row],
        out_shape=[jax.ShapeDtypeStruct((T, LANES), BF16), jax.ShapeDtypeStruct((1, LANES), F32)],
        scratch_shapes=[pltpu.VMEM((1, LANES), F32)],
        compiler_params=_params("arbitrary"),
    )(dfq, dfk, flog, bias)


def _head_col(block, h):
    lane = lax.broadcasted_iota(jnp.int32, block.shape, 1)
    return jnp.sum(jnp.where(lane == h, block, 0.0), axis=1, keepdims=True)


_NT = _DIMS["nt"]
_TN = _DIMS["tn"]


def _attn_fwd(qkv, fcol, frow, n_heads, name):
    T = qkv.shape[0]
    H = n_heads
    tq = _pick(T, 512, LANES)
    tk = tq
    scale = HEAD_DIM ** -0.5

    def body(q_ref, k_ref, v_ref, fc_ref, fr_ref, o_ref, lse_ref):
        h, i = pl.program_id(0), pl.program_id(1)
        q = q_ref[...]
        fi = _head_col(fc_ref[...], h)
        rows = i * tq + lax.broadcasted_iota(jnp.int32, (tq, tk), 0)
        cols = lax.broadcasted_iota(jnp.int32, (tq, tk), 1)

        def step(j, carry):
            m, l, acc = carry
            ks = pl.multiple_of(j * tk, tk)
            kb = k_ref[pl.ds(ks, tk), :]
            vb = v_ref[pl.ds(ks, tk), :]
            s = lax.dot_general(q, kb, _NT, preferred_element_type=F32) * scale
            s = s + fi - fr_ref[:, pl.ds(ks, tk)]
            s = jnp.where(cols + ks <= rows, s, NEG_BIG)
            m_new = jnp.maximum(m, jnp.max(s, axis=1, keepdims=True))
            alpha = jnp.exp(m - m_new)
            p = jnp.exp(s - m_new)
            l = alpha * l + jnp.sum(p, axis=1, keepdims=True)
            acc = alpha * acc + jnp.dot(p.astype(BF16), vb, preferred_element_type=F32)
            return m_new, l, acc

        init = (jnp.full((tq, 1), NEG_BIG, F32), jnp.zeros((tq, 1), F32), jnp.zeros((tq, HEAD_DIM), F32))
        m, l, acc = lax.fori_loop(0, i + 1, step, init)
        o_ref[...] = (acc / l).astype(BF16)
        lse_ref[...] = jnp.broadcast_to(m + jnp.log(l), (tq, LANES))

    return pl.pallas_call(
        body, name=name, grid=(H, T // tq),
        in_specs=[pl.BlockSpec((tq, HEAD_DIM), lambda h, i: (i, h)),
                  pl.BlockSpec((T, HEAD_DIM), lambda h, i: (0, H + h)),
                  pl.BlockSpec((T, HEAD_DIM), lambda h, i: (0, 2 * H + h)),
                  pl.BlockSpec((tq, LANES), lambda h, i: (i, 0)),
                  pl.BlockSpec((None, 1, T), lambda h, i: (h, 0, 0))],
        out_specs=[pl.BlockSpec((tq, HEAD_DIM), lambda h, i: (i, h)),
                   pl.BlockSpec((None, tq, LANES), lambda h, i: (h, i, 0))],
        out_shape=[jax.ShapeDtypeStruct((T, H * HEAD_DIM), BF16), jax.ShapeDtypeStruct((H, T, LANES), F32)],
        compiler_params=_params("parallel", "arbitrary"),
    )(qkv, qkv, qkv, fcol, frow)


def _attn_dq(qkv, o, do, lse, fcol, frow, n_heads, name):
    T = qkv.shape[0]
    H = n_heads
    tq = _pick(T, 512, LANES)
    tk = tq
    scale = HEAD_DIM ** -0.5

    def body(q_ref, k_ref, v_ref, o_ref, do_ref, lse_ref, fc_ref, fr_ref, dq_ref, df_ref):
        h, i = pl.program_id(0), pl.program_id(1)
        q = q_ref[...]
        dob = do_ref[...]
        delta = jnp.sum(dob.astype(F32) * o_ref[...].astype(F32), axis=1, keepdims=True)
        lse_i = lse_ref[:, 0:1]
        fi = _head_col(fc_ref[...], h)
        rows = i * tq + lax.broadcasted_iota(jnp.int32, (tq, tk), 0)
        cols = lax.broadcasted_iota(jnp.int32, (tq, tk), 1)

        def step(j, carry):
            dq, rsum = carry
            ks = pl.multiple_of(j * tk, tk)
            kb = k_ref[pl.ds(ks, tk), :]
            vb = v_ref[pl.ds(ks, tk), :]
            s = lax.dot_general(q, kb, _NT, preferred_element_type=F32) * scale
            s = s + fi - fr_ref[:, pl.ds(ks, tk)]
            p = jnp.where(cols + ks <= rows, jnp.exp(s - lse_i), 0.0)
            dp = lax.dot_general(dob, vb, _NT, preferred_element_type=F32)
            ds = p * (dp - delta)
            return (dq + jnp.dot(ds.astype(BF16), kb, preferred_element_type=F32),
                    rsum + jnp.sum(ds, axis=1, keepdims=True))

        dq, rsum = lax.fori_loop(0, i + 1, step, (jnp.zeros((tq, HEAD_DIM), F32), jnp.zeros((tq, 1), F32)))
        dq_ref[...] = (dq * scale).astype(BF16)
        df_ref[...] = jnp.transpose(jnp.broadcast_to(rsum, (tq, LANES)))[0:1, :]

    qtile = lambda off: pl.BlockSpec((tq, HEAD_DIM), lambda h, i: (i, off + h))
    return pl.pallas_call(
        body, name=name, grid=(H, T // tq),
        in_specs=[qtile(0),
                  pl.BlockSpec((T, HEAD_DIM), lambda h, i: (0, H + h)),
                  pl.BlockSpec((T, HEAD_DIM), lambda h, i: (0, 2 * H + h)),
                  qtile(0), qtile(0),
                  pl.BlockSpec((None, tq, LANES), lambda h, i: (h, i, 0)),
                  pl.BlockSpec((tq, LANES), lambda h, i: (i, 0)),
                  pl.BlockSpec((None, 1, T), lambda h, i: (h, 0, 0))],
        out_specs=[qtile(0), pl.BlockSpec((None, 1, tq), lambda h, i: (h, 0, i))],
        out_shape=[jax.ShapeDtypeStruct((T, H * HEAD_DIM), BF16), jax.ShapeDtypeStruct((H, 1, T), F32)],
        compiler_params=_params("parallel", "arbitrary"),
    )(qkv, qkv, qkv, o, do, lse, fcol, frow)


def _attn_dkv(qkv, o, do, lse, fcol, frow, n_heads, name):
    T = qkv.shape[0]
    H = n_heads
    tk = _pick(T, 512, LANES)
    tq = tk
    nq = T // tq
    scale = HEAD_DIM ** -0.5

    def body(k_ref, v_ref, q_ref, o_ref, do_ref, lse_ref, fc_ref, fr_ref, dk_ref, dv_ref, df_ref):
        h, j = pl.program_id(0), pl.program_id(1)
        kb = k_ref[...]
        vb = v_ref[...]
        fj = fr_ref[...]
        cols = j * tk + lax.broadcasted_iota(jnp.int32, (tq, tk), 1)
        rows = lax.broadcasted_iota(jnp.int32, (tq, tk), 0)

        def step(i, carry):
            dk, dv, df = carry
            qs = pl.multiple_of(i * tq, tq)
            qb = q_ref[pl.ds(qs, tq), :]
            dob = do_ref[pl.ds(qs, tq), :]
            delta = jnp.sum(dob.astype(F32) * o_ref[pl.ds(qs, tq), :].astype(F32), axis=1, keepdims=True)
            lse_i = lse_ref[pl.ds(qs, tq), 0:1]
            fi = _head_col(fc_ref[pl.ds(qs, tq), :], h)
            s = lax.dot_general(qb, kb, _NT, preferred_element_type=F32) * scale
            s = s + fi - fj
            p = jnp.where(cols <= rows + qs, jnp.exp(s - lse_i), 0.0)
            dv = dv + lax.dot_general(p.astype(BF16), dob, _TN, preferred_element_type=F32)
            dp = lax.dot_general(dob, vb, _NT, preferred_element_type=F32)
            ds = p * (dp - delta)
            dk = dk + lax.dot_general(ds.astype(BF16), qb, _TN, preferred_element_type=F32)
            df = df - _colsum(ds)
            return dk, dv, df

        init = (jnp.zeros((tk, HEAD_DIM), F32), jnp.zeros((tk, HEAD_DIM), F32), jnp.zeros((1, tk), F32))
        dk, dv, df = lax.fori_loop(j, nq, step, init)
        dk_ref[...] = (dk * scale).astype(BF16)
        dv_ref[...] = dv.astype(BF16)
        df_ref[...] = df

    full = lambda off: pl.BlockSpec((T, HEAD_DIM), lambda h, j: (0, off + h))
    ktile = lambda off: pl.BlockSpec((tk, HEAD_DIM), lambda h, j: (j, off + h))
    return pl.pallas_call(
        body, name=name, grid=(H, T // tk),
        in_specs=[ktile(H), ktile(2 * H), full(0), full(0), full(0),
                  pl.BlockSpec((None, T, LANES), lambda h, j: (h, 0, 0)),
                  pl.BlockSpec((T, LANES), lambda h, j: (0, 0)),
                  pl.BlockSpec((None, 1, tk), lambda h, j: (h, 0, j))],
        out_specs=[ktile(0), ktile(0), pl.BlockSpec((None, 1, tk), lambda h, j: (h, 0, j))],
        out_shape=[jax.ShapeDtypeStruct((T, H * HEAD_DIM), BF16)] * 2 + [jax.ShapeDtypeStruct((H, 1, T), F32)],
        compiler_params=_params("parallel", "arbitrary"),
    )(qkv, qkv, qkv, o, do, lse, fcol, frow)


_GELU_C = math.sqrt(2.0 / math.pi)
_GELU_A = 0.044715


def _gelu(p):
    t = jnp.tanh(_GELU_C * (p + _GELU_A * p * p * p))
    return 0.5 * p * (1.0 + t), t


def _gelu_grad(p, t):
    return 0.5 * (1.0 + t) + 0.5 * p * (1.0 - t * t) * _GELU_C * (1.0 + 3.0 * _GELU_A * p * p)


def _tril(shape, upper=False):
    r = lax.broadcasted_iota(jnp.int32, shape, 0)
    c = lax.broadcasted_iota(jnp.int32, shape, 1)
    return (c >= r) if upper else (r >= c)


def _gm_fwd(p, vg, ws, bt, name):
    T, D2 = p.shape
    D = D2 // 2
    NG = D // GROUP
    tm = _pick(T, 2 * CHUNK, CHUNK)

    def body(p_ref, vg_ref, ws_ref, bt_ref, o_ref):
        z, _ = _gelu(p_ref[...].astype(F32))
        u, v = z[:, :D], z[:, D:]
        rstd = lax.rsqrt(jnp.mean(v * v, axis=-1, keepdims=True) + EPS)
        vn = (v * rstd * vg_ref[...]).astype(BF16)
        low = _tril((CHUNK, CHUNK))
        for g in range(NG):
            wm = jnp.where(low, ws_ref[g], 0.0).astype(BF16)
            bcol = bt_ref[:, g:g + 1]
            cs = slice(g * GROUP, (g + 1) * GROUP)
            for r in range(tm // CHUNK):
                rs = slice(r * CHUNK, (r + 1) * CHUNK)
                sv = jnp.dot(wm, vn[rs, cs], preferred_element_type=F32) + bcol
                o_ref[rs, cs] = (u[rs, cs] * sv).astype(BF16)

    return pl.pallas_call(
        body, name=name, grid=(T // tm,),
        in_specs=[pl.BlockSpec((tm, D2), lambda i: (i, 0)), pl.BlockSpec((1, D), lambda i: (0, 0)),
                  pl.BlockSpec((NG, CHUNK, CHUNK), lambda i: (0, 0, 0)),
                  pl.BlockSpec((CHUNK, LANES), lambda i: (0, 0))],
        out_specs=pl.BlockSpec((tm, D), lambda i: (i, 0)),
        out_shape=jax.ShapeDtypeStruct((T, D), BF16),
        compiler_params=_params("parallel"),
    )(p, vg, ws, bt)


def _gm_bwd(p, dgated, vg, ws, wst, bt, name):
    T, D2 = p.shape
    D = D2 // 2
    NG = D // GROUP
    tm = _pick(T, 2 * CHUNK, CHUNK)
    nb = T // tm

    def body(p_ref, dg_ref, vg_ref, ws_ref, wst_ref, bt_ref, dp_ref, dws_ref, dbt_ref, dvg_ref, du_s, dvn_s):
        step = pl.program_id(0)

        @pl.when(step == 0)
        def _():
            dws_ref[...] = jnp.zeros_like(dws_ref)
            dbt_ref[...] = jnp.zeros_like(dbt_ref)
            dvg_ref[...] = jnp.zeros_like(dvg_ref)

        pv = p_ref[...].astype(F32)
        z, t = _gelu(pv)
        u, v = z[:, :D], z[:, D:]
        rstd = lax.rsqrt(jnp.mean(v * v, axis=-1, keepdims=True) + EPS)
        vhat = v * rstd
        vgv = vg_ref[...]
        vn = (vhat * vgv).astype(BF16)
        dgt = dg_ref[...].astype(F32)
        low = _tril((CHUNK, CHUNK))
        up = _tril((CHUNK, CHUNK), upper=True)
        lane = lax.broadcasted_iota(jnp.int32, (CHUNK, LANES), 1)
        dbt = jnp.zeros((CHUNK, LANES), F32)
        for g in range(NG):
            wm = jnp.where(low, ws_ref[g], 0.0).astype(BF16)
            wmt = jnp.where(up, wst_ref[g], 0.0).astype(BF16)
            bcol = bt_ref[:, g:g + 1]
            cs = slice(g * GROUP, (g + 1) * GROUP)
            dw = jnp.zeros((CHUNK, CHUNK), F32)
            for r in range(tm // CHUNK):
                rs = slice(r * CHUNK, (r + 1) * CHUNK)
                blk = vn[rs, cs]
                sv = jnp.dot(wm, blk, preferred_element_type=F32) + bcol
                dgb = dgt[rs, cs]
                du_s[rs, cs] = dgb * sv
                dsv = dgb * u[rs, cs]
                dsvb = dsv.astype(BF16)
                dvn_s[rs, cs] = jnp.dot(wmt, dsvb, preferred_element_type=F32)
                dw = dw + lax.dot_general(dsvb, blk, _NT, preferred_element_type=F32)
                dbt = dbt + jnp.where(lane == g, jnp.sum(dsv, axis=1, keepdims=True), 0.0)
            dws_ref[g] += dw
        dbt_ref[...] += dbt
        dvn = dvn_s[...]
        dvg_ref[...] += _colsum(dvn * vhat)
        dvhat = dvn * vgv
        dv = rstd * (dvhat - vhat * jnp.mean(dvhat * vhat, axis=-1, keepdims=True))
        gg = _gelu_grad(pv, t)
        dp_ref[:, :D] = (du_s[...] * gg[:, :D]).astype(BF16)
        dp_ref[:, D:] = (dv * gg[:, D:]).astype(BF16)

        @pl.when(step == nb - 1)
        def _():
            for g in range(NG):
                dws_ref[g] = jnp.where(low, dws_ref[g], 0.0)

    const3 = pl.BlockSpec((NG, CHUNK, CHUNK), lambda i: (0, 0, 0))
    return pl.pallas_call(
        body, name=name, grid=(nb,),
        in_specs=[pl.BlockSpec((tm, D2), lambda i: (i, 0)), pl.BlockSpec((tm, D), lambda i: (i, 0)),
                  pl.BlockSpec((1, D), lambda i: (0, 0)), const3, const3,
                  pl.BlockSpec((CHUNK, LANES), lambda i: (0, 0))],
        out_specs=[pl.BlockSpec((tm, D2), lambda i: (i, 0)), const3,
                   pl.BlockSpec((CHUNK, LANES), lambda i: (0, 0)), pl.BlockSpec((1, D), lambda i: (0, 0))],
        out_shape=[jax.ShapeDtypeStruct((T, D2), BF16), jax.ShapeDtypeStruct((NG, CHUNK, CHUNK), F32),
                   jax.ShapeDtypeStruct((CHUNK, LANES), F32), jax.ShapeDtypeStruct((1, D), F32)],
        scratch_shapes=[pltpu.VMEM((tm, D), F32), pltpu.VMEM((tm, D), F32)],
        compiler_params=_params("arbitrary"),
    )(p, dgated, vg, ws, wst, bt)


HALO = BF16_ROWS


def _ffn_act(a, cw, cb, name):
    T, F2 = a.shape
    Fh = F2 // 2
    tm = _pick(T, 256, HALO)
    per = tm // HALO
    nch = Fh // LANES

    def body(a_ref, h_ref, cw_ref, cb_ref, o_ref):
        i = pl.program_id(0)
        row = lax.broadcasted_iota(jnp.int32, (tm, LANES), 0)
        keep = jnp.where(i > 0, 1.0, 0.0)

        def conv(off):
            af = a_ref[:, pl.ds(off, LANES)].astype(F32)
            hf = h_ref[:, pl.ds(off, LANES)].astype(F32) * keep
            r1 = jnp.where(row == 0, hf[HALO - 1:HALO], pltpu.roll(af, 1, 0))
            r2 = jnp.where(row == 0, hf[HALO - 2:HALO - 1], jnp.where(row == 1, hf[HALO - 1:HALO], pltpu.roll(af, 2, 0)))
            w = cw_ref[:, pl.ds(off, LANES)]
            return w[0:1] * r2 + w[1:2] * r1 + w[2:3] * af + cb_ref[:, pl.ds(off, LANES)]

        def chunk(c, carry):
            off = pl.multiple_of(c * LANES, LANES)
            gt = conv(off)
            upv = conv(pl.multiple_of(off + Fh, LANES))
            o_ref[:, pl.ds(off, LANES)] = (gt / (1.0 + jnp.exp(-gt)) * upv).astype(BF16)
            return carry

        lax.fori_loop(0, nch, chunk, 0)

    return pl.pallas_call(
        body, name=name, grid=(T // tm,),
        in_specs=[pl.BlockSpec((tm, F2), lambda i: (i, 0)),
                  pl.BlockSpec((HALO, F2), lambda i: (jnp.maximum(i * per - 1, 0), 0)),
                  pl.BlockSpec((3, F2), lambda i: (0, 0)), pl.BlockSpec((1, F2), lambda i: (0, 0))],
        out_specs=pl.BlockSpec((tm, Fh), lambda i: (i, 0)),
        out_shape=jax.ShapeDtypeStruct((T, Fh), BF16),
        compiler_params=_params("parallel"),
    )(a, a, cw, cb)


def _ffn_act_bwd(a, dhm, cw, cb, name):
    T, F2 = a.shape
    Fh = F2 // 2
    tm = _pick(T, 256, HALO)
    per = tm // HALO
    nb = T // tm
    last_halo = T // HALO - 1
    nch = Fh // LANES
    ext = tm + 2 * HALO

    def body(a_ref, hp_ref, hn_ref, d_ref, dn_ref, cw_ref, cb_ref, da_ref, dcw_ref, dcb_ref):
        i = pl.program_id(0)

        @pl.when(i == 0)
        def _():
            dcw_ref[...] = jnp.zeros_like(dcw_ref)
            dcb_ref[...] = jnp.zeros_like(dcb_ref)

        keep_p = jnp.where(i > 0, 1.0, 0.0)
        keep_n = jnp.where(i < nb - 1, 1.0, 0.0)
        row = lax.broadcasted_iota(jnp.int32, (ext, LANES), 0)
        main = jnp.logical_and(row >= HALO, row < HALO + tm)

        def conv(off):
            e = jnp.concatenate([hp_ref[:, pl.ds(off, LANES)].astype(F32) * keep_p,
                                 a_ref[:, pl.ds(off, LANES)].astype(F32),
                                 hn_ref[:, pl.ds(off, LANES)].astype(F32) * keep_n], axis=0)
            r1 = pltpu.roll(e, 1, 0)
            r2 = pltpu.roll(e, 2, 0)
            w = cw_ref[:, pl.ds(off, LANES)]
            return w, e, r1, r2, w[0:1] * r2 + w[1:2] * r1 + w[2:3] * e + cb_ref[:, pl.ds(off, LANES)]

        def back(off, w, e, r1, r2, dc):
            dcm = jnp.where(main, dc, 0.0)
            for tap, shifted in enumerate((r2, r1, e)):
                dcw_ref[tap:tap + 1, pl.ds(off, LANES)] += _colsum(dcm * shifted)
            dcb_ref[:, pl.ds(off, LANES)] += _colsum(dcm)
            da = w[2:3] * dc + w[1:2] * pltpu.roll(dc, ext - 1, 0) + w[0:1] * pltpu.roll(dc, ext - 2, 0)
            da_ref[:, pl.ds(off, LANES)] = da[HALO:HALO + tm].astype(BF16)

        def chunk(c, carry):
            off = pl.multiple_of(c * LANES, LANES)
            off_u = pl.multiple_of(off + Fh, LANES)
            dh = jnp.concatenate([jnp.zeros((HALO, LANES), F32), d_ref[:, pl.ds(off, LANES)].astype(F32),
                                  dn_ref[:, pl.ds(off, LANES)].astype(F32) * keep_n], axis=0)
            wg, eg, r1g, r2g, gt = conv(off)
            wu, eu, r1u, r2u, upv = conv(off_u)
            sg = 1.0 / (1.0 + jnp.exp(-gt))
            back(off, wg, eg, r1g, r2g, dh * upv * (sg * (1.0 + gt * (1.0 - sg))))
            back(off_u, wu, eu, r1u, r2u, dh * (gt * sg))
            return carry

        lax.fori_loop(0, nch, chunk, 0)

    prev = lambda i: (jnp.maximum(i * per - 1, 0), 0)
    nxt = lambda i: (jnp.minimum((i + 1) * per, last_halo), 0)
    return pl.pallas_call(
        body, name=name, grid=(nb,),
        in_specs=[pl.BlockSpec((tm, F2), lambda i: (i, 0)), pl.BlockSpec((HALO, F2), prev),
                  pl.BlockSpec((HALO, F2), nxt), pl.BlockSpec((tm, Fh), lambda i: (i, 0)),
                  pl.BlockSpec((HALO, Fh), nxt),
                  pl.BlockSpec((3, F2), lambda i: (0, 0)), pl.BlockSpec((1, F2), lambda i: (0, 0))],
        out_specs=[pl.BlockSpec((tm, F2), lambda i: (i, 0)), pl.BlockSpec((3, F2), lambda i: (0, 0)),
                   pl.BlockSpec((1, F2), lambda i: (0, 0))],
        out_shape=[jax.ShapeDtypeStruct((T, F2), BF16), jax.ShapeDtypeStruct((3, F2), F32),
                   jax.ShapeDtypeStruct((1, F2), F32)],
        compiler_params=_params("arbitrary"),
    )(a, a, a, dhm, dhm, cw, cb)


def _round_up(n, k):
    return -(-n // k) * k


def _pad_axis(a, axis, size):
    pad = [(0, 0)] * a.ndim
    pad[axis] = (0, size - a.shape[axis])
    return jnp.pad(a, pad)


def _pad_blocks(a, axis, nblk, to):
    shp = a.shape
    blk = shp[axis] // nblk
    a = a.reshape(shp[:axis] + (nblk, blk) + shp[axis + 1:])
    a = _pad_axis(a, axis + 1, to)
    return a.reshape(shp[:axis] + (nblk * to,) + shp[axis + 1:])


def _unpad_blocks(a, axis, nblk, blk):
    shp = a.shape
    to = shp[axis] // nblk
    a = a.reshape(shp[:axis] + (nblk, to) + shp[axis + 1:])
    a = lax.slice_in_dim(a, 0, blk, axis=axis + 1)
    return a.reshape(shp[:axis] + (nblk * blk,) + shp[axis + 1:])


def _gather_weight(wt, core, name):
    L, n, D = wt.shape
    half = _cast_half(wt, core, name + "_cast")
    g = _all_gather(half, ("x", "y"), 1, name + "_ag_xy")
    g = _all_gather(g, ("c",), 2, name + "_ag_c")
    return g.reshape(L, 4 * n, D)


def _reduce_weight_grad(gt, name):
    L, n4, D = gt.shape
    n2 = n4 // 8
    ex = _all_to_all(gt.reshape(L, 4, 2, n2, D), ("c",), 2, name + "_a2a_c")
    pair = _sum_slots(ex.reshape(L * 4, 2, n2, D), BF16, name + "_sum_c").reshape(L, 4, n2, D)
    ex = _all_to_all(pair, ("x", "y"), 1, name + "_a2a_xy")
    tot = _sum_slots(ex, F32, name + "_sum_xy")
    return _all_gather(tot, ("c",), 1, name + "_ag_c").reshape(L, 2 * n2, D)


def _all_reduce_small(v, name):
    pair = _sum_slots(_all_gather(v, ("c",), 0, name + "_ag_c")[None], F32, name + "_sum_c")[0]
    return _sum_slots(_all_gather(pair, ("x", "y"), 0, name + "_ag_xy")[None], F32, name + "_sum_xy")[0]


def _pack(arrs):
    flat = jnp.concatenate([a.reshape(-1) for a in arrs])
    rows = _round_up(-(-flat.shape[0] // SMALL_PACK_COLS), BF16_ROWS)
    return _pad_axis(flat, 0, rows * SMALL_PACK_COLS).reshape(rows, SMALL_PACK_COLS)


def _unpack(buf, like):
    flat = buf.reshape(-1)
    out, off = [], 0
    for a in like:
        out.append(flat[off:off + a.size].reshape(a.shape))
        off += a.size
    return out


def kernel(x, c, mod_w, mod_b, mix_norm_g, ffn_norm_g, attn_w_in, attn_b_f, attn_w_o, gm_w_in, gm_v_g, gm_w_s, gm_b_s, gm_w_o, ffn_w_in, ffn_conv_w, ffn_conv_b, ffn_w_out, final_g, loss_target, m_mod_w, m_mod_b, m_mix_norm_g, m_ffn_norm_g, m_attn_w_in, m_attn_b_f, m_attn_w_o, m_gm_w_in, m_gm_v_g, m_gm_w_s, m_gm_b_s, m_gm_w_o, m_ffn_w_in, m_ffn_conv_w, m_ffn_conv_b, m_ffn_w_out, m_final_g, v_mod_w, v_mod_b, v_mix_norm_g, v_ffn_norm_g, v_attn_w_in, v_attn_b_f, v_attn_w_o, v_gm_w_in, v_gm_v_g, v_gm_w_s, v_gm_b_s, v_gm_w_o, v_ffn_w_in, v_ffn_conv_w, v_ffn_conv_b, v_ffn_w_out, v_final_g):
    _, T, D = x.shape
    L = mod_w.shape[0]
    NA, NB = attn_w_in.shape[0], gm_w_in.shape[0]
    H = D // HEAD_DIM
    NG = D // GROUP
    Fq = ffn_w_out.shape[1]
    Fh = 4 * Fq
    Fqp = _round_up(Fq, LANES)
    Fp = 4 * Fqp
    n_attn = attn_w_in.shape[2]
    n_attn_p = _round_up(n_attn, 2 * BF16_ROWS)
    nmod = mod_w.shape[2]

    xi, yi, ci = lax.axis_index("x"), lax.axis_index("y"), lax.axis_index("c")
    chip = 2 * xi + yi
    example = 2 * chip + ci
    core = ci.astype(jnp.int32).reshape(1)

    wt_attn_in = _gather_weight(_pad_axis(jnp.swapaxes(attn_w_in, 1, 2), 1, n_attn_p), core, "w_attn_in")
    wt_attn_in = wt_attn_in.reshape(NA, 4, n_attn_p, D)[:, :, :n_attn].reshape(NA, 4 * n_attn, D)
    wt_qkv = wt_attn_in[:, :3 * D]
    wt_f = _pad_axis(wt_attn_in[:, 3 * D:], 1, LANES)
    w_attn_o = _gather_weight(attn_w_o, core, "w_attn_o")
    wt_gm_in = _gather_weight(jnp.swapaxes(gm_w_in, 1, 2), core, "w_gm_in")
    w_gm_o = _gather_weight(gm_w_o, core, "w_gm_o")
    wt_ffn_in = _gather_weight(_pad_blocks(jnp.swapaxes(ffn_w_in, 1, 2), 1, 2, Fqp), core, "w_ffn_in")
    w_ffn_out = _gather_weight(_pad_axis(ffn_w_out, 1, Fqp), core, "w_ffn_out")
    cw_all = _all_gather(ffn_conv_w, ("x", "y"), 0, "conv_w_ag")
    conv_w_full = jnp.moveaxis(cw_all, 0, 2).reshape(L, 3, 2 * Fh)
    conv_w = _pad_blocks(conv_w_full, 2, 8, Fqp)
    conv_b = _pad_blocks(ffn_conv_b, 1, 8, Fqp)
    vg_all = _all_gather(gm_v_g, ("x", "y"), 0, "gm_vg_ag")
    vg_full = jnp.moveaxis(vg_all, 0, 1).reshape(NB, D)

    c_all = _all_gather(c, MESH_AXES, 0, "c_ag").reshape(8, D)
    mod_b_mine = lax.dynamic_slice_in_dim(mod_b, chip * nmod, nmod, axis=1)
    c_act, mod_part = _mod_fwd(c_all, mod_w, mod_b_mine, "mod_fwd")
    mod_all = _all_gather(mod_part, ("x", "y"), 0, "mod_ag")
    mod = lax.dynamic_index_in_dim(mod_all, example, axis=2, keepdims=False)
    mod = jnp.swapaxes(mod, 0, 1).reshape(L, 6, 1, D)

    xs = x.reshape(T, D)
    target = loss_target.reshape(T, D)
    saved = []
    for i in range(L):
        j = i // 2
        sh1, sc1, g1, sh2, sc2, g2 = (mod[i, k] for k in range(6))
        h1 = _norm_mod(xs, mix_norm_g[i:i + 1], sc1, sh1, "norm_mod")
        if i % 2 == 0:
            qkv = _matmul(h1, wt_qkv[j], "nt", "attn_qkv", tm=1024, tn=1024)
            flog = _matmul(h1, wt_f[j], "nt", "attn_f", out_dtype=F32, tm=1024)
            bias = _pad_axis(attn_b_f[j:j + 1], 1, LANES)
            fcol = _fox_gate(flog, bias, "fox_gate")
            frow = jnp.transpose(fcol)[:H].reshape(H, 1, T)
            o, lse = _attn_fwd(qkv, fcol, frow, H, "attn_fwd")
            y, x1 = _matmul(o, w_attn_o[j], "nn", "attn_out", out_dtype=F32, tm=1024, tn=1024,
                            res=xs, gate=g1, emit_y=True)
            mix = ("fox", qkv, flog, bias, fcol, frow, o, lse)
        else:
            p = _matmul(h1, wt_gm_in[j], "nt", "gm_in", tm=1024, tn=1024)
            bt = _pad_axis(jnp.transpose(gm_b_s[j]), 1, LANES)
            gated = _gm_fwd(p, vg_full[j:j + 1], gm_w_s[j], bt, "gm_fwd")
            y, x1 = _matmul(gated, w_gm_o[j], "nn", "gm_out", out_dtype=F32, tm=1024, tn=1024,
                            res=xs, gate=g1, emit_y=True)
            mix = ("gm", p, bt, gated)
        h2 = _norm_mod(x1, ffn_norm_g[i:i + 1], sc2, sh2, "norm_mod")
        a = _matmul(h2, wt_ffn_in[i], "nt", "ffn_in", tm=1024, tn=1024)
        hmid = _ffn_act(a, conv_w[i], conv_b[i:i + 1], "ffn_act")
        y2, x2 = _matmul(hmid, w_ffn_out[i], "nn", "ffn_out", out_dtype=F32, tm=512, tn=1024,
                         res=x1, gate=g2, emit_y=True)
        saved.append((xs, h1, mix, y, x1, h2, a, hmid, y2))
        xs = x2

    loss_blk, dx, g_final, dy2, dg2 = _final_loss_bwd(
        xs, final_g.reshape(1, D), target, saved[L - 1][8], mod[L - 1, 5], "final_loss_bwd")
    gw_attn_in, gw_attn_o, gw_gm_in, gw_gm_o = [None] * NA, [None] * NA, [None] * NB, [None] * NB
    gw_ffn_in, gw_ffn_out = [None] * L, [None] * L
    g_mix, g_ffn_n, g_cw, g_cb, dmod = [None] * L, [None] * L, [None] * L, [None] * L, [None] * L
    g_bf, g_vg, g_ws, g_bs = [None] * NA, [None] * NB, [None] * NB, [None] * NB
    for i in reversed(range(L)):
        j = i // 2
        xin, h1, mix, y, x1, h2, a, hmid, y2 = saved[i]
        sh1, sc1, g1, sh2, sc2, g2 = (mod[i, k] for k in range(6))
        dhm = _matmul(dy2, w_ffn_out[i], "nt", "ffn_out_dx", tm=1024, tn=512)
        gw_ffn_out[i] = _matmul(hmid, dy2, "tn", "ffn_out_dw", tm=512, tn=1024)
        da, dcw, dcb = _ffn_act_bwd(a, dhm, conv_w[i], conv_b[i:i + 1], "ffn_act_bwd")
        dh2 = _matmul(da, wt_ffn_in[i], "nn", "ffn_in_dx", out_dtype=F32, tm=512, tn=1024, tk=2816)
        gw_ffn_in[i] = _matmul(da, h2, "tn", "ffn_in_dw", tm=512, tn=1024)
        g_cw[i] = _unpad_blocks(dcw, 1, 8, Fq)
        g_cb[i] = _unpad_blocks(dcb, 1, 8, Fq)[0]
        dx1, dsc2, dsh2, dgn, dy, dg1 = _norm_bwd(
            x1, ffn_norm_g[i:i + 1], sc2, sh2, dh2, dx, (y, g1), "norm_bwd_prev")
        g_ffn_n[i] = dgn[0]
        if mix[0] == "fox":
            _, qkv, flog, bias, fcol, frow, o, lse = mix
            do = _matmul(dy, w_attn_o[j], "nt", "attn_out_dx", tm=1024, tn=1024)
            gw_attn_o[j] = _matmul(o, dy, "tn", "attn_out_dw", tm=512, tn=1024)
            dq, dfq = _attn_dq(qkv, o, do, lse, fcol, frow, H, "attn_dq")
            dk, dv, dfk = _attn_dkv(qkv, o, do, lse, fcol, frow, H, "attn_dkv")
            heads_on_lanes = lambda t: _pad_axis(jnp.transpose(t.reshape(H, T)), 1, LANES)
            dflog, dbf = _fox_gate_bwd(heads_on_lanes(dfq), heads_on_lanes(dfk), flog, bias, "fox_gate_bwd")
            dproj = jnp.concatenate([dq, dk, dv], axis=1)
            dh_f = _matmul(dflog, wt_f[j], "nn", "attn_f_dx", out_dtype=F32, tm=1024, tn=1024)
            dh1 = _matmul(dproj, wt_qkv[j], "nn", "attn_qkv_dx", out_dtype=F32, tm=512, tn=1024,
                          tk=2048, res=dh_f)
            gwt_qkv = _matmul(dproj, h1, "tn", "attn_qkv_dw", tm=512, tn=1024)
            gwt_f = _matmul(dflog, h1, "tn", "attn_f_dw", tm=128, tn=1024)
            gw_attn_in[j] = jnp.concatenate([gwt_qkv, gwt_f[:H]], axis=0)
            g_bf[j] = dbf[0, :H]
        else:
            _, p, bt, gated = mix
            dgated = _matmul(dy, w_gm_o[j], "nt", "gm_out_dx", tm=1024, tn=1024)
            gw_gm_o[j] = _matmul(gated, dy, "tn", "gm_out_dw", tm=512, tn=1024)
            dp, dws, dbt, dvg = _gm_bwd(p, dgated, vg_full[j:j + 1], gm_w_s[j],
                                        jnp.swapaxes(gm_w_s[j], 1, 2), bt, "gm_bwd")
            dh1 = _matmul(dp, wt_gm_in[j], "nn", "gm_in_dx", out_dtype=F32, tm=512, tn=1024, tk=2048)
            gw_gm_in[j] = _matmul(dp, h1, "tn", "gm_in_dw", tm=512, tn=1024)
            g_ws[j] = dws
            g_bs[j] = jnp.transpose(dbt)[:NG]
            g_vg[j] = dvg[0]
        if i > 0:
            dx, dsc1, dsh1, dgn, dy2, dg2_prev = _norm_bwd(
                xin, mix_norm_g[i:i + 1], sc1, sh1, dh1, dx1, (saved[i - 1][8], mod[i - 1, 5]), "norm_bwd_prev")
        else:
            dx, dsc1, dsh1, dgn = _norm_bwd(xin, mix_norm_g[i:i + 1], sc1, sh1, dh1, dx1, None, "norm_bwd_first")
            dg2_prev = None
        g_mix[i] = dgn[0]
        dmod[i] = jnp.concatenate([dsh1, dsc1, dg1, dsh2, dsc2, dg2], axis=1)[0]
        dg2 = dg2_prev
    grad_x = dx.reshape(1, T, D)
    loss = lax.psum(loss_blk[0, 0], MESH_AXES)

    gt = jnp.stack(gw_attn_in).reshape(NA, 4, n_attn, D)
    gt = _pad_axis(gt, 2, n_attn_p).reshape(NA, 4 * n_attn_p, D)
    grad_attn_w_in = jnp.swapaxes(_reduce_weight_grad(gt, "g_attn_in")[:, :n_attn], 1, 2)
    grad_attn_w_o = _reduce_weight_grad(jnp.stack(gw_attn_o), "g_attn_o")
    grad_gm_w_in = jnp.swapaxes(_reduce_weight_grad(jnp.stack(gw_gm_in), "g_gm_in"), 1, 2)
    grad_gm_w_o = _reduce_weight_grad(jnp.stack(gw_gm_o), "g_gm_o")
    grad_ffn_w_in = jnp.swapaxes(
        _unpad_blocks(_reduce_weight_grad(jnp.stack(gw_ffn_in), "g_ffn_in"), 1, 2, Fq), 1, 2)
    grad_ffn_w_out = _reduce_weight_grad(jnp.stack(gw_ffn_out), "g_ffn_out")[:, :Fq]

    small = [jnp.stack(g_mix), jnp.stack(g_ffn_n), jnp.stack(g_bf), jnp.stack(g_vg), jnp.stack(g_ws),
             jnp.stack(g_bs), jnp.stack(g_cw), jnp.stack(g_cb), g_final[0]]
    (grad_mix_norm_g, grad_ffn_norm_g, grad_attn_b_f, g_vg_full, grad_gm_w_s, grad_gm_b_s, g_cw_full,
     grad_ffn_conv_b, grad_final_g) = _unpack(_all_reduce_small(_pack(small), "g_small"), small)
    grad_gm_v_g = lax.dynamic_slice_in_dim(g_vg_full, chip * (D // 4), D // 4, axis=1)
    grad_ffn_conv_w = lax.dynamic_slice_in_dim(g_cw_full, chip * 2 * Fq, 2 * Fq, axis=2)

    dmod_all = _all_gather(jnp.stack(dmod), MESH_AXES, 0, "dmod_ag")
    grad_mod_b = _sum_slots(dmod_all[None], F32, "dmod_sum")[0]
    dmod_cols = jnp.swapaxes(lax.dynamic_slice_in_dim(dmod_all, chip * nmod, nmod, axis=2), 0, 1)

    grad_mod_w, delta_mod_w, new_m_mod_w, new_v_mod_w = _modw_adamw(
        jnp.transpose(c_act), dmod_cols, mod_w, m_mod_w, v_mod_w, "mod_w_adamw")

    def big(w, g, m, v, name):
        shp = w.shape
        flat = lambda t: t.reshape(-1, shp[-1])
        return [t.reshape(shp) for t in _adamw(flat(w), flat(g), flat(m), flat(v), name)]

    upd_attn_in = big(attn_w_in, grad_attn_w_in, m_attn_w_in, v_attn_w_in, "adamw_attn_in")
    upd_attn_o = big(attn_w_o, grad_attn_w_o, m_attn_w_o, v_attn_w_o, "adamw_attn_o")
    upd_gm_in = big(gm_w_in, grad_gm_w_in, m_gm_w_in, v_gm_w_in, "adamw_gm_in")
    upd_gm_o = big(gm_w_o, grad_gm_w_o, m_gm_w_o, v_gm_w_o, "adamw_gm_o")
    upd_ffn_in = big(ffn_w_in, grad_ffn_w_in, m_ffn_w_in, v_ffn_w_in, "adamw_ffn_in")
    upd_ffn_out = big(ffn_w_out, grad_ffn_w_out, m_ffn_w_out, v_ffn_w_out, "adamw_ffn_out")

    sm_w = [mod_b, mix_norm_g, ffn_norm_g, attn_b_f, gm_v_g, gm_w_s, gm_b_s, ffn_conv_w, ffn_conv_b, final_g]
    sm_g = [grad_mod_b, grad_mix_norm_g, grad_ffn_norm_g, grad_attn_b_f, grad_gm_v_g, grad_gm_w_s, grad_gm_b_s,
            grad_ffn_conv_w, grad_ffn_conv_b, grad_final_g]
    sm_m = [m_mod_b, m_mix_norm_g, m_ffn_norm_g, m_attn_b_f, m_gm_v_g, m_gm_w_s, m_gm_b_s, m_ffn_conv_w,
            m_ffn_conv_b, m_final_g]
    sm_v = [v_mod_b, v_mix_norm_g, v_ffn_norm_g, v_attn_b_f, v_gm_v_g, v_gm_w_s, v_gm_b_s, v_ffn_conv_w,
            v_ffn_conv_b, v_final_g]
    sm_d, sm_nm, sm_nv = (_unpack(t, sm_w) for t in _adamw(_pack(sm_w), _pack(sm_g), _pack(sm_m), _pack(sm_v),
                                                             "adamw_small"))

    def ordered(mod_w_item, sm, k):
        return [mod_w_item, sm[0], sm[1], sm[2], upd_attn_in[k], sm[3], upd_attn_o[k], upd_gm_in[k], sm[4], sm[5],
                sm[6], upd_gm_o[k], upd_ffn_in[k], sm[7], sm[8], upd_ffn_out[k], sm[9]]

    grads = [grad_mod_w, grad_mod_b, grad_mix_norm_g, grad_ffn_norm_g, grad_attn_w_in, grad_attn_b_f, grad_attn_w_o,
             grad_gm_w_in, grad_gm_v_g, grad_gm_w_s, grad_gm_b_s, grad_gm_w_o, grad_ffn_w_in, grad_ffn_conv_w,
             grad_ffn_conv_b, grad_ffn_w_out, grad_final_g]
    return (loss, grad_x, *grads, *ordered(delta_mod_w, sm_d, 0), *ordered(new_m_mod_w, sm_nm, 1),
            *ordered(new_v_mod_w, sm_nv, 2))
```

```python
import functools
import math

import jax
import jax.numpy as jnp
from jax import lax
from jax.experimental import pallas as pl
from jax.experimental.pallas import tpu as pltpu

F32 = jnp.float32
BF16 = jnp.bfloat16

EPS = 1e-6
HEAD_DIM = 128
CHUNK = 128
GROUP = 128
LANES = 128
BF16_ROWS = 16
VMEM_LIMIT_BYTES = 52 * 1024 * 1024
NEG_BIG = -1e30
SMALL_PACK_COLS = 1024

ADAM_LR, ADAM_B1, ADAM_B2, ADAM_EPS, ADAM_WD, ADAM_STEP = 0.001, 0.9, 0.999, 1e-08, 0.01, 10

MESH_AXES = ("x", "y", "c")
MESH = pl.DeviceIdType.MESH


def _pick(n, pref, align):
    t = (min(pref, n) // align) * align
    while t >= align:
        if n % t == 0:
            return t
        t -= align
    return n


def _params(*sem):
    return pltpu.CompilerParams(dimension_semantics=sem, vmem_limit_bytes=VMEM_LIMIT_BYTES)


def _group(axes):
    pos = {ax: lax.axis_index(ax) for ax in MESH_AXES}
    rank = 0
    for ax in axes:
        rank = rank * 2 + pos[ax]
    peers = []
    for mask in range(1, 2 ** len(axes)):
        peer = dict(pos)
        for bit, ax in enumerate(reversed(axes)):
            if (mask >> bit) & 1:
                peer[ax] = 1 - pos[ax]
        peers.append((mask, tuple(peer[ax] for ax in MESH_AXES)))
    return rank, peers


def _slot(ref, gaxis, idx):
    return ref.at[(slice(None),) * gaxis + (idx,)]


D2D_STREAMS = 32
ICI_STREAMS = 8


def _pieces(shape, want):
    idx = [()]
    d = 0
    while d < len(shape) - 2 and len(idx) * shape[d] <= want:
        idx = [i + (k,) for i in idx for k in range(shape[d])]
        d += 1
    if d == len(shape) - 2:
        rows = shape[-2]
        r = max(1, want // len(idx))
        while r > 1 and not (rows % r == 0 and (rows // r) % BF16_ROWS == 0):
            r -= 1
        if r > 1:
            idx = [i + (pl.ds(k * (rows // r), rows // r),) for i in idx for k in range(r)]
    return idx


def _exchange(x, axes, gaxis, name, gather):
    n = 2 ** len(axes)
    if gather:
        out_shape = x.shape[:gaxis] + (n,) + x.shape[gaxis:]
        slab = x.shape
    else:
        assert x.shape[gaxis] == n
        out_shape = x.shape
        slab = x.shape[:gaxis] + x.shape[gaxis + 1:]
    pieces = _pieces(slab, D2D_STREAMS if axes == ("c",) else ICI_STREAMS)

    def body(x_ref, o_ref, send_sems, recv_sems, local_sem):
        rank, peers = _group(axes)
        src = (lambda member: x_ref) if gather else (lambda member: _slot(x_ref, gaxis, member))
        dst = lambda member: _slot(o_ref, gaxis, member)

        def remote(mask, peer, s, d):
            return pltpu.make_async_remote_copy(
                src_ref=s, dst_ref=d, send_sem=send_sems.at[mask - 1], recv_sem=recv_sems.at[mask - 1],
                device_id=peer, device_id_type=MESH)

        cut = lambda ref, pc: ref.at[pc] if pc else ref
        for pc in pieces:
            pltpu.make_async_copy(cut(src(rank), pc), cut(dst(rank), pc), local_sem).start()
            for mask, peer in peers:
                remote(mask, peer, cut(src(rank ^ mask), pc), cut(dst(rank), pc)).start()
        for mask, peer in peers:
            whole = remote(mask, peer, src(rank), dst(rank ^ mask))
            whole.wait_recv()
            whole.wait_send()
        pltpu.make_async_copy(src(rank), dst(rank), local_sem).wait()

    return pl.pallas_call(
        body, name=name,
        out_shape=jax.ShapeDtypeStruct(out_shape, x.dtype),
        in_specs=[pl.BlockSpec(memory_space=pl.ANY)],
        out_specs=pl.BlockSpec(memory_space=pl.ANY),
        scratch_shapes=[pltpu.SemaphoreType.DMA((n - 1,)), pltpu.SemaphoreType.DMA((n - 1,)),
                        pltpu.SemaphoreType.DMA(())],
    )(x)


def _all_gather(x, axes, gaxis, name):
    return _exchange(x, axes, gaxis, name, True)


def _all_to_all(x, axes, gaxis, name):
    return _exchange(x, axes, gaxis, name, False)


def _cast_half(w, core, name):
    L, n, D = w.shape
    n2 = n // 2
    tr = _pick(n2, 256, BF16_ROWS)
    nb = n2 // tr

    def body(core_ref, w_ref, o_ref):
        o_ref[...] = w_ref[...].astype(BF16)

    return pl.pallas_call(
        body, name=name,
        grid_spec=pltpu.PrefetchScalarGridSpec(
            num_scalar_prefetch=1, grid=(L, nb),
            in_specs=[pl.BlockSpec((None, tr, D), lambda l, i, core_ref: (l, core_ref[0] * nb + i, 0))],
            out_specs=pl.BlockSpec((None, tr, D), lambda l, i, core_ref: (l, i, 0))),
        out_shape=jax.ShapeDtypeStruct((L, n2, D), BF16),
        compiler_params=_params("parallel", "parallel"),
    )(core, w)


def _sum_slots(x, out_dtype, name):
    A, G, R, C = x.shape
    budget = (2 * 1024 * 1024) // (G * C * x.dtype.itemsize)
    tr = _pick(R, max(budget, BF16_ROWS), BF16_ROWS)

    def body(x_ref, o_ref):
        acc = x_ref[0].astype(F32)
        for g in range(1, G):
            acc = acc + x_ref[g].astype(F32)
        o_ref[...] = acc.astype(out_dtype)

    return pl.pallas_call(
        body, name=name, grid=(A, R // tr),
        in_specs=[pl.BlockSpec((None, G, tr, C), lambda a, i: (a, 0, i, 0))],
        out_specs=pl.BlockSpec((None, tr, C), lambda a, i: (a, i, 0)),
        out_shape=jax.ShapeDtypeStruct((A, R, C), out_dtype),
        compiler_params=_params("parallel", "parallel"),
    )(x)


def _adamw_math(w, g, m, v):
    m = ADAM_B1 * m + (1.0 - ADAM_B1) * g
    v = ADAM_B2 * v + (1.0 - ADAM_B2) * (g * g)
    m_hat = m / (1.0 - ADAM_B1 ** ADAM_STEP)
    v_hat = v / (1.0 - ADAM_B2 ** ADAM_STEP)
    delta = -ADAM_LR * (m_hat / (jnp.sqrt(v_hat) + ADAM_EPS) + ADAM_WD * w)
    return delta, m, v


def _adamw(w, g, m, v, name):
    R, C = w.shape
    budget = (768 * 1024) // (C * 4)
    tr = _pick(R, max(budget, 8), 8)

    def body(w_ref, g_ref, m_ref, v_ref, d_ref, nm_ref, nv_ref):
        d, nm, nv = _adamw_math(w_ref[...], g_ref[...], m_ref[...], v_ref[...])
        d_ref[...] = d
        nm_ref[...] = nm
        nv_ref[...] = nv

    spec = pl.BlockSpec((tr, C), lambda i: (i, 0))
    return pl.pallas_call(
        body, name=name, grid=(R // tr,),
        in_specs=[spec] * 4, out_specs=[spec] * 3,
        out_shape=[jax.ShapeDtypeStruct((R, C), F32)] * 3,
        compiler_params=_params("parallel"),
    )(w, g, m, v)


def _modw_adamw(ct, dm, w, m, v, name):
    L, D, N = w.shape
    E = ct.shape[1]
    tr = _pick(D, 64, 8)

    def body(ct_ref, dm_ref, w_ref, m_ref, v_ref, g_ref, d_ref, nm_ref, nv_ref):
        g = ct_ref[:, 0:1] * dm_ref[0:1, :]
        for e in range(1, E):
            g = g + ct_ref[:, e:e + 1] * dm_ref[e:e + 1, :]
        d, nm, nv = _adamw_math(w_ref[...], g, m_ref[...], v_ref[...])
        g_ref[...] = g
        d_ref[...] = d
        nm_ref[...] = nm
        nv_ref[...] = nv

    spec = pl.BlockSpec((None, tr, N), lambda l, i: (l, i, 0))
    return pl.pallas_call(
        body, name=name, grid=(L, D // tr),
        in_specs=[pl.BlockSpec((tr, E), lambda l, i: (i, 0)),
                  pl.BlockSpec((None, E, N), lambda l, i: (l, 0, 0)), spec, spec, spec],
        out_specs=[spec] * 4,
        out_shape=[jax.ShapeDtypeStruct((L, D, N), F32)] * 4,
        compiler_params=_params("parallel", "parallel"),
    )(ct, dm, w, m, v)


def _mod_fwd(c_all, w, b, name):
    L, D, N = w.shape
    E = c_all.shape[0]
    tn = _pick(N, 512, LANES)

    def body(c_ref, w_ref, b_ref, act_ref, o_ref):
        cv = c_ref[...]
        act = cv / (1.0 + jnp.exp(-cv))
        act_ref[...] = act
        o_ref[...] = jnp.dot(act.astype(BF16), w_ref[...].astype(BF16),
                             preferred_element_type=F32) + b_ref[...]

    return pl.pallas_call(
        body, name=name, grid=(L, N // tn),
        in_specs=[pl.BlockSpec((E, D), lambda l, j: (0, 0)),
                  pl.BlockSpec((None, D, tn), lambda l, j: (l, 0, j)),
                  pl.BlockSpec((None, 1, tn), lambda l, j: (l, 0, j))],
        out_specs=[pl.BlockSpec((E, D), lambda l, j: (0, 0)),
                   pl.BlockSpec((None, E, tn), lambda l, j: (l, 0, j))],
        out_shape=[jax.ShapeDtypeStruct((E, D), F32), jax.ShapeDtypeStruct((L, E, N), F32)],
        compiler_params=_params("arbitrary", "arbitrary"),
    )(c_all, w, b.reshape(L, 1, N))


_DIMS = {"nn": (((1,), (0,)), ((), ())), "nt": (((1,), (1,)), ((), ())), "tn": (((0,), (0,)), ((), ()))}


def _matmul(a, b, form, name, *, out_dtype=BF16, tm=512, tn=512, tk=None, a_resident=False,
            res=None, gate=None, emit_y=False):
    if form == "nn":
        (M, K), N = a.shape, b.shape[1]
    elif form == "nt":
        (M, K), N = a.shape, b.shape[0]
    else:
        (K, M), N = a.shape, b.shape[1]
    tm = _pick(M, tm, LANES if form == "tn" else BF16_ROWS)
    tn = _pick(N, tn, LANES)
    tk = K if (tk is None or form == "tn") else _pick(K, tk, LANES)
    nk = K // tk

    def ij(g0, g1):
        return (g0, g1) if a_resident else (g1, g0)

    grid = (M // tm, N // tn, nk) if a_resident else (N // tn, M // tm, nk)
    if form == "tn":
        a_spec = pl.BlockSpec((K, tm), lambda g0, g1, k: (0, ij(g0, g1)[0]))
        b_spec = pl.BlockSpec((K, tn), lambda g0, g1, k: (0, ij(g0, g1)[1]))
    else:
        a_spec = pl.BlockSpec((tm, tk), lambda g0, g1, k: (ij(g0, g1)[0], k))
        if form == "nn":
            b_spec = pl.BlockSpec((tk, tn), lambda g0, g1, k: (k, ij(g0, g1)[1]))
        else:
            b_spec = pl.BlockSpec((tn, tk), lambda g0, g1, k: (ij(g0, g1)[1], k))
    tile = pl.BlockSpec((tm, tn), lambda g0, g1, k: ij(g0, g1))
    row = pl.BlockSpec((1, tn), lambda g0, g1, k: (0, ij(g0, g1)[1]))
    in_specs, args = [a_spec, b_spec], [a, b]
    if res is not None:
        in_specs.append(tile)
        args.append(res)
    if gate is not None:
        in_specs.append(row)
        args.append(gate)
    if emit_y:
        out_specs = [tile, tile]
        out_shape = [jax.ShapeDtypeStruct((M, N), BF16), jax.ShapeDtypeStruct((M, N), out_dtype)]
    else:
        out_specs = tile
        out_shape = jax.ShapeDtypeStruct((M, N), out_dtype)
    n_in = len(args)

    def body(*refs):
        a_ref, b_ref = refs[0], refs[1]
        res_ref = refs[2] if res is not None else None
        gate_ref = refs[n_in - 1] if gate is not None else None
        outs = refs[n_in:n_in + (2 if emit_y else 1)]

        def finish(acc):
            val = acc
            if gate_ref is not None:
                val = gate_ref[...] * val
            if res_ref is not None:
                val = res_ref[...] + val
            if emit_y:
                outs[0][...] = acc.astype(BF16)
            outs[-1][...] = val.astype(out_dtype)

        part = lax.dot_general(a_ref[...], b_ref[...], _DIMS[form], preferred_element_type=F32)
        if nk == 1:
            finish(part)
        else:
            acc_ref = refs[-1]
            k = pl.program_id(2)

            @pl.when(k == 0)
            def _():
                acc_ref[...] = part

            @pl.when(k > 0)
            def _():
                acc_ref[...] += part

            @pl.when(k == nk - 1)
            def _():
                finish(acc_ref[...])

    return pl.pallas_call(
        body, name=name, grid=grid, in_specs=in_specs, out_specs=out_specs, out_shape=out_shape,
        scratch_shapes=[pltpu.VMEM((tm, tn), F32)] if nk > 1 else [],
        compiler_params=_params("parallel", "parallel", "arbitrary"),
    )(*args)


def _norm_mod(x, g, sc, sh, name):
    T, D = x.shape
    tm = _pick(T, 256, BF16_ROWS)

    def body(x_ref, g_ref, sc_ref, sh_ref, h_ref):
        xv = x_ref[...]
        rstd = lax.rsqrt(jnp.mean(xv * xv, axis=-1, keepdims=True) + EPS)
        y = xv * rstd * g_ref[...]
        h_ref[...] = (y * (1.0 + sc_ref[...]) + sh_ref[...]).astype(BF16)

    tile = pl.BlockSpec((tm, D), lambda i: (i, 0))
    row = pl.BlockSpec((1, D), lambda i: (0, 0))
    return pl.pallas_call(
        body, name=name, grid=(T // tm,), in_specs=[tile, row, row, row], out_specs=tile,
        out_shape=jax.ShapeDtypeStruct((T, D), BF16), compiler_params=_params("parallel"),
    )(x, g, sc, sh)


def _colsum(v):
    return jnp.sum(v, axis=0, keepdims=True)


def _norm_bwd(x, g, sc, sh, dh, dres, prev, name):
    T, D = x.shape
    tm = _pick(T, 256, BF16_ROWS)
    has_prev = prev is not None

    def body(*refs):
        x_ref, g_ref, sc_ref, sh_ref, dh_ref, dres_ref = refs[:6]
        k = 6
        if has_prev:
            y_ref, gate_ref = refs[6:8]
            k = 8
        dx_ref, dsc_ref, dsh_ref, dg_ref = refs[k:k + 4]
        first = pl.program_id(0) == 0
        xv = x_ref[...]
        rstd = lax.rsqrt(jnp.mean(xv * xv, axis=-1, keepdims=True) + EPS)
        xhat = xv * rstd
        gv = g_ref[...]
        dhv = dh_ref[...].astype(F32)
        dyv = dhv * (1.0 + sc_ref[...])
        dxhat = dyv * gv
        dx = rstd * (dxhat - xhat * jnp.mean(dxhat * xhat, axis=-1, keepdims=True))
        dxt = dres_ref[...] + dx
        dx_ref[...] = dxt
        sums = [(dsc_ref, _colsum(dhv * (xhat * gv))), (dsh_ref, _colsum(dhv)), (dg_ref, _colsum(dyv * xhat))]
        if has_prev:
            dy_ref, dgate_ref = refs[k + 4:k + 6]
            dy_ref[...] = (dxt * gate_ref[...]).astype(BF16)
            sums.append((dgate_ref, _colsum(dxt * y_ref[...].astype(F32))))
        for ref, val in sums:
            @pl.when(first)
            def _(ref=ref, val=val):
                ref[...] = val

            @pl.when(jnp.logical_not(first))
            def _(ref=ref, val=val):
                ref[...] += val

    tile = pl.BlockSpec((tm, D), lambda i: (i, 0))
    row = pl.BlockSpec((1, D), lambda i: (0, 0))
    rowshape = jax.ShapeDtypeStruct((1, D), F32)
    in_specs = [tile, row, row, row, tile, tile]
    args = [x, g, sc, sh, dh, dres]
    out_specs = [tile, row, row, row]
    out_shape = [jax.ShapeDtypeStruct((T, D), F32), rowshape, rowshape, rowshape]
    if has_prev:
        in_specs += [tile, row]
        args += list(prev)
        out_specs += [tile, row]
        out_shape += [jax.ShapeDtypeStruct((T, D), BF16), rowshape]
    return pl.pallas_call(
        body, name=name, grid=(T // tm,), in_specs=in_specs, out_specs=out_specs, out_shape=out_shape,
        compiler_params=_params("arbitrary"),
    )(*args)


def _final_loss_bwd(x, g, target, y_prev, gate_prev, name):
    T, D = x.shape
    tm = _pick(T, 256, BF16_ROWS)

    def body(x_ref, g_ref, t_ref, y_ref, gate_ref, loss_ref, dx_ref, dg_ref, dy_ref, dgate_ref):
        first = pl.program_id(0) == 0
        xv = x_ref[...]
        rstd = lax.rsqrt(jnp.mean(xv * xv, axis=-1, keepdims=True) + EPS)
        xhat = xv * rstd
        gv = g_ref[...]
        err = xhat * gv - t_ref[...]
        part = 0.5 * jnp.sum(jnp.mean(err * err, axis=-1, keepdims=True))
        dyf = err * (1.0 / D)
        dxhat = dyf * gv
        dx = rstd * (dxhat - xhat * jnp.mean(dxhat * xhat, axis=-1, keepdims=True))
        dx_ref[...] = dx
        dy_ref[...] = (dx * gate_ref[...]).astype(BF16)
        sums = [(dg_ref, _colsum(dyf * xhat)), (dgate_ref, _colsum(dx * y_ref[...].astype(F32))),
                (loss_ref, jnp.full((8, LANES), part, F32))]
        for ref, val in sums:
            @pl.when(first)
            def _(ref=ref, val=val):
                ref[...] = val

            @pl.when(jnp.logical_not(first))
            def _(ref=ref, val=val):
                ref[...] += val

    tile = pl.BlockSpec((tm, D), lambda i: (i, 0))
    row = pl.BlockSpec((1, D), lambda i: (0, 0))
    rowshape = jax.ShapeDtypeStruct((1, D), F32)
    return pl.pallas_call(
        body, name=name, grid=(T // tm,),
        in_specs=[tile, row, tile, tile, row],
        out_specs=[pl.BlockSpec((8, LANES), lambda i: (0, 0)), tile, row, tile, row],
        out_shape=[jax.ShapeDtypeStruct((8, LANES), F32), jax.ShapeDtypeStruct((T, D), F32), rowshape,
                   jax.ShapeDtypeStruct((T, D), BF16), rowshape],
        compiler_params=_params("arbitrary"),
    )(x, g, target, y_prev, gate_prev)


def _split3(v):
    hi = v.astype(BF16)
    r1 = v - hi.astype(F32)
    mid = r1.astype(BF16)
    lo = (r1 - mid.astype(F32)).astype(BF16)
    return hi, mid, lo


def _tri_dot(tri, v):
    hi, mid, lo = _split3(v)
    dot = functools.partial(jnp.dot, preferred_element_type=F32)
    return dot(tri, hi) + dot(tri, mid) + dot(tri, lo)


def _fox_gate(flog, bias, name):
    T = flog.shape[0]
    tb = _pick(T, 256, BF16_ROWS)

    def body(f_ref, b_ref, o_ref, carry_ref):
        @pl.when(pl.program_id(0) == 0)
        def _():
            carry_ref[...] = jnp.zeros_like(carry_ref)

        xv = f_ref[...] + b_ref[...]
        lf = jnp.minimum(xv, 0.0) - jnp.log(1.0 + jnp.exp(-jnp.abs(xv)))
        r = lax.broadcasted_iota(jnp.int32, (tb, tb), 0)
        cidx = lax.broadcasted_iota(jnp.int32, (tb, tb), 1)
        tri = jnp.where(r >= cidx, 1.0, 0.0).astype(BF16)
        out = _tri_dot(tri, lf) + carry_ref[...]
        o_ref[...] = out
        carry_ref[...] = out[tb - 1:tb, :]

    return pl.pallas_call(
        body, name=name, grid=(T // tb,),
        in_specs=[pl.BlockSpec((tb, LANES), lambda i: (i, 0)), pl.BlockSpec((1, LANES), lambda i: (0, 0))],
        out_specs=pl.BlockSpec((tb, LANES), lambda i: (i, 0)),
        out_shape=jax.ShapeDtypeStruct((T, LANES), F32),
        scratch_shapes=[pltpu.VMEM((1, LANES), F32)],
        compiler_params=_params("arbitrary"),
    )(flog, bias)


def _fox_gate_bwd(dfq, dfk, flog, bias, name):
    T = flog.shape[0]
    tb = _pick(T, 256, BF16_ROWS)
    nb = T // tb

    def body(dq_ref, dk_ref, f_ref, b_ref, o_ref, db_ref, carry_ref):
        first = pl.program_id(0) == 0

        @pl.when(first)
        def _():
            carry_ref[...] = jnp.zeros_like(carry_ref)

        dv = dq_ref[...] + dk_ref[...]
        r = lax.broadcasted_iota(jnp.int32, (tb, tb), 0)
        cidx = lax.broadcasted_iota(jnp.int32, (tb, tb), 1)
        tri = jnp.where(cidx >= r, 1.0, 0.0).astype(BF16)
        dl = _tri_dot(tri, dv) + carry_ref[...]
        carry_ref[...] = dl[0:1, :]
        xv = f_ref[...] + b_ref[...]
        dfl = dl / (1.0 + jnp.exp(xv))
        o_ref[...] = dfl.astype(BF16)
        val = _colsum(dfl)

        @pl.when(first)
        def _():
            db_ref[...] = val

        @pl.when(jnp.logical_not(first))
        def _():
            db_ref[...] += val

    rev = pl.BlockSpec((tb, LANES), lambda i: (nb - 1 - i, 0))
    row = pl.BlockSpec((1, LANES), lambda i: (0, 0))
    return pl.pallas_call(
        body, name=name, grid=(nb,), in_specs=[rev, rev, rev, row], out_specs=[rev, row],
        out_shape=[jax.ShapeDtypeStruct((T, LANES), BF16), jax.ShapeDtypeStruct((1, LANES), F32)],
        scratch_shapes=[pltpu.VMEM((1, LANES), F32)],
        compiler_params=_params("arbitrary"),
    )(dfq, dfk, flog, bias)


def _head_col(block, h):
    lane = lax.broadcasted_iota(jnp.int32, block.shape, 1)
    return jnp.sum(jnp.where(lane == h, block, 0.0), axis=1, keepdims=True)


_NT = _DIMS["nt"]
_TN = _DIMS["tn"]


def _attn_fwd(qkv, fcol, frow, n_heads, name):
    T = qkv.shape[0]
    H = n_heads
    tq = _pick(T, 512, LANES)
    tk = tq
    scale = HEAD_DIM ** -0.5

    def body(q_ref, k_ref, v_ref, fc_ref, fr_ref, o_ref, lse_ref):
        h, i = pl.program_id(0), pl.program_id(1)
        q = q_ref[...]
        fi = _head_col(fc_ref[...], h)
        rows = i * tq + lax.broadcasted_iota(jnp.int32, (tq, tk), 0)
        cols = lax.broadcasted_iota(jnp.int32, (tq, tk), 1)

        def step(j, carry):
            m, l, acc = carry
            ks = pl.multiple_of(j * tk, tk)
            kb = k_ref[pl.ds(ks, tk), :]
            vb = v_ref[pl.ds(ks, tk), :]
            s = lax.dot_general(q, kb, _NT, preferred_element_type=F32) * scale
            s = s + fi - fr_ref[:, pl.ds(ks, tk)]
            s = jnp.where(cols + ks <= rows, s, NEG_BIG)
            m_new = jnp.maximum(m, jnp.max(s, axis=1, keepdims=True))
            alpha = jnp.exp(m - m_new)
            p = jnp.exp(s - m_new)
            l = alpha * l + jnp.sum(p, axis=1, keepdims=True)
            acc = alpha * acc + jnp.dot(p.astype(BF16), vb, preferred_element_type=F32)
            return m_new, l, acc

        init = (jnp.full((tq, 1), NEG_BIG, F32), jnp.zeros((tq, 1), F32), jnp.zeros((tq, HEAD_DIM), F32))
        m, l, acc = lax.fori_loop(0, i + 1, step, init)
        o_ref[...] = (acc / l).astype(BF16)
        lse_ref[...] = jnp.broadcast_to(m + jnp.log(l), (tq, LANES))

    return pl.pallas_call(
        body, name=name, grid=(H, T // tq),
        in_specs=[pl.BlockSpec((tq, HEAD_DIM), lambda h, i: (i, h)),
                  pl.BlockSpec((T, HEAD_DIM), lambda h, i: (0, H + h)),
                  pl.BlockSpec((T, HEAD_DIM), lambda h, i: (0, 2 * H + h)),
                  pl.BlockSpec((tq, LANES), lambda h, i: (i, 0)),
                  pl.BlockSpec((None, 1, T), lambda h, i: (h, 0, 0))],
        out_specs=[pl.BlockSpec((tq, HEAD_DIM), lambda h, i: (i, h)),
                   pl.BlockSpec((None, tq, LANES), lambda h, i: (h, i, 0))],
        out_shape=[jax.ShapeDtypeStruct((T, H * HEAD_DIM), BF16), jax.ShapeDtypeStruct((H, T, LANES), F32)],
        compiler_params=_params("parallel", "arbitrary"),
    )(qkv, qkv, qkv, fcol, frow)


def _attn_dq(qkv, o, do, lse, fcol, frow, n_heads, name):
    T = qkv.shape[0]
    H = n_heads
    tq = _pick(T, 512, LANES)
    tk = tq
    scale = HEAD_DIM ** -0.5

    def body(q_ref, k_ref, v_ref, o_ref, do_ref, lse_ref, fc_ref, fr_ref, dq_ref, df_ref):
        h, i = pl.program_id(0), pl.program_id(1)
        q = q_ref[...]
        dob = do_ref[...]
        delta = jnp.sum(dob.astype(F32) * o_ref[...].astype(F32), axis=1, keepdims=True)
        lse_i = lse_ref[:, 0:1]
        fi = _head_col(fc_ref[...], h)
        rows = i * tq + lax.broadcasted_iota(jnp.int32, (tq, tk), 0)
        cols = lax.broadcasted_iota(jnp.int32, (tq, tk), 1)

        def step(j, carry):
            dq, rsum = carry
            ks = pl.multiple_of(j * tk, tk)
            kb = k_ref[pl.ds(ks, tk), :]
            vb = v_ref[pl.ds(ks, tk), :]
            s = lax.dot_general(q, kb, _NT, preferred_element_type=F32) * scale
            s = s + fi - fr_ref[:, pl.ds(ks, tk)]
            p = jnp.where(cols + ks <= rows, jnp.exp(s - lse_i), 0.0)
            dp = lax.dot_general(dob, vb, _NT, preferred_element_type=F32)
            ds = p * (dp - delta)
            return (dq + jnp.dot(ds.astype(BF16), kb, preferred_element_type=F32),
                    rsum + jnp.sum(ds, axis=1, keepdims=True))

        dq, rsum = lax.fori_loop(0, i + 1, step, (jnp.zeros((tq, HEAD_DIM), F32), jnp.zeros((tq, 1), F32)))
        dq_ref[...] = (dq * scale).astype(BF16)
        df_ref[...] = jnp.transpose(jnp.broadcast_to(rsum, (tq, LANES)))[0:1, :]

    qtile = lambda off: pl.BlockSpec((tq, HEAD_DIM), lambda h, i: (i, off + h))
    return pl.pallas_call(
        body, name=name, grid=(H, T // tq),
        in_specs=[qtile(0),
                  pl.BlockSpec((T, HEAD_DIM), lambda h, i: (0, H + h)),
                  pl.BlockSpec((T, HEAD_DIM), lambda h, i: (0, 2 * H + h)),
                  qtile(0), qtile(0),
                  pl.BlockSpec((None, tq, LANES), lambda h, i: (h, i, 0)),
                  pl.BlockSpec((tq, LANES), lambda h, i: (i, 0)),
                  pl.BlockSpec((None, 1, T), lambda h, i: (h, 0, 0))],
        out_specs=[qtile(0), pl.BlockSpec((None, 1, tq), lambda h, i: (h, 0, i))],
        out_shape=[jax.ShapeDtypeStruct((T, H * HEAD_DIM), BF16), jax.ShapeDtypeStruct((H, 1, T), F32)],
        compiler_params=_params("parallel", "arbitrary"),
    )(qkv, qkv, qkv, o, do, lse, fcol, frow)


def _attn_dkv(qkv, o, do, lse, fcol, frow, n_heads, name):
    T = qkv.shape[0]
    H = n_heads
    tk = _pick(T, 512, LANES)
    tq = tk
    nq = T // tq
    scale = HEAD_DIM ** -0.5

    def body(k_ref, v_ref, q_ref, o_ref, do_ref, lse_ref, fc_ref, fr_ref, dk_ref, dv_ref, df_ref):
        h, j = pl.program_id(0), pl.program_id(1)
        kb = k_ref[...]
        vb = v_ref[...]
        fj = fr_ref[...]
        cols = j * tk + lax.broadcasted_iota(jnp.int32, (tq, tk), 1)
        rows = lax.broadcasted_iota(jnp.int32, (tq, tk), 0)

        def step(i, carry):
            dk, dv, df = carry
            qs = pl.multiple_of(i * tq, tq)
            qb = q_ref[pl.ds(qs, tq), :]
            dob = do_ref[pl.ds(qs, tq), :]
            delta = jnp.sum(dob.astype(F32) * o_ref[pl.ds(qs, tq), :].astype(F32), axis=1, keepdims=True)
            lse_i = lse_ref[pl.ds(qs, tq), 0:1]
            fi = _head_col(fc_ref[pl.ds(qs, tq), :], h)
            s = lax.dot_general(qb, kb, _NT, preferred_element_type=F32) * scale
            s = s + fi - fj
            p = jnp.where(cols <= rows + qs, jnp.exp(s - lse_i), 0.0)
            dv = dv + lax.dot_general(p.astype(BF16), dob, _TN, preferred_element_type=F32)
            dp = lax.dot_general(dob, vb, _NT, preferred_element_type=F32)
            ds = p * (dp - delta)
            dk = dk + lax.dot_general(ds.astype(BF16), qb, _TN, preferred_element_type=F32)
            df = df - _colsum(ds)
            return dk, dv, df

        init = (jnp.zeros((tk, HEAD_DIM), F32), jnp.zeros((tk, HEAD_DIM), F32), jnp.zeros((1, tk), F32))
        dk, dv, df = lax.fori_loop(j, nq, step, init)
        dk_ref[...] = (dk * scale).astype(BF16)
        dv_ref[...] = dv.astype(BF16)
        df_ref[...] = df

    full = lambda off: pl.BlockSpec((T, HEAD_DIM), lambda h, j: (0, off + h))
    ktile = lambda off: pl.BlockSpec((tk, HEAD_DIM), lambda h, j: (j, off + h))
    return pl.pallas_call(
        body, name=name, grid=(H, T // tk),
        in_specs=[ktile(H), ktile(2 * H), full(0), full(0), full(0),
                  pl.BlockSpec((None, T, LANES), lambda h, j: (h, 0, 0)),
                  pl.BlockSpec((T, LANES), lambda h, j: (0, 0)),
                  pl.BlockSpec((None, 1, tk), lambda h, j: (h, 0, j))],
        out_specs=[ktile(0), ktile(0), pl.BlockSpec((None, 1, tk), lambda h, j: (h, 0, j))],
        out_shape=[jax.ShapeDtypeStruct((T, H * HEAD_DIM), BF16)] * 2 + [jax.ShapeDtypeStruct((H, 1, T), F32)],
        compiler_params=_params("parallel", "arbitrary"),
    )(qkv, qkv, qkv, o, do, lse, fcol, frow)


_GELU_C = math.sqrt(2.0 / math.pi)
_GELU_A = 0.044715


def _gelu(p):
    t = jnp.tanh(_GELU_C * (p + _GELU_A * p * p * p))
    return 0.5 * p * (1.0 + t), t


def _gelu_grad(p, t):
    return 0.5 * (1.0 + t) + 0.5 * p * (1.0 - t * t) * _GELU_C * (1.0 + 3.0 * _GELU_A * p * p)


def _tril(shape, upper=False):
    r = lax.broadcasted_iota(jnp.int32, shape, 0)
    c = lax.broadcasted_iota(jnp.int32, shape, 1)
    return (c >= r) if upper else (r >= c)


def _gm_fwd(p, vg, ws, bt, name):
    T, D2 = p.shape
    D = D2 // 2
    NG = D // GROUP
    tm = _pick(T, 2 * CHUNK, CHUNK)

    def body(p_ref, vg_ref, ws_ref, bt_ref, o_ref):
        z, _ = _gelu(p_ref[...].astype(F32))
        u, v = z[:, :D], z[:, D:]
        rstd = lax.rsqrt(jnp.mean(v * v, axis=-1, keepdims=True) + EPS)
        vn = (v * rstd * vg_ref[...]).astype(BF16)
        low = _tril((CHUNK, CHUNK))
        for g in range(NG):
            wm = jnp.where(low, ws_ref[g], 0.0).astype(BF16)
            bcol = bt_ref[:, g:g + 1]
            cs = slice(g * GROUP, (g + 1) * GROUP)
            for r in range(tm // CHUNK):
                rs = slice(r * CHUNK, (r + 1) * CHUNK)
                sv = jnp.dot(wm, vn[rs, cs], preferred_element_type=F32) + bcol
                o_ref[rs, cs] = (u[rs, cs] * sv).astype(BF16)

    return pl.pallas_call(
        body, name=name, grid=(T // tm,),
        in_specs=[pl.BlockSpec((tm, D2), lambda i: (i, 0)), pl.BlockSpec((1, D), lambda i: (0, 0)),
                  pl.BlockSpec((NG, CHUNK, CHUNK), lambda i: (0, 0, 0)),
                  pl.BlockSpec((CHUNK, LANES), lambda i: (0, 0))],
        out_specs=pl.BlockSpec((tm, D), lambda i: (i, 0)),
        out_shape=jax.ShapeDtypeStruct((T, D), BF16),
        compiler_params=_params("parallel"),
    )(p, vg, ws, bt)


def _gm_bwd(p, dgated, vg, ws, wst, bt, name):
    T, D2 = p.shape
    D = D2 // 2
    NG = D // GROUP
    tm = _pick(T, 2 * CHUNK, CHUNK)
    nb = T // tm

    def body(p_ref, dg_ref, vg_ref, ws_ref, wst_ref, bt_ref, dp_ref, dws_ref, dbt_ref, dvg_ref, du_s, dvn_s):
        step = pl.program_id(0)

        @pl.when(step == 0)
        def _():
            dws_ref[...] = jnp.zeros_like(dws_ref)
            dbt_ref[...] = jnp.zeros_like(dbt_ref)
            dvg_ref[...] = jnp.zeros_like(dvg_ref)

        pv = p_ref[...].astype(F32)
        z, t = _gelu(pv)
        u, v = z[:, :D], z[:, D:]
        rstd = lax.rsqrt(jnp.mean(v * v, axis=-1, keepdims=True) + EPS)
        vhat = v * rstd
        vgv = vg_ref[...]
        vn = (vhat * vgv).astype(BF16)
        dgt = dg_ref[...].astype(F32)
        low = _tril((CHUNK, CHUNK))
        up = _tril((CHUNK, CHUNK), upper=True)
        lane = lax.broadcasted_iota(jnp.int32, (CHUNK, LANES), 1)
        dbt = jnp.zeros((CHUNK, LANES), F32)
        for g in range(NG):
            wm = jnp.where(low, ws_ref[g], 0.0).astype(BF16)
            wmt = jnp.where(up, wst_ref[g], 0.0).astype(BF16)
            bcol = bt_ref[:, g:g + 1]
            cs = slice(g * GROUP, (g + 1) * GROUP)
            dw = jnp.zeros((CHUNK, CHUNK), F32)
            for r in range(tm // CHUNK):
                rs = slice(r * CHUNK, (r + 1) * CHUNK)
                blk = vn[rs, cs]
                sv = jnp.dot(wm, blk, preferred_element_type=F32) + bcol
                dgb = dgt[rs, cs]
                du_s[rs, cs] = dgb * sv
                dsv = dgb * u[rs, cs]
                dsvb = dsv.astype(BF16)
                dvn_s[rs, cs] = jnp.dot(wmt, dsvb, preferred_element_type=F32)
                dw = dw + lax.dot_general(dsvb, blk, _NT, preferred_element_type=F32)
                dbt = dbt + jnp.where(lane == g, jnp.sum(dsv, axis=1, keepdims=True), 0.0)
            dws_ref[g] += dw
        dbt_ref[...] += dbt
        dvn = dvn_s[...]
        dvg_ref[...] += _colsum(dvn * vhat)
        dvhat = dvn * vgv
        dv = rstd * (dvhat - vhat * jnp.mean(dvhat * vhat, axis=-1, keepdims=True))
        gg = _gelu_grad(pv, t)
        dp_ref[:, :D] = (du_s[...] * gg[:, :D]).astype(BF16)
        dp_ref[:, D:] = (dv * gg[:, D:]).astype(BF16)

        @pl.when(step == nb - 1)
        def _():
            for g in range(NG):
                dws_ref[g] = jnp.where(low, dws_ref[g], 0.0)

    const3 = pl.BlockSpec((NG, CHUNK, CHUNK), lambda i: (0, 0, 0))
    return pl.pallas_call(
        body, name=name, grid=(nb,),
        in_specs=[pl.BlockSpec((tm, D2), lambda i: (i, 0)), pl.BlockSpec((tm, D), lambda i: (i, 0)),
                  pl.BlockSpec((1, D), lambda i: (0, 0)), const3, const3,
                  pl.BlockSpec((CHUNK, LANES), lambda i: (0, 0))],
        out_specs=[pl.BlockSpec((tm, D2), lambda i: (i, 0)), const3,
                   pl.BlockSpec((CHUNK, LANES), lambda i: (0, 0)), pl.BlockSpec((1, D), lambda i: (0, 0))],
        out_shape=[jax.ShapeDtypeStruct((T, D2), BF16), jax.ShapeDtypeStruct((NG, CHUNK, CHUNK), F32),
                   jax.ShapeDtypeStruct((CHUNK, LANES), F32), jax.ShapeDtypeStruct((1, D), F32)],
        scratch_shapes=[pltpu.VMEM((tm, D), F32), pltpu.VMEM((tm, D), F32)],
        compiler_params=_params("arbitrary"),
    )(p, dgated, vg, ws, wst, bt)


HALO = BF16_ROWS


def _ffn_act(a, cw, cb, name):
    T, F2 = a.shape
    Fh = F2 // 2
    tm = _pick(T, 256, HALO)
    per = tm // HALO
    nch = Fh // LANES

    def body(a_ref, h_ref, cw_ref, cb_ref, o_ref):
        i = pl.program_id(0)
        row = lax.broadcasted_iota(jnp.int32, (tm, LANES), 0)
        keep = jnp.where(i > 0, 1.0, 0.0)

        def conv(off):
            af = a_ref[:, pl.ds(off, LANES)].astype(F32)
            hf = h_ref[:, pl.ds(off, LANES)].astype(F32) * keep
            r1 = jnp.where(row == 0, hf[HALO - 1:HALO], pltpu.roll(af, 1, 0))
            r2 = jnp.where(row == 0, hf[HALO - 2:HALO - 1], jnp.where(row == 1, hf[HALO - 1:HALO], pltpu.roll(af, 2, 0)))
            w = cw_ref[:, pl.ds(off, LANES)]
            return w[0:1] * r2 + w[1:2] * r1 + w[2:3] * af + cb_ref[:, pl.ds(off, LANES)]

        def chunk(c, carry):
            off = pl.multiple_of(c * LANES, LANES)
            gt = conv(off)
            upv = conv(pl.multiple_of(off + Fh, LANES))
            o_ref[:, pl.ds(off, LANES)] = (gt / (1.0 + jnp.exp(-gt)) * upv).astype(BF16)
            return carry

        lax.fori_loop(0, nch, chunk, 0)

    return pl.pallas_call(
        body, name=name, grid=(T // tm,),
        in_specs=[pl.BlockSpec((tm, F2), lambda i: (i, 0)),
                  pl.BlockSpec((HALO, F2), lambda i: (jnp.maximum(i * per - 1, 0), 0)),
                  pl.BlockSpec((3, F2), lambda i: (0, 0)), pl.BlockSpec((1, F2), lambda i: (0, 0))],
        out_specs=pl.BlockSpec((tm, Fh), lambda i: (i, 0)),
        out_shape=jax.ShapeDtypeStruct((T, Fh), BF16),
        compiler_params=_params("parallel"),
    )(a, a, cw, cb)


def _ffn_act_bwd(a, dhm, cw, cb, name):
    T, F2 = a.shape
    Fh = F2 // 2
    tm = _pick(T, 256, HALO)
    per = tm // HALO
    nb = T // tm
    last_halo = T // HALO - 1
    nch = Fh // LANES
    ext = tm + 2 * HALO

    def body(a_ref, hp_ref, hn_ref, d_ref, dn_ref, cw_ref, cb_ref, da_ref, dcw_ref, dcb_ref):
        i = pl.program_id(0)

        @pl.when(i == 0)
        def _():
            dcw_ref[...] = jnp.zeros_like(dcw_ref)
            dcb_ref[...] = jnp.zeros_like(dcb_ref)

        keep_p = jnp.where(i > 0, 1.0, 0.0)
        keep_n = jnp.where(i < nb - 1, 1.0, 0.0)
        row = lax.broadcasted_iota(jnp.int32, (ext, LANES), 0)
        main = jnp.logical_and(row >= HALO, row < HALO + tm)

        def conv(off):
            e = jnp.concatenate([hp_ref[:, pl.ds(off, LANES)].astype(F32) * keep_p,
                                 a_ref[:, pl.ds(off, LANES)].astype(F32),
                                 hn_ref[:, pl.ds(off, LANES)].astype(F32) * keep_n], axis=0)
            r1 = pltpu.roll(e, 1, 0)
            r2 = pltpu.roll(e, 2, 0)
            w = cw_ref[:, pl.ds(off, LANES)]
            return w, e, r1, r2, w[0:1] * r2 + w[1:2] * r1 + w[2:3] * e + cb_ref[:, pl.ds(off, LANES)]

        def back(off, w, e, r1, r2, dc):
            dcm = jnp.where(main, dc, 0.0)
            for tap, shifted in enumerate((r2, r1, e)):
                dcw_ref[tap:tap + 1, pl.ds(off, LANES)] += _colsum(dcm * shifted)
            dcb_ref[:, pl.ds(off, LANES)] += _colsum(dcm)
            da = w[2:3] * dc + w[1:2] * pltpu.roll(dc, ext - 1, 0) + w[0:1] * pltpu.roll(dc, ext - 2, 0)
            da_ref[:, pl.ds(off, LANES)] = da[HALO:HALO + tm].astype(BF16)

        def chunk(c, carry):
            off = pl.multiple_of(c * LANES, LANES)
            off_u = pl.multiple_of(off + Fh, LANES)
            dh = jnp.concatenate([jnp.zeros((HALO, LANES), F32), d_ref[:, pl.ds(off, LANES)].astype(F32),
                                  dn_ref[:, pl.ds(off, LANES)].astype(F32) * keep_n], axis=0)
            wg, eg, r1g, r2g, gt = conv(off)
            wu, eu, r1u, r2u, upv = conv(off_u)
            sg = 1.0 / (1.0 + jnp.exp(-gt))
            back(off, wg, eg, r1g, r2g, dh * upv * (sg * (1.0 + gt * (1.0 - sg))))
            back(off_u, wu, eu, r1u, r2u, dh * (gt * sg))
            return carry

        lax.fori_loop(0, nch, chunk, 0)

    prev = lambda i: (jnp.maximum(i * per - 1, 0), 0)
    nxt = lambda i: (jnp.minimum((i + 1) * per, last_halo), 0)
    return pl.pallas_call(
        body, name=name, grid=(nb,),
        in_specs=[pl.BlockSpec((tm, F2), lambda i: (i, 0)), pl.BlockSpec((HALO, F2), prev),
                  pl.BlockSpec((HALO, F2), nxt), pl.BlockSpec((tm, Fh), lambda i: (i, 0)),
                  pl.BlockSpec((HALO, Fh), nxt),
                  pl.BlockSpec((3, F2), lambda i: (0, 0)), pl.BlockSpec((1, F2), lambda i: (0, 0))],
        out_specs=[pl.BlockSpec((tm, F2), lambda i: (i, 0)), pl.BlockSpec((3, F2), lambda i: (0, 0)),
                   pl.BlockSpec((1, F2), lambda i: (0, 0))],
        out_shape=[jax.ShapeDtypeStruct((T, F2), BF16), jax.ShapeDtypeStruct((3, F2), F32),
                   jax.ShapeDtypeStruct((1, F2), F32)],
        compiler_params=_params("arbitrary"),
    )(a, a, a, dhm, dhm, cw, cb)


def _round_up(n, k):
    return -(-n // k) * k


def _pad_axis(a, axis, size):
    pad = [(0, 0)] * a.ndim
    pad[axis] = (0, size - a.shape[axis])
    return jnp.pad(a, pad)


def _pad_blocks(a, axis, nblk, to):
    shp = a.shape
    blk = shp[axis] // nblk
    a = a.reshape(shp[:axis] + (nblk, blk) + shp[axis + 1:])
    a = _pad_axis(a, axis + 1, to)
    return a.reshape(shp[:axis] + (nblk * to,) + shp[axis + 1:])


def _unpad_blocks(a, axis, nblk, blk):
    shp = a.shape
    to = shp[axis] // nblk
    a = a.reshape(shp[:axis] + (nblk, to) + shp[axis + 1:])
    a = lax.slice_in_dim(a, 0, blk, axis=axis + 1)
    return a.reshape(shp[:axis] + (nblk * blk,) + shp[axis + 1:])


def _gather_weight(wt, core, name):
    L, n, D = wt.shape
    half = _cast_half(wt, core, name + "_cast")
    g = _all_gather(half, ("x", "y"), 1, name + "_ag_xy")
    g = _all_gather(g, ("c",), 2, name + "_ag_c")
    return g.reshape(L, 4 * n, D)


def _reduce_weight_grad(gt, name):
    L, n4, D = gt.shape
    n2 = n4 // 8
    ex = _all_to_all(gt.reshape(L, 4, 2, n2, D), ("c",), 2, name + "_a2a_c")
    pair = _sum_slots(ex.reshape(L * 4, 2, n2, D), BF16, name + "_sum_c").reshape(L, 4, n2, D)
    ex = _all_to_all(pair, ("x", "y"), 1, name + "_a2a_xy")
    tot = _sum_slots(ex, F32, name + "_sum_xy")
    return _all_gather(tot, ("c",), 1, name + "_ag_c").reshape(L, 2 * n2, D)


def _all_reduce_small(v, name):
    pair = _sum_slots(_all_gather(v, ("c",), 0, name + "_ag_c")[None], F32, name + "_sum_c")[0]
    return _sum_slots(_all_gather(pair, ("x", "y"), 0, name + "_ag_xy")[None], F32, name + "_sum_xy")[0]


def _pack(arrs):
    flat = jnp.concatenate([a.reshape(-1) for a in arrs])
    rows = _round_up(-(-flat.shape[0] // SMALL_PACK_COLS), BF16_ROWS)
    return _pad_axis(flat, 0, rows * SMALL_PACK_COLS).reshape(rows, SMALL_PACK_COLS)


def _unpack(buf, like):
    flat = buf.reshape(-1)
    out, off = [], 0
    for a in like:
        out.append(flat[off:off + a.size].reshape(a.shape))
        off += a.size
    return out


def kernel(x, c, mod_w, mod_b, mix_norm_g, ffn_norm_g, attn_w_in, attn_b_f, attn_w_o, gm_w_in, gm_v_g, gm_w_s, gm_b_s, gm_w_o, ffn_w_in, ffn_conv_w, ffn_conv_b, ffn_w_out, final_g, loss_target, m_mod_w, m_mod_b, m_mix_norm_g, m_ffn_norm_g, m_attn_w_in, m_attn_b_f, m_attn_w_o, m_gm_w_in, m_gm_v_g, m_gm_w_s, m_gm_b_s, m_gm_w_o, m_ffn_w_in, m_ffn_conv_w, m_ffn_conv_b, m_ffn_w_out, m_final_g, v_mod_w, v_mod_b, v_mix_norm_g, v_ffn_norm_g, v_attn_w_in, v_attn_b_f, v_attn_w_o, v_gm_w_in, v_gm_v_g, v_gm_w_s, v_gm_b_s, v_gm_w_o, v_ffn_w_in, v_ffn_conv_w, v_ffn_conv_b, v_ffn_w_out, v_final_g):
    _, T, D = x.shape
    L = mod_w.shape[0]
    NA, NB = attn_w_in.shape[0], gm_w_in.shape[0]
    H = D // HEAD_DIM
    NG = D // GROUP
    Fq = ffn_w_out.shape[1]
    Fh = 4 * Fq
    Fqp = _round_up(Fq, LANES)
    Fp = 4 * Fqp
    n_attn = attn_w_in.shape[2]
    n_attn_p = _round_up(n_attn, 2 * BF16_ROWS)
    nmod = mod_w.shape[2]

    xi, yi, ci = lax.axis_index("x"), lax.axis_index("y"), lax.axis_index("c")
    chip = 2 * xi + yi
    example = 2 * chip + ci
    core = ci.astype(jnp.int32).reshape(1)

    wt_attn_in = _gather_weight(_pad_axis(jnp.swapaxes(attn_w_in, 1, 2), 1, n_attn_p), core, "w_attn_in")
    wt_attn_in = wt_attn_in.reshape(NA, 4, n_attn_p, D)[:, :, :n_attn].reshape(NA, 4 * n_attn, D)
    wt_qkv = wt_attn_in[:, :3 * D]
    wt_f = _pad_axis(wt_attn_in[:, 3 * D:], 1, LANES)
    w_attn_o = _gather_weight(attn_w_o, core, "w_attn_o")
    wt_gm_in = _gather_weight(jnp.swapaxes(gm_w_in, 1, 2), core, "w_gm_in")
    w_gm_o = _gather_weight(gm_w_o, core, "w_gm_o")
    wt_ffn_in = _gather_weight(_pad_blocks(jnp.swapaxes(ffn_w_in, 1, 2), 1, 2, Fqp), core, "w_ffn_in")
    w_ffn_out = _gather_weight(_pad_axis(ffn_w_out, 1, Fqp), core, "w_ffn_out")
    cw_all = _all_gather(ffn_conv_w, ("x", "y"), 0, "conv_w_ag")
    conv_w_full = jnp.moveaxis(cw_all, 0, 2).reshape(L, 3, 2 * Fh)
    conv_w = _pad_blocks(conv_w_full, 2, 8, Fqp)
    conv_b = _pad_blocks(ffn_conv_b, 1, 8, Fqp)
    vg_all = _all_gather(gm_v_g, ("x", "y"), 0, "gm_vg_ag")
    vg_full = jnp.moveaxis(vg_all, 0, 1).reshape(NB, D)

    c_all = _all_gather(c, MESH_AXES, 0, "c_ag").reshape(8, D)
    mod_b_mine = lax.dynamic_slice_in_dim(mod_b, chip * nmod, nmod, axis=1)
    c_act, mod_part = _mod_fwd(c_all, mod_w, mod_b_mine, "mod_fwd")
    mod_all = _all_gather(mod_part, ("x", "y"), 0, "mod_ag")
    mod = lax.dynamic_index_in_dim(mod_all, example, axis=2, keepdims=False)
    mod = jnp.swapaxes(mod, 0, 1).reshape(L, 6, 1, D)

    xs = x.reshape(T, D)
    target = loss_target.reshape(T, D)
    saved = []
    for i in range(L):
        j = i // 2
        sh1, sc1, g1, sh2, sc2, g2 = (mod[i, k] for k in range(6))
        h1 = _norm_mod(xs, mix_norm_g[i:i + 1], sc1, sh1, "norm_mod")
        if i % 2 == 0:
            qkv = _matmul(h1, wt_qkv[j], "nt", "attn_qkv", tm=1024, tn=1024)
            flog = _matmul(h1, wt_f[j], "nt", "attn_f", out_dtype=F32, tm=1024)
            bias = _pad_axis(attn_b_f[j:j + 1], 1, LANES)
            fcol = _fox_gate(flog, bias, "fox_gate")
            frow = jnp.transpose(fcol)[:H].reshape(H, 1, T)
            o, lse = _attn_fwd(qkv, fcol, frow, H, "attn_fwd")
            y, x1 = _matmul(o, w_attn_o[j], "nn", "attn_out", out_dtype=F32, tm=1024, tn=1024,
                            res=xs, gate=g1, emit_y=True)
            mix = ("fox", qkv, flog, bias, fcol, frow, o, lse)
        else:
            p = _matmul(h1, wt_gm_in[j], "nt", "gm_in", tm=1024, tn=1024)
            bt = _pad_axis(jnp.transpose(gm_b_s[j]), 1, LANES)
            gated = _gm_fwd(p, vg_full[j:j + 1], gm_w_s[j], bt, "gm_fwd")
            y, x1 = _matmul(gated, w_gm_o[j], "nn", "gm_out", out_dtype=F32, tm=1024, tn=1024,
                            res=xs, gate=g1, emit_y=True)
            mix = ("gm", p, bt, gated)
        h2 = _norm_mod(x1, ffn_norm_g[i:i + 1], sc2, sh2, "norm_mod")
        a = _matmul(h2, wt_ffn_in[i], "nt", "ffn_in", tm=1024, tn=1024)
        hmid = _ffn_act(a, conv_w[i], conv_b[i:i + 1], "ffn_act")
        y2, x2 = _matmul(hmid, w_ffn_out[i], "nn", "ffn_out", out_dtype=F32, tm=512, tn=1024,
                         res=x1, gate=g2, emit_y=True)
        saved.append((xs, h1, mix, y, x1, h2, a, hmid, y2))
        xs = x2

    loss_blk, dx, g_final, dy2, dg2 = _final_loss_bwd(
        xs, final_g.reshape(1, D), target, saved[L - 1][8], mod[L - 1, 5], "final_loss_bwd")
    gw_attn_in, gw_attn_o, gw_gm_in, gw_gm_o = [None] * NA, [None] * NA, [None] * NB, [None] * NB
    gw_ffn_in, gw_ffn_out = [None] * L, [None] * L
    g_mix, g_ffn_n, g_cw, g_cb, dmod = [None] * L, [None] * L, [None] * L, [None] * L, [None] * L
    g_bf, g_vg, g_ws, g_bs = [None] * NA, [None] * NB, [None] * NB, [None] * NB
    for i in reversed(range(L)):
        j = i // 2
        xin, h1, mix, y, x1, h2, a, hmid, y2 = saved[i]
        sh1, sc1, g1, sh2, sc2, g2 = (mod[i, k] for k in range(6))
        dhm = _matmul(dy2, w_ffn_out[i], "nt", "ffn_out_dx", tm=1024, tn=512)
        gw_ffn_out[i] = _matmul(hmid, dy2, "tn", "ffn_out_dw", tm=512, tn=1024)
        da, dcw, dcb = _ffn_act_bwd(a, dhm, conv_w[i], conv_b[i:i + 1], "ffn_act_bwd")
        dh2 = _matmul(da, wt_ffn_in[i], "nn", "ffn_in_dx", out_dtype=F32, tm=512, tn=1024, tk=2816)
        gw_ffn_in[i] = _matmul(da, h2, "tn", "ffn_in_dw", tm=512, tn=1024)
        g_cw[i] = _unpad_blocks(dcw, 1, 8, Fq)
        g_cb[i] = _unpad_blocks(dcb, 1, 8, Fq)[0]
        dx1, dsc2, dsh2, dgn, dy, dg1 = _norm_bwd(
            x1, ffn_norm_g[i:i + 1], sc2, sh2, dh2, dx, (y, g1), "norm_bwd_prev")
        g_ffn_n[i] = dgn[0]
        if mix[0] == "fox":
            _, qkv, flog, bias, fcol, frow, o, lse = mix
            do = _matmul(dy, w_attn_o[j], "nt", "attn_out_dx", tm=1024, tn=1024)
            gw_attn_o[j] = _matmul(o, dy, "tn", "attn_out_dw", tm=512, tn=1024)
            dq, dfq = _attn_dq(qkv, o, do, lse, fcol, frow, H, "attn_dq")
            dk, dv, dfk = _attn_dkv(qkv, o, do, lse, fcol, frow, H, "attn_dkv")
            heads_on_lanes = lambda t: _pad_axis(jnp.transpose(t.reshape(H, T)), 1, LANES)
            dflog, dbf = _fox_gate_bwd(heads_on_lanes(dfq), heads_on_lanes(dfk), flog, bias, "fox_gate_bwd")
            dproj = jnp.concatenate([dq, dk, dv], axis=1)
            dh_f = _matmul(dflog, wt_f[j], "nn", "attn_f_dx", out_dtype=F32, tm=1024, tn=1024)
            dh1 = _matmul(dproj, wt_qkv[j], "nn", "attn_qkv_dx", out_dtype=F32, tm=512, tn=1024,
                          tk=2048, res=dh_f)
            gwt_qkv = _matmul(dproj, h1, "tn", "attn_qkv_dw", tm=512, tn=1024)
            gwt_f = _matmul(dflog, h1, "tn", "attn_f_dw", tm=128, tn=1024)
            gw_attn_in[j] = jnp.concatenate([gwt_qkv, gwt_f[:H]], axis=0)
            g_bf[j] = dbf[0, :H]
        else:
            _, p, bt, gated = mix
            dgated = _matmul(dy, w_gm_o[j], "nt", "gm_out_dx", tm=1024, tn=1024)
            gw_gm_o[j] = _matmul(gated, dy, "tn", "gm_out_dw", tm=512, tn=1024)
            dp, dws, dbt, dvg = _gm_bwd(p, dgated, vg_full[j:j + 1], gm_w_s[j],
                                        jnp.swapaxes(gm_w_s[j], 1, 2), bt, "gm_bwd")
            dh1 = _matmul(dp, wt_gm_in[j], "nn", "gm_in_dx", out_dtype=F32, tm=512, tn=1024, tk=2048)
            gw_gm_in[j] = _matmul(dp, h1, "tn", "gm_in_dw", tm=512, tn=1024)
            g_ws[j] = dws
            g_bs[j] = jnp.transpose(dbt)[:NG]
            g_vg[j] = dvg[0]
        if i > 0:
            dx, dsc1, dsh1, dgn, dy2, dg2_prev = _norm_bwd(
                xin, mix_norm_g[i:i + 1], sc1, sh1, dh1, dx1, (saved[i - 1][8], mod[i - 1, 5]), "norm_bwd_prev")
        else:
            dx, dsc1, dsh1, dgn = _norm_bwd(xin, mix_norm_g[i:i + 1], sc1, sh1, dh1, dx1, None, "norm_bwd_first")
            dg2_prev = None
        g_mix[i] = dgn[0]
        dmod[i] = jnp.concatenate([dsh1, dsc1, dg1, dsh2, dsc2, dg2], axis=1)[0]
        dg2 = dg2_prev
    grad_x = dx.reshape(1, T, D)
    loss = lax.psum(loss_blk[0, 0], MESH_AXES)

    gt = jnp.stack(gw_attn_in).reshape(NA, 4, n_attn, D)
    gt = _pad_axis(gt, 2, n_attn_p).reshape(NA, 4 * n_attn_p, D)
    grad_attn_w_in = jnp.swapaxes(_reduce_weight_grad(gt, "g_attn_in")[:, :n_attn], 1, 2)
    grad_attn_w_o = _reduce_weight_grad(jnp.stack(gw_attn_o), "g_attn_o")
    grad_gm_w_in = jnp.swapaxes(_reduce_weight_grad(jnp.stack(gw_gm_in), "g_gm_in"), 1, 2)
    grad_gm_w_o = _reduce_weight_grad(jnp.stack(gw_gm_o), "g_gm_o")
    grad_ffn_w_in = jnp.swapaxes(
        _unpad_blocks(_reduce_weight_grad(jnp.stack(gw_ffn_in), "g_ffn_in"), 1, 2, Fq), 1, 2)
    grad_ffn_w_out = _reduce_weight_grad(jnp.stack(gw_ffn_out), "g_ffn_out")[:, :Fq]

    small = [jnp.stack(g_mix), jnp.stack(g_ffn_n), jnp.stack(g_bf), jnp.stack(g_vg), jnp.stack(g_ws),
             jnp.stack(g_bs), jnp.stack(g_cw), jnp.stack(g_cb), g_final[0]]
    (grad_mix_norm_g, grad_ffn_norm_g, grad_attn_b_f, g_vg_full, grad_gm_w_s, grad_gm_b_s, g_cw_full,
     grad_ffn_conv_b, grad_final_g) = _unpack(_all_reduce_small(_pack(small), "g_small"), small)
    grad_gm_v_g = lax.dynamic_slice_in_dim(g_vg_full, chip * (D // 4), D // 4, axis=1)
    grad_ffn_conv_w = lax.dynamic_slice_in_dim(g_cw_full, chip * 2 * Fq, 2 * Fq, axis=2)

    dmod_all = _all_gather(jnp.stack(dmod), MESH_AXES, 0, "dmod_ag")
    grad_mod_b = _sum_slots(dmod_all[None], F32, "dmod_sum")[0]
    dmod_cols = jnp.swapaxes(lax.dynamic_slice_in_dim(dmod_all, chip * nmod, nmod, axis=2), 0, 1)

    grad_mod_w, delta_mod_w, new_m_mod_w, new_v_mod_w = _modw_adamw(
        jnp.transpose(c_act), dmod_cols, mod_w, m_mod_w, v_mod_w, "mod_w_adamw")

    def big(w, g, m, v, name):
        shp = w.shape
        flat = lambda t: t.reshape(-1, shp[-1])
        return [t.reshape(shp) for t in _adamw(flat(w), flat(g), flat(m), flat(v), name)]

    upd_attn_in = big(attn_w_in, grad_attn_w_in, m_attn_w_in, v_attn_w_in, "adamw_attn_in")
    upd_attn_o = big(attn_w_o, grad_attn_w_o, m_attn_w_o, v_attn_w_o, "adamw_attn_o")
    upd_gm_in = big(gm_w_in, grad_gm_w_in, m_gm_w_in, v_gm_w_in, "adamw_gm_in")
    upd_gm_o = big(gm_w_o, grad_gm_w_o, m_gm_w_o, v_gm_w_o, "adamw_gm_o")
    upd_ffn_in = big(ffn_w_in, grad_ffn_w_in, m_ffn_w_in, v_ffn_w_in, "adamw_ffn_in")
    upd_ffn_out = big(ffn_w_out, grad_ffn_w_out, m_ffn_w_out, v_ffn_w_out, "adamw_ffn_out")

    sm_w = [mod_b, mix_norm_g, ffn_norm_g, attn_b_f, gm_v_g, gm_w_s, gm_b_s, ffn_conv_w, ffn_conv_b, final_g]
    sm_g = [grad_mod_b, grad_mix_norm_g, grad_ffn_norm_g, grad_attn_b_f, grad_gm_v_g, grad_gm_w_s, grad_gm_b_s,
            grad_ffn_conv_w, grad_ffn_conv_b, grad_final_g]
    sm_m = [m_mod_b, m_mix_norm_g, m_ffn_norm_g, m_attn_b_f, m_gm_v_g, m_gm_w_s, m_gm_b_s, m_ffn_conv_w,
            m_ffn_conv_b, m_final_g]
    sm_v = [v_mod_b, v_mix_norm_g, v_ffn_norm_g, v_attn_b_f, v_gm_v_g, v_gm_w_s, v_gm_b_s, v_ffn_conv_w,
            v_ffn_conv_b, v_final_g]
    sm_d, sm_nm, sm_nv = (_unpack(t, sm_w) for t in _adamw(_pack(sm_w), _pack(sm_g), _pack(sm_m), _pack(sm_v),
                                                             "adamw_small"))

    def ordered(mod_w_item, sm, k):
        return [mod_w_item, sm[0], sm[1], sm[2], upd_attn_in[k], sm[3], upd_attn_o[k], upd_gm_in[k], sm[4], sm[5],
                sm[6], upd_gm_o[k], upd_ffn_in[k], sm[7], sm[8], upd_ffn_out[k], sm[9]]

    grads = [grad_mod_w, grad_mod_b, grad_mix_norm_g, grad_ffn_norm_g, grad_attn_w_in, grad_attn_b_f, grad_attn_w_o,
             grad_gm_w_in, grad_gm_v_g, grad_gm_w_s, grad_gm_b_s, grad_gm_w_o, grad_ffn_w_in, grad_ffn_conv_w,
             grad_ffn_conv_b, grad_ffn_w_out, grad_final_g]
    return (loss, grad_x, *grads, *ordered(delta_mod_w, sm_d, 0), *ordered(new_m_mod_w, sm_nm, 1),
            *ordered(new_v_mod_w, sm_nv, 2))
```

```python
import functools
import math

import jax
import jax.numpy as jnp
from jax import lax
from jax.experimental import pallas as pl
from jax.experimental.pallas import tpu as pltpu

F32 = jnp.float32
BF16 = jnp.bfloat16

EPS = 1e-6
HEAD_DIM = 128
CHUNK = 128
GROUP = 128
LANES = 128
BF16_ROWS = 16
VMEM_LIMIT_BYTES = 52 * 1024 * 1024
NEG_BIG = -1e30
SMALL_PACK_COLS = 1024

ADAM_LR, ADAM_B1, ADAM_B2, ADAM_EPS, ADAM_WD, ADAM_STEP = 0.001, 0.9, 0.999, 1e-08, 0.01, 10

MESH_AXES = ("x", "y", "c")
MESH = pl.DeviceIdType.MESH


def _pick(n, pref, align):
    t = (min(pref, n) // align) * align
    while t >= align:
        if n % t == 0:
            return t
        t -= align
    return n


def _params(*sem):
    return pltpu.CompilerParams(dimension_semantics=sem, vmem_limit_bytes=VMEM_LIMIT_BYTES)


def _group(axes):
    pos = {ax: lax.axis_index(ax) for ax in MESH_AXES}
    rank = 0
    for ax in axes:
        rank = rank * 2 + pos[ax]
    peers = []
    for mask in range(1, 2 ** len(axes)):
        peer = dict(pos)
        for bit, ax in enumerate(reversed(axes)):
            if (mask >> bit) & 1:
                peer[ax] = 1 - pos[ax]
        peers.append((mask, tuple(peer[ax] for ax in MESH_AXES)))
    return rank, peers


def _slot(ref, gaxis, idx):
    return ref.at[(slice(None),) * gaxis + (idx,)]


D2D_STREAMS = 32
ICI_STREAMS = 8


def _pieces(shape, want):
    idx = [()]
    d = 0
    while d < len(shape) - 2 and len(idx) * shape[d] <= want:
        idx = [i + (k,) for i in idx for k in range(shape[d])]
        d += 1
    if d == len(shape) - 2:
        rows = shape[-2]
        r = max(1, want // len(idx))
        while r > 1 and not (rows % r == 0 and (rows // r) % BF16_ROWS == 0):
            r -= 1
        if r > 1:
            idx = [i + (pl.ds(k * (rows // r), rows // r),) for i in idx for k in range(r)]
    return idx


def _exchange(x, axes, gaxis, name, gather):
    n = 2 ** len(axes)
    if gather:
        out_shape = x.shape[:gaxis] + (n,) + x.shape[gaxis:]
        slab = x.shape
    else:
        assert x.shape[gaxis] == n
        out_shape = x.shape
        slab = x.shape[:gaxis] + x.shape[gaxis + 1:]
    pieces = _pieces(slab, D2D_STREAMS if axes == ("c",) else ICI_STREAMS)

    def body(x_ref, o_ref, send_sems, recv_sems, local_sem):
        rank, peers = _group(axes)
        src = (lambda member: x_ref) if gather else (lambda member: _slot(x_ref, gaxis, member))
        dst = lambda member: _slot(o_ref, gaxis, member)

        def remote(mask, peer, s, d):
            return pltpu.make_async_remote_copy(
                src_ref=s, dst_ref=d, send_sem=send_sems.at[mask - 1], recv_sem=recv_sems.at[mask - 1],
                device_id=peer, device_id_type=MESH)

        cut = lambda ref, pc: ref.at[pc] if pc else ref
        for pc in pieces:
            pltpu.make_async_copy(cut(src(rank), pc), cut(dst(rank), pc), local_sem).start()
            for mask, peer in peers:
                remote(mask, peer, cut(src(rank ^ mask), pc), cut(dst(rank), pc)).start()
        for mask, peer in peers:
            whole = remote(mask, peer, src(rank), dst(rank ^ mask))
            whole.wait_recv()
            whole.wait_send()
        pltpu.make_async_copy(src(rank), dst(rank), local_sem).wait()

    return pl.pallas_call(
        body, name=name,
        out_shape=jax.ShapeDtypeStruct(out_shape, x.dtype),
        in_specs=[pl.BlockSpec(memory_space=pl.ANY)],
        out_specs=pl.BlockSpec(memory_space=pl.ANY),
        scratch_shapes=[pltpu.SemaphoreType.DMA((n - 1,)), pltpu.SemaphoreType.DMA((n - 1,)),
                        pltpu.SemaphoreType.DMA(())],
    )(x)


def _all_gather(x, axes, gaxis, name):
    return _exchange(x, axes, gaxis, name, True)


def _all_to_all(x, axes, gaxis, name):
    return _exchange(x, axes, gaxis, name, False)


def _cast_half(w, core, name):
    L, n, D = w.shape
    n2 = n // 2
    tr = _pick(n2, 256, BF16_ROWS)
    nb = n2 // tr

    def body(core_ref, w_ref, o_ref):
        o_ref[...] = w_ref[...].astype(BF16)

    return pl.pallas_call(
        body, name=name,
        grid_spec=pltpu.PrefetchScalarGridSpec(
            num_scalar_prefetch=1, grid=(L, nb),
            in_specs=[pl.BlockSpec((None, tr, D), lambda l, i, core_ref: (l, core_ref[0] * nb + i, 0))],
            out_specs=pl.BlockSpec((None, tr, D), lambda l, i, core_ref: (l, i, 0))),
        out_shape=jax.ShapeDtypeStruct((L, n2, D), BF16),
        compiler_params=_params("parallel", "parallel"),
    )(core, w)


def _sum_slots(x, out_dtype, name):
    A, G, R, C = x.shape
    budget = (2 * 1024 * 1024) // (G * C * x.dtype.itemsize)
    tr = _pick(R, max(budget, BF16_ROWS), BF16_ROWS)

    def body(x_ref, o_ref):
        acc = x_ref[0].astype(F32)
        for g in range(1, G):
            acc = acc + x_ref[g].astype(F32)
        o_ref[...] = acc.astype(out_dtype)

    return pl.pallas_call(
        body, name=name, grid=(A, R // tr),
        in_specs=[pl.BlockSpec((None, G, tr, C), lambda a, i: (a, 0, i, 0))],
        out_specs=pl.BlockSpec((None, tr, C), lambda a, i: (a, i, 0)),
        out_shape=jax.ShapeDtypeStruct((A, R, C), out_dtype),
        compiler_params=_params("parallel", "parallel"),
    )(x)


def _adamw_math(w, g, m, v):
    m = ADAM_B1 * m + (1.0 - ADAM_B1) * g
    v = ADAM_B2 * v + (1.0 - ADAM_B2) * (g * g)
    m_hat = m / (1.0 - ADAM_B1 ** ADAM_STEP)
    v_hat = v / (1.0 - ADAM_B2 ** ADAM_STEP)
    delta = -ADAM_LR * (m_hat / (jnp.sqrt(v_hat) + ADAM_EPS) + ADAM_WD * w)
    return delta, m, v


def _adamw(w, g, m, v, name):
    R, C = w.shape
    budget = (768 * 1024) // (C * 4)
    tr = _pick(R, max(budget, 8), 8)

    def body(w_ref, g_ref, m_ref, v_ref, d_ref, nm_ref, nv_ref):
        d, nm, nv = _adamw_math(w_ref[...], g_ref[...], m_ref[...], v_ref[...])
        d_ref[...] = d
        nm_ref[...] = nm
        nv_ref[...] = nv

    spec = pl.BlockSpec((tr, C), lambda i: (i, 0))
    return pl.pallas_call(
        body, name=name, grid=(R // tr,),
        in_specs=[spec] * 4, out_specs=[spec] * 3,
        out_shape=[jax.ShapeDtypeStruct((R, C), F32)] * 3,
        compiler_params=_params("parallel"),
    )(w, g, m, v)


def _modw_adamw(ct, dm, w, m, v, name):
    L, D, N = w.shape
    E = ct.shape[1]
    tr = _pick(D, 64, 8)

    def body(ct_ref, dm_ref, w_ref, m_ref, v_ref, g_ref, d_ref, nm_ref, nv_ref):
        g = ct_ref[:, 0:1] * dm_ref[0:1, :]
        for e in range(1, E):
            g = g + ct_ref[:, e:e + 1] * dm_ref[e:e + 1, :]
        d, nm, nv = _adamw_math(w_ref[...], g, m_ref[...], v_ref[...])
        g_ref[...] = g
        d_ref[...] = d
        nm_ref[...] = nm
        nv_ref[...] = nv

    spec = pl.BlockSpec((None, tr, N), lambda l, i: (l, i, 0))
    return pl.pallas_call(
        body, name=name, grid=(L, D // tr),
        in_specs=[pl.BlockSpec((tr, E), lambda l, i: (i, 0)),
                  pl.BlockSpec((None, E, N), lambda l, i: (l, 0, 0)), spec, spec, spec],
        out_specs=[spec] * 4,
        out_shape=[jax.ShapeDtypeStruct((L, D, N), F32)] * 4,
        compiler_params=_params("parallel", "parallel"),
    )(ct, dm, w, m, v)


def _mod_fwd(c_all, w, b, name):
    L, D, N = w.shape
    E = c_all.shape[0]
    tn = _pick(N, 512, LANES)

    def body(c_ref, w_ref, b_ref, act_ref, o_ref):
        cv = c_ref[...]
        act = cv / (1.0 + jnp.exp(-cv))
        act_ref[...] = act
        o_ref[...] = jnp.dot(act.astype(BF16), w_ref[...].astype(BF16),
                             preferred_element_type=F32) + b_ref[...]

    return pl.pallas_call(
        body, name=name, grid=(L, N // tn),
        in_specs=[pl.BlockSpec((E, D), lambda l, j: (0, 0)),
                  pl.BlockSpec((None, D, tn), lambda l, j: (l, 0, j)),
                  pl.BlockSpec((None, 1, tn), lambda l, j: (l, 0, j))],
        out_specs=[pl.BlockSpec((E, D), lambda l, j: (0, 0)),
                   pl.BlockSpec((None, E, tn), lambda l, j: (l, 0, j))],
        out_shape=[jax.ShapeDtypeStruct((E, D), F32), jax.ShapeDtypeStruct((L, E, N), F32)],
        compiler_params=_params("arbitrary", "arbitrary"),
    )(c_all, w, b.reshape(L, 1, N))


_DIMS = {"nn": (((1,), (0,)), ((), ())), "nt": (((1,), (1,)), ((), ())), "tn": (((0,), (0,)), ((), ()))}


def _matmul(a, b, form, name, *, out_dtype=BF16, tm=512, tn=512, tk=None, a_resident=False,
            res=None, gate=None, emit_y=False):
    if form == "nn":
        (M, K), N = a.shape, b.shape[1]
    elif form == "nt":
        (M, K), N = a.shape, b.shape[0]
    else:
        (K, M), N = a.shape, b.shape[1]
    tm = _pick(M, tm, LANES if form == "tn" else BF16_ROWS)
    tn = _pick(N, tn, LANES)
    tk = K if (tk is None or form == "tn") else _pick(K, tk, LANES)
    nk = K // tk

    def ij(g0, g1):
        return (g0, g1) if a_resident else (g1, g0)

    grid = (M // tm, N // tn, nk) if a_resident else (N // tn, M // tm, nk)
    if form == "tn":
        a_spec = pl.BlockSpec((K, tm), lambda g0, g1, k: (0, ij(g0, g1)[0]))
        b_spec = pl.BlockSpec((K, tn), lambda g0, g1, k: (0, ij(g0, g1)[1]))
    else:
        a_spec = pl.BlockSpec((tm, tk), lambda g0, g1, k: (ij(g0, g1)[0], k))
        if form == "nn":
            b_spec = pl.BlockSpec((tk, tn), lambda g0, g1, k: (k, ij(g0, g1)[1]))
        else:
            b_spec = pl.BlockSpec((tn, tk), lambda g0, g1, k: (ij(g0, g1)[1], k))
    tile = pl.BlockSpec((tm, tn), lambda g0, g1, k: ij(g0, g1))
    row = pl.BlockSpec((1, tn), lambda g0, g1, k: (0, ij(g0, g1)[1]))
    in_specs, args = [a_spec, b_spec], [a, b]
    if res is not None:
        in_specs.append(tile)
        args.append(res)
    if gate is not None:
        in_specs.append(row)
        args.append(gate)
    if emit_y:
        out_specs = [tile, tile]
        out_shape = [jax.ShapeDtypeStruct((M, N), BF16), jax.ShapeDtypeStruct((M, N), out_dtype)]
    else:
        out_specs = tile
        out_shape = jax.ShapeDtypeStruct((M, N), out_dtype)
    n_in = len(args)

    def body(*refs):
        a_ref, b_ref = refs[0], refs[1]
        res_ref = refs[2] if res is not None else None
        gate_ref = refs[n_in - 1] if gate is not None else None
        outs = refs[n_in:n_in + (2 if emit_y else 1)]

        def finish(acc):
            val = acc
            if gate_ref is not None:
                val = gate_ref[...] * val
            if res_ref is not None:
                val = res_ref[...] + val
            if emit_y:
                outs[0][...] = acc.astype(BF16)
            outs[-1][...] = val.astype(out_dtype)

        part = lax.dot_general(a_ref[...], b_ref[...], _DIMS[form], preferred_element_type=F32)
        if nk == 1:
            finish(part)
        else:
            acc_ref = refs[-1]
            k = pl.program_id(2)

            @pl.when(k == 0)
            def _():
                acc_ref[...] = part

            @pl.when(k > 0)
            def _():
                acc_ref[...] += part

            @pl.when(k == nk - 1)
            def _():
                finish(acc_ref[...])

    return pl.pallas_call(
        body, name=name, grid=grid, in_specs=in_specs, out_specs=out_specs, out_shape=out_shape,
        scratch_shapes=[pltpu.VMEM((tm, tn), F32)] if nk > 1 else [],
        compiler_params=_params("parallel", "parallel", "arbitrary"),
    )(*args)


def _norm_mod(x, g, sc, sh, name):
    T, D = x.shape
    tm = _pick(T, 256, BF16_ROWS)

    def body(x_ref, g_ref, sc_ref, sh_ref, h_ref):
        xv = x_ref[...]
        rstd = lax.rsqrt(jnp.mean(xv * xv, axis=-1, keepdims=True) + EPS)
        y = xv * rstd * g_ref[...]
        h_ref[...] = (y * (1.0 + sc_ref[...]) + sh_ref[...]).astype(BF16)

    tile = pl.BlockSpec((tm, D), lambda i: (i, 0))
    row = pl.BlockSpec((1, D), lambda i: (0, 0))
    return pl.pallas_call(
        body, name=name, grid=(T // tm,), in_specs=[tile, row, row, row], out_specs=tile,
        out_shape=jax.ShapeDtypeStruct((T, D), BF16), compiler_params=_params("parallel"),
    )(x, g, sc, sh)


def _colsum(v):
    return jnp.sum(v, axis=0, keepdims=True)


def _norm_bwd(x, g, sc, sh, dh, dres, prev, name):
    T, D = x.shape
    tm = _pick(T, 256, BF16_ROWS)
    has_prev = prev is not None

    def body(*refs):
        x_ref, g_ref, sc_ref, sh_ref, dh_ref, dres_ref = refs[:6]
        k = 6
        if has_prev:
            y_ref, gate_ref = refs[6:8]
            k = 8
        dx_ref, dsc_ref, dsh_ref, dg_ref = refs[k:k + 4]
        first = pl.program_id(0) == 0
        xv = x_ref[...]
        rstd = lax.rsqrt(jnp.mean(xv * xv, axis=-1, keepdims=True) + EPS)
        xhat = xv * rstd
        gv = g_ref[...]
        dhv = dh_ref[...].astype(F32)
        dyv = dhv * (1.0 + sc_ref[...])
        dxhat = dyv * gv
        dx = rstd * (dxhat - xhat * jnp.mean(dxhat * xhat, axis=-1, keepdims=True))
        dxt = dres_ref[...] + dx
        dx_ref[...] = dxt
        sums = [(dsc_ref, _colsum(dhv * (xhat * gv))), (dsh_ref, _colsum(dhv)), (dg_ref, _colsum(dyv * xhat))]
        if has_prev:
            dy_ref, dgate_ref = refs[k + 4:k + 6]
            dy_ref[...] = (dxt * gate_ref[...]).astype(BF16)
            sums.append((dgate_ref, _colsum(dxt * y_ref[...].astype(F32))))
        for ref, val in sums:
            @pl.when(first)
            def _(ref=ref, val=val):
                ref[...] = val

            @pl.when(jnp.logical_not(first))
            def _(ref=ref, val=val):
                ref[...] += val

    tile = pl.BlockSpec((tm, D), lambda i: (i, 0))
    row = pl.BlockSpec((1, D), lambda i: (0, 0))
    rowshape = jax.ShapeDtypeStruct((1, D), F32)
    in_specs = [tile, row, row, row, tile, tile]
    args = [x, g, sc, sh, dh, dres]
    out_specs = [tile, row, row, row]
    out_shape = [jax.ShapeDtypeStruct((T, D), F32), rowshape, rowshape, rowshape]
    if has_prev:
        in_specs += [tile, row]
        args += list(prev)
        out_specs += [tile, row]
        out_shape += [jax.ShapeDtypeStruct((T, D), BF16), rowshape]
    return pl.pallas_call(
        body, name=name, grid=(T // tm,), in_specs=in_specs, out_specs=out_specs, out_shape=out_shape,
        compiler_params=_params("arbitrary"),
    )(*args)


def _final_loss_bwd(x, g, target, y_prev, gate_prev, name):
    T, D = x.shape
    tm = _pick(T, 256, BF16_ROWS)

    def body(x_ref, g_ref, t_ref, y_ref, gate_ref, loss_ref, dx_ref, dg_ref, dy_ref, dgate_ref):
        first = pl.program_id(0) == 0
        xv = x_ref[...]
        rstd = lax.rsqrt(jnp.mean(xv * xv, axis=-1, keepdims=True) + EPS)
        xhat = xv * rstd
        gv = g_ref[...]
        err = xhat * gv - t_ref[...]
        part = 0.5 * jnp.sum(jnp.mean(err * err, axis=-1, keepdims=True))
        dyf = err * (1.0 / D)
        dxhat = dyf * gv
        dx = rstd * (dxhat - xhat * jnp.mean(dxhat * xhat, axis=-1, keepdims=True))
        dx_ref[...] = dx
        dy_ref[...] = (dx * gate_ref[...]).astype(BF16)
        sums = [(dg_ref, _colsum(dyf * xhat)), (dgate_ref, _colsum(dx * y_ref[...].astype(F32))),
                (loss_ref, jnp.full((8, LANES), part, F32))]
        for ref, val in sums:
            @pl.when(first)
            def _(ref=ref, val=val):
                ref[...] = val

            @pl.when(jnp.logical_not(first))
            def _(ref=ref, val=val):
                ref[...] += val

    tile = pl.BlockSpec((tm, D), lambda i: (i, 0))
    row = pl.BlockSpec((1, D), lambda i: (0, 0))
    rowshape = jax.ShapeDtypeStruct((1, D), F32)
    return pl.pallas_call(
        body, name=name, grid=(T // tm,),
        in_specs=[tile, row, tile, tile, row],
        out_specs=[pl.BlockSpec((8, LANES), lambda i: (0, 0)), tile, row, tile, row],
        out_shape=[jax.ShapeDtypeStruct((8, LANES), F32), jax.ShapeDtypeStruct((T, D), F32), rowshape,
                   jax.ShapeDtypeStruct((T, D), BF16), rowshape],
        compiler_params=_params("arbitrary"),
    )(x, g, target, y_prev, gate_prev)


def _split3(v):
    hi = v.astype(BF16)
    r1 = v - hi.astype(F32)
    mid = r1.astype(BF16)
    lo = (r1 - mid.astype(F32)).astype(BF16)
    return hi, mid, lo


def _tri_dot(tri, v):
    hi, mid, lo = _split3(v)
    dot = functools.partial(jnp.dot, preferred_element_type=F32)
    return dot(tri, hi) + dot(tri, mid) + dot(tri, lo)


def _fox_gate(flog, bias, name):
    T = flog.shape[0]
    tb = _pick(T, 256, BF16_ROWS)

    def body(f_ref, b_ref, o_ref, carry_ref):
        @pl.when(pl.program_id(0) == 0)
        def _():
            carry_ref[...] = jnp.zeros_like(carry_ref)

        xv = f_ref[...] + b_ref[...]
        lf = jnp.minimum(xv, 0.0) - jnp.log(1.0 + jnp.exp(-jnp.abs(xv)))
        r = lax.broadcasted_iota(jnp.int32, (tb, tb), 0)
        cidx = lax.broadcasted_iota(jnp.int32, (tb, tb), 1)
        tri = jnp.where(r >= cidx, 1.0, 0.0).astype(BF16)
        out = _tri_dot(tri, lf) + carry_ref[...]
        o_ref[...] = out
        carry_ref[...] = out[tb - 1:tb, :]

    return pl.pallas_call(
        body, name=name, grid=(T // tb,),
        in_specs=[pl.BlockSpec((tb, LANES), lambda i: (i, 0)), pl.BlockSpec((1, LANES), lambda i: (0, 0))],
        out_specs=pl.BlockSpec((tb, LANES), lambda i: (i, 0)),
        out_shape=jax.ShapeDtypeStruct((T, LANES), F32),
        scratch_shapes=[pltpu.VMEM((1, LANES), F32)],
        compiler_params=_params("arbitrary"),
    )(flog, bias)


def _fox_gate_bwd(dfq, dfk, flog, bias, name):
    T = flog.shape[0]
    tb = _pick(T, 256, BF16_ROWS)
    nb = T // tb

    def body(dq_ref, dk_ref, f_ref, b_ref, o_ref, db_ref, carry_ref):
        first = pl.program_id(0) == 0

        @pl.when(first)
        def _():
            carry_ref[...] = jnp.zeros_like(carry_ref)

        dv = dq_ref[...] + dk_ref[...]
        r = lax.broadcasted_iota(jnp.int32, (tb, tb), 0)
        cidx = lax.broadcasted_iota(jnp.int32, (tb, tb), 1)
        tri = jnp.where(cidx >= r, 1.0, 0.0).astype(BF16)
        dl = _tri_dot(tri, dv) + carry_ref[...]
        carry_ref[...] = dl[0:1, :]
        xv = f_ref[...] + b_ref[...]
        dfl = dl / (1.0 + jnp.exp(xv))
        o_ref[...] = dfl.astype(BF16)
        val = _colsum(dfl)

        @pl.when(first)
        def _():
            db_ref[...] = val

        @pl.when(jnp.logical_not(first))
        def _():
            db_ref[...] += val

    rev = pl.BlockSpec((tb, LANES), lambda i: (nb - 1 - i, 0))
    row = pl.BlockSpec((1, LANES), lambda i: (0, 0))
    return pl.pallas_call(
        body, name=name, grid=(nb,), in_specs=[rev, rev, rev, row], out_specs=[rev, row],
        out_shape=[jax.ShapeDtypeStruct((T, LANES), BF16), jax.ShapeDtypeStruct((1, LANES), F32)],
        scratch_shapes=[pltpu.VMEM((1, LANES), F32)],
        compiler_params=_params("arbitrary"),
    )(dfq, dfk, flog, bias)


def _head_col(block, h):
    lane = lax.broadcasted_iota(jnp.int32, block.shape, 1)
    return jnp.sum(jnp.where(lane == h, block, 0.0), axis=1, keepdims=True)


_NT = _DIMS["nt"]
_TN = _DIMS["tn"]


def _attn_fwd(qkv, fcol, frow, n_heads, name):
    T = qkv.shape[0]
    H = n_heads
    tq = _pick(T, 512, LANES)
    tk = tq
    scale = HEAD_DIM ** -0.5

    def body(q_ref, k_ref, v_ref, fc_ref, fr_ref, o_ref, lse_ref):
        h, i = pl.program_id(0), pl.program_id(1)
        q = q_ref[...]
        fi = _head_col(fc_ref[...], h)
        rows = i * tq + lax.broadcasted_iota(jnp.int32, (tq, tk), 0)
        cols = lax.broadcasted_iota(jnp.int32, (tq, tk), 1)

        def step(j, carry):
            m, l, acc = carry
            ks = pl.multiple_of(j * tk, tk)
            kb = k_ref[pl.ds(ks, tk), :]
            vb = v_ref[pl.ds(ks, tk), :]
            s = lax.dot_general(q, kb, _NT, preferred_element_type=F32) * scale
            s = s + fi - fr_ref[:, pl.ds(ks, tk)]
            s = jnp.where(cols + ks <= rows, s, NEG_BIG)
            m_new = jnp.maximum(m, jnp.max(s, axis=1, keepdims=True))
            alpha = jnp.exp(m - m_new)
            p = jnp.exp(s - m_new)
            l = alpha * l + jnp.sum(p, axis=1, keepdims=True)
            acc = alpha * acc + jnp.dot(p.astype(BF16), vb, preferred_element_type=F32)
            return m_new, l, acc

        init = (jnp.full((tq, 1), NEG_BIG, F32), jnp.zeros((tq, 1), F32), jnp.zeros((tq, HEAD_DIM), F32))
        m, l, acc = lax.fori_loop(0, i + 1, step, init)
        o_ref[...] = (acc / l).astype(BF16)
        lse_ref[...] = jnp.broadcast_to(m + jnp.log(l), (tq, LANES))

    return pl.pallas_call(
        body, name=name, grid=(H, T // tq),
        in_specs=[pl.BlockSpec((tq, HEAD_DIM), lambda h, i: (i, h)),
                  pl.BlockSpec((T, HEAD_DIM), lambda h, i: (0, H + h)),
                  pl.BlockSpec((T, HEAD_DIM), lambda h, i: (0, 2 * H + h)),
                  pl.BlockSpec((tq, LANES), lambda h, i: (i, 0)),
                  pl.BlockSpec((None, 1, T), lambda h, i: (h, 0, 0))],
        out_specs=[pl.BlockSpec((tq, HEAD_DIM), lambda h, i: (i, h)),
                   pl.BlockSpec((None, tq, LANES), lambda h, i: (h, i, 0))],
        out_shape=[jax.ShapeDtypeStruct((T, H * HEAD_DIM), BF16), jax.ShapeDtypeStruct((H, T, LANES), F32)],
        compiler_params=_params("parallel", "arbitrary"),
    )(qkv, qkv, qkv, fcol, frow)


def _attn_dq(qkv, o, do, lse, fcol, frow, n_heads, name):
    T = qkv.shape[0]
    H = n_heads
    tq = _pick(T, 512, LANES)
    tk = tq
    scale = HEAD_DIM ** -0.5

    def body(q_ref, k_ref, v_ref, o_ref, do_ref, lse_ref, fc_ref, fr_ref, dq_ref, df_ref):
        h, i = pl.program_id(0), pl.program_id(1)
        q = q_ref[...]
        dob = do_ref[...]
        delta = jnp.sum(dob.astype(F32) * o_ref[...].astype(F32), axis=1, keepdims=True)
        lse_i = lse_ref[:, 0:1]
        fi = _head_col(fc_ref[...], h)
        rows = i * tq + lax.broadcasted_iota(jnp.int32, (tq, tk), 0)
        cols = lax.broadcasted_iota(jnp.int32, (tq, tk), 1)

        def step(j, carry):
            dq, rsum = carry
            ks = pl.multiple_of(j * tk, tk)
            kb = k_ref[pl.ds(ks, tk), :]
            vb = v_ref[pl.ds(ks, tk), :]
            s = lax.dot_general(q, kb, _NT, preferred_element_type=F32) * scale
            s = s + fi - fr_ref[:, pl.ds(ks, tk)]
            p = jnp.where(cols + ks <= rows, jnp.exp(s - lse_i), 0.0)
            dp = lax.dot_general(dob, vb, _NT, preferred_element_type=F32)
            ds = p * (dp - delta)
            return (dq + jnp.dot(ds.astype(BF16), kb, preferred_element_type=F32),
                    rsum + jnp.sum(ds, axis=1, keepdims=True))

        dq, rsum = lax.fori_loop(0, i + 1, step, (jnp.zeros((tq, HEAD_DIM), F32), jnp.zeros((tq, 1), F32)))
        dq_ref[...] = (dq * scale).astype(BF16)
        df_ref[...] = jnp.transpose(jnp.broadcast_to(rsum, (tq, LANES)))[0:1, :]

    qtile = lambda off: pl.BlockSpec((tq, HEAD_DIM), lambda h, i: (i, off + h))
    return pl.pallas_call(
        body, name=name, grid=(H, T // tq),
        in_specs=[qtile(0),
                  pl.BlockSpec((T, HEAD_DIM), lambda h, i: (0, H + h)),
                  pl.BlockSpec((T, HEAD_DIM), lambda h, i: (0, 2 * H + h)),
                  qtile(0), qtile(0),
                  pl.BlockSpec((None, tq, LANES), lambda h, i: (h, i, 0)),
                  pl.BlockSpec((tq, LANES), lambda h, i: (i, 0)),
                  pl.BlockSpec((None, 1, T), lambda h, i: (h, 0, 0))],
        out_specs=[qtile(0), pl.BlockSpec((None, 1, tq), lambda h, i: (h, 0, i))],
        out_shape=[jax.ShapeDtypeStruct((T, H * HEAD_DIM), BF16), jax.ShapeDtypeStruct((H, 1, T), F32)],
        compiler_params=_params("parallel", "arbitrary"),
    )(qkv, qkv, qkv, o, do, lse, fcol, frow)


def _attn_dkv(qkv, o, do, lse, fcol, frow, n_heads, name):
    T = qkv.shape[0]
    H = n_heads
    tk = _pick(T, 512, LANES)
    tq = tk
    nq = T // tq
    scale = HEAD_DIM ** -0.5

    def body(k_ref, v_ref, q_ref, o_ref, do_ref, lse_ref, fc_ref, fr_ref, dk_ref, dv_ref, df_ref):
        h, j = pl.program_id(0), pl.program_id(1)
        kb = k_ref[...]
        vb = v_ref[...]
        fj = fr_ref[...]
        cols = j * tk + lax.broadcasted_iota(jnp.int32, (tq, tk), 1)
        rows = lax.broadcasted_iota(jnp.int32, (tq, tk), 0)

        def step(i, carry):
            dk, dv, df = carry
            qs = pl.multiple_of(i * tq, tq)
            qb = q_ref[pl.ds(qs, tq), :]
            dob = do_ref[pl.ds(qs, tq), :]
            delta = jnp.sum(dob.astype(F32) * o_ref[pl.ds(qs, tq), :].astype(F32), axis=1, keepdims=True)
            lse_i = lse_ref[pl.ds(qs, tq), 0:1]
            fi = _head_col(fc_ref[pl.ds(qs, tq), :], h)
            s = lax.dot_general(qb, kb, _NT, preferred_element_type=F32) * scale
            s = s + fi - fj
            p = jnp.where(cols <= rows + qs, jnp.exp(s - lse_i), 0.0)
            dv = dv + lax.dot_general(p.astype(BF16), dob, _TN, preferred_element_type=F32)
            dp = lax.dot_general(dob, vb, _NT, preferred_element_type=F32)
            ds = p * (dp - delta)
            dk = dk + lax.dot_general(ds.astype(BF16), qb, _TN, preferred_element_type=F32)
            df = df - _colsum(ds)
            return dk, dv, df

        init = (jnp.zeros((tk, HEAD_DIM), F32), jnp.zeros((tk, HEAD_DIM), F32), jnp.zeros((1, tk), F32))
        dk, dv, df = lax.fori_loop(j, nq, step, init)
        dk_ref[...] = (dk * scale).astype(BF16)
        dv_ref[...] = dv.astype(BF16)
        df_ref[...] = df

    full = lambda off: pl.BlockSpec((T, HEAD_DIM), lambda h, j: (0, off + h))
    ktile = lambda off: pl.BlockSpec((tk, HEAD_DIM), lambda h, j: (j, off + h))
    return pl.pallas_call(
        body, name=name, grid=(H, T // tk),
        in_specs=[ktile(H), ktile(2 * H), full(0), full(0), full(0),
                  pl.BlockSpec((None, T, LANES), lambda h, j: (h, 0, 0)),
                  pl.BlockSpec((T, LANES), lambda h, j: (0, 0)),
                  pl.BlockSpec((None, 1, tk), lambda h, j: (h, 0, j))],
        out_specs=[ktile(0), ktile(0), pl.BlockSpec((None, 1, tk), lambda h, j: (h, 0, j))],
        out_shape=[jax.ShapeDtypeStruct((T, H * HEAD_DIM), BF16)] * 2 + [jax.ShapeDtypeStruct((H, 1, T), F32)],
        compiler_params=_params("parallel", "arbitrary"),
    )(qkv, qkv, qkv, o, do, lse, fcol, frow)


_GELU_C = math.sqrt(2.0 / math.pi)
_GELU_A = 0.044715


def _gelu(p):
    t = jnp.tanh(_GELU_C * (p + _GELU_A * p * p * p))
    return 0.5 * p * (1.0 + t), t


def _gelu_grad(p, t):
    return 0.5 * (1.0 + t) + 0.5 * p * (1.0 - t * t) * _GELU_C * (1.0 + 3.0 * _GELU_A * p * p)


def _tril(shape, upper=False):
    r = lax.broadcasted_iota(jnp.int32, shape, 0)
    c = lax.broadcasted_iota(jnp.int32, shape, 1)
    return (c >= r) if upper else (r >= c)


def _gm_fwd(p, vg, ws, bt, name):
    T, D2 = p.shape
    D = D2 // 2
    NG = D // GROUP
    tm = _pick(T, 2 * CHUNK, CHUNK)

    def body(p_ref, vg_ref, ws_ref, bt_ref, o_ref):
        z, _ = _gelu(p_ref[...].astype(F32))
        u, v = z[:, :D], z[:, D:]
        rstd = lax.rsqrt(jnp.mean(v * v, axis=-1, keepdims=True) + EPS)
        vn = (v * rstd * vg_ref[...]).astype(BF16)
        low = _tril((CHUNK, CHUNK))
        for g in range(NG):
            wm = jnp.where(low, ws_ref[g], 0.0).astype(BF16)
            bcol = bt_ref[:, g:g + 1]
            cs = slice(g * GROUP, (g + 1) * GROUP)
            for r in range(tm // CHUNK):
                rs = slice(r * CHUNK, (r + 1) * CHUNK)
                sv = jnp.dot(wm, vn[rs, cs], preferred_element_type=F32) + bcol
                o_ref[rs, cs] = (u[rs, cs] * sv).astype(BF16)

    return pl.pallas_call(
        body, name=name, grid=(T // tm,),
        in_specs=[pl.BlockSpec((tm, D2), lambda i: (i, 0)), pl.BlockSpec((1, D), lambda i: (0, 0)),
                  pl.BlockSpec((NG, CHUNK, CHUNK), lambda i: (0, 0, 0)),
                  pl.BlockSpec((CHUNK, LANES), lambda i: (0, 0))],
        out_specs=pl.BlockSpec((tm, D), lambda i: (i, 0)),
        out_shape=jax.ShapeDtypeStruct((T, D), BF16),
        compiler_params=_params("parallel"),
    )(p, vg, ws, bt)


def _gm_bwd(p, dgated, vg, ws, wst, bt, name):
    T, D2 = p.shape
    D = D2 // 2
    NG = D // GROUP
    tm = _pick(T, 2 * CHUNK, CHUNK)
    nb = T // tm

    def body(p_ref, dg_ref, vg_ref, ws_ref, wst_ref, bt_ref, dp_ref, dws_ref, dbt_ref, dvg_ref, du_s, dvn_s):
        step = pl.program_id(0)

        @pl.when(step == 0)
        def _():
            dws_ref[...] = jnp.zeros_like(dws_ref)
            dbt_ref[...] = jnp.zeros_like(dbt_ref)
            dvg_ref[...] = jnp.zeros_like(dvg_ref)

        pv = p_ref[...].astype(F32)
        z, t = _gelu(pv)
        u, v = z[:, :D], z[:, D:]
        rstd = lax.rsqrt(jnp.mean(v * v, axis=-1, keepdims=True) + EPS)
        vhat = v * rstd
        vgv = vg_ref[...]
        vn = (vhat * vgv).astype(BF16)
        dgt = dg_ref[...].astype(F32)
        low = _tril((CHUNK, CHUNK))
        up = _tril((CHUNK, CHUNK), upper=True)
        lane = lax.broadcasted_iota(jnp.int32, (CHUNK, LANES), 1)
        dbt = jnp.zeros((CHUNK, LANES), F32)
        for g in range(NG):
            wm = jnp.where(low, ws_ref[g], 0.0).astype(BF16)
            wmt = jnp.where(up, wst_ref[g], 0.0).astype(BF16)
            bcol = bt_ref[:, g:g + 1]
            cs = slice(g * GROUP, (g + 1) * GROUP)
            dw = jnp.zeros((CHUNK, CHUNK), F32)
            for r in range(tm // CHUNK):
                rs = slice(r * CHUNK, (r + 1) * CHUNK)
                blk = vn[rs, cs]
                sv = jnp.dot(wm, blk, preferred_element_type=F32) + bcol
                dgb = dgt[rs, cs]
                du_s[rs, cs] = dgb * sv
                dsv = dgb * u[rs, cs]
                dsvb = dsv.astype(BF16)
                dvn_s[rs, cs] = jnp.dot(wmt, dsvb, preferred_element_type=F32)
                dw = dw + lax.dot_general(dsvb, blk, _NT, preferred_element_type=F32)
                dbt = dbt + jnp.where(lane == g, jnp.sum(dsv, axis=1, keepdims=True), 0.0)
            dws_ref[g] += dw
        dbt_ref[...] += dbt
        dvn = dvn_s[...]
        dvg_ref[...] += _colsum(dvn * vhat)
        dvhat = dvn * vgv
        dv = rstd * (dvhat - vhat * jnp.mean(dvhat * vhat, axis=-1, keepdims=True))
        gg = _gelu_grad(pv, t)
        dp_ref[:, :D] = (du_s[...] * gg[:, :D]).astype(BF16)
        dp_ref[:, D:] = (dv * gg[:, D:]).astype(BF16)

        @pl.when(step == nb - 1)
        def _():
            for g in range(NG):
                dws_ref[g] = jnp.where(low, dws_ref[g], 0.0)

    const3 = pl.BlockSpec((NG, CHUNK, CHUNK), lambda i: (0, 0, 0))
    return pl.pallas_call(
        body, name=name, grid=(nb,),
        in_specs=[pl.BlockSpec((tm, D2), lambda i: (i, 0)), pl.BlockSpec((tm, D), lambda i: (i, 0)),
                  pl.BlockSpec((1, D), lambda i: (0, 0)), const3, const3,
                  pl.BlockSpec((CHUNK, LANES), lambda i: (0, 0))],
        out_specs=[pl.BlockSpec((tm, D2), lambda i: (i, 0)), const3,
                   pl.BlockSpec((CHUNK, LANES), lambda i: (0, 0)), pl.BlockSpec((1, D), lambda i: (0, 0))],
        out_shape=[jax.ShapeDtypeStruct((T, D2), BF16), jax.ShapeDtypeStruct((NG, CHUNK, CHUNK), F32),
                   jax.ShapeDtypeStruct((CHUNK, LANES), F32), jax.ShapeDtypeStruct((1, D), F32)],
        scratch_shapes=[pltpu.VMEM((tm, D), F32), pltpu.VMEM((tm, D), F32)],
        compiler_params=_params("arbitrary"),
    )(p, dgated, vg, ws, wst, bt)


HALO = BF16_ROWS


def _ffn_act(a, cw, cb, name):
    T, F2 = a.shape
    Fh = F2 // 2
    tm = _pick(T, 256, HALO)
    per = tm // HALO
    nch = Fh // LANES

    def body(a_ref, h_ref, cw_ref, cb_ref, o_ref):
        i = pl.program_id(0)
        row = lax.broadcasted_iota(jnp.int32, (tm, LANES), 0)
        keep = jnp.where(i > 0, 1.0, 0.0)

        def conv(off):
            af = a_ref[:, pl.ds(off, LANES)].astype(F32)
            hf = h_ref[:, pl.ds(off, LANES)].astype(F32) * keep
            r1 = jnp.where(row == 0, hf[HALO - 1:HALO], pltpu.roll(af, 1, 0))
            r2 = jnp.where(row == 0, hf[HALO - 2:HALO - 1], jnp.where(row == 1, hf[HALO - 1:HALO], pltpu.roll(af, 2, 0)))
            w = cw_ref[:, pl.ds(off, LANES)]
            return w[0:1] * r2 + w[1:2] * r1 + w[2:3] * af + cb_ref[:, pl.ds(off, LANES)]

        def chunk(c, carry):
            off = pl.multiple_of(c * LANES, LANES)
            gt = conv(off)
            upv = conv(pl.multiple_of(off + Fh, LANES))
            o_ref[:, pl.ds(off, LANES)] = (gt / (1.0 + jnp.exp(-gt)) * upv).astype(BF16)
            return carry

        lax.fori_loop(0, nch, chunk, 0)

    return pl.pallas_call(
        body, name=name, grid=(T // tm,),
        in_specs=[pl.BlockSpec((tm, F2), lambda i: (i, 0)),
                  pl.BlockSpec((HALO, F2), lambda i: (jnp.maximum(i * per - 1, 0), 0)),
                  pl.BlockSpec((3, F2), lambda i: (0, 0)), pl.BlockSpec((1, F2), lambda i: (0, 0))],
        out_specs=pl.BlockSpec((tm, Fh), lambda i: (i, 0)),
        out_shape=jax.ShapeDtypeStruct((T, Fh), BF16),
        compiler_params=_params("parallel"),
    )(a, a, cw, cb)


def _ffn_act_bwd(a, dhm, cw, cb, name):
    T, F2 = a.shape
    Fh = F2 // 2
    tm = _pick(T, 256, HALO)
    per = tm // HALO
    nb = T // tm
    last_halo = T // HALO - 1
    nch = Fh // LANES
    ext = tm + 2 * HALO

    def body(a_ref, hp_ref, hn_ref, d_ref, dn_ref, cw_ref, cb_ref, da_ref, dcw_ref, dcb_ref):
        i = pl.program_id(0)

        @pl.when(i == 0)
        def _():
            dcw_ref[...] = jnp.zeros_like(dcw_ref)
            dcb_ref[...] = jnp.zeros_like(dcb_ref)

        keep_p = jnp.where(i > 0, 1.0, 0.0)
        keep_n = jnp.where(i < nb - 1, 1.0, 0.0)
        row = lax.broadcasted_iota(jnp.int32, (ext, LANES), 0)
        main = jnp.logical_and(row >= HALO, row < HALO + tm)

        def conv(off):
            e = jnp.concatenate([hp_ref[:, pl.ds(off, LANES)].astype(F32) * keep_p,
                                 a_ref[:, pl.ds(off, LANES)].astype(F32),
                                 hn_ref[:, pl.ds(off, LANES)].astype(F32) * keep_n], axis=0)
            r1 = pltpu.roll(e, 1, 0)
            r2 = pltpu.roll(e, 2, 0)
            w = cw_ref[:, pl.ds(off, LANES)]
            return w, e, r1, r2, w[0:1] * r2 + w[1:2] * r1 + w[2:3] * e + cb_ref[:, pl.ds(off, LANES)]

        def back(off, w, e, r1, r2, dc):
            dcm = jnp.where(main, dc, 0.0)
            for tap, shifted in enumerate((r2, r1, e)):
                dcw_ref[tap:tap + 1, pl.ds(off, LANES)] += _colsum(dcm * shifted)
            dcb_ref[:, pl.ds(off, LANES)] += _colsum(dcm)
            da = w[2:3] * dc + w[1:2] * pltpu.roll(dc, ext - 1, 0) + w[0:1] * pltpu.roll(dc, ext - 2, 0)
            da_ref[:, pl.ds(off, LANES)] = da[HALO:HALO + tm].astype(BF16)

        def chunk(c, carry):
            off = pl.multiple_of(c * LANES, LANES)
            off_u = pl.multiple_of(off + Fh, LANES)
            dh = jnp.concatenate([jnp.zeros((HALO, LANES), F32), d_ref[:, pl.ds(off, LANES)].astype(F32),
                                  dn_ref[:, pl.ds(off, LANES)].astype(F32) * keep_n], axis=0)
            wg, eg, r1g, r2g, gt = conv(off)
            wu, eu, r1u, r2u, upv = conv(off_u)
            sg = 1.0 / (1.0 + jnp.exp(-gt))
            back(off, wg, eg, r1g, r2g, dh * upv * (sg * (1.0 + gt * (1.0 - sg))))
            back(off_u, wu, eu, r1u, r2u, dh * (gt * sg))
            return carry

        lax.fori_loop(0, nch, chunk, 0)

    prev = lambda i: (jnp.maximum(i * per - 1, 0), 0)
    nxt = lambda i: (jnp.minimum((i + 1) * per, last_halo), 0)
    return pl.pallas_call(
        body, name=name, grid=(nb,),
        in_specs=[pl.BlockSpec((tm, F2), lambda i: (i, 0)), pl.BlockSpec((HALO, F2), prev),
                  pl.BlockSpec((HALO, F2), nxt), pl.BlockSpec((tm, Fh), lambda i: (i, 0)),
                  pl.BlockSpec((HALO, Fh), nxt),
                  pl.BlockSpec((3, F2), lambda i: (0, 0)), pl.BlockSpec((1, F2), lambda i: (0, 0))],
        out_specs=[pl.BlockSpec((tm, F2), lambda i: (i, 0)), pl.BlockSpec((3, F2), lambda i: (0, 0)),
                   pl.BlockSpec((1, F2), lambda i: (0, 0))],
        out_shape=[jax.ShapeDtypeStruct((T, F2), BF16), jax.ShapeDtypeStruct((3, F2), F32),
                   jax.ShapeDtypeStruct((1, F2), F32)],
        compiler_params=_params("arbitrary"),
    )(a, a, a, dhm, dhm, cw, cb)


def _round_up(n, k):
    return -(-n // k) * k


def _pad_axis(a, axis, size):
    pad = [(0, 0)] * a.ndim
    pad[axis] = (0, size - a.shape[axis])
    return jnp.pad(a, pad)


def _pad_blocks(a, axis, nblk, to):
    shp = a.shape
    blk = shp[axis] // nblk
    a = a.reshape(shp[:axis] + (nblk, blk) + shp[axis + 1:])
    a = _pad_axis(a, axis + 1, to)
    return a.reshape(shp[:axis] + (nblk * to,) + shp[axis + 1:])


def _unpad_blocks(a, axis, nblk, blk):
    shp = a.shape
    to = shp[axis] // nblk
    a = a.reshape(shp[:axis] + (nblk, to) + shp[axis + 1:])
    a = lax.slice_in_dim(a, 0, blk, axis=axis + 1)
    return a.reshape(shp[:axis] + (nblk * blk,) + shp[axis + 1:])


SIB_TILE_BYTES = 2 * 1024 * 1024


def _sib_stream(x, core, name, gather):
    if gather:
        B, R, D = x.shape
        out_shape = (B, 2, R, D)
    else:
        B, _, R, D = x.shape
        out_shape = (B, R, D)
    tr = _pick(R, max(BF16_ROWS, SIB_TILE_BYTES // (D * x.dtype.itemsize)), BF16_ROWS)
    nr = R // tr
    if gather:
        in_spec = pl.BlockSpec((tr, D), lambda b, i, core_ref: (b * nr + i, 0))
    else:
        in_spec = pl.BlockSpec((tr, D), lambda b, i, core_ref: ((2 * b + 1 - core_ref[0]) * nr + i, 0))

    def body(core_ref, x_ref, o_ref, send_sem, recv_sem, local_sem):
        b, i = pl.program_id(0), pl.program_id(1)
        me = lax.axis_index("c")
        sibling = (lax.axis_index("x"), lax.axis_index("y"), 1 - me)
        rows = pl.ds(pl.multiple_of(i * tr, tr), tr)
        dst = o_ref.at[b, me, rows] if gather else o_ref.at[b, rows]
        push = pltpu.make_async_remote_copy(src_ref=x_ref, dst_ref=dst, send_sem=send_sem, recv_sem=recv_sem,
                                            device_id=sibling, device_id_type=MESH)
        push.start()
        if gather:
            keep = pltpu.make_async_copy(x_ref, dst, local_sem)
            keep.start()
            keep.wait()
        push.wait_send()

        @pl.when(jnp.logical_and(b == B - 1, i == nr - 1))
        def _():
            landed = o_ref.at[:, 0] if gather else o_ref
            pltpu.make_async_remote_copy(src_ref=landed, dst_ref=landed, send_sem=send_sem, recv_sem=recv_sem,
                                         device_id=sibling, device_id_type=MESH).wait_recv()

    return pl.pallas_call(
        body, name=name,
        grid_spec=pltpu.PrefetchScalarGridSpec(
            num_scalar_prefetch=1, grid=(B, nr), in_specs=[in_spec],
            out_specs=pl.BlockSpec(memory_space=pl.ANY),
            scratch_shapes=[pltpu.SemaphoreType.DMA(()), pltpu.SemaphoreType.DMA(()), pltpu.SemaphoreType.DMA(())]),
        out_shape=jax.ShapeDtypeStruct(out_shape, x.dtype),
        compiler_params=_params("arbitrary", "arbitrary"),
    )(core, x.reshape(-1, D))


def _pair_sum(g, recv, core, name):
    B, _, R, D = g.shape
    tr = _pick(R, 256, BF16_ROWS)

    def body(core_ref, g_ref, r_ref, o_ref):
        o_ref[...] = (g_ref[...].astype(F32) + r_ref[...].astype(F32)).astype(BF16)

    tile = pl.BlockSpec((None, tr, D), lambda b, i, core_ref: (b, i, 0))
    return pl.pallas_call(
        body, name=name,
        grid_spec=pltpu.PrefetchScalarGridSpec(
            num_scalar_prefetch=1, grid=(B, R // tr),
            in_specs=[pl.BlockSpec((None, None, tr, D), lambda b, i, core_ref: (b, core_ref[0], i, 0)), tile],
            out_specs=tile),
        out_shape=jax.ShapeDtypeStruct((B, R, D), BF16),
        compiler_params=_params("parallel", "parallel"),
    )(core, g, recv)


def _gather_weight(wt, core, name):
    L, n, D = wt.shape
    half = _cast_half(wt, core, name + "_cast")
    g = _all_gather(half, ("x", "y"), 1, name + "_ag_xy")
    g = _sib_stream(g.reshape(L * 4, n // 2, D), core, name + "_ag_c", True)
    return g.reshape(L, 4 * n, D)


def _reduce_weight_grad(gt, core, name):
    L, n4, D = gt.shape
    n2 = n4 // 8
    g = gt.reshape(L * 4, 2, n2, D)
    pair = _pair_sum(g, _sib_stream(g, core, name + "_swap_c", False), core, name + "_sum_c")
    ex = _all_to_all(pair.reshape(L, 4, n2, D), ("x", "y"), 1, name + "_a2a_xy")
    tot = _sum_slots(ex, F32, name + "_sum_xy")
    return _sib_stream(tot, core, name + "_ag_c", True).reshape(L, 2 * n2, D)


def _all_reduce_small(v, name):
    pair = _sum_slots(_all_gather(v, ("c",), 0, name + "_ag_c")[None], F32, name + "_sum_c")[0]
    return _sum_slots(_all_gather(pair, ("x", "y"), 0, name + "_ag_xy")[None], F32, name + "_sum_xy")[0]


def _pack(arrs):
    flat = jnp.concatenate([a.reshape(-1) for a in arrs])
    rows = _round_up(-(-flat.shape[0] // SMALL_PACK_COLS), BF16_ROWS)
    return _pad_axis(flat, 0, rows * SMALL_PACK_COLS).reshape(rows, SMALL_PACK_COLS)


def _unpack(buf, like):
    flat = buf.reshape(-1)
    out, off = [], 0
    for a in like:
        out.append(flat[off:off + a.size].reshape(a.shape))
        off += a.size
    return out


def kernel(x, c, mod_w, mod_b, mix_norm_g, ffn_norm_g, attn_w_in, attn_b_f, attn_w_o, gm_w_in, gm_v_g, gm_w_s, gm_b_s, gm_w_o, ffn_w_in, ffn_conv_w, ffn_conv_b, ffn_w_out, final_g, loss_target, m_mod_w, m_mod_b, m_mix_norm_g, m_ffn_norm_g, m_attn_w_in, m_attn_b_f, m_attn_w_o, m_gm_w_in, m_gm_v_g, m_gm_w_s, m_gm_b_s, m_gm_w_o, m_ffn_w_in, m_ffn_conv_w, m_ffn_conv_b, m_ffn_w_out, m_final_g, v_mod_w, v_mod_b, v_mix_norm_g, v_ffn_norm_g, v_attn_w_in, v_attn_b_f, v_attn_w_o, v_gm_w_in, v_gm_v_g, v_gm_w_s, v_gm_b_s, v_gm_w_o, v_ffn_w_in, v_ffn_conv_w, v_ffn_conv_b, v_ffn_w_out, v_final_g):
    _, T, D = x.shape
    L = mod_w.shape[0]
    NA, NB = attn_w_in.shape[0], gm_w_in.shape[0]
    H = D // HEAD_DIM
    NG = D // GROUP
    Fq = ffn_w_out.shape[1]
    Fh = 4 * Fq
    Fqp = _round_up(Fq, LANES)
    Fp = 4 * Fqp
    n_attn = attn_w_in.shape[2]
    n_attn_p = _round_up(n_attn, 2 * BF16_ROWS)
    nmod = mod_w.shape[2]

    xi, yi, ci = lax.axis_index("x"), lax.axis_index("y"), lax.axis_index("c")
    chip = 2 * xi + yi
    example = 2 * chip + ci
    core = ci.astype(jnp.int32).reshape(1)

    wt_attn_in = _gather_weight(_pad_axis(jnp.swapaxes(attn_w_in, 1, 2), 1, n_attn_p), core, "w_attn_in")
    wt_attn_in = wt_attn_in.reshape(NA, 4, n_attn_p, D)[:, :, :n_attn].reshape(NA, 4 * n_attn, D)
    wt_qkv = wt_attn_in[:, :3 * D]
    wt_f = _pad_axis(wt_attn_in[:, 3 * D:], 1, LANES)
    w_attn_o = _gather_weight(attn_w_o, core, "w_attn_o")
    wt_gm_in = _gather_weight(jnp.swapaxes(gm_w_in, 1, 2), core, "w_gm_in")
    w_gm_o = _gather_weight(gm_w_o, core, "w_gm_o")
    wt_ffn_in = _gather_weight(_pad_blocks(jnp.swapaxes(ffn_w_in, 1, 2), 1, 2, Fqp), core, "w_ffn_in")
    w_ffn_out = _gather_weight(_pad_axis(ffn_w_out, 1, Fqp), core, "w_ffn_out")
    cw_all = _all_gather(ffn_conv_w, ("x", "y"), 0, "conv_w_ag")
    conv_w_full = jnp.moveaxis(cw_all, 0, 2).reshape(L, 3, 2 * Fh)
    conv_w = _pad_blocks(conv_w_full, 2, 8, Fqp)
    conv_b = _pad_blocks(ffn_conv_b, 1, 8, Fqp)
    vg_all = _all_gather(gm_v_g, ("x", "y"), 0, "gm_vg_ag")
    vg_full = jnp.moveaxis(vg_all, 0, 1).reshape(NB, D)

    c_all = _all_gather(c, MESH_AXES, 0, "c_ag").reshape(8, D)
    mod_b_mine = lax.dynamic_slice_in_dim(mod_b, chip * nmod, nmod, axis=1)
    c_act, mod_part = _mod_fwd(c_all, mod_w, mod_b_mine, "mod_fwd")
    mod_all = _all_gather(mod_part, ("x", "y"), 0, "mod_ag")
    mod = lax.dynamic_index_in_dim(mod_all, example, axis=2, keepdims=False)
    mod = jnp.swapaxes(mod, 0, 1).reshape(L, 6, 1, D)

    xs = x.reshape(T, D)
    target = loss_target.reshape(T, D)
    saved = []
    for i in range(L):
        j = i // 2
        sh1, sc1, g1, sh2, sc2, g2 = (mod[i, k] for k in range(6))
        h1 = _norm_mod(xs, mix_norm_g[i:i + 1], sc1, sh1, "norm_mod")
        if i % 2 == 0:
            qkv = _matmul(h1, wt_qkv[j], "nt", "attn_qkv", tm=1024, tn=1024)
            flog = _matmul(h1, wt_f[j], "nt", "attn_f", out_dtype=F32, tm=1024)
            bias = _pad_axis(attn_b_f[j:j + 1], 1, LANES)
            fcol = _fox_gate(flog, bias, "fox_gate")
            frow = jnp.transpose(fcol)[:H].reshape(H, 1, T)
            o, lse = _attn_fwd(qkv, fcol, frow, H, "attn_fwd")
            y, x1 = _matmul(o, w_attn_o[j], "nn", "attn_out", out_dtype=F32, tm=1024, tn=1024,
                            res=xs, gate=g1, emit_y=True)
            mix = ("fox", qkv, flog, bias, fcol, frow, o, lse)
        else:
            p = _matmul(h1, wt_gm_in[j], "nt", "gm_in", tm=1024, tn=1024)
            bt = _pad_axis(jnp.transpose(gm_b_s[j]), 1, LANES)
            gated = _gm_fwd(p, vg_full[j:j + 1], gm_w_s[j], bt, "gm_fwd")
            y, x1 = _matmul(gated, w_gm_o[j], "nn", "gm_out", out_dtype=F32, tm=1024, tn=1024,
                            res=xs, gate=g1, emit_y=True)
            mix = ("gm", p, bt, gated)
        h2 = _norm_mod(x1, ffn_norm_g[i:i + 1], sc2, sh2, "norm_mod")
        a = _matmul(h2, wt_ffn_in[i], "nt", "ffn_in", tm=1024, tn=1024)
        hmid = _ffn_act(a, conv_w[i], conv_b[i:i + 1], "ffn_act")
        y2, x2 = _matmul(hmid, w_ffn_out[i], "nn", "ffn_out", out_dtype=F32, tm=512, tn=1024,
                         res=x1, gate=g2, emit_y=True)
        saved.append((xs, h1, mix, y, x1, h2, a, hmid, y2))
        xs = x2

    loss_blk, dx, g_final, dy2, dg2 = _final_loss_bwd(
        xs, final_g.reshape(1, D), target, saved[L - 1][8], mod[L - 1, 5], "final_loss_bwd")
    gw_attn_in, gw_attn_o, gw_gm_in, gw_gm_o = [None] * NA, [None] * NA, [None] * NB, [None] * NB
    gw_ffn_in, gw_ffn_out = [None] * L, [None] * L
    g_mix, g_ffn_n, g_cw, g_cb, dmod = [None] * L, [None] * L, [None] * L, [None] * L, [None] * L
    g_bf, g_vg, g_ws, g_bs = [None] * NA, [None] * NB, [None] * NB, [None] * NB
    for i in reversed(range(L)):
        j = i // 2
        xin, h1, mix, y, x1, h2, a, hmid, y2 = saved[i]
        sh1, sc1, g1, sh2, sc2, g2 = (mod[i, k] for k in range(6))
        dhm = _matmul(dy2, w_ffn_out[i], "nt", "ffn_out_dx", tm=1024, tn=512)
        gw_ffn_out[i] = _matmul(hmid, dy2, "tn", "ffn_out_dw", tm=512, tn=1024)
        da, dcw, dcb = _ffn_act_bwd(a, dhm, conv_w[i], conv_b[i:i + 1], "ffn_act_bwd")
        dh2 = _matmul(da, wt_ffn_in[i], "nn", "ffn_in_dx", out_dtype=F32, tm=512, tn=1024, tk=2816)
        gw_ffn_in[i] = _matmul(da, h2, "tn", "ffn_in_dw", tm=512, tn=1024)
        g_cw[i] = _unpad_blocks(dcw, 1, 8, Fq)
        g_cb[i] = _unpad_blocks(dcb, 1, 8, Fq)[0]
        dx1, dsc2, dsh2, dgn, dy, dg1 = _norm_bwd(
            x1, ffn_norm_g[i:i + 1], sc2, sh2, dh2, dx, (y, g1), "norm_bwd_prev")
        g_ffn_n[i] = dgn[0]
        if mix[0] == "fox":
            _, qkv, flog, bias, fcol, frow, o, lse = mix
            do = _matmul(dy, w_attn_o[j], "nt", "attn_out_dx", tm=1024, tn=1024)
            gw_attn_o[j] = _matmul(o, dy, "tn", "attn_out_dw", tm=512, tn=1024)
            dq, dfq = _attn_dq(qkv, o, do, lse, fcol, frow, H, "attn_dq")
            dk, dv, dfk = _attn_dkv(qkv, o, do, lse, fcol, frow, H, "attn_dkv")
            heads_on_lanes = lambda t: _pad_axis(jnp.transpose(t.reshape(H, T)), 1, LANES)
            dflog, dbf = _fox_gate_bwd(heads_on_lanes(dfq), heads_on_lanes(dfk), flog, bias, "fox_gate_bwd")
            dproj = jnp.concatenate([dq, dk, dv], axis=1)
            dh_f = _matmul(dflog, wt_f[j], "nn", "attn_f_dx", out_dtype=F32, tm=1024, tn=1024)
            dh1 = _matmul(dproj, wt_qkv[j], "nn", "attn_qkv_dx", out_dtype=F32, tm=512, tn=1024,
                          tk=2048, res=dh_f)
            gwt_qkv = _matmul(dproj, h1, "tn", "attn_qkv_dw", tm=512, tn=1024)
            gwt_f = _matmul(dflog, h1, "tn", "attn_f_dw", tm=128, tn=1024)
            gw_attn_in[j] = jnp.concatenate([gwt_qkv, gwt_f[:H]], axis=0)
            g_bf[j] = dbf[0, :H]
        else:
            _, p, bt, gated = mix
            dgated = _matmul(dy, w_gm_o[j], "nt", "gm_out_dx", tm=1024, tn=1024)
            gw_gm_o[j] = _matmul(gated, dy, "tn", "gm_out_dw", tm=512, tn=1024)
            dp, dws, dbt, dvg = _gm_bwd(p, dgated, vg_full[j:j + 1], gm_w_s[j],
                                        jnp.swapaxes(gm_w_s[j], 1, 2), bt, "gm_bwd")
            dh1 = _matmul(dp, wt_gm_in[j], "nn", "gm_in_dx", out_dtype=F32, tm=512, tn=1024, tk=2048)
            gw_gm_in[j] = _matmul(dp, h1, "tn", "gm_in_dw", tm=512, tn=1024)
            g_ws[j] = dws
            g_bs[j] = jnp.transpose(dbt)[:NG]
            g_vg[j] = dvg[0]
        if i > 0:
            dx, dsc1, dsh1, dgn, dy2, dg2_prev = _norm_bwd(
                xin, mix_norm_g[i:i + 1], sc1, sh1, dh1, dx1, (saved[i - 1][8], mod[i - 1, 5]), "norm_bwd_prev")
        else:
            dx, dsc1, dsh1, dgn = _norm_bwd(xin, mix_norm_g[i:i + 1], sc1, sh1, dh1, dx1, None, "norm_bwd_first")
            dg2_prev = None
        g_mix[i] = dgn[0]
        dmod[i] = jnp.concatenate([dsh1, dsc1, dg1, dsh2, dsc2, dg2], axis=1)[0]
        dg2 = dg2_prev
    grad_x = dx.reshape(1, T, D)
    loss = lax.psum(loss_blk[0, 0], MESH_AXES)

    gt = jnp.stack(gw_attn_in).reshape(NA, 4, n_attn, D)
    gt = _pad_axis(gt, 2, n_attn_p).reshape(NA, 4 * n_attn_p, D)
    grad_attn_w_in = jnp.swapaxes(_reduce_weight_grad(gt, core, "g_attn_in")[:, :n_attn], 1, 2)
    grad_attn_w_o = _reduce_weight_grad(jnp.stack(gw_attn_o), core, "g_attn_o")
    grad_gm_w_in = jnp.swapaxes(_reduce_weight_grad(jnp.stack(gw_gm_in), core, "g_gm_in"), 1, 2)
    grad_gm_w_o = _reduce_weight_grad(jnp.stack(gw_gm_o), core, "g_gm_o")
    grad_ffn_w_in = jnp.swapaxes(
        _unpad_blocks(_reduce_weight_grad(jnp.stack(gw_ffn_in), core, "g_ffn_in"), 1, 2, Fq), 1, 2)
    grad_ffn_w_out = _reduce_weight_grad(jnp.stack(gw_ffn_out), core, "g_ffn_out")[:, :Fq]

    small = [jnp.stack(g_mix), jnp.stack(g_ffn_n), jnp.stack(g_bf), jnp.stack(g_vg), jnp.stack(g_ws),
             jnp.stack(g_bs), jnp.stack(g_cw), jnp.stack(g_cb), g_final[0]]
    (grad_mix_norm_g, grad_ffn_norm_g, grad_attn_b_f, g_vg_full, grad_gm_w_s, grad_gm_b_s, g_cw_full,
     grad_ffn_conv_b, grad_final_g) = _unpack(_all_reduce_small(_pack(small), "g_small"), small)
    grad_gm_v_g = lax.dynamic_slice_in_dim(g_vg_full, chip * (D // 4), D // 4, axis=1)
    grad_ffn_conv_w = lax.dynamic_slice_in_dim(g_cw_full, chip * 2 * Fq, 2 * Fq, axis=2)

    dmod_all = _all_gather(jnp.stack(dmod), MESH_AXES, 0, "dmod_ag")
    grad_mod_b = _sum_slots(dmod_all[None], F32, "dmod_sum")[0]
    dmod_cols = jnp.swapaxes(lax.dynamic_slice_in_dim(dmod_all, chip * nmod, nmod, axis=2), 0, 1)

    grad_mod_w, delta_mod_w, new_m_mod_w, new_v_mod_w = _modw_adamw(
        jnp.transpose(c_act), dmod_cols, mod_w, m_mod_w, v_mod_w, "mod_w_adamw")

    def big(w, g, m, v, name):
        shp = w.shape
        flat = lambda t: t.reshape(-1, shp[-1])
        return [t.reshape(shp) for t in _adamw(flat(w), flat(g), flat(m), flat(v), name)]

    upd_attn_in = big(attn_w_in, grad_attn_w_in, m_attn_w_in, v_attn_w_in, "adamw_attn_in")
    upd_attn_o = big(attn_w_o, grad_attn_w_o, m_attn_w_o, v_attn_w_o, "adamw_attn_o")
    upd_gm_in = big(gm_w_in, grad_gm_w_in, m_gm_w_in, v_gm_w_in, "adamw_gm_in")
    upd_gm_o = big(gm_w_o, grad_gm_w_o, m_gm_w_o, v_gm_w_o, "adamw_gm_o")
    upd_ffn_in = big(ffn_w_in, grad_ffn_w_in, m_ffn_w_in, v_ffn_w_in, "adamw_ffn_in")
    upd_ffn_out = big(ffn_w_out, grad_ffn_w_out, m_ffn_w_out, v_ffn_w_out, "adamw_ffn_out")

    sm_w = [mod_b, mix_norm_g, ffn_norm_g, attn_b_f, gm_v_g, gm_w_s, gm_b_s, ffn_conv_w, ffn_conv_b, final_g]
    sm_g = [grad_mod_b, grad_mix_norm_g, grad_ffn_norm_g, grad_attn_b_f, grad_gm_v_g, grad_gm_w_s, grad_gm_b_s,
            grad_ffn_conv_w, grad_ffn_conv_b, grad_final_g]
    sm_m = [m_mod_b, m_mix_norm_g, m_ffn_norm_g, m_attn_b_f, m_gm_v_g, m_gm_w_s, m_gm_b_s, m_ffn_conv_w,
            m_ffn_conv_b, m_final_g]
    sm_v = [v_mod_b, v_mix_norm_g, v_ffn_norm_g, v_attn_b_f, v_gm_v_g, v_gm_w_s, v_gm_b_s, v_ffn_conv_w,
            v_ffn_conv_b, v_final_g]
    sm_d, sm_nm, sm_nv = (_unpack(t, sm_w) for t in _adamw(_pack(sm_w), _pack(sm_g), _pack(sm_m), _pack(sm_v),
                                                             "adamw_small"))

    def ordered(mod_w_item, sm, k):
        return [mod_w_item, sm[0], sm[1], sm[2], upd_attn_in[k], sm[3], upd_attn_o[k], upd_gm_in[k], sm[4], sm[5],
                sm[6], upd_gm_o[k], upd_ffn_in[k], sm[7], sm[8], upd_ffn_out[k], sm[9]]

    grads = [grad_mod_w, grad_mod_b, grad_mix_norm_g, grad_ffn_norm_g, grad_attn_w_in, grad_attn_b_f, grad_attn_w_o,
             grad_gm_w_in, grad_gm_v_g, grad_gm_w_s, grad_gm_b_s, grad_gm_w_o, grad_ffn_w_in, grad_ffn_conv_w,
             grad_ffn_conv_b, grad_ffn_w_out, grad_final_g]
    return (loss, grad_x, *grads, *ordered(delta_mod_w, sm_d, 0), *ordered(new_m_mod_w, sm_nm, 1),
            *ordered(new_v_mod_w, sm_nv, 2))
```

```python
import functools
import math

import jax
import jax.numpy as jnp
from jax import lax
from jax.experimental import pallas as pl
from jax.experimental.pallas import tpu as pltpu

F32 = jnp.float32
BF16 = jnp.bfloat16

EPS = 1e-6
HEAD_DIM = 128
CHUNK = 128
GROUP = 128
LANES = 128
BF16_ROWS = 16
VMEM_LIMIT_BYTES = 52 * 1024 * 1024
NEG_BIG = -1e30
SMALL_PACK_COLS = 1024

ADAM_LR, ADAM_B1, ADAM_B2, ADAM_EPS, ADAM_WD, ADAM_STEP = 0.001, 0.9, 0.999, 1e-08, 0.01, 10

MESH_AXES = ("x", "y", "c")
MESH = pl.DeviceIdType.MESH


def _pick(n, pref, align):
    t = (min(pref, n) // align) * align
    while t >= align:
        if n % t == 0:
            return t
        t -= align
    return n


def _params(*sem):
    return pltpu.CompilerParams(dimension_semantics=sem, vmem_limit_bytes=VMEM_LIMIT_BYTES)


def _group(axes):
    pos = {ax: lax.axis_index(ax) for ax in MESH_AXES}
    rank = 0
    for ax in axes:
        rank = rank * 2 + pos[ax]
    peers = []
    for mask in range(1, 2 ** len(axes)):
        peer = dict(pos)
        for bit, ax in enumerate(reversed(axes)):
            if (mask >> bit) & 1:
                peer[ax] = 1 - pos[ax]
        peers.append((mask, tuple(peer[ax] for ax in MESH_AXES)))
    return rank, peers


def _slot(ref, gaxis, idx):
    return ref.at[(slice(None),) * gaxis + (idx,)]


D2D_STREAMS = 32
ICI_STREAMS = 8


def _pieces(shape, want):
    idx = [()]
    d = 0
    while d < len(shape) - 2 and len(idx) * shape[d] <= want:
        idx = [i + (k,) for i in idx for k in range(shape[d])]
        d += 1
    if d == len(shape) - 2:
        rows = shape[-2]
        r = max(1, want // len(idx))
        while r > 1 and not (rows % r == 0 and (rows // r) % BF16_ROWS == 0):
            r -= 1
        if r > 1:
            idx = [i + (pl.ds(k * (rows // r), rows // r),) for i in idx for k in range(r)]
    return idx


class _Exchange:
    def __init__(self, x, axes, gaxis, gather, sel=None):
        self.x, self.axes, self.gaxis, self.gather, self.sel = x, axes, gaxis, gather, sel
        n = 2 ** len(axes)
        shape = x.shape if sel is None else x.shape[1:]
        if gather:
            out_shape = shape[:gaxis] + (n,) + shape[gaxis:]
            slab = shape
        else:
            assert shape[gaxis] == n
            out_shape = shape
            slab = shape[:gaxis] + shape[gaxis + 1:]
        self.out_shape = jax.ShapeDtypeStruct(out_shape, x.dtype)
        self.pieces = _pieces(slab, D2D_STREAMS if axes == ("c",) else ICI_STREAMS)
        self.scratch = [pltpu.SemaphoreType.DMA((n - 1,)), pltpu.SemaphoreType.DMA((n - 1,)),
                        pltpu.SemaphoreType.DMA(())]

    def _views(self, x_ref, o_ref):
        if self.sel is not None:
            x_ref = x_ref.at[self.sel]
        src = (lambda member: x_ref) if self.gather else (lambda member: _slot(x_ref, self.gaxis, member))
        dst = lambda member: _slot(o_ref, self.gaxis, member)
        return src, dst

    @staticmethod
    def _remote(mask, peer, s, d, send_sems, recv_sems):
        return pltpu.make_async_remote_copy(
            src_ref=s, dst_ref=d, send_sem=send_sems.at[mask - 1], recv_sem=recv_sems.at[mask - 1],
            device_id=peer, device_id_type=MESH)

    def start(self, x_ref, o_ref, send_sems, recv_sems, local_sem):
        rank, peers = _group(self.axes)
        src, dst = self._views(x_ref, o_ref)
        cut = lambda ref, pc: ref.at[pc] if pc else ref
        for pc in self.pieces:
            pltpu.make_async_copy(cut(src(rank), pc), cut(dst(rank), pc), local_sem).start()
            for mask, peer in peers:
                self._remote(mask, peer, cut(src(rank ^ mask), pc), cut(dst(rank), pc), send_sems, recv_sems).start()

    def finish(self, x_ref, o_ref, send_sems, recv_sems, local_sem):
        rank, peers = _group(self.axes)
        src, dst = self._views(x_ref, o_ref)
        for mask, peer in peers:
            whole = self._remote(mask, peer, src(rank), dst(rank ^ mask), send_sems, recv_sems)
            whole.wait_recv()
            whole.wait_send()
        pltpu.make_async_copy(src(rank), dst(rank), local_sem).wait()


ANY_SPEC = pl.BlockSpec(memory_space=pl.ANY)


def _call(body, *, name, grid, in_specs, out_specs, out_shape, scratch_shapes, args, rides=()):
    out_specs, out_shape = list(out_specs), list(out_shape)
    n_in, n_out, n_s, nr = len(in_specs), len(out_specs), len(scratch_shapes), len(rides)
    if not rides:
        sem = ("arbitrary",) * len(grid)
        outs = pl.pallas_call(body, name=name, grid=grid, in_specs=in_specs, out_specs=out_specs,
                              out_shape=out_shape, scratch_shapes=scratch_shapes,
                              compiler_params=_params(*sem))(*args)
        return list(outs), []

    def carried(*refs):
        k = 0
        parts = []
        for cnt in (n_in, nr, n_out, nr, n_s, 3 * nr):
            parts.append(refs[k:k + cnt])
            k += cnt
        h_in, r_in, h_out, r_out, h_s, r_s = parts
        ids = [pl.program_id(d) for d in range(len(grid))]
        first = functools.reduce(jnp.logical_and, [i == 0 for i in ids])
        last = functools.reduce(jnp.logical_and, [i == g - 1 for i, g in zip(ids, grid)])

        @pl.when(first)
        def _():
            for r, ride in enumerate(rides):
                ride.start(r_in[r], r_out[r], *r_s[3 * r:3 * r + 3])

        body(*h_in, *h_out, *h_s)

        @pl.when(last)
        def _():
            for r, ride in enumerate(rides):
                ride.finish(r_in[r], r_out[r], *r_s[3 * r:3 * r + 3])

    outs = pl.pallas_call(
        carried, name=name, grid=grid,
        in_specs=list(in_specs) + [ANY_SPEC] * nr,
        out_specs=out_specs + [ANY_SPEC] * nr,
        out_shape=out_shape + [r.out_shape for r in rides],
        scratch_shapes=list(scratch_shapes) + [s for r in rides for s in r.scratch],
        compiler_params=_params(*(("arbitrary",) * len(grid))),
    )(*args, *[r.x for r in rides])
    return list(outs[:n_out]), list(outs[n_out:])


def _exchange(x, axes, gaxis, name, gather):
    ex = _Exchange(x, axes, gaxis, gather)

    def body(x_ref, o_ref, send_sems, recv_sems, local_sem):
        ex.start(x_ref, o_ref, send_sems, recv_sems, local_sem)
        ex.finish(x_ref, o_ref, send_sems, recv_sems, local_sem)

    return pl.pallas_call(body, name=name, out_shape=ex.out_shape, in_specs=[ANY_SPEC], out_specs=ANY_SPEC,
                          scratch_shapes=ex.scratch)(x)


def _all_gather(x, axes, gaxis, name):
    return _exchange(x, axes, gaxis, name, True)


def _all_to_all(x, axes, gaxis, name):
    return _exchange(x, axes, gaxis, name, False)


def _cast_half(w, core, name):
    L, n, D = w.shape
    n2 = n // 2
    tr = _pick(n2, 256, BF16_ROWS)
    nb = n2 // tr

    def body(core_ref, w_ref, o_ref):
        o_ref[...] = w_ref[...].astype(BF16)

    return pl.pallas_call(
        body, name=name,
        grid_spec=pltpu.PrefetchScalarGridSpec(
            num_scalar_prefetch=1, grid=(L, nb),
            in_specs=[pl.BlockSpec((None, tr, D), lambda l, i, core_ref: (l, core_ref[0] * nb + i, 0))],
            out_specs=pl.BlockSpec((None, tr, D), lambda l, i, core_ref: (l, i, 0))),
        out_shape=jax.ShapeDtypeStruct((L, n2, D), BF16),
        compiler_params=_params("parallel", "parallel"),
    )(core, w)


def _sum_slots(x, out_dtype, name):
    A, G, R, C = x.shape
    budget = (2 * 1024 * 1024) // (G * C * x.dtype.itemsize)
    tr = _pick(R, max(budget, BF16_ROWS), BF16_ROWS)

    def body(x_ref, o_ref):
        acc = x_ref[0].astype(F32)
        for g in range(1, G):
            acc = acc + x_ref[g].astype(F32)
        o_ref[...] = acc.astype(out_dtype)

    return pl.pallas_call(
        body, name=name, grid=(A, R // tr),
        in_specs=[pl.BlockSpec((None, G, tr, C), lambda a, i: (a, 0, i, 0))],
        out_specs=pl.BlockSpec((None, tr, C), lambda a, i: (a, i, 0)),
        out_shape=jax.ShapeDtypeStruct((A, R, C), out_dtype),
        compiler_params=_params("parallel", "parallel"),
    )(x)


def _adamw_math(w, g, m, v):
    m = ADAM_B1 * m + (1.0 - ADAM_B1) * g
    v = ADAM_B2 * v + (1.0 - ADAM_B2) * (g * g)
    m_hat = m / (1.0 - ADAM_B1 ** ADAM_STEP)
    v_hat = v / (1.0 - ADAM_B2 ** ADAM_STEP)
    delta = -ADAM_LR * (m_hat / (jnp.sqrt(v_hat) + ADAM_EPS) + ADAM_WD * w)
    return delta, m, v


def _adamw(w, g, m, v, name):
    R, C = w.shape
    budget = (768 * 1024) // (C * 4)
    tr = _pick(R, max(budget, 8), 8)

    def body(w_ref, g_ref, m_ref, v_ref, d_ref, nm_ref, nv_ref):
        d, nm, nv = _adamw_math(w_ref[...], g_ref[...], m_ref[...], v_ref[...])
        d_ref[...] = d
        nm_ref[...] = nm
        nv_ref[...] = nv

    spec = pl.BlockSpec((tr, C), lambda i: (i, 0))
    return pl.pallas_call(
        body, name=name, grid=(R // tr,),
        in_specs=[spec] * 4, out_specs=[spec] * 3,
        out_shape=[jax.ShapeDtypeStruct((R, C), F32)] * 3,
        compiler_params=_params("parallel"),
    )(w, g, m, v)


def _modw_adamw(ct, dm, w, m, v, name, rides=()):
    L, D, N = w.shape
    E = ct.shape[1]
    tr = _pick(D, 64, 8)

    def body(ct_ref, dm_ref, w_ref, m_ref, v_ref, g_ref, d_ref, nm_ref, nv_ref):
        g = ct_ref[:, 0:1] * dm_ref[0:1, :]
        for e in range(1, E):
            g = g + ct_ref[:, e:e + 1] * dm_ref[e:e + 1, :]
        d, nm, nv = _adamw_math(w_ref[...], g, m_ref[...], v_ref[...])
        g_ref[...] = g
        d_ref[...] = d
        nm_ref[...] = nm
        nv_ref[...] = nv

    spec = pl.BlockSpec((None, tr, N), lambda l, i: (l, i, 0))
    outs, carried = _call(
        body, name=name, grid=(L, D // tr),
        in_specs=[pl.BlockSpec((tr, E), lambda l, i: (i, 0)),
                  pl.BlockSpec((None, E, N), lambda l, i: (l, 0, 0)), spec, spec, spec],
        out_specs=[spec] * 4,
        out_shape=[jax.ShapeDtypeStruct((L, D, N), F32)] * 4,
        scratch_shapes=[], args=(ct, dm, w, m, v), rides=rides)
    return (outs, carried) if rides else outs


def _mod_fwd(c_all, w, b, name):
    L, D, N = w.shape
    E = c_all.shape[0]
    tn = _pick(N, 512, LANES)

    def body(c_ref, w_ref, b_ref, act_ref, o_ref):
        cv = c_ref[...]
        act = cv / (1.0 + jnp.exp(-cv))
        act_ref[...] = act
        o_ref[...] = jnp.dot(act.astype(BF16), w_ref[...].astype(BF16),
                             preferred_element_type=F32) + b_ref[...]

    return pl.pallas_call(
        body, name=name, grid=(L, N // tn),
        in_specs=[pl.BlockSpec((E, D), lambda l, j: (0, 0)),
                  pl.BlockSpec((None, D, tn), lambda l, j: (l, 0, j)),
                  pl.BlockSpec((None, 1, tn), lambda l, j: (l, 0, j))],
        out_specs=[pl.BlockSpec((E, D), lambda l, j: (0, 0)),
                   pl.BlockSpec((None, E, tn), lambda l, j: (l, 0, j))],
        out_shape=[jax.ShapeDtypeStruct((E, D), F32), jax.ShapeDtypeStruct((L, E, N), F32)],
        compiler_params=_params("arbitrary", "arbitrary"),
    )(c_all, w, b.reshape(L, 1, N))


_DIMS = {"nn": (((1,), (0,)), ((), ())), "nt": (((1,), (1,)), ((), ())), "tn": (((0,), (0,)), ((), ()))}


def _matmul(a, b, form, name, *, out_dtype=BF16, tm=512, tn=512, tk=None, a_resident=False,
            res=None, gate=None, emit_y=False, rides=()):
    if form == "nn":
        (M, K), N = a.shape, b.shape[1]
    elif form == "nt":
        (M, K), N = a.shape, b.shape[0]
    else:
        (K, M), N = a.shape, b.shape[1]
    tm = _pick(M, tm, LANES if form == "tn" else BF16_ROWS)
    tn = _pick(N, tn, LANES)
    tk = K if (tk is None or form == "tn") else _pick(K, tk, LANES)
    nk = K // tk

    def ij(g0, g1):
        return (g0, g1) if a_resident else (g1, g0)

    grid = (M // tm, N // tn, nk) if a_resident else (N // tn, M // tm, nk)
    if form == "tn":
        a_spec = pl.BlockSpec((K, tm), lambda g0, g1, k: (0, ij(g0, g1)[0]))
        b_spec = pl.BlockSpec((K, tn), lambda g0, g1, k: (0, ij(g0, g1)[1]))
    else:
        a_spec = pl.BlockSpec((tm, tk), lambda g0, g1, k: (ij(g0, g1)[0], k))
        if form == "nn":
            b_spec = pl.BlockSpec((tk, tn), lambda g0, g1, k: (k, ij(g0, g1)[1]))
        else:
            b_spec = pl.BlockSpec((tn, tk), lambda g0, g1, k: (ij(g0, g1)[1], k))
    tile = pl.BlockSpec((tm, tn), lambda g0, g1, k: ij(g0, g1))
    row = pl.BlockSpec((1, tn), lambda g0, g1, k: (0, ij(g0, g1)[1]))
    in_specs, args = [a_spec, b_spec], [a, b]
    if res is not None:
        in_specs.append(tile)
        args.append(res)
    if gate is not None:
        in_specs.append(row)
        args.append(gate)
    if emit_y:
        out_specs = [tile, tile]
        out_shape = [jax.ShapeDtypeStruct((M, N), BF16), jax.ShapeDtypeStruct((M, N), out_dtype)]
    else:
        out_specs = tile
        out_shape = jax.ShapeDtypeStruct((M, N), out_dtype)
    n_in = len(args)

    def body(*refs):
        a_ref, b_ref = refs[0], refs[1]
        res_ref = refs[2] if res is not None else None
        gate_ref = refs[n_in - 1] if gate is not None else None
        outs = refs[n_in:n_in + (2 if emit_y else 1)]

        def finish(acc):
            val = acc
            if gate_ref is not None:
                val = gate_ref[...] * val
            if res_ref is not None:
                val = res_ref[...] + val
            if emit_y:
                outs[0][...] = acc.astype(BF16)
            outs[-1][...] = val.astype(out_dtype)

        part = lax.dot_general(a_ref[...], b_ref[...], _DIMS[form], preferred_element_type=F32)
        if nk == 1:
            finish(part)
        else:
            acc_ref = refs[-1]
            k = pl.program_id(2)

            @pl.when(k == 0)
            def _():
                acc_ref[...] = part

            @pl.when(k > 0)
            def _():
                acc_ref[...] += part

            @pl.when(k == nk - 1)
            def _():
                finish(acc_ref[...])

    outs, carried = _call(
        body, name=name, grid=grid, in_specs=in_specs, out_specs=out_specs if emit_y else [out_specs],
        out_shape=out_shape if emit_y else [out_shape],
        scratch_shapes=[pltpu.VMEM((tm, tn), F32)] if nk > 1 else [], args=args, rides=rides)
    result = outs if emit_y else outs[0]
    return (result, carried) if rides else result


def _norm_mod(x, g, sc, sh, name):
    T, D = x.shape
    tm = _pick(T, 256, BF16_ROWS)

    def body(x_ref, g_ref, sc_ref, sh_ref, h_ref):
        xv = x_ref[...]
        rstd = lax.rsqrt(jnp.mean(xv * xv, axis=-1, keepdims=True) + EPS)
        y = xv * rstd * g_ref[...]
        h_ref[...] = (y * (1.0 + sc_ref[...]) + sh_ref[...]).astype(BF16)

    tile = pl.BlockSpec((tm, D), lambda i: (i, 0))
    row = pl.BlockSpec((1, D), lambda i: (0, 0))
    return pl.pallas_call(
        body, name=name, grid=(T // tm,), in_specs=[tile, row, row, row], out_specs=tile,
        out_shape=jax.ShapeDtypeStruct((T, D), BF16), compiler_params=_params("parallel"),
    )(x, g, sc, sh)


def _colsum(v):
    return jnp.sum(v, axis=0, keepdims=True)


def _norm_bwd(x, g, sc, sh, dh, dres, prev, name):
    T, D = x.shape
    tm = _pick(T, 256, BF16_ROWS)
    has_prev = prev is not None

    def body(*refs):
        x_ref, g_ref, sc_ref, sh_ref, dh_ref, dres_ref = refs[:6]
        k = 6
        if has_prev:
            y_ref, gate_ref = refs[6:8]
            k = 8
        dx_ref, dsc_ref, dsh_ref, dg_ref = refs[k:k + 4]
        first = pl.program_id(0) == 0
        xv = x_ref[...]
        rstd = lax.rsqrt(jnp.mean(xv * xv, axis=-1, keepdims=True) + EPS)
        xhat = xv * rstd
        gv = g_ref[...]
        dhv = dh_ref[...].astype(F32)
        dyv = dhv * (1.0 + sc_ref[...])
        dxhat = dyv * gv
        dx = rstd * (dxhat - xhat * jnp.mean(dxhat * xhat, axis=-1, keepdims=True))
        dxt = dres_ref[...] + dx
        dx_ref[...] = dxt
        sums = [(dsc_ref, _colsum(dhv * (xhat * gv))), (dsh_ref, _colsum(dhv)), (dg_ref, _colsum(dyv * xhat))]
        if has_prev:
            dy_ref, dgate_ref = refs[k + 4:k + 6]
            dy_ref[...] = (dxt * gate_ref[...]).astype(BF16)
            sums.append((dgate_ref, _colsum(dxt * y_ref[...].astype(F32))))
        for ref, val in sums:
            @pl.when(first)
            def _(ref=ref, val=val):
                ref[...] = val

            @pl.when(jnp.logical_not(first))
            def _(ref=ref, val=val):
                ref[...] += val

    tile = pl.BlockSpec((tm, D), lambda i: (i, 0))
    row = pl.BlockSpec((1, D), lambda i: (0, 0))
    rowshape = jax.ShapeDtypeStruct((1, D), F32)
    in_specs = [tile, row, row, row, tile, tile]
    args = [x, g, sc, sh, dh, dres]
    out_specs = [tile, row, row, row]
    out_shape = [jax.ShapeDtypeStruct((T, D), F32), rowshape, rowshape, rowshape]
    if has_prev:
        in_specs += [tile, row]
        args += list(prev)
        out_specs += [tile, row]
        out_shape += [jax.ShapeDtypeStruct((T, D), BF16), rowshape]
    return pl.pallas_call(
        body, name=name, grid=(T // tm,), in_specs=in_specs, out_specs=out_specs, out_shape=out_shape,
        compiler_params=_params("arbitrary"),
    )(*args)


def _final_loss_bwd(x, g, target, y_prev, gate_prev, name):
    T, D = x.shape
    tm = _pick(T, 256, BF16_ROWS)

    def body(x_ref, g_ref, t_ref, y_ref, gate_ref, loss_ref, dx_ref, dg_ref, dy_ref, dgate_ref):
        first = pl.program_id(0) == 0
        xv = x_ref[...]
        rstd = lax.rsqrt(jnp.mean(xv * xv, axis=-1, keepdims=True) + EPS)
        xhat = xv * rstd
        gv = g_ref[...]
        err = xhat * gv - t_ref[...]
        part = 0.5 * jnp.sum(jnp.mean(err * err, axis=-1, keepdims=True))
        dyf = err * (1.0 / D)
        dxhat = dyf * gv
        dx = rstd * (dxhat - xhat * jnp.mean(dxhat * xhat, axis=-1, keepdims=True))
        dx_ref[...] = dx
        dy_ref[...] = (dx * gate_ref[...]).astype(BF16)
        sums = [(dg_ref, _colsum(dyf * xhat)), (dgate_ref, _colsum(dx * y_ref[...].astype(F32))),
                (loss_ref, jnp.full((8, LANES), part, F32))]
        for ref, val in sums:
            @pl.when(first)
            def _(ref=ref, val=val):
                ref[...] = val

            @pl.when(jnp.logical_not(first))
            def _(ref=ref, val=val):
                ref[...] += val

    tile = pl.BlockSpec((tm, D), lambda i: (i, 0))
    row = pl.BlockSpec((1, D), lambda i: (0, 0))
    rowshape = jax.ShapeDtypeStruct((1, D), F32)
    return pl.pallas_call(
        body, name=name, grid=(T // tm,),
        in_specs=[tile, row, tile, tile, row],
        out_specs=[pl.BlockSpec((8, LANES), lambda i: (0, 0)), tile, row, tile, row],
        out_shape=[jax.ShapeDtypeStruct((8, LANES), F32), jax.ShapeDtypeStruct((T, D), F32), rowshape,
                   jax.ShapeDtypeStruct((T, D), BF16), rowshape],
        compiler_params=_params("arbitrary"),
    )(x, g, target, y_prev, gate_prev)


def _split3(v):
    hi = v.astype(BF16)
    r1 = v - hi.astype(F32)
    mid = r1.astype(BF16)
    lo = (r1 - mid.astype(F32)).astype(BF16)
    return hi, mid, lo


def _tri_dot(tri, v):
    hi, mid, lo = _split3(v)
    dot = functools.partial(jnp.dot, preferred_element_type=F32)
    return dot(tri, hi) + dot(tri, mid) + dot(tri, lo)


def _fox_gate(flog, bias, name):
    T = flog.shape[0]
    tb = _pick(T, 256, BF16_ROWS)

    def body(f_ref, b_ref, o_ref, carry_ref):
        @pl.when(pl.program_id(0) == 0)
        def _():
            carry_ref[...] = jnp.zeros_like(carry_ref)

        xv = f_ref[...] + b_ref[...]
        lf = jnp.minimum(xv, 0.0) - jnp.log(1.0 + jnp.exp(-jnp.abs(xv)))
        r = lax.broadcasted_iota(jnp.int32, (tb, tb), 0)
        cidx = lax.broadcasted_iota(jnp.int32, (tb, tb), 1)
        tri = jnp.where(r >= cidx, 1.0, 0.0).astype(BF16)
        out = _tri_dot(tri, lf) + carry_ref[...]
        o_ref[...] = out
        carry_ref[...] = out[tb - 1:tb, :]

    return pl.pallas_call(
        body, name=name, grid=(T // tb,),
        in_specs=[pl.BlockSpec((tb, LANES), lambda i: (i, 0)), pl.BlockSpec((1, LANES), lambda i: (0, 0))],
        out_specs=pl.BlockSpec((tb, LANES), lambda i: (i, 0)),
        out_shape=jax.ShapeDtypeStruct((T, LANES), F32),
        scratch_shapes=[pltpu.VMEM((1, LANES), F32)],
        compiler_params=_params("arbitrary"),
    )(flog, bias)


def _fox_gate_bwd(dfq, dfk, flog, bias, name):
    T = flog.shape[0]
    tb = _pick(T, 256, BF16_ROWS)
    nb = T // tb

    def body(dq_ref, dk_ref, f_ref, b_ref, o_ref, db_ref, carry_ref):
        first = pl.program_id(0) == 0

        @pl.when(first)
        def _():
            carry_ref[...] = jnp.zeros_like(carry_ref)

        dv = dq_ref[...] + dk_ref[...]
        r = lax.broadcasted_iota(jnp.int32, (tb, tb), 0)
        cidx = lax.broadcasted_iota(jnp.int32, (tb, tb), 1)
        tri = jnp.where(cidx >= r, 1.0, 0.0).astype(BF16)
        dl = _tri_dot(tri, dv) + carry_ref[...]
        carry_ref[...] = dl[0:1, :]
        xv = f_ref[...] + b_ref[...]
        dfl = dl / (1.0 + jnp.exp(xv))
        o_ref[...] = dfl.astype(BF16)
        val = _colsum(dfl)

        @pl.when(first)
        def _():
            db_ref[...] = val

        @pl.when(jnp.logical_not(first))
        def _():
            db_ref[...] += val

    rev = pl.BlockSpec((tb, LANES), lambda i: (nb - 1 - i, 0))
    row = pl.BlockSpec((1, LANES), lambda i: (0, 0))
    return pl.pallas_call(
        body, name=name, grid=(nb,), in_specs=[rev, rev, rev, row], out_specs=[rev, row],
        out_shape=[jax.ShapeDtypeStruct((T, LANES), BF16), jax.ShapeDtypeStruct((1, LANES), F32)],
        scratch_shapes=[pltpu.VMEM((1, LANES), F32)],
        compiler_params=_params("arbitrary"),
    )(dfq, dfk, flog, bias)


_NT = _DIMS["nt"]
_TN = _DIMS["tn"]


def _attn_fwd(qkv, frow, n_heads, name, rides=()):
    T = qkv.shape[0]
    H = n_heads
    tq = _pick(T, 512, LANES)
    tk = tq
    scale = HEAD_DIM ** -0.5

    def body(q_ref, k_ref, v_ref, fr_ref, o_ref, lse_ref):
        i = pl.program_id(1)
        q = q_ref[...]
        causal = _tril((tq, tk))

        def step(j, carry, diagonal):
            m, l, acc = carry
            ks = pl.multiple_of(j * tk, tk)
            kb = k_ref[pl.ds(ks, tk), :]
            vb = v_ref[pl.ds(ks, tk), :]
            s = lax.dot_general(q, kb, _NT, preferred_element_type=F32) * scale - fr_ref[:, pl.ds(ks, tk)]
            if diagonal:
                s = jnp.where(causal, s, NEG_BIG)
            m_new = jnp.maximum(m, jnp.max(s, axis=1, keepdims=True))
            alpha = jnp.exp(m - m_new)
            p = jnp.exp(s - m_new)
            l = alpha * l + jnp.sum(p, axis=1, keepdims=True)
            acc = alpha * acc + jnp.dot(p.astype(BF16), vb, preferred_element_type=F32)
            return m_new, l, acc

        init = (jnp.full((tq, 1), NEG_BIG, F32), jnp.zeros((tq, 1), F32), jnp.zeros((tq, HEAD_DIM), F32))
        carry = lax.fori_loop(0, i, lambda j, c: step(j, c, False), init)
        m, l, acc = step(i, carry, True)
        o_ref[...] = (acc / l).astype(BF16)
        lse_ref[...] = jnp.broadcast_to(m + jnp.log(l), (tq, LANES))

    outs, carried = _call(
        body, name=name, grid=(H, T // tq),
        in_specs=[pl.BlockSpec((tq, HEAD_DIM), lambda h, i: (i, h)),
                  pl.BlockSpec((T, HEAD_DIM), lambda h, i: (0, H + h)),
                  pl.BlockSpec((T, HEAD_DIM), lambda h, i: (0, 2 * H + h)),
                  pl.BlockSpec((None, 1, T), lambda h, i: (h, 0, 0))],
        out_specs=[pl.BlockSpec((tq, HEAD_DIM), lambda h, i: (i, h)),
                   pl.BlockSpec((None, tq, LANES), lambda h, i: (h, i, 0))],
        out_shape=[jax.ShapeDtypeStruct((T, H * HEAD_DIM), BF16), jax.ShapeDtypeStruct((H, T, LANES), F32)],
        scratch_shapes=[], args=(qkv, qkv, qkv, frow), rides=rides)
    return (outs, carried) if rides else outs


def _attn_dq(qkv, o, do, lse, frow, n_heads, name):
    T = qkv.shape[0]
    H = n_heads
    tq = _pick(T, 512, LANES)
    tk = tq
    scale = HEAD_DIM ** -0.5

    def body(q_ref, k_ref, v_ref, o_ref, do_ref, lse_ref, fr_ref, dq_ref, df_ref):
        i = pl.program_id(1)
        q = q_ref[...]
        dob = do_ref[...]
        delta = jnp.sum(dob.astype(F32) * o_ref[...].astype(F32), axis=1, keepdims=True)
        lse_i = lse_ref[:, 0:1]
        causal = _tril((tq, tk))

        def step(j, carry, diagonal):
            dq, rsum = carry
            ks = pl.multiple_of(j * tk, tk)
            kb = k_ref[pl.ds(ks, tk), :]
            vb = v_ref[pl.ds(ks, tk), :]
            s = lax.dot_general(q, kb, _NT, preferred_element_type=F32) * scale - fr_ref[:, pl.ds(ks, tk)]
            p = jnp.exp(s - lse_i)
            if diagonal:
                p = jnp.where(causal, p, 0.0)
            dp = lax.dot_general(dob, vb, _NT, preferred_element_type=F32)
            ds = p * (dp - delta)
            return (dq + jnp.dot(ds.astype(BF16), kb, preferred_element_type=F32),
                    rsum + jnp.sum(ds, axis=1, keepdims=True))

        init = (jnp.zeros((tq, HEAD_DIM), F32), jnp.zeros((tq, 1), F32))
        dq, rsum = step(i, lax.fori_loop(0, i, lambda j, c: step(j, c, False), init), True)
        dq_ref[...] = (dq * scale).astype(BF16)
        df_ref[...] = jnp.transpose(jnp.broadcast_to(rsum, (tq, LANES)))[0:1, :]

    qtile = lambda off: pl.BlockSpec((tq, HEAD_DIM), lambda h, i: (i, off + h))
    return pl.pallas_call(
        body, name=name, grid=(H, T // tq),
        in_specs=[qtile(0),
                  pl.BlockSpec((T, HEAD_DIM), lambda h, i: (0, H + h)),
                  pl.BlockSpec((T, HEAD_DIM), lambda h, i: (0, 2 * H + h)),
                  qtile(0), qtile(0),
                  pl.BlockSpec((None, tq, LANES), lambda h, i: (h, i, 0)),
                  pl.BlockSpec((None, 1, T), lambda h, i: (h, 0, 0))],
        out_specs=[qtile(0), pl.BlockSpec((None, 1, tq), lambda h, i: (h, 0, i))],
        out_shape=[jax.ShapeDtypeStruct((T, H * HEAD_DIM), BF16), jax.ShapeDtypeStruct((H, 1, T), F32)],
        compiler_params=_params("parallel", "arbitrary"),
    )(qkv, qkv, qkv, o, do, lse, frow)


def _attn_dkv(qkv, o, do, lse, frow, n_heads, name, rides=()):
    T = qkv.shape[0]
    H = n_heads
    tk = _pick(T, 512, LANES)
    tq = tk
    nq = T // tq
    scale = HEAD_DIM ** -0.5

    def body(k_ref, v_ref, q_ref, o_ref, do_ref, lse_ref, fr_ref, dk_ref, dv_ref, df_ref):
        j = pl.program_id(1)
        kb = k_ref[...]
        vb = v_ref[...]
        fj = fr_ref[...]
        causal = _tril((tq, tk))

        def step(i, carry, diagonal):
            dk, dv, df = carry
            qs = pl.multiple_of(i * tq, tq)
            qb = q_ref[pl.ds(qs, tq), :]
            dob = do_ref[pl.ds(qs, tq), :]
            delta = jnp.sum(dob.astype(F32) * o_ref[pl.ds(qs, tq), :].astype(F32), axis=1, keepdims=True)
            lse_i = lse_ref[pl.ds(qs, tq), 0:1]
            s = lax.dot_general(qb, kb, _NT, preferred_element_type=F32) * scale - fj
            p = jnp.exp(s - lse_i)
            if diagonal:
                p = jnp.where(causal, p, 0.0)
            dv = dv + lax.dot_general(p.astype(BF16), dob, _TN, preferred_element_type=F32)
            dp = lax.dot_general(dob, vb, _NT, preferred_element_type=F32)
            ds = p * (dp - delta)
            dk = dk + lax.dot_general(ds.astype(BF16), qb, _TN, preferred_element_type=F32)
            df = df - _colsum(ds)
            return dk, dv, df

        init = (jnp.zeros((tk, HEAD_DIM), F32), jnp.zeros((tk, HEAD_DIM), F32), jnp.zeros((1, tk), F32))
        dk, dv, df = lax.fori_loop(j + 1, nq, lambda i, c: step(i, c, False), step(j, init, True))
        dk_ref[...] = (dk * scale).astype(BF16)
        dv_ref[...] = dv.astype(BF16)
        df_ref[...] = df

    full = lambda off: pl.BlockSpec((T, HEAD_DIM), lambda h, j: (0, off + h))
    ktile = lambda off: pl.BlockSpec((tk, HEAD_DIM), lambda h, j: (j, off + h))
    outs, carried = _call(
        body, name=name, grid=(H, T // tk),
        in_specs=[ktile(H), ktile(2 * H), full(0), full(0), full(0),
                  pl.BlockSpec((None, T, LANES), lambda h, j: (h, 0, 0)),
                  pl.BlockSpec((None, 1, tk), lambda h, j: (h, 0, j))],
        out_specs=[ktile(0), ktile(0), pl.BlockSpec((None, 1, tk), lambda h, j: (h, 0, j))],
        out_shape=[jax.ShapeDtypeStruct((T, H * HEAD_DIM), BF16)] * 2 + [jax.ShapeDtypeStruct((H, 1, T), F32)],
        scratch_shapes=[], args=(qkv, qkv, qkv, o, do, lse, frow), rides=rides)
    return (outs, carried) if rides else outs


_GELU_C = math.sqrt(2.0 / math.pi)
_GELU_A = 0.044715


def _gelu(p):
    t = jnp.tanh(_GELU_C * (p + _GELU_A * p * p * p))
    return 0.5 * p * (1.0 + t), t


def _gelu_grad(p, t):
    return 0.5 * (1.0 + t) + 0.5 * p * (1.0 - t * t) * _GELU_C * (1.0 + 3.0 * _GELU_A * p * p)


def _tril(shape, upper=False):
    r = lax.broadcasted_iota(jnp.int32, shape, 0)
    c = lax.broadcasted_iota(jnp.int32, shape, 1)
    return (c >= r) if upper else (r >= c)


def _gm_fwd(p, vg, ws, bt, name):
    T, D2 = p.shape
    D = D2 // 2
    NG = D // GROUP
    tm = _pick(T, 2 * CHUNK, CHUNK)

    def body(p_ref, vg_ref, ws_ref, bt_ref, o_ref):
        z, _ = _gelu(p_ref[...].astype(F32))
        u, v = z[:, :D], z[:, D:]
        rstd = lax.rsqrt(jnp.mean(v * v, axis=-1, keepdims=True) + EPS)
        vn = (v * rstd * vg_ref[...]).astype(BF16)
        low = _tril((CHUNK, CHUNK))
        for g in range(NG):
            wm = jnp.where(low, ws_ref[g], 0.0).astype(BF16)
            bcol = bt_ref[:, g:g + 1]
            cs = slice(g * GROUP, (g + 1) * GROUP)
            for r in range(tm // CHUNK):
                rs = slice(r * CHUNK, (r + 1) * CHUNK)
                sv = jnp.dot(wm, vn[rs, cs], preferred_element_type=F32) + bcol
                o_ref[rs, cs] = (u[rs, cs] * sv).astype(BF16)

    return pl.pallas_call(
        body, name=name, grid=(T // tm,),
        in_specs=[pl.BlockSpec((tm, D2), lambda i: (i, 0)), pl.BlockSpec((1, D), lambda i: (0, 0)),
                  pl.BlockSpec((NG, CHUNK, CHUNK), lambda i: (0, 0, 0)),
                  pl.BlockSpec((CHUNK, LANES), lambda i: (0, 0))],
        out_specs=pl.BlockSpec((tm, D), lambda i: (i, 0)),
        out_shape=jax.ShapeDtypeStruct((T, D), BF16),
        compiler_params=_params("parallel"),
    )(p, vg, ws, bt)


def _gm_bwd(p, dgated, vg, ws, wst, bt, name):
    T, D2 = p.shape
    D = D2 // 2
    NG = D // GROUP
    tm = _pick(T, 2 * CHUNK, CHUNK)
    nb = T // tm

    def body(p_ref, dg_ref, vg_ref, ws_ref, wst_ref, bt_ref, dp_ref, dws_ref, dbt_ref, dvg_ref, du_s, dvn_s):
        step = pl.program_id(0)

        @pl.when(step == 0)
        def _():
            dws_ref[...] = jnp.zeros_like(dws_ref)
            dbt_ref[...] = jnp.zeros_like(dbt_ref)
            dvg_ref[...] = jnp.zeros_like(dvg_ref)

        pv = p_ref[...].astype(F32)
        z, t = _gelu(pv)
        u, v = z[:, :D], z[:, D:]
        rstd = lax.rsqrt(jnp.mean(v * v, axis=-1, keepdims=True) + EPS)
        vhat = v * rstd
        vgv = vg_ref[...]
        vn = (vhat * vgv).astype(BF16)
        dgt = dg_ref[...].astype(F32)
        low = _tril((CHUNK, CHUNK))
        up = _tril((CHUNK, CHUNK), upper=True)
        lane = lax.broadcasted_iota(jnp.int32, (CHUNK, LANES), 1)
        dbt = jnp.zeros((CHUNK, LANES), F32)
        for g in range(NG):
            wm = jnp.where(low, ws_ref[g], 0.0).astype(BF16)
            wmt = jnp.where(up, wst_ref[g], 0.0).astype(BF16)
            bcol = bt_ref[:, g:g + 1]
            cs = slice(g * GROUP, (g + 1) * GROUP)
            dw = jnp.zeros((CHUNK, CHUNK), F32)
            for r in range(tm // CHUNK):
                rs = slice(r * CHUNK, (r + 1) * CHUNK)
                blk = vn[rs, cs]
                sv = jnp.dot(wm, blk, preferred_element_type=F32) + bcol
                dgb = dgt[rs, cs]
                du_s[rs, cs] = dgb * sv
                dsv = dgb * u[rs, cs]
                dsvb = dsv.astype(BF16)
                dvn_s[rs, cs] = jnp.dot(wmt, dsvb, preferred_element_type=F32)
                dw = dw + lax.dot_general(dsvb, blk, _NT, preferred_element_type=F32)
                dbt = dbt + jnp.where(lane == g, jnp.sum(dsv, axis=1, keepdims=True), 0.0)
            dws_ref[g] += dw
        dbt_ref[...] += dbt
        dvn = dvn_s[...]
        dvg_ref[...] += _colsum(dvn * vhat)
        dvhat = dvn * vgv
        dv = rstd * (dvhat - vhat * jnp.mean(dvhat * vhat, axis=-1, keepdims=True))
        gg = _gelu_grad(pv, t)
        dp_ref[:, :D] = (du_s[...] * gg[:, :D]).astype(BF16)
        dp_ref[:, D:] = (dv * gg[:, D:]).astype(BF16)

        @pl.when(step == nb - 1)
        def _():
            for g in range(NG):
                dws_ref[g] = jnp.where(low, dws_ref[g], 0.0)

    const3 = pl.BlockSpec((NG, CHUNK, CHUNK), lambda i: (0, 0, 0))
    return pl.pallas_call(
        body, name=name, grid=(nb,),
        in_specs=[pl.BlockSpec((tm, D2), lambda i: (i, 0)), pl.BlockSpec((tm, D), lambda i: (i, 0)),
                  pl.BlockSpec((1, D), lambda i: (0, 0)), const3, const3,
                  pl.BlockSpec((CHUNK, LANES), lambda i: (0, 0))],
        out_specs=[pl.BlockSpec((tm, D2), lambda i: (i, 0)), const3,
                   pl.BlockSpec((CHUNK, LANES), lambda i: (0, 0)), pl.BlockSpec((1, D), lambda i: (0, 0))],
        out_shape=[jax.ShapeDtypeStruct((T, D2), BF16), jax.ShapeDtypeStruct((NG, CHUNK, CHUNK), F32),
                   jax.ShapeDtypeStruct((CHUNK, LANES), F32), jax.ShapeDtypeStruct((1, D), F32)],
        scratch_shapes=[pltpu.VMEM((tm, D), F32), pltpu.VMEM((tm, D), F32)],
        compiler_params=_params("arbitrary"),
    )(p, dgated, vg, ws, wst, bt)


HALO = BF16_ROWS


def _ffn_act(a, cw, cb, name):
    T, F2 = a.shape
    Fh = F2 // 2
    tm = _pick(T, 256, HALO)
    per = tm // HALO
    nch = Fh // LANES

    def body(a_ref, h_ref, cw_ref, cb_ref, o_ref):
        i = pl.program_id(0)
        row = lax.broadcasted_iota(jnp.int32, (tm, LANES), 0)
        keep = jnp.where(i > 0, 1.0, 0.0)

        def conv(off):
            af = a_ref[:, pl.ds(off, LANES)].astype(F32)
            hf = h_ref[:, pl.ds(off, LANES)].astype(F32) * keep
            r1 = jnp.where(row == 0, hf[HALO - 1:HALO], pltpu.roll(af, 1, 0))
            r2 = jnp.where(row == 0, hf[HALO - 2:HALO - 1], jnp.where(row == 1, hf[HALO - 1:HALO], pltpu.roll(af, 2, 0)))
            w = cw_ref[:, pl.ds(off, LANES)]
            return w[0:1] * r2 + w[1:2] * r1 + w[2:3] * af + cb_ref[:, pl.ds(off, LANES)]

        def chunk(c, carry):
            off = pl.multiple_of(c * LANES, LANES)
            gt = conv(off)
            upv = conv(pl.multiple_of(off + Fh, LANES))
            o_ref[:, pl.ds(off, LANES)] = (gt / (1.0 + jnp.exp(-gt)) * upv).astype(BF16)
            return carry

        lax.fori_loop(0, nch, chunk, 0)

    return pl.pallas_call(
        body, name=name, grid=(T // tm,),
        in_specs=[pl.BlockSpec((tm, F2), lambda i: (i, 0)),
                  pl.BlockSpec((HALO, F2), lambda i: (jnp.maximum(i * per - 1, 0), 0)),
                  pl.BlockSpec((3, F2), lambda i: (0, 0)), pl.BlockSpec((1, F2), lambda i: (0, 0))],
        out_specs=pl.BlockSpec((tm, Fh), lambda i: (i, 0)),
        out_shape=jax.ShapeDtypeStruct((T, Fh), BF16),
        compiler_params=_params("parallel"),
    )(a, a, cw, cb)


def _ffn_act_bwd(a, dhm, cw, cb, name, rides=()):
    T, F2 = a.shape
    Fh = F2 // 2
    tm = _pick(T, 256, HALO)
    per = tm // HALO
    nb = T // tm
    last_halo = T // HALO - 1
    nch = Fh // LANES
    ext = tm + 2 * HALO

    def body(a_ref, hp_ref, hn_ref, d_ref, dn_ref, cw_ref, cb_ref, da_ref, dcw_ref, dcb_ref):
        i = pl.program_id(0)

        @pl.when(i == 0)
        def _():
            dcw_ref[...] = jnp.zeros_like(dcw_ref)
            dcb_ref[...] = jnp.zeros_like(dcb_ref)

        keep_p = jnp.where(i > 0, 1.0, 0.0)
        keep_n = jnp.where(i < nb - 1, 1.0, 0.0)
        row = lax.broadcasted_iota(jnp.int32, (ext, LANES), 0)
        main = jnp.logical_and(row >= HALO, row < HALO + tm)

        def conv(off):
            e = jnp.concatenate([hp_ref[:, pl.ds(off, LANES)].astype(F32) * keep_p,
                                 a_ref[:, pl.ds(off, LANES)].astype(F32),
                                 hn_ref[:, pl.ds(off, LANES)].astype(F32) * keep_n], axis=0)
            r1 = pltpu.roll(e, 1, 0)
            r2 = pltpu.roll(e, 2, 0)
            w = cw_ref[:, pl.ds(off, LANES)]
            return w, e, r1, r2, w[0:1] * r2 + w[1:2] * r1 + w[2:3] * e + cb_ref[:, pl.ds(off, LANES)]

        def back(off, w, e, r1, r2, dc):
            dcm = jnp.where(main, dc, 0.0)
            for tap, shifted in enumerate((r2, r1, e)):
                dcw_ref[tap:tap + 1, pl.ds(off, LANES)] += _colsum(dcm * shifted)
            dcb_ref[:, pl.ds(off, LANES)] += _colsum(dcm)
            da = w[2:3] * dc + w[1:2] * pltpu.roll(dc, ext - 1, 0) + w[0:1] * pltpu.roll(dc, ext - 2, 0)
            da_ref[:, pl.ds(off, LANES)] = da[HALO:HALO + tm].astype(BF16)

        def chunk(c, carry):
            off = pl.multiple_of(c * LANES, LANES)
            off_u = pl.multiple_of(off + Fh, LANES)
            dh = jnp.concatenate([jnp.zeros((HALO, LANES), F32), d_ref[:, pl.ds(off, LANES)].astype(F32),
                                  dn_ref[:, pl.ds(off, LANES)].astype(F32) * keep_n], axis=0)
            wg, eg, r1g, r2g, gt = conv(off)
            wu, eu, r1u, r2u, upv = conv(off_u)
            sg = 1.0 / (1.0 + jnp.exp(-gt))
            back(off, wg, eg, r1g, r2g, dh * upv * (sg * (1.0 + gt * (1.0 - sg))))
            back(off_u, wu, eu, r1u, r2u, dh * (gt * sg))
            return carry

        lax.fori_loop(0, nch, chunk, 0)

    prev = lambda i: (jnp.maximum(i * per - 1, 0), 0)
    nxt = lambda i: (jnp.minimum((i + 1) * per, last_halo), 0)
    outs, carried = _call(
        body, name=name, grid=(nb,),
        in_specs=[pl.BlockSpec((tm, F2), lambda i: (i, 0)), pl.BlockSpec((HALO, F2), prev),
                  pl.BlockSpec((HALO, F2), nxt), pl.BlockSpec((tm, Fh), lambda i: (i, 0)),
                  pl.BlockSpec((HALO, Fh), nxt),
                  pl.BlockSpec((3, F2), lambda i: (0, 0)), pl.BlockSpec((1, F2), lambda i: (0, 0))],
        out_specs=[pl.BlockSpec((tm, F2), lambda i: (i, 0)), pl.BlockSpec((3, F2), lambda i: (0, 0)),
                   pl.BlockSpec((1, F2), lambda i: (0, 0))],
        out_shape=[jax.ShapeDtypeStruct((T, F2), BF16), jax.ShapeDtypeStruct((3, F2), F32),
                   jax.ShapeDtypeStruct((1, F2), F32)],
        scratch_shapes=[], args=(a, a, a, dhm, dhm, cw, cb), rides=rides)
    return (outs, carried) if rides else outs


def _round_up(n, k):
    return -(-n // k) * k


def _pad_axis(a, axis, size):
    pad = [(0, 0)] * a.ndim
    pad[axis] = (0, size - a.shape[axis])
    return jnp.pad(a, pad)


def _pad_blocks(a, axis, nblk, to):
    shp = a.shape
    blk = shp[axis] // nblk
    a = a.reshape(shp[:axis] + (nblk, blk) + shp[axis + 1:])
    a = _pad_axis(a, axis + 1, to)
    return a.reshape(shp[:axis] + (nblk * to,) + shp[axis + 1:])


def _unpad_blocks(a, axis, nblk, blk):
    shp = a.shape
    to = shp[axis] // nblk
    a = a.reshape(shp[:axis] + (nblk, to) + shp[axis + 1:])
    a = lax.slice_in_dim(a, 0, blk, axis=axis + 1)
    return a.reshape(shp[:axis] + (nblk * blk,) + shp[axis + 1:])


SIB_TILE_BYTES = 2 * 1024 * 1024


def _sib_stream(x, core, name, gather):
    if gather:
        B, R, D = x.shape
        out_shape = (B, 2, R, D)
    else:
        B, _, R, D = x.shape
        out_shape = (B, R, D)
    tr = _pick(R, max(BF16_ROWS, SIB_TILE_BYTES // (D * x.dtype.itemsize)), BF16_ROWS)
    nr = R // tr
    if gather:
        in_spec = pl.BlockSpec((tr, D), lambda b, i, core_ref: (b * nr + i, 0))
    else:
        in_spec = pl.BlockSpec((tr, D), lambda b, i, core_ref: ((2 * b + 1 - core_ref[0]) * nr + i, 0))

    def body(core_ref, x_ref, o_ref, send_sem, recv_sem, local_sem):
        b, i = pl.program_id(0), pl.program_id(1)
        me = lax.axis_index("c")
        sibling = (lax.axis_index("x"), lax.axis_index("y"), 1 - me)
        rows = pl.ds(pl.multiple_of(i * tr, tr), tr)
        dst = o_ref.at[b, me, rows] if gather else o_ref.at[b, rows]
        push = pltpu.make_async_remote_copy(src_ref=x_ref, dst_ref=dst, send_sem=send_sem, recv_sem=recv_sem,
                                            device_id=sibling, device_id_type=MESH)
        push.start()
        if gather:
            keep = pltpu.make_async_copy(x_ref, dst, local_sem)
            keep.start()
            keep.wait()
        push.wait_send()

        @pl.when(jnp.logical_and(b == B - 1, i == nr - 1))
        def _():
            landed = o_ref.at[:, 0] if gather else o_ref
            pltpu.make_async_remote_copy(src_ref=landed, dst_ref=landed, send_sem=send_sem, recv_sem=recv_sem,
                                         device_id=sibling, device_id_type=MESH).wait_recv()

    return pl.pallas_call(
        body, name=name,
        grid_spec=pltpu.PrefetchScalarGridSpec(
            num_scalar_prefetch=1, grid=(B, nr), in_specs=[in_spec],
            out_specs=pl.BlockSpec(memory_space=pl.ANY),
            scratch_shapes=[pltpu.SemaphoreType.DMA(()), pltpu.SemaphoreType.DMA(()), pltpu.SemaphoreType.DMA(())]),
        out_shape=jax.ShapeDtypeStruct(out_shape, x.dtype),
        compiler_params=_params("arbitrary", "arbitrary"),
    )(core, x.reshape(-1, D))


def _pair_sum(g, recv, core, name):
    B, _, R, D = g.shape
    tr = _pick(R, 256, BF16_ROWS)

    def body(core_ref, g_ref, r_ref, o_ref):
        o_ref[...] = (g_ref[...].astype(F32) + r_ref[...].astype(F32)).astype(BF16)

    tile = pl.BlockSpec((None, tr, D), lambda b, i, core_ref: (b, i, 0))
    return pl.pallas_call(
        body, name=name,
        grid_spec=pltpu.PrefetchScalarGridSpec(
            num_scalar_prefetch=1, grid=(B, R // tr),
            in_specs=[pl.BlockSpec((None, None, tr, D), lambda b, i, core_ref: (b, core_ref[0], i, 0)), tile],
            out_specs=tile),
        out_shape=jax.ShapeDtypeStruct((B, R, D), BF16),
        compiler_params=_params("parallel", "parallel"),
    )(core, g, recv)


def _all_reduce_small(v, name):
    pair = _sum_slots(_all_gather(v, ("c",), 0, name + "_ag_c")[None], F32, name + "_sum_c")[0]
    return _sum_slots(_all_gather(pair, ("x", "y"), 0, name + "_ag_xy")[None], F32, name + "_sum_xy")[0]


def _pack(arrs):
    flat = jnp.concatenate([a.reshape(-1) for a in arrs])
    rows = _round_up(-(-flat.shape[0] // SMALL_PACK_COLS), BF16_ROWS)
    return _pad_axis(flat, 0, rows * SMALL_PACK_COLS).reshape(rows, SMALL_PACK_COLS)


def _unpack(buf, like):
    flat = buf.reshape(-1)
    out, off = [], 0
    for a in like:
        out.append(flat[off:off + a.size].reshape(a.shape))
        off += a.size
    return out


def kernel(x, c, mod_w, mod_b, mix_norm_g, ffn_norm_g, attn_w_in, attn_b_f, attn_w_o, gm_w_in, gm_v_g, gm_w_s, gm_b_s, gm_w_o, ffn_w_in, ffn_conv_w, ffn_conv_b, ffn_w_out, final_g, loss_target, m_mod_w, m_mod_b, m_mix_norm_g, m_ffn_norm_g, m_attn_w_in, m_attn_b_f, m_attn_w_o, m_gm_w_in, m_gm_v_g, m_gm_w_s, m_gm_b_s, m_gm_w_o, m_ffn_w_in, m_ffn_conv_w, m_ffn_conv_b, m_ffn_w_out, m_final_g, v_mod_w, v_mod_b, v_mix_norm_g, v_ffn_norm_g, v_attn_w_in, v_attn_b_f, v_attn_w_o, v_gm_w_in, v_gm_v_g, v_gm_w_s, v_gm_b_s, v_gm_w_o, v_ffn_w_in, v_ffn_conv_w, v_ffn_conv_b, v_ffn_w_out, v_final_g):
    _, T, D = x.shape
    L = mod_w.shape[0]
    NA, NB = attn_w_in.shape[0], gm_w_in.shape[0]
    H = D // HEAD_DIM
    NG = D // GROUP
    Fq = ffn_w_out.shape[1]
    Fh = 4 * Fq
    Fqp = _round_up(Fq, LANES)
    Fp = 4 * Fqp
    n_attn = attn_w_in.shape[2]
    n_attn_p = _round_up(n_attn, 2 * BF16_ROWS)
    nmod = mod_w.shape[2]

    xi, yi, ci = lax.axis_index("x"), lax.axis_index("y"), lax.axis_index("c")
    chip = 2 * xi + yi
    example = 2 * chip + ci
    core = ci.astype(jnp.int32).reshape(1)

    halves = {
        "attn_in": _cast_half(_pad_axis(jnp.swapaxes(attn_w_in, 1, 2), 1, n_attn_p), core, "w_attn_in_cast"),
        "attn_o": _cast_half(attn_w_o, core, "w_attn_o_cast"),
        "gm_in": _cast_half(jnp.swapaxes(gm_w_in, 1, 2), core, "w_gm_in_cast"),
        "gm_o": _cast_half(gm_w_o, core, "w_gm_o_cast"),
        "ffn_in": _cast_half(_pad_blocks(jnp.swapaxes(ffn_w_in, 1, 2), 1, 2, Fqp), core, "w_ffn_in_cast"),
        "ffn_out": _cast_half(_pad_axis(ffn_w_out, 1, Fqp), core, "w_ffn_out_cast"),
    }
    weights = {}

    def mixer_kinds(i):
        return [("attn_in", i // 2), ("attn_o", i // 2)] if i % 2 == 0 else [("gm_in", i // 2), ("gm_o", i // 2)]

    def land(kind, l, gathered):
        n2 = gathered.shape[1]
        full = _sib_stream(gathered, core, "w_" + kind + "_ag_c", True).reshape(8 * n2, D)
        if kind == "attn_in":
            full = full.reshape(4, n_attn_p, D)[:, :n_attn].reshape(4 * n_attn, D)
            weights[("qkv", l)] = full[:3 * D]
            weights[("f", l)] = _pad_axis(full[3 * D:], 0, LANES)
        else:
            weights[(kind, l)] = full

    def fetching(host, kinds):
        if not kinds:
            return host(())
        result, carried = host([_Exchange(halves[k], ("x", "y"), 0, True, sel=l) for k, l in kinds])
        for (k, l), g in zip(kinds, carried):
            land(k, l, g)
        return result

    land("attn_in", 0, _all_gather(halves["attn_in"][0], ("x", "y"), 0, "w_attn_in_ag_xy"))
    cw_all = _all_gather(ffn_conv_w, ("x", "y"), 0, "conv_w_ag")
    conv_w_full = jnp.moveaxis(cw_all, 0, 2).reshape(L, 3, 2 * Fh)
    conv_w = _pad_blocks(conv_w_full, 2, 8, Fqp)
    conv_b = _pad_blocks(ffn_conv_b, 1, 8, Fqp)
    vg_all = _all_gather(gm_v_g, ("x", "y"), 0, "gm_vg_ag")
    vg_full = jnp.moveaxis(vg_all, 0, 1).reshape(NB, D)

    c_all = _all_gather(c, MESH_AXES, 0, "c_ag").reshape(8, D)
    mod_b_mine = lax.dynamic_slice_in_dim(mod_b, chip * nmod, nmod, axis=1)
    c_act, mod_part = _mod_fwd(c_all, mod_w, mod_b_mine, "mod_fwd")
    mod_all = _all_gather(mod_part, ("x", "y"), 0, "mod_ag")
    mod = lax.dynamic_index_in_dim(mod_all, example, axis=2, keepdims=False)
    mod = jnp.swapaxes(mod, 0, 1).reshape(L, 6, 1, D)

    xs = x.reshape(T, D)
    target = loss_target.reshape(T, D)
    saved = []
    for i in range(L):
        j = i // 2
        sh1, sc1, g1, sh2, sc2, g2 = (mod[i, k] for k in range(6))
        h1 = _norm_mod(xs, mix_norm_g[i:i + 1], sc1, sh1, "norm_mod")
        nxt = i + 1 < L
        if i % 2 == 0:
            qkv = fetching(lambda r: _matmul(h1, weights[("qkv", j)], "nt", "attn_qkv", tm=1024, tn=1024, rides=r),
                           [("attn_o", j)] if i == 0 else [])
            flog = _matmul(h1, weights[("f", j)], "nt", "attn_f", out_dtype=F32, tm=1024)
            bias = _pad_axis(attn_b_f[j:j + 1], 1, LANES)
            fcol = _fox_gate(flog, bias, "fox_gate")
            frow = jnp.transpose(fcol)[:H].reshape(H, 1, T)
            o, lse = fetching(lambda r: _attn_fwd(qkv, frow, H, "attn_fwd", rides=r),
                              [("ffn_in", i), ("ffn_out", i)] + (mixer_kinds(i + 1) if nxt else []))
            y, x1 = _matmul(o, weights[("attn_o", j)], "nn", "attn_out", out_dtype=F32, tm=1024, tn=1024,
                            res=xs, gate=g1, emit_y=True)
            mix = ("fox", qkv, flog, bias, frow, o, lse)
            ride_in = [("ffn_in", i + 1)] if nxt else []
            ride_out = [("ffn_out", i + 1)] if nxt else []
        else:
            p = _matmul(h1, weights[("gm_in", j)], "nt", "gm_in", tm=1024, tn=1024)
            bt = _pad_axis(jnp.transpose(gm_b_s[j]), 1, LANES)
            gated = _gm_fwd(p, vg_full[j:j + 1], gm_w_s[j], bt, "gm_fwd")
            y, x1 = _matmul(gated, weights[("gm_o", j)], "nn", "gm_out", out_dtype=F32, tm=1024, tn=1024,
                            res=xs, gate=g1, emit_y=True)
            mix = ("gm", p, bt, gated)
            ride_in = mixer_kinds(i + 1) if nxt else []
            ride_out = []
        h2 = _norm_mod(x1, ffn_norm_g[i:i + 1], sc2, sh2, "norm_mod")
        a = fetching(lambda r: _matmul(h2, weights[("ffn_in", i)], "nt", "ffn_in", tm=1024, tn=1024, rides=r), ride_in)
        hmid = _ffn_act(a, conv_w[i], conv_b[i:i + 1], "ffn_act")
        y2, x2 = fetching(lambda r: _matmul(hmid, weights[("ffn_out", i)], "nn", "ffn_out", out_dtype=F32, tm=512,
                                            tn=1024, res=x1, gate=g2, emit_y=True, rides=r), ride_out)
        saved.append((xs, h1, mix, y, x1, h2, a, hmid, y2))
        xs = x2

    loss_blk, dx, g_final, dy2, dg2 = _final_loss_bwd(
        xs, final_g.reshape(1, D), target, saved[L - 1][8], mod[L - 1, 5], "final_loss_bwd")
    g_mix, g_ffn_n, g_cw, g_cb, dmod = [None] * L, [None] * L, [None] * L, [None] * L, [None] * L
    g_bf, g_vg, g_ws, g_bs = [None] * NA, [None] * NB, [None] * NB, [None] * NB
    reduced = {}

    def pair(kind, gw):
        g = gw.reshape(4, 2, gw.shape[0] // 8, D)
        return _pair_sum(g, _sib_stream(g, core, "g_" + kind + "_swap_c", False), core, "g_" + kind + "_sum_c")

    def sending(host, items):
        if not items:
            return host(())
        result, carried = host([_Exchange(ps, ("x", "y"), 0, False) for _, ps in items])
        for (key, _), got in zip(items, carried):
            reduced[key] = got
        return result

    late_act, late_dx = [], []
    for i in reversed(range(L)):
        j = i // 2
        xin, h1, mix, y, x1, h2, a, hmid, y2 = saved[i]
        sh1, sc1, g1, sh2, sc2, g2 = (mod[i, k] for k in range(6))
        dhm = _matmul(dy2, weights[("ffn_out", i)], "nt", "ffn_out_dx", tm=1024, tn=512)
        p_out = pair("ffn_out", _matmul(hmid, dy2, "tn", "ffn_out_dw", tm=512, tn=1024))
        da, dcw, dcb = sending(lambda r: _ffn_act_bwd(a, dhm, conv_w[i], conv_b[i:i + 1], "ffn_act_bwd", rides=r),
                               [(("ffn_out", i), p_out)] + late_act)
        dh2 = sending(lambda r: _matmul(da, weights[("ffn_in", i)], "nn", "ffn_in_dx", out_dtype=F32, tm=512,
                                        tn=1024, tk=2816, rides=r), late_dx)
        late_act, late_dx = [], []
        p_in = pair("ffn_in", _matmul(da, h2, "tn", "ffn_in_dw", tm=512, tn=1024))
        g_cw[i] = _unpad_blocks(dcw, 1, 8, Fq)
        g_cb[i] = _unpad_blocks(dcb, 1, 8, Fq)[0]
        dx1, dsc2, dsh2, dgn, dy, dg1 = _norm_bwd(
            x1, ffn_norm_g[i:i + 1], sc2, sh2, dh2, dx, (y, g1), "norm_bwd_prev")
        g_ffn_n[i] = dgn[0]
        if mix[0] == "fox":
            _, qkv, flog, bias, frow, o, lse = mix
            do = _matmul(dy, weights[("attn_o", j)], "nt", "attn_out_dx", tm=1024, tn=1024)
            p_o = pair("attn_o", _matmul(o, dy, "tn", "attn_out_dw", tm=512, tn=1024))
            dq, dfq = _attn_dq(qkv, o, do, lse, frow, H, "attn_dq")
            dk, dv, dfk = sending(lambda r: _attn_dkv(qkv, o, do, lse, frow, H, "attn_dkv", rides=r),
                                  [(("ffn_in", i), p_in)])
            heads_on_lanes = lambda t: _pad_axis(jnp.transpose(t.reshape(H, T)), 1, LANES)
            dflog, dbf = _fox_gate_bwd(heads_on_lanes(dfq), heads_on_lanes(dfk), flog, bias, "fox_gate_bwd")
            dproj = jnp.concatenate([dq, dk, dv], axis=1)
            dh_f = _matmul(dflog, weights[("f", j)], "nn", "attn_f_dx", out_dtype=F32, tm=1024, tn=1024)
            dh1 = _matmul(dproj, weights[("qkv", j)], "nn", "attn_qkv_dx", out_dtype=F32, tm=512, tn=1024,
                          tk=2048, res=dh_f)
            gwt_qkv = _matmul(dproj, h1, "tn", "attn_qkv_dw", tm=512, tn=1024)
            gwt_f = _matmul(dflog, h1, "tn", "attn_f_dw", tm=128, tn=1024)
            gwt = jnp.concatenate([gwt_qkv, gwt_f[:H]], axis=0).reshape(4, n_attn, D)
            p_a = pair("attn_in", _pad_axis(gwt, 1, n_attn_p).reshape(4 * n_attn_p, D))
            late_dx = [(("attn_in", j), p_a), (("attn_o", j), p_o)]
            g_bf[j] = dbf[0, :H]
        else:
            _, p, bt, gated = mix
            late_act = [(("ffn_in", i), p_in)]
            dgated = _matmul(dy, weights[("gm_o", j)], "nt", "gm_out_dx", tm=1024, tn=1024)
            p_o = pair("gm_o", _matmul(gated, dy, "tn", "gm_out_dw", tm=512, tn=1024))
            dp, dws, dbt, dvg = _gm_bwd(p, dgated, vg_full[j:j + 1], gm_w_s[j],
                                        jnp.swapaxes(gm_w_s[j], 1, 2), bt, "gm_bwd")
            dh1 = _matmul(dp, weights[("gm_in", j)], "nn", "gm_in_dx", out_dtype=F32, tm=512, tn=1024, tk=2048)
            p_a = pair("gm_in", _matmul(dp, h1, "tn", "gm_in_dw", tm=512, tn=1024))
            late_dx = [(("gm_in", j), p_a), (("gm_o", j), p_o)]
            g_ws[j] = dws
            g_bs[j] = jnp.transpose(dbt)[:NG]
            g_vg[j] = dvg[0]
        if i > 0:
            dx, dsc1, dsh1, dgn, dy2, dg2_prev = _norm_bwd(
                xin, mix_norm_g[i:i + 1], sc1, sh1, dh1, dx1, (saved[i - 1][8], mod[i - 1, 5]), "norm_bwd_prev")
        else:
            dx, dsc1, dsh1, dgn = _norm_bwd(xin, mix_norm_g[i:i + 1], sc1, sh1, dh1, dx1, None, "norm_bwd_first")
            dg2_prev = None
        g_mix[i] = dgn[0]
        dmod[i] = jnp.concatenate([dsh1, dsc1, dg1, dsh2, dsc2, dg2], axis=1)[0]
        dg2 = dg2_prev
    grad_x = dx.reshape(1, T, D)
    loss = lax.psum(loss_blk[0, 0], MESH_AXES)

    small =[jnp.stack(g_mix), jnp.stack(g_ffn_n), jnp.stack(g_bf), jnp.stack(g_vg), jnp.stack(g_ws),
             jnp.stack(g_bs), jnp.stack(g_cw), jnp.stack(g_cb), g_final[0]]
    (grad_mix_norm_g, grad_ffn_norm_g, grad_attn_b_f, g_vg_full, grad_gm_w_s, grad_gm_b_s, g_cw_full,
     grad_ffn_conv_b, grad_final_g) = _unpack(_all_reduce_small(_pack(small), "g_small"), small)
    grad_gm_v_g = lax.dynamic_slice_in_dim(g_vg_full, chip * (D // 4), D // 4, axis=1)
    grad_ffn_conv_w = lax.dynamic_slice_in_dim(g_cw_full, chip * 2 * Fq, 2 * Fq, axis=2)

    dmod_all = _all_gather(jnp.stack(dmod), MESH_AXES, 0, "dmod_ag")
    grad_mod_b = _sum_slots(dmod_all[None], F32, "dmod_sum")[0]
    dmod_cols = jnp.swapaxes(lax.dynamic_slice_in_dim(dmod_all, chip * nmod, nmod, axis=2), 0, 1)

    grad_mod_w, delta_mod_w, new_m_mod_w, new_v_mod_w = sending(
        lambda r: _modw_adamw(jnp.transpose(c_act), dmod_cols, mod_w, m_mod_w, v_mod_w, "mod_w_adamw", rides=r),
        late_act + late_dx)

    def total(kind, n_layers):
        got = jnp.stack([reduced[(kind, l)] for l in range(n_layers)])
        tot = _sum_slots(got, F32, "g_" + kind + "_sum_xy")
        return _sib_stream(tot, core, "g_" + kind + "_ag_c", True).reshape(n_layers, -1, D)

    grad_attn_w_in = jnp.swapaxes(total("attn_in", NA)[:, :n_attn], 1, 2)
    grad_attn_w_o = total("attn_o", NA)
    grad_gm_w_in = jnp.swapaxes(total("gm_in", NB), 1, 2)
    grad_gm_w_o = total("gm_o", NB)
    grad_ffn_w_in = jnp.swapaxes(_unpad_blocks(total("ffn_in", L), 1, 2, Fq), 1, 2)
    grad_ffn_w_out = total("ffn_out", L)[:, :Fq]

    def big(w, g, m, v, name):
        shp = w.shape
        flat = lambda t: t.reshape(-1, shp[-1])
        return [t.reshape(shp) for t in _adamw(flat(w), flat(g), flat(m), flat(v), name)]

    upd_attn_in = big(attn_w_in, grad_attn_w_in, m_attn_w_in, v_attn_w_in, "adamw_attn_in")
    upd_attn_o = big(attn_w_o, grad_attn_w_o, m_attn_w_o, v_attn_w_o, "adamw_attn_o")
    upd_gm_in = big(gm_w_in, grad_gm_w_in, m_gm_w_in, v_gm_w_in, "adamw_gm_in")
    upd_gm_o = big(gm_w_o, grad_gm_w_o, m_gm_w_o, v_gm_w_o, "adamw_gm_o")
    upd_ffn_in = big(ffn_w_in, grad_ffn_w_in, m_ffn_w_in, v_ffn_w_in, "adamw_ffn_in")
    upd_ffn_out = big(ffn_w_out, grad_ffn_w_out, m_ffn_w_out, v_ffn_w_out, "adamw_ffn_out")

    sm_w = [mod_b, mix_norm_g, ffn_norm_g, attn_b_f, gm_v_g, gm_w_s, gm_b_s, ffn_conv_w, ffn_conv_b, final_g]
    sm_g = [grad_mod_b, grad_mix_norm_g, grad_ffn_norm_g, grad_attn_b_f, grad_gm_v_g, grad_gm_w_s, grad_gm_b_s,
            grad_ffn_conv_w, grad_ffn_conv_b, grad_final_g]
    sm_m = [m_mod_b, m_mix_norm_g, m_ffn_norm_g, m_attn_b_f, m_gm_v_g, m_gm_w_s, m_gm_b_s, m_ffn_conv_w,
            m_ffn_conv_b, m_final_g]
    sm_v = [v_mod_b, v_mix_norm_g, v_ffn_norm_g, v_attn_b_f, v_gm_v_g, v_gm_w_s, v_gm_b_s, v_ffn_conv_w,
            v_ffn_conv_b, v_final_g]
    sm_d, sm_nm, sm_nv = (_unpack(t, sm_w) for t in _adamw(_pack(sm_w), _pack(sm_g), _pack(sm_m), _pack(sm_v),
                                                             "adamw_small"))

    def ordered(mod_w_item, sm, k):
        return [mod_w_item, sm[0], sm[1], sm[2], upd_attn_in[k], sm[3], upd_attn_o[k], upd_gm_in[k], sm[4], sm[5],
                sm[6], upd_gm_o[k], upd_ffn_in[k], sm[7], sm[8], upd_ffn_out[k], sm[9]]

    grads = [grad_mod_w, grad_mod_b, grad_mix_norm_g, grad_ffn_norm_g, grad_attn_w_in, grad_attn_b_f, grad_attn_w_o,
             grad_gm_w_in, grad_gm_v_g, grad_gm_w_s, grad_gm_b_s, grad_gm_w_o, grad_ffn_w_in, grad_ffn_conv_w,
             grad_ffn_conv_b, grad_ffn_w_out, grad_final_g]
    return (loss, grad_x, *grads, *ordered(delta_mod_w, sm_d, 0), *ordered(new_m_mod_w, sm_nm, 1),
            *ordered(new_v_mod_w, sm_nv, 2))
```

```python
import functools
import math

import jax
import jax.numpy as jnp
from jax import lax
from jax.experimental import pallas as pl
from jax.experimental.pallas import tpu as pltpu

F32 = jnp.float32
BF16 = jnp.bfloat16

EPS = 1e-6
HEAD_DIM = 128
CHUNK = 128
GROUP = 128
LANES = 128
BF16_ROWS = 16
VMEM_LIMIT_BYTES = 52 * 1024 * 1024
NEG_BIG = -1e30
SMALL_PACK_COLS = 1024

ADAM_LR, ADAM_B1, ADAM_B2, ADAM_EPS, ADAM_WD, ADAM_STEP = 0.001, 0.9, 0.999, 1e-08, 0.01, 10

MESH_AXES = ("x", "y", "c")
MESH = pl.DeviceIdType.MESH


def _pick(n, pref, align):
    t = (min(pref, n) // align) * align
    while t >= align:
        if n % t == 0:
            return t
        t -= align
    return n


def _params(*sem):
    return pltpu.CompilerParams(dimension_semantics=sem, vmem_limit_bytes=VMEM_LIMIT_BYTES)


def _group(axes):
    pos = {ax: lax.axis_index(ax) for ax in MESH_AXES}
    rank = 0
    for ax in axes:
        rank = rank * 2 + pos[ax]
    peers = []
    for mask in range(1, 2 ** len(axes)):
        peer = dict(pos)
        for bit, ax in enumerate(reversed(axes)):
            if (mask >> bit) & 1:
                peer[ax] = 1 - pos[ax]
        peers.append((mask, tuple(peer[ax] for ax in MESH_AXES)))
    return rank, peers


def _slot(ref, gaxis, idx):
    return ref.at[(slice(None),) * gaxis + (idx,)]


D2D_STREAMS = 32
ICI_STREAMS = 8


def _pieces(shape, want):
    idx = [()]
    d = 0
    while d < len(shape) - 2 and len(idx) * shape[d] <= want:
        idx = [i + (k,) for i in idx for k in range(shape[d])]
        d += 1
    if d == len(shape) - 2:
        rows = shape[-2]
        r = max(1, want // len(idx))
        while r > 1 and not (rows % r == 0 and (rows // r) % BF16_ROWS == 0):
            r -= 1
        if r > 1:
            idx = [i + (pl.ds(k * (rows // r), rows // r),) for i in idx for k in range(r)]
    return idx


class _Exchange:
    def __init__(self, x, axes, gaxis, gather, sel=None):
        self.x, self.axes, self.gaxis, self.gather, self.sel = x, axes, gaxis, gather, sel
        n = 2 ** len(axes)
        shape = x.shape if sel is None else x.shape[1:]
        if gather:
            out_shape = shape[:gaxis] + (n,) + shape[gaxis:]
            slab = shape
        else:
            assert shape[gaxis] == n
            out_shape = shape
            slab = shape[:gaxis] + shape[gaxis + 1:]
        self.out_shape = jax.ShapeDtypeStruct(out_shape, x.dtype)
        self.pieces = _pieces(slab, D2D_STREAMS if axes == ("c",) else ICI_STREAMS)
        self.scratch = [pltpu.SemaphoreType.DMA((n - 1,)), pltpu.SemaphoreType.DMA((n - 1,)),
                        pltpu.SemaphoreType.DMA(())]

    def _views(self, x_ref, o_ref):
        if self.sel is not None:
            x_ref = x_ref.at[self.sel]
        src = (lambda member: x_ref) if self.gather else (lambda member: _slot(x_ref, self.gaxis, member))
        dst = lambda member: _slot(o_ref, self.gaxis, member)
        return src, dst

    @staticmethod
    def _remote(mask, peer, s, d, send_sems, recv_sems):
        return pltpu.make_async_remote_copy(
            src_ref=s, dst_ref=d, send_sem=send_sems.at[mask - 1], recv_sem=recv_sems.at[mask - 1],
            device_id=peer, device_id_type=MESH)

    def start(self, x_ref, o_ref, send_sems, recv_sems, local_sem):
        rank, peers = _group(self.axes)
        src, dst = self._views(x_ref, o_ref)
        cut = lambda ref, pc: ref.at[pc] if pc else ref
        for pc in self.pieces:
            pltpu.make_async_copy(cut(src(rank), pc), cut(dst(rank), pc), local_sem).start()
            for mask, peer in peers:
                self._remote(mask, peer, cut(src(rank ^ mask), pc), cut(dst(rank), pc), send_sems, recv_sems).start()

    def finish(self, x_ref, o_ref, send_sems, recv_sems, local_sem):
        rank, peers = _group(self.axes)
        src, dst = self._views(x_ref, o_ref)
        for mask, peer in peers:
            whole = self._remote(mask, peer, src(rank), dst(rank ^ mask), send_sems, recv_sems)
            whole.wait_recv()
            whole.wait_send()
        pltpu.make_async_copy(src(rank), dst(rank), local_sem).wait()


ANY_SPEC = pl.BlockSpec(memory_space=pl.ANY)


def _call(body, *, name, grid, in_specs, out_specs, out_shape, scratch_shapes, args, rides=()):
    out_specs, out_shape = list(out_specs), list(out_shape)
    n_in, n_out, n_s, nr = len(in_specs), len(out_specs), len(scratch_shapes), len(rides)
    if not rides:
        sem = ("arbitrary",) * len(grid)
        outs = pl.pallas_call(body, name=name, grid=grid, in_specs=in_specs, out_specs=out_specs,
                              out_shape=out_shape, scratch_shapes=scratch_shapes,
                              compiler_params=_params(*sem))(*args)
        return list(outs), []

    def carried(*refs):
        k = 0
        parts = []
        for cnt in (n_in, nr, n_out, nr, n_s, 3 * nr):
            parts.append(refs[k:k + cnt])
            k += cnt
        h_in, r_in, h_out, r_out, h_s, r_s = parts
        ids = [pl.program_id(d) for d in range(len(grid))]
        first = functools.reduce(jnp.logical_and, [i == 0 for i in ids])
        last = functools.reduce(jnp.logical_and, [i == g - 1 for i, g in zip(ids, grid)])

        @pl.when(first)
        def _():
            for r, ride in enumerate(rides):
                ride.start(r_in[r], r_out[r], *r_s[3 * r:3 * r + 3])

        body(*h_in, *h_out, *h_s)

        @pl.when(last)
        def _():
            for r, ride in enumerate(rides):
                ride.finish(r_in[r], r_out[r], *r_s[3 * r:3 * r + 3])

    outs = pl.pallas_call(
        carried, name=name, grid=grid,
        in_specs=list(in_specs) + [ANY_SPEC] * nr,
        out_specs=out_specs + [ANY_SPEC] * nr,
        out_shape=out_shape + [r.out_shape for r in rides],
        scratch_shapes=list(scratch_shapes) + [s for r in rides for s in r.scratch],
        compiler_params=_params(*(("arbitrary",) * len(grid))),
    )(*args, *[r.x for r in rides])
    return list(outs[:n_out]), list(outs[n_out:])


def _exchange(x, axes, gaxis, name, gather):
    ex = _Exchange(x, axes, gaxis, gather)

    def body(x_ref, o_ref, send_sems, recv_sems, local_sem):
        ex.start(x_ref, o_ref, send_sems, recv_sems, local_sem)
        ex.finish(x_ref, o_ref, send_sems, recv_sems, local_sem)

    return pl.pallas_call(body, name=name, out_shape=ex.out_shape, in_specs=[ANY_SPEC], out_specs=ANY_SPEC,
                          scratch_shapes=ex.scratch)(x)


def _all_gather(x, axes, gaxis, name):
    return _exchange(x, axes, gaxis, name, True)


def _all_to_all(x, axes, gaxis, name):
    return _exchange(x, axes, gaxis, name, False)


def _cast_half(w, core, name):
    L, n, D = w.shape
    n2 = n // 2
    tr = _pick(n2, 256, BF16_ROWS)
    nb = n2 // tr

    def body(core_ref, w_ref, o_ref):
        o_ref[...] = w_ref[...].astype(BF16)

    return pl.pallas_call(
        body, name=name,
        grid_spec=pltpu.PrefetchScalarGridSpec(
            num_scalar_prefetch=1, grid=(L, nb),
            in_specs=[pl.BlockSpec((None, tr, D), lambda l, i, core_ref: (l, core_ref[0] * nb + i, 0))],
            out_specs=pl.BlockSpec((None, tr, D), lambda l, i, core_ref: (l, i, 0))),
        out_shape=jax.ShapeDtypeStruct((L, n2, D), BF16),
        compiler_params=_params("parallel", "parallel"),
    )(core, w)


def _sum_slots(x, out_dtype, name):
    A, G, R, C = x.shape
    budget = (2 * 1024 * 1024) // (G * C * x.dtype.itemsize)
    tr = _pick(R, max(budget, BF16_ROWS), BF16_ROWS)

    def body(x_ref, o_ref):
        acc = x_ref[0].astype(F32)
        for g in range(1, G):
            acc = acc + x_ref[g].astype(F32)
        o_ref[...] = acc.astype(out_dtype)

    return pl.pallas_call(
        body, name=name, grid=(A, R // tr),
        in_specs=[pl.BlockSpec((None, G, tr, C), lambda a, i: (a, 0, i, 0))],
        out_specs=pl.BlockSpec((None, tr, C), lambda a, i: (a, i, 0)),
        out_shape=jax.ShapeDtypeStruct((A, R, C), out_dtype),
        compiler_params=_params("parallel", "parallel"),
    )(x)


def _adamw_math(w, g, m, v):
    m = ADAM_B1 * m + (1.0 - ADAM_B1) * g
    v = ADAM_B2 * v + (1.0 - ADAM_B2) * (g * g)
    m_hat = m / (1.0 - ADAM_B1 ** ADAM_STEP)
    v_hat = v / (1.0 - ADAM_B2 ** ADAM_STEP)
    delta = -ADAM_LR * (m_hat / (jnp.sqrt(v_hat) + ADAM_EPS) + ADAM_WD * w)
    return delta, m, v


def _adamw(w, g, m, v, name):
    R, C = w.shape
    budget = (768 * 1024) // (C * 4)
    tr = _pick(R, max(budget, 8), 8)

    def body(w_ref, g_ref, m_ref, v_ref, d_ref, nm_ref, nv_ref):
        d, nm, nv = _adamw_math(w_ref[...], g_ref[...], m_ref[...], v_ref[...])
        d_ref[...] = d
        nm_ref[...] = nm
        nv_ref[...] = nv

    spec = pl.BlockSpec((tr, C), lambda i: (i, 0))
    return pl.pallas_call(
        body, name=name, grid=(R // tr,),
        in_specs=[spec] * 4, out_specs=[spec] * 3,
        out_shape=[jax.ShapeDtypeStruct((R, C), F32)] * 3,
        compiler_params=_params("parallel"),
    )(w, g, m, v)


def _modw_adamw(ct, dm, w, m, v, name, rides=()):
    L, D, N = w.shape
    E = ct.shape[1]
    tr = _pick(D, 64, 8)

    def body(ct_ref, dm_ref, w_ref, m_ref, v_ref, g_ref, d_ref, nm_ref, nv_ref):
        g = ct_ref[:, 0:1] * dm_ref[0:1, :]
        for e in range(1, E):
            g = g + ct_ref[:, e:e + 1] * dm_ref[e:e + 1, :]
        d, nm, nv = _adamw_math(w_ref[...], g, m_ref[...], v_ref[...])
        g_ref[...] = g
        d_ref[...] = d
        nm_ref[...] = nm
        nv_ref[...] = nv

    spec = pl.BlockSpec((None, tr, N), lambda l, i: (l, i, 0))
    outs, carried = _call(
        body, name=name, grid=(L, D // tr),
        in_specs=[pl.BlockSpec((tr, E), lambda l, i: (i, 0)),
                  pl.BlockSpec((None, E, N), lambda l, i: (l, 0, 0)), spec, spec, spec],
        out_specs=[spec] * 4,
        out_shape=[jax.ShapeDtypeStruct((L, D, N), F32)] * 4,
        scratch_shapes=[], args=(ct, dm, w, m, v), rides=rides)
    return (outs, carried) if rides else outs


def _mod_fwd(c_all, w, b, name):
    L, D, N = w.shape
    E = c_all.shape[0]
    tn = _pick(N, 512, LANES)

    def body(c_ref, w_ref, b_ref, act_ref, o_ref):
        cv = c_ref[...]
        act = cv / (1.0 + jnp.exp(-cv))
        act_ref[...] = act
        o_ref[...] = jnp.dot(act.astype(BF16), w_ref[...].astype(BF16),
                             preferred_element_type=F32) + b_ref[...]

    return pl.pallas_call(
        body, name=name, grid=(L, N // tn),
        in_specs=[pl.BlockSpec((E, D), lambda l, j: (0, 0)),
                  pl.BlockSpec((None, D, tn), lambda l, j: (l, 0, j)),
                  pl.BlockSpec((None, 1, tn), lambda l, j: (l, 0, j))],
        out_specs=[pl.BlockSpec((E, D), lambda l, j: (0, 0)),
                   pl.BlockSpec((None, E, tn), lambda l, j: (l, 0, j))],
        out_shape=[jax.ShapeDtypeStruct((E, D), F32), jax.ShapeDtypeStruct((L, E, N), F32)],
        compiler_params=_params("arbitrary", "arbitrary"),
    )(c_all, w, b.reshape(L, 1, N))


_DIMS = {"nn": (((1,), (0,)), ((), ())), "nt": (((1,), (1,)), ((), ())), "tn": (((0,), (0,)), ((), ()))}


def _matmul(a, b, form, name, *, out_dtype=BF16, tm=512, tn=512, tk=None, a_resident=False,
            res=None, gate=None, emit_y=False, rides=()):
    if form == "nn":
        (M, K), N = a.shape, b.shape[1]
    elif form == "nt":
        (M, K), N = a.shape, b.shape[0]
    else:
        (K, M), N = a.shape, b.shape[1]
    tm = _pick(M, tm, LANES if form == "tn" else BF16_ROWS)
    tn = _pick(N, tn, LANES)
    tk = K if (tk is None or form == "tn") else _pick(K, tk, LANES)
    nk = K // tk

    def ij(g0, g1):
        return (g0, g1) if a_resident else (g1, g0)

    grid = (M // tm, N // tn, nk) if a_resident else (N // tn, M // tm, nk)
    if form == "tn":
        a_spec = pl.BlockSpec((K, tm), lambda g0, g1, k: (0, ij(g0, g1)[0]))
        b_spec = pl.BlockSpec((K, tn), lambda g0, g1, k: (0, ij(g0, g1)[1]))
    else:
        a_spec = pl.BlockSpec((tm, tk), lambda g0, g1, k: (ij(g0, g1)[0], k))
        if form == "nn":
            b_spec = pl.BlockSpec((tk, tn), lambda g0, g1, k: (k, ij(g0, g1)[1]))
        else:
            b_spec = pl.BlockSpec((tn, tk), lambda g0, g1, k: (ij(g0, g1)[1], k))
    tile = pl.BlockSpec((tm, tn), lambda g0, g1, k: ij(g0, g1))
    row = pl.BlockSpec((1, tn), lambda g0, g1, k: (0, ij(g0, g1)[1]))
    in_specs, args = [a_spec, b_spec], [a, b]
    if res is not None:
        in_specs.append(tile)
        args.append(res)
    if gate is not None:
        in_specs.append(row)
        args.append(gate)
    if emit_y:
        out_specs = [tile, tile]
        out_shape = [jax.ShapeDtypeStruct((M, N), BF16), jax.ShapeDtypeStruct((M, N), out_dtype)]
    else:
        out_specs = tile
        out_shape = jax.ShapeDtypeStruct((M, N), out_dtype)
    n_in = len(args)

    def body(*refs):
        a_ref, b_ref = refs[0], refs[1]
        res_ref = refs[2] if res is not None else None
        gate_ref = refs[n_in - 1] if gate is not None else None
        outs = refs[n_in:n_in + (2 if emit_y else 1)]

        def finish(acc):
            val = acc
            if gate_ref is not None:
                val = gate_ref[...] * val
            if res_ref is not None:
                val = res_ref[...] + val
            if emit_y:
                outs[0][...] = acc.astype(BF16)
            outs[-1][...] = val.astype(out_dtype)

        part = lax.dot_general(a_ref[...], b_ref[...], _DIMS[form], preferred_element_type=F32)
        if nk == 1:
            finish(part)
        else:
            acc_ref = refs[-1]
            k = pl.program_id(2)

            @pl.when(k == 0)
            def _():
                acc_ref[...] = part

            @pl.when(k > 0)
            def _():
                acc_ref[...] += part

            @pl.when(k == nk - 1)
            def _():
                finish(acc_ref[...])

    outs, carried = _call(
        body, name=name, grid=grid, in_specs=in_specs, out_specs=out_specs if emit_y else [out_specs],
        out_shape=out_shape if emit_y else [out_shape],
        scratch_shapes=[pltpu.VMEM((tm, tn), F32)] if nk > 1 else [], args=args, rides=rides)
    result = outs if emit_y else outs[0]
    return (result, carried) if rides else result


def _norm_mod(x, g, sc, sh, name):
    T, D = x.shape
    tm = _pick(T, 256, BF16_ROWS)

    def body(x_ref, g_ref, sc_ref, sh_ref, h_ref):
        xv = x_ref[...]
        rstd = lax.rsqrt(jnp.mean(xv * xv, axis=-1, keepdims=True) + EPS)
        y = xv * rstd * g_ref[...]
        h_ref[...] = (y * (1.0 + sc_ref[...]) + sh_ref[...]).astype(BF16)

    tile = pl.BlockSpec((tm, D), lambda i: (i, 0))
    row = pl.BlockSpec((1, D), lambda i: (0, 0))
    return pl.pallas_call(
        body, name=name, grid=(T // tm,), in_specs=[tile, row, row, row], out_specs=tile,
        out_shape=jax.ShapeDtypeStruct((T, D), BF16), compiler_params=_params("parallel"),
    )(x, g, sc, sh)


def _colsum(v):
    return jnp.sum(v, axis=0, keepdims=True)


def _norm_bwd(x, g, sc, sh, dh, dres, prev, name):
    T, D = x.shape
    tm = _pick(T, 256, BF16_ROWS)
    has_prev = prev is not None

    def body(*refs):
        x_ref, g_ref, sc_ref, sh_ref, dh_ref, dres_ref = refs[:6]
        k = 6
        if has_prev:
            y_ref, gate_ref = refs[6:8]
            k = 8
        dx_ref, dsc_ref, dsh_ref, dg_ref = refs[k:k + 4]
        first = pl.program_id(0) == 0
        xv = x_ref[...]
        rstd = lax.rsqrt(jnp.mean(xv * xv, axis=-1, keepdims=True) + EPS)
        xhat = xv * rstd
        gv = g_ref[...]
        dhv = dh_ref[...].astype(F32)
        dyv = dhv * (1.0 + sc_ref[...])
        dxhat = dyv * gv
        dx = rstd * (dxhat - xhat * jnp.mean(dxhat * xhat, axis=-1, keepdims=True))
        dxt = dres_ref[...] + dx
        dx_ref[...] = dxt
        sums = [(dsc_ref, _colsum(dhv * (xhat * gv))), (dsh_ref, _colsum(dhv)), (dg_ref, _colsum(dyv * xhat))]
        if has_prev:
            dy_ref, dgate_ref = refs[k + 4:k + 6]
            dy_ref[...] = (dxt * gate_ref[...]).astype(BF16)
            sums.append((dgate_ref, _colsum(dxt * y_ref[...].astype(F32))))
        for ref, val in sums:
            @pl.when(first)
            def _(ref=ref, val=val):
                ref[...] = val

            @pl.when(jnp.logical_not(first))
            def _(ref=ref, val=val):
                ref[...] += val

    tile = pl.BlockSpec((tm, D), lambda i: (i, 0))
    row = pl.BlockSpec((1, D), lambda i: (0, 0))
    rowshape = jax.ShapeDtypeStruct((1, D), F32)
    in_specs = [tile, row, row, row, tile, tile]
    args = [x, g, sc, sh, dh, dres]
    out_specs = [tile, row, row, row]
    out_shape = [jax.ShapeDtypeStruct((T, D), F32), rowshape, rowshape, rowshape]
    if has_prev:
        in_specs += [tile, row]
        args += list(prev)
        out_specs += [tile, row]
        out_shape += [jax.ShapeDtypeStruct((T, D), BF16), rowshape]
    return pl.pallas_call(
        body, name=name, grid=(T // tm,), in_specs=in_specs, out_specs=out_specs, out_shape=out_shape,
        compiler_params=_params("arbitrary"),
    )(*args)


def _final_loss_bwd(x, g, target, y_prev, gate_prev, name):
    T, D = x.shape
    tm = _pick(T, 256, BF16_ROWS)

    def body(x_ref, g_ref, t_ref, y_ref, gate_ref, loss_ref, dx_ref, dg_ref, dy_ref, dgate_ref):
        first = pl.program_id(0) == 0
        xv = x_ref[...]
        rstd = lax.rsqrt(jnp.mean(xv * xv, axis=-1, keepdims=True) + EPS)
        xhat = xv * rstd
        gv = g_ref[...]
        err = xhat * gv - t_ref[...]
        part = 0.5 * jnp.sum(jnp.mean(err * err, axis=-1, keepdims=True))
        dyf = err * (1.0 / D)
        dxhat = dyf * gv
        dx = rstd * (dxhat - xhat * jnp.mean(dxhat * xhat, axis=-1, keepdims=True))
        dx_ref[...] = dx
        dy_ref[...] = (dx * gate_ref[...]).astype(BF16)
        sums = [(dg_ref, _colsum(dyf * xhat)), (dgate_ref, _colsum(dx * y_ref[...].astype(F32))),
                (loss_ref, jnp.full((8, LANES), part, F32))]
        for ref, val in sums:
            @pl.when(first)
            def _(ref=ref, val=val):
                ref[...] = val

            @pl.when(jnp.logical_not(first))
            def _(ref=ref, val=val):
                ref[...] += val

    tile = pl.BlockSpec((tm, D), lambda i: (i, 0))
    row = pl.BlockSpec((1, D), lambda i: (0, 0))
    rowshape = jax.ShapeDtypeStruct((1, D), F32)
    return pl.pallas_call(
        body, name=name, grid=(T // tm,),
        in_specs=[tile, row, tile, tile, row],
        out_specs=[pl.BlockSpec((8, LANES), lambda i: (0, 0)), tile, row, tile, row],
        out_shape=[jax.ShapeDtypeStruct((8, LANES), F32), jax.ShapeDtypeStruct((T, D), F32), rowshape,
                   jax.ShapeDtypeStruct((T, D), BF16), rowshape],
        compiler_params=_params("arbitrary"),
    )(x, g, target, y_prev, gate_prev)


def _split3(v):
    hi = v.astype(BF16)
    r1 = v - hi.astype(F32)
    mid = r1.astype(BF16)
    lo = (r1 - mid.astype(F32)).astype(BF16)
    return hi, mid, lo


def _tri_dot(tri, v):
    hi, mid, lo = _split3(v)
    dot = functools.partial(jnp.dot, preferred_element_type=F32)
    return dot(tri, hi) + dot(tri, mid) + dot(tri, lo)


def _fox_gate(flog, bias, name):
    T = flog.shape[0]
    tb = _pick(T, 256, BF16_ROWS)

    def body(f_ref, b_ref, o_ref, carry_ref):
        @pl.when(pl.program_id(0) == 0)
        def _():
            carry_ref[...] = jnp.zeros_like(carry_ref)

        xv = f_ref[...] + b_ref[...]
        lf = jnp.minimum(xv, 0.0) - jnp.log(1.0 + jnp.exp(-jnp.abs(xv)))
        r = lax.broadcasted_iota(jnp.int32, (tb, tb), 0)
        cidx = lax.broadcasted_iota(jnp.int32, (tb, tb), 1)
        tri = jnp.where(r >= cidx, 1.0, 0.0).astype(BF16)
        out = _tri_dot(tri, lf) + carry_ref[...]
        o_ref[...] = out
        carry_ref[...] = out[tb - 1:tb, :]

    return pl.pallas_call(
        body, name=name, grid=(T // tb,),
        in_specs=[pl.BlockSpec((tb, LANES), lambda i: (i, 0)), pl.BlockSpec((1, LANES), lambda i: (0, 0))],
        out_specs=pl.BlockSpec((tb, LANES), lambda i: (i, 0)),
        out_shape=jax.ShapeDtypeStruct((T, LANES), F32),
        scratch_shapes=[pltpu.VMEM((1, LANES), F32)],
        compiler_params=_params("arbitrary"),
    )(flog, bias)


def _fox_gate_bwd(dfq, dfk, flog, bias, name):
    T = flog.shape[0]
    tb = _pick(T, 256, BF16_ROWS)
    nb = T // tb

    def body(dq_ref, dk_ref, f_ref, b_ref, o_ref, db_ref, carry_ref):
        first = pl.program_id(0) == 0

        @pl.when(first)
        def _():
            carry_ref[...] = jnp.zeros_like(carry_ref)

        dv = dq_ref[...] + dk_ref[...]
        r = lax.broadcasted_iota(jnp.int32, (tb, tb), 0)
        cidx = lax.broadcasted_iota(jnp.int32, (tb, tb), 1)
        tri = jnp.where(cidx >= r, 1.0, 0.0).astype(BF16)
        dl = _tri_dot(tri, dv) + carry_ref[...]
        carry_ref[...] = dl[0:1, :]
        xv = f_ref[...] + b_ref[...]
        dfl = dl / (1.0 + jnp.exp(xv))
        o_ref[...] = dfl.astype(BF16)
        val = _colsum(dfl)

        @pl.when(first)
        def _():
            db_ref[...] = val

        @pl.when(jnp.logical_not(first))
        def _():
            db_ref[...] += val

    rev = pl.BlockSpec((tb, LANES), lambda i: (nb - 1 - i, 0))
    row = pl.BlockSpec((1, LANES), lambda i: (0, 0))
    return pl.pallas_call(
        body, name=name, grid=(nb,), in_specs=[rev, rev, rev, row], out_specs=[rev, row],
        out_shape=[jax.ShapeDtypeStruct((T, LANES), BF16), jax.ShapeDtypeStruct((1, LANES), F32)],
        scratch_shapes=[pltpu.VMEM((1, LANES), F32)],
        compiler_params=_params("arbitrary"),
    )(dfq, dfk, flog, bias)


_NT = _DIMS["nt"]
_TN = _DIMS["tn"]


ATTN_TQ, ATTN_TK = 1024, 1024
ATTN_TKB, ATTN_TQI = 1024, 1024


def _visible(tq, tk, shift):
    r = lax.broadcasted_iota(jnp.int32, (tq, tk), 0)
    c = lax.broadcasted_iota(jnp.int32, (tq, tk), 1)
    return c + shift <= r


def _attn_fwd(qkv, frow, n_heads, name, rides=()):
    T = qkv.shape[0]
    H = n_heads
    tq = _pick(T, ATTN_TQ, LANES)
    tk = _pick(tq, ATTN_TK, LANES)
    per = tq // tk
    scale = HEAD_DIM ** -0.5

    def body(q_ref, k_ref, v_ref, fr_ref, o_ref, lse_ref):
        i = pl.program_id(1)
        q = q_ref[...]

        def step(j, carry, shift):
            m, l, acc = carry
            ks = pl.multiple_of(j * tk, tk)
            kb = k_ref[pl.ds(ks, tk), :]
            vb = v_ref[pl.ds(ks, tk), :]
            s = lax.dot_general(q, kb, _NT, preferred_element_type=F32) * scale - fr_ref[:, pl.ds(ks, tk)]
            if shift is not None:
                s = jnp.where(_visible(tq, tk, shift), s, NEG_BIG)
            m_new = jnp.maximum(m, jnp.max(s, axis=1, keepdims=True))
            alpha = jnp.exp(m - m_new)
            p = jnp.exp(s - m_new)
            l = alpha * l + jnp.sum(p, axis=1, keepdims=True)
            acc = alpha * acc + jnp.dot(p.astype(BF16), vb, preferred_element_type=F32)
            return m_new, l, acc

        carry = (jnp.full((tq, 1), NEG_BIG, F32), jnp.zeros((tq, 1), F32), jnp.zeros((tq, HEAD_DIM), F32))
        carry = lax.fori_loop(0, i * per, lambda j, c: step(j, c, None), carry)
        for d in range(per):
            carry = step(i * per + d, carry, d * tk)
        m, l, acc = carry
        o_ref[...] = (acc / l).astype(BF16)
        lse_ref[...] = jnp.broadcast_to(m + jnp.log(l), (tq, LANES))

    outs, carried = _call(
        body, name=name, grid=(H, T // tq),
        in_specs=[pl.BlockSpec((tq, HEAD_DIM), lambda h, i: (i, h)),
                  pl.BlockSpec((T, HEAD_DIM), lambda h, i: (0, H + h)),
                  pl.BlockSpec((T, HEAD_DIM), lambda h, i: (0, 2 * H + h)),
                  pl.BlockSpec((None, 1, T), lambda h, i: (h, 0, 0))],
        out_specs=[pl.BlockSpec((tq, HEAD_DIM), lambda h, i: (i, h)),
                   pl.BlockSpec((None, tq, LANES), lambda h, i: (h, i, 0))],
        out_shape=[jax.ShapeDtypeStruct((T, H * HEAD_DIM), BF16), jax.ShapeDtypeStruct((H, T, LANES), F32)],
        scratch_shapes=[], args=(qkv, qkv, qkv, frow), rides=rides)
    return (outs, carried) if rides else outs


def _attn_dq(qkv, o, do, lse, frow, n_heads, name):
    T = qkv.shape[0]
    H = n_heads
    tq = _pick(T, ATTN_TQ, LANES)
    tk = _pick(tq, ATTN_TK, LANES)
    per = tq // tk
    scale = HEAD_DIM ** -0.5

    def body(q_ref, k_ref, v_ref, o_ref, do_ref, lse_ref, fr_ref, dq_ref, df_ref):
        i = pl.program_id(1)
        q = q_ref[...]
        dob = do_ref[...]
        delta = jnp.sum(dob.astype(F32) * o_ref[...].astype(F32), axis=1, keepdims=True)
        lse_i = lse_ref[:, 0:1]

        def step(j, carry, shift):
            dq, rsum = carry
            ks = pl.multiple_of(j * tk, tk)
            kb = k_ref[pl.ds(ks, tk), :]
            vb = v_ref[pl.ds(ks, tk), :]
            s = lax.dot_general(q, kb, _NT, preferred_element_type=F32) * scale - fr_ref[:, pl.ds(ks, tk)]
            p = jnp.exp(s - lse_i)
            if shift is not None:
                p = jnp.where(_visible(tq, tk, shift), p, 0.0)
            dp = lax.dot_general(dob, vb, _NT, preferred_element_type=F32)
            ds = p * (dp - delta)
            return (dq + jnp.dot(ds.astype(BF16), kb, preferred_element_type=F32),
                    rsum + jnp.sum(ds, axis=1, keepdims=True))

        carry = (jnp.zeros((tq, HEAD_DIM), F32), jnp.zeros((tq, 1), F32))
        carry = lax.fori_loop(0, i * per, lambda j, c: step(j, c, None), carry)
        for d in range(per):
            carry = step(i * per + d, carry, d * tk)
        dq, rsum = carry
        dq_ref[...] = (dq * scale).astype(BF16)
        df_ref[...] = jnp.transpose(jnp.broadcast_to(rsum, (tq, LANES)))[0:1, :]

    qtile = lambda off: pl.BlockSpec((tq, HEAD_DIM), lambda h, i: (i, off + h))
    return pl.pallas_call(
        body, name=name, grid=(H, T // tq),
        in_specs=[qtile(0),
                  pl.BlockSpec((T, HEAD_DIM), lambda h, i: (0, H + h)),
                  pl.BlockSpec((T, HEAD_DIM), lambda h, i: (0, 2 * H + h)),
                  qtile(0), qtile(0),
                  pl.BlockSpec((None, tq, LANES), lambda h, i: (h, i, 0)),
                  pl.BlockSpec((None, 1, T), lambda h, i: (h, 0, 0))],
        out_specs=[qtile(0), pl.BlockSpec((None, 1, tq), lambda h, i: (h, 0, i))],
        out_shape=[jax.ShapeDtypeStruct((T, H * HEAD_DIM), BF16), jax.ShapeDtypeStruct((H, 1, T), F32)],
        compiler_params=_params("parallel", "arbitrary"),
    )(qkv, qkv, qkv, o, do, lse, frow)


def _attn_dkv(qkv, o, do, lse, frow, n_heads, name, rides=()):
    T = qkv.shape[0]
    H = n_heads
    tk = _pick(T, ATTN_TKB, LANES)
    tq = _pick(tk, ATTN_TQI, LANES)
    per = tk // tq
    nq = T // tq
    scale = HEAD_DIM ** -0.5

    def body(k_ref, v_ref, q_ref, o_ref, do_ref, lse_ref, fr_ref, dk_ref, dv_ref, df_ref):
        j = pl.program_id(1)
        kb = k_ref[...]
        vb = v_ref[...]
        fj = fr_ref[...]

        def step(i, carry, shift):
            dk, dv, df = carry
            qs = pl.multiple_of(i * tq, tq)
            qb = q_ref[pl.ds(qs, tq), :]
            dob = do_ref[pl.ds(qs, tq), :]
            delta = jnp.sum(dob.astype(F32) * o_ref[pl.ds(qs, tq), :].astype(F32), axis=1, keepdims=True)
            lse_i = lse_ref[pl.ds(qs, tq), 0:1]
            s = lax.dot_general(qb, kb, _NT, preferred_element_type=F32) * scale - fj
            p = jnp.exp(s - lse_i)
            if shift is not None:
                p = jnp.where(_visible(tq, tk, shift), p, 0.0)
            dv = dv + lax.dot_general(p.astype(BF16), dob, _TN, preferred_element_type=F32)
            dp = lax.dot_general(dob, vb, _NT, preferred_element_type=F32)
            ds = p * (dp - delta)
            dk = dk + lax.dot_general(ds.astype(BF16), qb, _TN, preferred_element_type=F32)
            df = df - _colsum(ds)
            return dk, dv, df

        carry = (jnp.zeros((tk, HEAD_DIM), F32), jnp.zeros((tk, HEAD_DIM), F32), jnp.zeros((1, tk), F32))
        for d in range(per):
            carry = step(j * per + d, carry, -d * tq)
        dk, dv, df = lax.fori_loop((j + 1) * per, nq, lambda i, c: step(i, c, None), carry)
        dk_ref[...] = (dk * scale).astype(BF16)
        dv_ref[...] = dv.astype(BF16)
        df_ref[...] = df

    full = lambda off: pl.BlockSpec((T, HEAD_DIM), lambda h, j: (0, off + h))
    ktile = lambda off: pl.BlockSpec((tk, HEAD_DIM), lambda h, j: (j, off + h))
    outs, carried = _call(
        body, name=name, grid=(H, T // tk),
        in_specs=[ktile(H), ktile(2 * H), full(0), full(0), full(0),
                  pl.BlockSpec((None, T, LANES), lambda h, j: (h, 0, 0)),
                  pl.BlockSpec((None, 1, tk), lambda h, j: (h, 0, j))],
        out_specs=[ktile(0), ktile(0), pl.BlockSpec((None, 1, tk), lambda h, j: (h, 0, j))],
        out_shape=[jax.ShapeDtypeStruct((T, H * HEAD_DIM), BF16)] * 2 + [jax.ShapeDtypeStruct((H, 1, T), F32)],
        scratch_shapes=[], args=(qkv, qkv, qkv, o, do, lse, frow), rides=rides)
    return (outs, carried) if rides else outs


_GELU_C = math.sqrt(2.0 / math.pi)
_GELU_A = 0.044715


def _gelu(p):
    t = jnp.tanh(_GELU_C * (p + _GELU_A * p * p * p))
    return 0.5 * p * (1.0 + t), t


def _gelu_grad(p, t):
    return 0.5 * (1.0 + t) + 0.5 * p * (1.0 - t * t) * _GELU_C * (1.0 + 3.0 * _GELU_A * p * p)


def _tril(shape, upper=False):
    r = lax.broadcasted_iota(jnp.int32, shape, 0)
    c = lax.broadcasted_iota(jnp.int32, shape, 1)
    return (c >= r) if upper else (r >= c)


def _gm_fwd(p, vg, ws, bt, name):
    T, D2 = p.shape
    D = D2 // 2
    NG = D // GROUP
    tm = _pick(T, 2 * CHUNK, CHUNK)

    def body(p_ref, vg_ref, ws_ref, bt_ref, o_ref):
        z, _ = _gelu(p_ref[...].astype(F32))
        u, v = z[:, :D], z[:, D:]
        rstd = lax.rsqrt(jnp.mean(v * v, axis=-1, keepdims=True) + EPS)
        vn = (v * rstd * vg_ref[...]).astype(BF16)
        low = _tril((CHUNK, CHUNK))
        for g in range(NG):
            wm = jnp.where(low, ws_ref[g], 0.0).astype(BF16)
            bcol = bt_ref[:, g:g + 1]
            cs = slice(g * GROUP, (g + 1) * GROUP)
            for r in range(tm // CHUNK):
                rs = slice(r * CHUNK, (r + 1) * CHUNK)
                sv = jnp.dot(wm, vn[rs, cs], preferred_element_type=F32) + bcol
                o_ref[rs, cs] = (u[rs, cs] * sv).astype(BF16)

    return pl.pallas_call(
        body, name=name, grid=(T // tm,),
        in_specs=[pl.BlockSpec((tm, D2), lambda i: (i, 0)), pl.BlockSpec((1, D), lambda i: (0, 0)),
                  pl.BlockSpec((NG, CHUNK, CHUNK), lambda i: (0, 0, 0)),
                  pl.BlockSpec((CHUNK, LANES), lambda i: (0, 0))],
        out_specs=pl.BlockSpec((tm, D), lambda i: (i, 0)),
        out_shape=jax.ShapeDtypeStruct((T, D), BF16),
        compiler_params=_params("parallel"),
    )(p, vg, ws, bt)


def _gm_bwd(p, dgated, vg, ws, wst, bt, name):
    T, D2 = p.shape
    D = D2 // 2
    NG = D // GROUP
    tm = _pick(T, 2 * CHUNK, CHUNK)
    nb = T // tm

    def body(p_ref, dg_ref, vg_ref, ws_ref, wst_ref, bt_ref, dp_ref, dws_ref, dbt_ref, dvg_ref, du_s, dvn_s):
        step = pl.program_id(0)

        @pl.when(step == 0)
        def _():
            dws_ref[...] = jnp.zeros_like(dws_ref)
            dbt_ref[...] = jnp.zeros_like(dbt_ref)
            dvg_ref[...] = jnp.zeros_like(dvg_ref)

        pv = p_ref[...].astype(F32)
        z, t = _gelu(pv)
        u, v = z[:, :D], z[:, D:]
        rstd = lax.rsqrt(jnp.mean(v * v, axis=-1, keepdims=True) + EPS)
        vhat = v * rstd
        vgv = vg_ref[...]
        vn = (vhat * vgv).astype(BF16)
        dgt = dg_ref[...].astype(F32)
        low = _tril((CHUNK, CHUNK))
        up = _tril((CHUNK, CHUNK), upper=True)
        lane = lax.broadcasted_iota(jnp.int32, (CHUNK, LANES), 1)
        dbt = jnp.zeros((CHUNK, LANES), F32)
        for g in range(NG):
            wm = jnp.where(low, ws_ref[g], 0.0).astype(BF16)
            wmt = jnp.where(up, wst_ref[g], 0.0).astype(BF16)
            bcol = bt_ref[:, g:g + 1]
            cs = slice(g * GROUP, (g + 1) * GROUP)
            dw = jnp.zeros((CHUNK, CHUNK), F32)
            for r in range(tm // CHUNK):
                rs = slice(r * CHUNK, (r + 1) * CHUNK)
                blk = vn[rs, cs]
                sv = jnp.dot(wm, blk, preferred_element_type=F32) + bcol
                dgb = dgt[rs, cs]
                du_s[rs, cs] = dgb * sv
                dsv = dgb * u[rs, cs]
                dsvb = dsv.astype(BF16)
                dvn_s[rs, cs] = jnp.dot(wmt, dsvb, preferred_element_type=F32)
                dw = dw + lax.dot_general(dsvb, blk, _NT, preferred_element_type=F32)
                dbt = dbt + jnp.where(lane == g, jnp.sum(dsv, axis=1, keepdims=True), 0.0)
            dws_ref[g] += dw
        dbt_ref[...] += dbt
        dvn = dvn_s[...]
        dvg_ref[...] += _colsum(dvn * vhat)
        dvhat = dvn * vgv
        dv = rstd * (dvhat - vhat * jnp.mean(dvhat * vhat, axis=-1, keepdims=True))
        gg = _gelu_grad(pv, t)
        dp_ref[:, :D] = (du_s[...] * gg[:, :D]).astype(BF16)
        dp_ref[:, D:] = (dv * gg[:, D:]).astype(BF16)

        @pl.when(step == nb - 1)
        def _():
            for g in range(NG):
                dws_ref[g] = jnp.where(low, dws_ref[g], 0.0)

    const3 = pl.BlockSpec((NG, CHUNK, CHUNK), lambda i: (0, 0, 0))
    return pl.pallas_call(
        body, name=name, grid=(nb,),
        in_specs=[pl.BlockSpec((tm, D2), lambda i: (i, 0)), pl.BlockSpec((tm, D), lambda i: (i, 0)),
                  pl.BlockSpec((1, D), lambda i: (0, 0)), const3, const3,
                  pl.BlockSpec((CHUNK, LANES), lambda i: (0, 0))],
        out_specs=[pl.BlockSpec((tm, D2), lambda i: (i, 0)), const3,
                   pl.BlockSpec((CHUNK, LANES), lambda i: (0, 0)), pl.BlockSpec((1, D), lambda i: (0, 0))],
        out_shape=[jax.ShapeDtypeStruct((T, D2), BF16), jax.ShapeDtypeStruct((NG, CHUNK, CHUNK), F32),
                   jax.ShapeDtypeStruct((CHUNK, LANES), F32), jax.ShapeDtypeStruct((1, D), F32)],
        scratch_shapes=[pltpu.VMEM((tm, D), F32), pltpu.VMEM((tm, D), F32)],
        compiler_params=_params("arbitrary"),
    )(p, dgated, vg, ws, wst, bt)


HALO = BF16_ROWS


def _ffn_act(a, cw, cb, name):
    T, F2 = a.shape
    Fh = F2 // 2
    tm = _pick(T, 256, HALO)
    per = tm // HALO
    nch = Fh // LANES

    def body(a_ref, h_ref, cw_ref, cb_ref, o_ref):
        i = pl.program_id(0)
        row = lax.broadcasted_iota(jnp.int32, (tm, LANES), 0)
        keep = jnp.where(i > 0, 1.0, 0.0)

        def conv(off):
            af = a_ref[:, pl.ds(off, LANES)].astype(F32)
            hf = h_ref[:, pl.ds(off, LANES)].astype(F32) * keep
            r1 = jnp.where(row == 0, hf[HALO - 1:HALO], pltpu.roll(af, 1, 0))
            r2 = jnp.where(row == 0, hf[HALO - 2:HALO - 1], jnp.where(row == 1, hf[HALO - 1:HALO], pltpu.roll(af, 2, 0)))
            w = cw_ref[:, pl.ds(off, LANES)]
            return w[0:1] * r2 + w[1:2] * r1 + w[2:3] * af + cb_ref[:, pl.ds(off, LANES)]

        def chunk(c, carry):
            off = pl.multiple_of(c * LANES, LANES)
            gt = conv(off)
            upv = conv(pl.multiple_of(off + Fh, LANES))
            o_ref[:, pl.ds(off, LANES)] = (gt / (1.0 + jnp.exp(-gt)) * upv).astype(BF16)
            return carry

        lax.fori_loop(0, nch, chunk, 0)

    return pl.pallas_call(
        body, name=name, grid=(T // tm,),
        in_specs=[pl.BlockSpec((tm, F2), lambda i: (i, 0)),
                  pl.BlockSpec((HALO, F2), lambda i: (jnp.maximum(i * per - 1, 0), 0)),
                  pl.BlockSpec((3, F2), lambda i: (0, 0)), pl.BlockSpec((1, F2), lambda i: (0, 0))],
        out_specs=pl.BlockSpec((tm, Fh), lambda i: (i, 0)),
        out_shape=jax.ShapeDtypeStruct((T, Fh), BF16),
        compiler_params=_params("parallel"),
    )(a, a, cw, cb)


def _ffn_act_bwd(a, dhm, cw, cb, name, rides=()):
    T, F2 = a.shape
    Fh = F2 // 2
    tm = _pick(T, 256, HALO)
    per = tm // HALO
    nb = T // tm
    last_halo = T // HALO - 1
    nch = Fh // LANES
    ext = tm + 2 * HALO

    def body(a_ref, hp_ref, hn_ref, d_ref, dn_ref, cw_ref, cb_ref, da_ref, dcw_ref, dcb_ref):
        i = pl.program_id(0)

        @pl.when(i == 0)
        def _():
            dcw_ref[...] = jnp.zeros_like(dcw_ref)
            dcb_ref[...] = jnp.zeros_like(dcb_ref)

        keep_p = jnp.where(i > 0, 1.0, 0.0)
        keep_n = jnp.where(i < nb - 1, 1.0, 0.0)
        row = lax.broadcasted_iota(jnp.int32, (ext, LANES), 0)
        main = jnp.logical_and(row >= HALO, row < HALO + tm)

        def conv(off):
            e = jnp.concatenate([hp_ref[:, pl.ds(off, LANES)].astype(F32) * keep_p,
                                 a_ref[:, pl.ds(off, LANES)].astype(F32),
                                 hn_ref[:, pl.ds(off, LANES)].astype(F32) * keep_n], axis=0)
            r1 = pltpu.roll(e, 1, 0)
            r2 = pltpu.roll(e, 2, 0)
            w = cw_ref[:, pl.ds(off, LANES)]
            return w, e, r1, r2, w[0:1] * r2 + w[1:2] * r1 + w[2:3] * e + cb_ref[:, pl.ds(off, LANES)]

        def back(off, w, e, r1, r2, dc):
            dcm = jnp.where(main, dc, 0.0)
            for tap, shifted in enumerate((r2, r1, e)):
                dcw_ref[tap:tap + 1, pl.ds(off, LANES)] += _colsum(dcm * shifted)
            dcb_ref[:, pl.ds(off, LANES)] += _colsum(dcm)
            da = w[2:3] * dc + w[1:2] * pltpu.roll(dc, ext - 1, 0) + w[0:1] * pltpu.roll(dc, ext - 2, 0)
            da_ref[:, pl.ds(off, LANES)] = da[HALO:HALO + tm].astype(BF16)

        def chunk(c, carry):
            off = pl.multiple_of(c * LANES, LANES)
            off_u = pl.multiple_of(off + Fh, LANES)
            dh = jnp.concatenate([jnp.zeros((HALO, LANES), F32), d_ref[:, pl.ds(off, LANES)].astype(F32),
                                  dn_ref[:, pl.ds(off, LANES)].astype(F32) * keep_n], axis=0)
            wg, eg, r1g, r2g, gt = conv(off)
            wu, eu, r1u, r2u, upv = conv(off_u)
            sg = 1.0 / (1.0 + jnp.exp(-gt))
            back(off, wg, eg, r1g, r2g, dh * upv * (sg * (1.0 + gt * (1.0 - sg))))
            back(off_u, wu, eu, r1u, r2u, dh * (gt * sg))
            return carry

        lax.fori_loop(0, nch, chunk, 0)

    prev = lambda i: (jnp.maximum(i * per - 1, 0), 0)
    nxt = lambda i: (jnp.minimum((i + 1) * per, last_halo), 0)
    outs, carried = _call(
        body, name=name, grid=(nb,),
        in_specs=[pl.BlockSpec((tm, F2), lambda i: (i, 0)), pl.BlockSpec((HALO, F2), prev),
                  pl.BlockSpec((HALO, F2), nxt), pl.BlockSpec((tm, Fh), lambda i: (i, 0)),
                  pl.BlockSpec((HALO, Fh), nxt),
                  pl.BlockSpec((3, F2), lambda i: (0, 0)), pl.BlockSpec((1, F2), lambda i: (0, 0))],
        out_specs=[pl.BlockSpec((tm, F2), lambda i: (i, 0)), pl.BlockSpec((3, F2), lambda i: (0, 0)),
                   pl.BlockSpec((1, F2), lambda i: (0, 0))],
        out_shape=[jax.ShapeDtypeStruct((T, F2), BF16), jax.ShapeDtypeStruct((3, F2), F32),
                   jax.ShapeDtypeStruct((1, F2), F32)],
        scratch_shapes=[], args=(a, a, a, dhm, dhm, cw, cb), rides=rides)
    return (outs, carried) if rides else outs


def _round_up(n, k):
    return -(-n // k) * k


def _pad_axis(a, axis, size):
    pad = [(0, 0)] * a.ndim
    pad[axis] = (0, size - a.shape[axis])
    return jnp.pad(a, pad)


def _pad_blocks(a, axis, nblk, to):
    shp = a.shape
    blk = shp[axis] // nblk
    a = a.reshape(shp[:axis] + (nblk, blk) + shp[axis + 1:])
    a = _pad_axis(a, axis + 1, to)
    return a.reshape(shp[:axis] + (nblk * to,) + shp[axis + 1:])


def _unpad_blocks(a, axis, nblk, blk):
    shp = a.shape
    to = shp[axis] // nblk
    a = a.reshape(shp[:axis] + (nblk, to) + shp[axis + 1:])
    a = lax.slice_in_dim(a, 0, blk, axis=axis + 1)
    return a.reshape(shp[:axis] + (nblk * blk,) + shp[axis + 1:])


SIB_TILE_BYTES = 2 * 1024 * 1024


def _sib_stream(x, core, name, gather):
    if gather:
        B, R, D = x.shape
        out_shape = (B, 2, R, D)
    else:
        B, _, R, D = x.shape
        out_shape = (B, R, D)
    tr = _pick(R, max(BF16_ROWS, SIB_TILE_BYTES // (D * x.dtype.itemsize)), BF16_ROWS)
    nr = R // tr
    if gather:
        in_spec = pl.BlockSpec((tr, D), lambda b, i, core_ref: (b * nr + i, 0))
    else:
        in_spec = pl.BlockSpec((tr, D), lambda b, i, core_ref: ((2 * b + 1 - core_ref[0]) * nr + i, 0))

    def body(core_ref, x_ref, o_ref, send_sem, recv_sem, local_sem):
        b, i = pl.program_id(0), pl.program_id(1)
        me = lax.axis_index("c")
        sibling = (lax.axis_index("x"), lax.axis_index("y"), 1 - me)
        rows = pl.ds(pl.multiple_of(i * tr, tr), tr)
        dst = o_ref.at[b, me, rows] if gather else o_ref.at[b, rows]
        push = pltpu.make_async_remote_copy(src_ref=x_ref, dst_ref=dst, send_sem=send_sem, recv_sem=recv_sem,
                                            device_id=sibling, device_id_type=MESH)
        push.start()
        if gather:
            keep = pltpu.make_async_copy(x_ref, dst, local_sem)
            keep.start()
            keep.wait()
        push.wait_send()

        @pl.when(jnp.logical_and(b == B - 1, i == nr - 1))
        def _():
            landed = o_ref.at[:, 0] if gather else o_ref
            pltpu.make_async_remote_copy(src_ref=landed, dst_ref=landed, send_sem=send_sem, recv_sem=recv_sem,
                                         device_id=sibling, device_id_type=MESH).wait_recv()

    return pl.pallas_call(
        body, name=name,
        grid_spec=pltpu.PrefetchScalarGridSpec(
            num_scalar_prefetch=1, grid=(B, nr), in_specs=[in_spec],
            out_specs=pl.BlockSpec(memory_space=pl.ANY),
            scratch_shapes=[pltpu.SemaphoreType.DMA(()), pltpu.SemaphoreType.DMA(()), pltpu.SemaphoreType.DMA(())]),
        out_shape=jax.ShapeDtypeStruct(out_shape, x.dtype),
        compiler_params=_params("arbitrary", "arbitrary"),
    )(core, x.reshape(-1, D))


def _pair_sum(g, recv, core, name):
    B, _, R, D = g.shape
    tr = _pick(R, 256, BF16_ROWS)

    def body(core_ref, g_ref, r_ref, o_ref):
        o_ref[...] = (g_ref[...].astype(F32) + r_ref[...].astype(F32)).astype(BF16)

    tile = pl.BlockSpec((None, tr, D), lambda b, i, core_ref: (b, i, 0))
    return pl.pallas_call(
        body, name=name,
        grid_spec=pltpu.PrefetchScalarGridSpec(
            num_scalar_prefetch=1, grid=(B, R // tr),
            in_specs=[pl.BlockSpec((None, None, tr, D), lambda b, i, core_ref: (b, core_ref[0], i, 0)), tile],
            out_specs=tile),
        out_shape=jax.ShapeDtypeStruct((B, R, D), BF16),
        compiler_params=_params("parallel", "parallel"),
    )(core, g, recv)


def _all_reduce_small(v, name):
    pair = _sum_slots(_all_gather(v, ("c",), 0, name + "_ag_c")[None], F32, name + "_sum_c")[0]
    return _sum_slots(_all_gather(pair, ("x", "y"), 0, name + "_ag_xy")[None], F32, name + "_sum_xy")[0]


def _pack(arrs):
    flat = jnp.concatenate([a.reshape(-1) for a in arrs])
    rows = _round_up(-(-flat.shape[0] // SMALL_PACK_COLS), BF16_ROWS)
    return _pad_axis(flat, 0, rows * SMALL_PACK_COLS).reshape(rows, SMALL_PACK_COLS)


def _unpack(buf, like):
    flat = buf.reshape(-1)
    out, off = [], 0
    for a in like:
        out.append(flat[off:off + a.size].reshape(a.shape))
        off += a.size
    return out


def kernel(x, c, mod_w, mod_b, mix_norm_g, ffn_norm_g, attn_w_in, attn_b_f, attn_w_o, gm_w_in, gm_v_g, gm_w_s, gm_b_s, gm_w_o, ffn_w_in, ffn_conv_w, ffn_conv_b, ffn_w_out, final_g, loss_target, m_mod_w, m_mod_b, m_mix_norm_g, m_ffn_norm_g, m_attn_w_in, m_attn_b_f, m_attn_w_o, m_gm_w_in, m_gm_v_g, m_gm_w_s, m_gm_b_s, m_gm_w_o, m_ffn_w_in, m_ffn_conv_w, m_ffn_conv_b, m_ffn_w_out, m_final_g, v_mod_w, v_mod_b, v_mix_norm_g, v_ffn_norm_g, v_attn_w_in, v_attn_b_f, v_attn_w_o, v_gm_w_in, v_gm_v_g, v_gm_w_s, v_gm_b_s, v_gm_w_o, v_ffn_w_in, v_ffn_conv_w, v_ffn_conv_b, v_ffn_w_out, v_final_g):
    _, T, D = x.shape
    L = mod_w.shape[0]
    NA, NB = attn_w_in.shape[0], gm_w_in.shape[0]
    H = D // HEAD_DIM
    NG = D // GROUP
    Fq = ffn_w_out.shape[1]
    Fh = 4 * Fq
    Fqp = _round_up(Fq, LANES)
    Fp = 4 * Fqp
    n_attn = attn_w_in.shape[2]
    n_attn_p = _round_up(n_attn, 2 * BF16_ROWS)
    nmod = mod_w.shape[2]

    xi, yi, ci = lax.axis_index("x"), lax.axis_index("y"), lax.axis_index("c")
    chip = 2 * xi + yi
    example = 2 * chip + ci
    core = ci.astype(jnp.int32).reshape(1)

    halves = {
        "attn_in": _cast_half(_pad_axis(jnp.swapaxes(attn_w_in, 1, 2), 1, n_attn_p), core, "w_attn_in_cast"),
        "attn_o": _cast_half(attn_w_o, core, "w_attn_o_cast"),
        "gm_in": _cast_half(jnp.swapaxes(gm_w_in, 1, 2), core, "w_gm_in_cast"),
        "gm_o": _cast_half(gm_w_o, core, "w_gm_o_cast"),
        "ffn_in": _cast_half(_pad_blocks(jnp.swapaxes(ffn_w_in, 1, 2), 1, 2, Fqp), core, "w_ffn_in_cast"),
        "ffn_out": _cast_half(_pad_axis(ffn_w_out, 1, Fqp), core, "w_ffn_out_cast"),
    }
    weights = {}

    def mixer_kinds(i):
        return [("attn_in", i // 2), ("attn_o", i // 2)] if i % 2 == 0 else [("gm_in", i // 2), ("gm_o", i // 2)]

    def land(kind, l, gathered):
        n2 = gathered.shape[1]
        full = _sib_stream(gathered, core, "w_" + kind + "_ag_c", True).reshape(8 * n2, D)
        if kind == "attn_in":
            full = full.reshape(4, n_attn_p, D)[:, :n_attn].reshape(4 * n_attn, D)
            weights[("qkv", l)] = full[:3 * D]
            weights[("f", l)] = _pad_axis(full[3 * D:], 0, LANES)
        else:
            weights[(kind, l)] = full

    def fetching(host, kinds):
        if not kinds:
            return host(())
        result, carried = host([_Exchange(halves[k], ("x", "y"), 0, True, sel=l) for k, l in kinds])
        for (k, l), g in zip(kinds, carried):
            land(k, l, g)
        return result

    land("attn_in", 0, _all_gather(halves["attn_in"][0], ("x", "y"), 0, "w_attn_in_ag_xy"))
    cw_all = _all_gather(ffn_conv_w, ("x", "y"), 0, "conv_w_ag")
    conv_w_full = jnp.moveaxis(cw_all, 0, 2).reshape(L, 3, 2 * Fh)
    conv_w = _pad_blocks(conv_w_full, 2, 8, Fqp)
    conv_b = _pad_blocks(ffn_conv_b, 1, 8, Fqp)
    vg_all = _all_gather(gm_v_g, ("x", "y"), 0, "gm_vg_ag")
    vg_full = jnp.moveaxis(vg_all, 0, 1).reshape(NB, D)

    c_all = _all_gather(c, MESH_AXES, 0, "c_ag").reshape(8, D)
    mod_b_mine = lax.dynamic_slice_in_dim(mod_b, chip * nmod, nmod, axis=1)
    c_act, mod_part = _mod_fwd(c_all, mod_w, mod_b_mine, "mod_fwd")
    mod_all = _all_gather(mod_part, ("x", "y"), 0, "mod_ag")
    mod = lax.dynamic_index_in_dim(mod_all, example, axis=2, keepdims=False)
    mod = jnp.swapaxes(mod, 0, 1).reshape(L, 6, 1, D)

    xs = x.reshape(T, D)
    target = loss_target.reshape(T, D)
    saved = []
    for i in range(L):
        j = i // 2
        sh1, sc1, g1, sh2, sc2, g2 = (mod[i, k] for k in range(6))
        h1 = _norm_mod(xs, mix_norm_g[i:i + 1], sc1, sh1, "norm_mod")
        nxt = i + 1 < L
        if i % 2 == 0:
            qkv = fetching(lambda r: _matmul(h1, weights[("qkv", j)], "nt", "attn_qkv", tm=1024, tn=1024, rides=r),
                           [("attn_o", j)] if i == 0 else [])
            flog = _matmul(h1, weights[("f", j)], "nt", "attn_f", out_dtype=F32, tm=1024)
            bias = _pad_axis(attn_b_f[j:j + 1], 1, LANES)
            fcol = _fox_gate(flog, bias, "fox_gate")
            frow = jnp.transpose(fcol)[:H].reshape(H, 1, T)
            o, lse = fetching(lambda r: _attn_fwd(qkv, frow, H, "attn_fwd", rides=r),
                              [("ffn_in", i), ("ffn_out", i)] + (mixer_kinds(i + 1) if nxt else []))
            y, x1 = _matmul(o, weights[("attn_o", j)], "nn", "attn_out", out_dtype=F32, tm=1024, tn=1024,
                            res=xs, gate=g1, emit_y=True)
            mix = ("fox", qkv, flog, bias, frow, o, lse)
            ride_in = [("ffn_in", i + 1)] if nxt else []
            ride_out = [("ffn_out", i + 1)] if nxt else []
        else:
            p = _matmul(h1, weights[("gm_in", j)], "nt", "gm_in", tm=1024, tn=1024)
            bt = _pad_axis(jnp.transpose(gm_b_s[j]), 1, LANES)
            gated = _gm_fwd(p, vg_full[j:j + 1], gm_w_s[j], bt, "gm_fwd")
            y, x1 = _matmul(gated, weights[("gm_o", j)], "nn", "gm_out", out_dtype=F32, tm=1024, tn=1024,
                            res=xs, gate=g1, emit_y=True)
            mix = ("gm", p, bt, gated)
            ride_in = mixer_kinds(i + 1) if nxt else []
            ride_out = []
        h2 = _norm_mod(x1, ffn_norm_g[i:i + 1], sc2, sh2, "norm_mod")
        a = fetching(lambda r: _matmul(h2, weights[("ffn_in", i)], "nt", "ffn_in", tm=1024, tn=1024, rides=r), ride_in)
        hmid = _ffn_act(a, conv_w[i], conv_b[i:i + 1], "ffn_act")
        y2, x2 = fetching(lambda r: _matmul(hmid, weights[("ffn_out", i)], "nn", "ffn_out", out_dtype=F32, tm=512,
                                            tn=1024, res=x1, gate=g2, emit_y=True, rides=r), ride_out)
        saved.append((xs, h1, mix, y, x1, h2, a, hmid, y2))
        xs = x2

    loss_blk, dx, g_final, dy2, dg2 = _final_loss_bwd(
        xs, final_g.reshape(1, D), target, saved[L - 1][8], mod[L - 1, 5], "final_loss_bwd")
    g_mix, g_ffn_n, g_cw, g_cb, dmod = [None] * L, [None] * L, [None] * L, [None] * L, [None] * L
    g_bf, g_vg, g_ws, g_bs = [None] * NA, [None] * NB, [None] * NB, [None] * NB
    reduced = {}

    def pair(kind, gw):
        g = gw.reshape(4, 2, gw.shape[0] // 8, D)
        return _pair_sum(g, _sib_stream(g, core, "g_" + kind + "_swap_c", False), core, "g_" + kind + "_sum_c")

    def sending(host, items):
        if not items:
            return host(())
        result, carried = host([_Exchange(ps, ("x", "y"), 0, False) for _, ps in items])
        for (key, _), got in zip(items, carried):
            reduced[key] = got
        return result

    late_act = []
    for i in reversed(range(L)):
        j = i // 2
        xin, h1, mix, y, x1, h2, a, hmid, y2 = saved[i]
        sh1, sc1, g1, sh2, sc2, g2 = (mod[i, k] for k in range(6))
        dhm = _matmul(dy2, weights[("ffn_out", i)], "nt", "ffn_out_dx", tm=1024, tn=512)
        p_out = pair("ffn_out", _matmul(hmid, dy2, "tn", "ffn_out_dw", tm=512, tn=1024))
        da, dcw, dcb = sending(lambda r: _ffn_act_bwd(a, dhm, conv_w[i], conv_b[i:i + 1], "ffn_act_bwd", rides=r),
                               [(("ffn_out", i), p_out)] + late_act)
        dh2 = _matmul(da, weights[("ffn_in", i)], "nn", "ffn_in_dx", out_dtype=F32, tm=512, tn=1024, tk=2816)
        late_act = []
        p_in = pair("ffn_in", _matmul(da, h2, "tn", "ffn_in_dw", tm=512, tn=1024))
        g_cw[i] = _unpad_blocks(dcw, 1, 8, Fq)
        g_cb[i] = _unpad_blocks(dcb, 1, 8, Fq)[0]
        dx1, dsc2, dsh2, dgn, dy, dg1 = _norm_bwd(
            x1, ffn_norm_g[i:i + 1], sc2, sh2, dh2, dx, (y, g1), "norm_bwd_prev")
        g_ffn_n[i] = dgn[0]
        if mix[0] == "fox":
            _, qkv, flog, bias, frow, o, lse = mix
            do = _matmul(dy, weights[("attn_o", j)], "nt", "attn_out_dx", tm=1024, tn=1024)
            p_o = pair("attn_o", _matmul(o, dy, "tn", "attn_out_dw", tm=512, tn=1024))
            dq, dfq = _attn_dq(qkv, o, do, lse, frow, H, "attn_dq")
            dk, dv, dfk = sending(lambda r: _attn_dkv(qkv, o, do, lse, frow, H, "attn_dkv", rides=r),
                                  [(("ffn_in", i), p_in), (("attn_o", j), p_o)])
            heads_on_lanes = lambda t: _pad_axis(jnp.transpose(t.reshape(H, T)), 1, LANES)
            dflog, dbf = _fox_gate_bwd(heads_on_lanes(dfq), heads_on_lanes(dfk), flog, bias, "fox_gate_bwd")
            dproj = jnp.concatenate([dq, dk, dv], axis=1)
            gwt_qkv = _matmul(dproj, h1, "tn", "attn_qkv_dw", tm=512, tn=1024)
            gwt_f = _matmul(dflog, h1, "tn", "attn_f_dw", tm=128, tn=1024)
            gwt = jnp.concatenate([gwt_qkv, gwt_f[:H]], axis=0).reshape(4, n_attn, D)
            p_a = pair("attn_in", _pad_axis(gwt, 1, n_attn_p).reshape(4 * n_attn_p, D))
            dh_f = _matmul(dflog, weights[("f", j)], "nn", "attn_f_dx", out_dtype=F32, tm=1024, tn=1024)
            dh1 = sending(lambda r: _matmul(dproj, weights[("qkv", j)], "nn", "attn_qkv_dx", out_dtype=F32, tm=512,
                                            tn=1024, tk=2048, res=dh_f, rides=r), [(("attn_in", j), p_a)])
            g_bf[j] = dbf[0, :H]
        else:
            _, p, bt, gated = mix
            late_act = [(("ffn_in", i), p_in)]
            dgated = _matmul(dy, weights[("gm_o", j)], "nt", "gm_out_dx", tm=1024, tn=1024)
            p_o = pair("gm_o", _matmul(gated, dy, "tn", "gm_out_dw", tm=512, tn=1024))
            dp, dws, dbt, dvg = _gm_bwd(p, dgated, vg_full[j:j + 1], gm_w_s[j],
                                        jnp.swapaxes(gm_w_s[j], 1, 2), bt, "gm_bwd")
            p_a = pair("gm_in", _matmul(dp, h1, "tn", "gm_in_dw", tm=512, tn=1024))
            dh1 = sending(lambda r: _matmul(dp, weights[("gm_in", j)], "nn", "gm_in_dx", out_dtype=F32, tm=512,
                                            tn=1024, tk=2048, rides=r), [(("gm_in", j), p_a), (("gm_o", j), p_o)])
            g_ws[j] = dws
            g_bs[j] = jnp.transpose(dbt)[:NG]
            g_vg[j] = dvg[0]
        if i > 0:
            dx, dsc1, dsh1, dgn, dy2, dg2_prev = _norm_bwd(
                xin, mix_norm_g[i:i + 1], sc1, sh1, dh1, dx1, (saved[i - 1][8], mod[i - 1, 5]), "norm_bwd_prev")
        else:
            dx, dsc1, dsh1, dgn = _norm_bwd(xin, mix_norm_g[i:i + 1], sc1, sh1, dh1, dx1, None, "norm_bwd_first")
            dg2_prev = None
        g_mix[i] = dgn[0]
        dmod[i] = jnp.concatenate([dsh1, dsc1, dg1, dsh2, dsc2, dg2], axis=1)[0]
        dg2 = dg2_prev
    grad_x = dx.reshape(1, T, D)
    loss = lax.psum(loss_blk[0, 0], MESH_AXES)

    small =[jnp.stack(g_mix), jnp.stack(g_ffn_n), jnp.stack(g_bf), jnp.stack(g_vg), jnp.stack(g_ws),
             jnp.stack(g_bs), jnp.stack(g_cw), jnp.stack(g_cb), g_final[0]]
    (grad_mix_norm_g, grad_ffn_norm_g, grad_attn_b_f, g_vg_full, grad_gm_w_s, grad_gm_b_s, g_cw_full,
     grad_ffn_conv_b, grad_final_g) = _unpack(_all_reduce_small(_pack(small), "g_small"), small)
    grad_gm_v_g = lax.dynamic_slice_in_dim(g_vg_full, chip * (D // 4), D // 4, axis=1)
    grad_ffn_conv_w = lax.dynamic_slice_in_dim(g_cw_full, chip * 2 * Fq, 2 * Fq, axis=2)

    dmod_all = _all_gather(jnp.stack(dmod), MESH_AXES, 0, "dmod_ag")
    grad_mod_b = _sum_slots(dmod_all[None], F32, "dmod_sum")[0]
    dmod_cols = jnp.swapaxes(lax.dynamic_slice_in_dim(dmod_all, chip * nmod, nmod, axis=2), 0, 1)

    grad_mod_w, delta_mod_w, new_m_mod_w, new_v_mod_w = _modw_adamw(
        jnp.transpose(c_act), dmod_cols, mod_w, m_mod_w, v_mod_w, "mod_w_adamw")

    def total(kind, n_layers):
        got = jnp.stack([reduced[(kind, l)] for l in range(n_layers)])
        tot = _sum_slots(got, F32, "g_" + kind + "_sum_xy")
        return _sib_stream(tot, core, "g_" + kind + "_ag_c", True).reshape(n_layers, -1, D)

    gt_attn_in = total("attn_in", NA)[:, :n_attn]
    grad_attn_w_in = jnp.swapaxes(gt_attn_in, 1, 2)
    grad_attn_w_o = total("attn_o", NA)
    grad_gm_w_in = jnp.swapaxes(total("gm_in", NB), 1, 2)
    grad_gm_w_o = total("gm_o", NB)
    gt_ffn_in = _unpad_blocks(total("ffn_in", L), 1, 2, Fq)
    grad_ffn_w_in = jnp.swapaxes(gt_ffn_in, 1, 2)
    grad_ffn_w_out = total("ffn_out", L)[:, :Fq]

    def big(w, g, m, v, name):
        shp = w.shape
        flat = lambda t: t.reshape(-1, shp[-1])
        return [t.reshape(shp) for t in _adamw(flat(w), flat(g), flat(m), flat(v), name)]

    def big_t(w, gt, m, v, name):
        t = lambda a: jnp.swapaxes(a, 1, 2)
        return [t(u) for u in big(t(w), gt, t(m), t(v), name)]

    upd_attn_in = big_t(attn_w_in, gt_attn_in, m_attn_w_in, v_attn_w_in, "adamw_attn_in")
    upd_attn_o = big(attn_w_o, grad_attn_w_o, m_attn_w_o, v_attn_w_o, "adamw_attn_o")
    upd_gm_in = big(gm_w_in, grad_gm_w_in, m_gm_w_in, v_gm_w_in, "adamw_gm_in")
    upd_gm_o = big(gm_w_o, grad_gm_w_o, m_gm_w_o, v_gm_w_o, "adamw_gm_o")
    upd_ffn_in = big_t(ffn_w_in, gt_ffn_in, m_ffn_w_in, v_ffn_w_in, "adamw_ffn_in")
    upd_ffn_out = big(ffn_w_out, grad_ffn_w_out, m_ffn_w_out, v_ffn_w_out, "adamw_ffn_out")

    sm_w = [mod_b, mix_norm_g, ffn_norm_g, attn_b_f, gm_v_g, gm_w_s, gm_b_s, ffn_conv_w, ffn_conv_b, final_g]
    sm_g = [grad_mod_b, grad_mix_norm_g, grad_ffn_norm_g, grad_attn_b_f, grad_gm_v_g, grad_gm_w_s, grad_gm_b_s,
            grad_ffn_conv_w, grad_ffn_conv_b, grad_final_g]
    sm_m = [m_mod_b, m_mix_norm_g, m_ffn_norm_g, m_attn_b_f, m_gm_v_g, m_gm_w_s, m_gm_b_s, m_ffn_conv_w,
            m_ffn_conv_b, m_final_g]
    sm_v = [v_mod_b, v_mix_norm_g, v_ffn_norm_g, v_attn_b_f, v_gm_v_g, v_gm_w_s, v_gm_b_s, v_ffn_conv_w,
            v_ffn_conv_b, v_final_g]
    sm_d, sm_nm, sm_nv = (_unpack(t, sm_w) for t in _adamw(_pack(sm_w), _pack(sm_g), _pack(sm_m), _pack(sm_v),
                                                             "adamw_small"))

    def ordered(mod_w_item, sm, k):
        return [mod_w_item, sm[0], sm[1], sm[2], upd_attn_in[k], sm[3], upd_attn_o[k], upd_gm_in[k], sm[4], sm[5],
                sm[6], upd_gm_o[k], upd_ffn_in[k], sm[7], sm[8], upd_ffn_out[k], sm[9]]

    grads = [grad_mod_w, grad_mod_b, grad_mix_norm_g, grad_ffn_norm_g, grad_attn_w_in, grad_attn_b_f, grad_attn_w_o,
             grad_gm_w_in, grad_gm_v_g, grad_gm_w_s, grad_gm_b_s, grad_gm_w_o, grad_ffn_w_in, grad_ffn_conv_w,
             grad_ffn_conv_b, grad_ffn_w_out, grad_final_g]
    return (loss, grad_x, *grads, *ordered(delta_mod_w, sm_d, 0), *ordered(new_m_mod_w, sm_nm, 1),
            *ordered(new_v_mod_w, sm_nv, 2))
```

```python
import functools
import math

import jax
import jax.numpy as jnp
from jax import lax
from jax.experimental import pallas as pl
from jax.experimental.pallas import tpu as pltpu

F32 = jnp.float32
BF16 = jnp.bfloat16

EPS = 1e-6
HEAD_DIM = 128
CHUNK = 128
GROUP = 128
LANES = 128
BF16_ROWS = 16
VMEM_LIMIT_BYTES = 52 * 1024 * 1024
NEG_BIG = -1e30
SMALL_PACK_COLS = 1024

ADAM_LR, ADAM_B1, ADAM_B2, ADAM_EPS, ADAM_WD, ADAM_STEP = 0.001, 0.9, 0.999, 1e-08, 0.01, 10

MESH_AXES = ("x", "y", "c")
MESH = pl.DeviceIdType.MESH


def _pick(n, pref, align):
    t = (min(pref, n) // align) * align
    while t >= align:
        if n % t == 0:
            return t
        t -= align
    return n


def _params(*sem):
    return pltpu.CompilerParams(dimension_semantics=sem, vmem_limit_bytes=VMEM_LIMIT_BYTES)


def _group(axes):
    pos = {ax: lax.axis_index(ax) for ax in MESH_AXES}
    rank = 0
    for ax in axes:
        rank = rank * 2 + pos[ax]
    peers = []
    for mask in range(1, 2 ** len(axes)):
        peer = dict(pos)
        for bit, ax in enumerate(reversed(axes)):
            if (mask >> bit) & 1:
                peer[ax] = 1 - pos[ax]
        peers.append((mask, tuple(peer[ax] for ax in MESH_AXES)))
    return rank, peers


def _slot(ref, gaxis, idx):
    return ref.at[(slice(None),) * gaxis + (idx,)]


D2D_STREAMS = 32
ICI_STREAMS = 8


def _pieces(shape, want):
    idx = [()]
    d = 0
    while d < len(shape) - 2 and len(idx) * shape[d] <= want:
        idx = [i + (k,) for i in idx for k in range(shape[d])]
        d += 1
    if d == len(shape) - 2:
        rows = shape[-2]
        r = max(1, want // len(idx))
        while r > 1 and not (rows % r == 0 and (rows // r) % BF16_ROWS == 0):
            r -= 1
        if r > 1:
            idx = [i + (pl.ds(k * (rows // r), rows // r),) for i in idx for k in range(r)]
    return idx


class _Exchange:
    def __init__(self, x, axes, gaxis, gather, sel=None):
        self.x, self.axes, self.gaxis, self.gather, self.sel = x, axes, gaxis, gather, sel
        n = 2 ** len(axes)
        shape = x.shape if sel is None else x.shape[1:]
        if gather:
            out_shape = shape[:gaxis] + (n,) + shape[gaxis:]
            slab = shape
        else:
            assert shape[gaxis] == n
            out_shape = shape
            slab = shape[:gaxis] + shape[gaxis + 1:]
        self.out_shape = jax.ShapeDtypeStruct(out_shape, x.dtype)
        self.pieces = _pieces(slab, D2D_STREAMS if axes == ("c",) else ICI_STREAMS)
        self.scratch = [pltpu.SemaphoreType.DMA((n - 1,)), pltpu.SemaphoreType.DMA((n - 1,)),
                        pltpu.SemaphoreType.DMA(())]

    def _views(self, x_ref, o_ref):
        if self.sel is not None:
            x_ref = x_ref.at[self.sel]
        src = (lambda member: x_ref) if self.gather else (lambda member: _slot(x_ref, self.gaxis, member))
        dst = lambda member: _slot(o_ref, self.gaxis, member)
        return src, dst

    @staticmethod
    def _remote(mask, peer, s, d, send_sems, recv_sems):
        return pltpu.make_async_remote_copy(
            src_ref=s, dst_ref=d, send_sem=send_sems.at[mask - 1], recv_sem=recv_sems.at[mask - 1],
            device_id=peer, device_id_type=MESH)

    def start(self, x_ref, o_ref, send_sems, recv_sems, local_sem):
        rank, peers = _group(self.axes)
        src, dst = self._views(x_ref, o_ref)
        cut = lambda ref, pc: ref.at[pc] if pc else ref
        for pc in self.pieces:
            pltpu.make_async_copy(cut(src(rank), pc), cut(dst(rank), pc), local_sem).start()
            for mask, peer in peers:
                self._remote(mask, peer, cut(src(rank ^ mask), pc), cut(dst(rank), pc), send_sems, recv_sems).start()

    def finish(self, x_ref, o_ref, send_sems, recv_sems, local_sem):
        rank, peers = _group(self.axes)
        src, dst = self._views(x_ref, o_ref)
        for mask, peer in peers:
            whole = self._remote(mask, peer, src(rank), dst(rank ^ mask), send_sems, recv_sems)
            whole.wait_recv()
            whole.wait_send()
        pltpu.make_async_copy(src(rank), dst(rank), local_sem).wait()


ANY_SPEC = pl.BlockSpec(memory_space=pl.ANY)


def _call(body, *, name, grid, in_specs, out_specs, out_shape, scratch_shapes, args, rides=()):
    out_specs, out_shape = list(out_specs), list(out_shape)
    n_in, n_out, n_s, nr = len(in_specs), len(out_specs), len(scratch_shapes), len(rides)
    if not rides:
        sem = ("arbitrary",) * len(grid)
        outs = pl.pallas_call(body, name=name, grid=grid, in_specs=in_specs, out_specs=out_specs,
                              out_shape=out_shape, scratch_shapes=scratch_shapes,
                              compiler_params=_params(*sem))(*args)
        return list(outs), []

    def carried(*refs):
        k = 0
        parts = []
        for cnt in (n_in, nr, n_out, nr, n_s, 3 * nr):
            parts.append(refs[k:k + cnt])
            k += cnt
        h_in, r_in, h_out, r_out, h_s, r_s = parts
        ids = [pl.program_id(d) for d in range(len(grid))]
        first = functools.reduce(jnp.logical_and, [i == 0 for i in ids])
        last = functools.reduce(jnp.logical_and, [i == g - 1 for i, g in zip(ids, grid)])

        @pl.when(first)
        def _():
            for r, ride in enumerate(rides):
                ride.start(r_in[r], r_out[r], *r_s[3 * r:3 * r + 3])

        body(*h_in, *h_out, *h_s)

        @pl.when(last)
        def _():
            for r, ride in enumerate(rides):
                ride.finish(r_in[r], r_out[r], *r_s[3 * r:3 * r + 3])

    outs = pl.pallas_call(
        carried, name=name, grid=grid,
        in_specs=list(in_specs) + [ANY_SPEC] * nr,
        out_specs=out_specs + [ANY_SPEC] * nr,
        out_shape=out_shape + [r.out_shape for r in rides],
        scratch_shapes=list(scratch_shapes) + [s for r in rides for s in r.scratch],
        compiler_params=_params(*(("arbitrary",) * len(grid))),
    )(*args, *[r.x for r in rides])
    return list(outs[:n_out]), list(outs[n_out:])


def _exchange(x, axes, gaxis, name, gather):
    ex = _Exchange(x, axes, gaxis, gather)

    def body(x_ref, o_ref, send_sems, recv_sems, local_sem):
        ex.start(x_ref, o_ref, send_sems, recv_sems, local_sem)
        ex.finish(x_ref, o_ref, send_sems, recv_sems, local_sem)

    return pl.pallas_call(body, name=name, out_shape=ex.out_shape, in_specs=[ANY_SPEC], out_specs=ANY_SPEC,
                          scratch_shapes=ex.scratch)(x)


def _all_gather(x, axes, gaxis, name):
    return _exchange(x, axes, gaxis, name, True)


def _all_to_all(x, axes, gaxis, name):
    return _exchange(x, axes, gaxis, name, False)


def _cast_half(w, core, name, blk=None, blk_p=None):
    L, n_src, D = w.shape
    if blk is None:
        blk = blk_p = n_src
    n2 = (n_src // blk) * blk_p // 2
    tr = _pick(math.gcd(math.gcd(blk, blk_p), n2), 256, BF16_ROWS)
    nb, src_per, dst_per = n2 // tr, blk // tr, blk_p // tr

    def body(core_ref, w_ref, o_ref):
        inside = (core_ref[0] * nb + pl.program_id(1)) % dst_per < src_per
        o_ref[...] = jnp.where(inside, w_ref[...], 0.0).astype(BF16)

    def src_block(l, i, core_ref):
        at = core_ref[0] * nb + i
        return l, (at // dst_per) * src_per + jnp.minimum(at % dst_per, src_per - 1), 0

    return pl.pallas_call(
        body, name=name,
        grid_spec=pltpu.PrefetchScalarGridSpec(
            num_scalar_prefetch=1, grid=(L, nb),
            in_specs=[pl.BlockSpec((None, tr, D), src_block)],
            out_specs=pl.BlockSpec((None, tr, D), lambda l, i, core_ref: (l, i, 0))),
        out_shape=jax.ShapeDtypeStruct((L, n2, D), BF16),
        compiler_params=_params("parallel", "parallel"),
    )(core, w)


def _sum_slots(x, out_dtype, name):
    A, G, R, C = x.shape
    budget = (2 * 1024 * 1024) // (G * C * x.dtype.itemsize)
    tr = _pick(R, max(budget, BF16_ROWS), BF16_ROWS)

    def body(x_ref, o_ref):
        acc = x_ref[0].astype(F32)
        for g in range(1, G):
            acc = acc + x_ref[g].astype(F32)
        o_ref[...] = acc.astype(out_dtype)

    return pl.pallas_call(
        body, name=name, grid=(A, R // tr),
        in_specs=[pl.BlockSpec((None, G, tr, C), lambda a, i: (a, 0, i, 0))],
        out_specs=pl.BlockSpec((None, tr, C), lambda a, i: (a, i, 0)),
        out_shape=jax.ShapeDtypeStruct((A, R, C), out_dtype),
        compiler_params=_params("parallel", "parallel"),
    )(x)


def _adamw_math(w, g, m, v):
    m = ADAM_B1 * m + (1.0 - ADAM_B1) * g
    v = ADAM_B2 * v + (1.0 - ADAM_B2) * (g * g)
    m_hat = m / (1.0 - ADAM_B1 ** ADAM_STEP)
    v_hat = v / (1.0 - ADAM_B2 ** ADAM_STEP)
    delta = -ADAM_LR * (m_hat / (jnp.sqrt(v_hat) + ADAM_EPS) + ADAM_WD * w)
    return delta, m, v


def _adamw(w, g, m, v, name):
    R, C = w.shape
    budget = (768 * 1024) // (C * 4)
    tr = _pick(R, max(budget, 8), 8)

    def body(w_ref, g_ref, m_ref, v_ref, d_ref, nm_ref, nv_ref):
        d, nm, nv = _adamw_math(w_ref[...], g_ref[...], m_ref[...], v_ref[...])
        d_ref[...] = d
        nm_ref[...] = nm
        nv_ref[...] = nv

    spec = pl.BlockSpec((tr, C), lambda i: (i, 0))
    return pl.pallas_call(
        body, name=name, grid=(R // tr,),
        in_specs=[spec] * 4, out_specs=[spec] * 3,
        out_shape=[jax.ShapeDtypeStruct((R, C), F32)] * 3,
        compiler_params=_params("parallel"),
    )(w, g, m, v)


def _modw_adamw(ct, dm, w, m, v, name, rides=()):
    L, D, N = w.shape
    E = ct.shape[1]
    tr = _pick(D, 64, 8)

    def body(ct_ref, dm_ref, w_ref, m_ref, v_ref, g_ref, d_ref, nm_ref, nv_ref):
        g = ct_ref[:, 0:1] * dm_ref[0:1, :]
        for e in range(1, E):
            g = g + ct_ref[:, e:e + 1] * dm_ref[e:e + 1, :]
        d, nm, nv = _adamw_math(w_ref[...], g, m_ref[...], v_ref[...])
        g_ref[...] = g
        d_ref[...] = d
        nm_ref[...] = nm
        nv_ref[...] = nv

    spec = pl.BlockSpec((None, tr, N), lambda l, i: (l, i, 0))
    outs, carried = _call(
        body, name=name, grid=(L, D // tr),
        in_specs=[pl.BlockSpec((tr, E), lambda l, i: (i, 0)),
                  pl.BlockSpec((None, E, N), lambda l, i: (l, 0, 0)), spec, spec, spec],
        out_specs=[spec] * 4,
        out_shape=[jax.ShapeDtypeStruct((L, D, N), F32)] * 4,
        scratch_shapes=[], args=(ct, dm, w, m, v), rides=rides)
    return (outs, carried) if rides else outs


def _mod_fwd(c_all, w, b, name):
    L, D, N = w.shape
    E = c_all.shape[0]
    tn = _pick(N, 512, LANES)

    def body(c_ref, w_ref, b_ref, act_ref, o_ref):
        cv = c_ref[...]
        act = cv / (1.0 + jnp.exp(-cv))
        act_ref[...] = act
        o_ref[...] = jnp.dot(act.astype(BF16), w_ref[...].astype(BF16),
                             preferred_element_type=F32) + b_ref[...]

    return pl.pallas_call(
        body, name=name, grid=(L, N // tn),
        in_specs=[pl.BlockSpec((E, D), lambda l, j: (0, 0)),
                  pl.BlockSpec((None, D, tn), lambda l, j: (l, 0, j)),
                  pl.BlockSpec((None, 1, tn), lambda l, j: (l, 0, j))],
        out_specs=[pl.BlockSpec((E, D), lambda l, j: (0, 0)),
                   pl.BlockSpec((None, E, tn), lambda l, j: (l, 0, j))],
        out_shape=[jax.ShapeDtypeStruct((E, D), F32), jax.ShapeDtypeStruct((L, E, N), F32)],
        compiler_params=_params("arbitrary", "arbitrary"),
    )(c_all, w, b.reshape(L, 1, N))


_DIMS = {"nn": (((1,), (0,)), ((), ())), "nt": (((1,), (1,)), ((), ())), "tn": (((0,), (0,)), ((), ()))}


def _matmul(a, b, form, name, *, out_dtype=BF16, tm=512, tn=512, tk=None, a_resident=False,
            res=None, gate=None, emit_y=False, rides=()):
    if form == "nn":
        (M, K), N = a.shape, b.shape[1]
    elif form == "nt":
        (M, K), N = a.shape, b.shape[0]
    else:
        (K, M), N = a.shape, b.shape[1]
    tm = _pick(M, tm, LANES if form == "tn" else BF16_ROWS)
    tn = _pick(N, tn, LANES)
    tk = K if (tk is None or form == "tn") else _pick(K, tk, LANES)
    nk = K // tk

    def ij(g0, g1):
        return (g0, g1) if a_resident else (g1, g0)

    grid = (M // tm, N // tn, nk) if a_resident else (N // tn, M // tm, nk)
    if form == "tn":
        a_spec = pl.BlockSpec((K, tm), lambda g0, g1, k: (0, ij(g0, g1)[0]))
        b_spec = pl.BlockSpec((K, tn), lambda g0, g1, k: (0, ij(g0, g1)[1]))
    else:
        a_spec = pl.BlockSpec((tm, tk), lambda g0, g1, k: (ij(g0, g1)[0], k))
        if form == "nn":
            b_spec = pl.BlockSpec((tk, tn), lambda g0, g1, k: (k, ij(g0, g1)[1]))
        else:
            b_spec = pl.BlockSpec((tn, tk), lambda g0, g1, k: (ij(g0, g1)[1], k))
    tile = pl.BlockSpec((tm, tn), lambda g0, g1, k: ij(g0, g1))
    row = pl.BlockSpec((1, tn), lambda g0, g1, k: (0, ij(g0, g1)[1]))
    in_specs, args = [a_spec, b_spec], [a, b]
    if res is not None:
        in_specs.append(tile)
        args.append(res)
    if gate is not None:
        in_specs.append(row)
        args.append(gate)
    if emit_y:
        out_specs = [tile, tile]
        out_shape = [jax.ShapeDtypeStruct((M, N), BF16), jax.ShapeDtypeStruct((M, N), out_dtype)]
    else:
        out_specs = tile
        out_shape = jax.ShapeDtypeStruct((M, N), out_dtype)
    n_in = len(args)

    def body(*refs):
        a_ref, b_ref = refs[0], refs[1]
        res_ref = refs[2] if res is not None else None
        gate_ref = refs[n_in - 1] if gate is not None else None
        outs = refs[n_in:n_in + (2 if emit_y else 1)]

        def finish(acc):
            val = acc
            if gate_ref is not None:
                val = gate_ref[...] * val
            if res_ref is not None:
                val = res_ref[...] + val
            if emit_y:
                outs[0][...] = acc.astype(BF16)
            outs[-1][...] = val.astype(out_dtype)

        part = lax.dot_general(a_ref[...], b_ref[...], _DIMS[form], preferred_element_type=F32)
        if nk == 1:
            finish(part)
        else:
            acc_ref = refs[-1]
            k = pl.program_id(2)

            @pl.when(k == 0)
            def _():
                acc_ref[...] = part

            @pl.when(k > 0)
            def _():
                acc_ref[...] += part

            @pl.when(k == nk - 1)
            def _():
                finish(acc_ref[...])

    outs, carried = _call(
        body, name=name, grid=grid, in_specs=in_specs, out_specs=out_specs if emit_y else [out_specs],
        out_shape=out_shape if emit_y else [out_shape],
        scratch_shapes=[pltpu.VMEM((tm, tn), F32)] if nk > 1 else [], args=args, rides=rides)
    result = outs if emit_y else outs[0]
    return (result, carried) if rides else result


def _norm_mod(x, g, sc, sh, name):
    T, D = x.shape
    tm = _pick(T, 256, BF16_ROWS)

    def body(x_ref, g_ref, sc_ref, sh_ref, h_ref):
        xv = x_ref[...]
        rstd = lax.rsqrt(jnp.mean(xv * xv, axis=-1, keepdims=True) + EPS)
        y = xv * rstd * g_ref[...]
        h_ref[...] = (y * (1.0 + sc_ref[...]) + sh_ref[...]).astype(BF16)

    tile = pl.BlockSpec((tm, D), lambda i: (i, 0))
    row = pl.BlockSpec((1, D), lambda i: (0, 0))
    return pl.pallas_call(
        body, name=name, grid=(T // tm,), in_specs=[tile, row, row, row], out_specs=tile,
        out_shape=jax.ShapeDtypeStruct((T, D), BF16), compiler_params=_params("parallel"),
    )(x, g, sc, sh)


def _colsum(v):
    return jnp.sum(v, axis=0, keepdims=True)


def _norm_bwd(x, g, sc, sh, dh, dres, prev, name):
    T, D = x.shape
    tm = _pick(T, 256, BF16_ROWS)
    has_prev = prev is not None

    def body(*refs):
        x_ref, g_ref, sc_ref, sh_ref, dh_ref, dres_ref = refs[:6]
        k = 6
        if has_prev:
            y_ref, gate_ref = refs[6:8]
            k = 8
        dx_ref, dsc_ref, dsh_ref, dg_ref = refs[k:k + 4]
        first = pl.program_id(0) == 0
        xv = x_ref[...]
        rstd = lax.rsqrt(jnp.mean(xv * xv, axis=-1, keepdims=True) + EPS)
        xhat = xv * rstd
        gv = g_ref[...]
        dhv = dh_ref[...].astype(F32)
        dyv = dhv * (1.0 + sc_ref[...])
        dxhat = dyv * gv
        dx = rstd * (dxhat - xhat * jnp.mean(dxhat * xhat, axis=-1, keepdims=True))
        dxt = dres_ref[...] + dx
        dx_ref[...] = dxt
        sums = [(dsc_ref, _colsum(dhv * (xhat * gv))), (dsh_ref, _colsum(dhv)), (dg_ref, _colsum(dyv * xhat))]
        if has_prev:
            dy_ref, dgate_ref = refs[k + 4:k + 6]
            dy_ref[...] = (dxt * gate_ref[...]).astype(BF16)
            sums.append((dgate_ref, _colsum(dxt * y_ref[...].astype(F32))))
        for ref, val in sums:
            @pl.when(first)
            def _(ref=ref, val=val):
                ref[...] = val

            @pl.when(jnp.logical_not(first))
            def _(ref=ref, val=val):
                ref[...] += val

    tile = pl.BlockSpec((tm, D), lambda i: (i, 0))
    row = pl.BlockSpec((1, D), lambda i: (0, 0))
    rowshape = jax.ShapeDtypeStruct((1, D), F32)
    in_specs = [tile, row, row, row, tile, tile]
    args = [x, g, sc, sh, dh, dres]
    out_specs = [tile, row, row, row]
    out_shape = [jax.ShapeDtypeStruct((T, D), F32), rowshape, rowshape, rowshape]
    if has_prev:
        in_specs += [tile, row]
        args += list(prev)
        out_specs += [tile, row]
        out_shape += [jax.ShapeDtypeStruct((T, D), BF16), rowshape]
    return pl.pallas_call(
        body, name=name, grid=(T // tm,), in_specs=in_specs, out_specs=out_specs, out_shape=out_shape,
        compiler_params=_params("arbitrary"),
    )(*args)


def _final_loss_bwd(x, g, target, y_prev, gate_prev, name):
    T, D = x.shape
    tm = _pick(T, 256, BF16_ROWS)

    def body(x_ref, g_ref, t_ref, y_ref, gate_ref, loss_ref, dx_ref, dg_ref, dy_ref, dgate_ref):
        first = pl.program_id(0) == 0
        xv = x_ref[...]
        rstd = lax.rsqrt(jnp.mean(xv * xv, axis=-1, keepdims=True) + EPS)
        xhat = xv * rstd
        gv = g_ref[...]
        err = xhat * gv - t_ref[...]
        part = 0.5 * jnp.sum(jnp.mean(err * err, axis=-1, keepdims=True))
        dyf = err * (1.0 / D)
        dxhat = dyf * gv
        dx = rstd * (dxhat - xhat * jnp.mean(dxhat * xhat, axis=-1, keepdims=True))
        dx_ref[...] = dx
        dy_ref[...] = (dx * gate_ref[...]).astype(BF16)
        sums = [(dg_ref, _colsum(dyf * xhat)), (dgate_ref, _colsum(dx * y_ref[...].astype(F32))),
                (loss_ref, jnp.full((8, LANES), part, F32))]
        for ref, val in sums:
            @pl.when(first)
            def _(ref=ref, val=val):
                ref[...] = val

            @pl.when(jnp.logical_not(first))
            def _(ref=ref, val=val):
                ref[...] += val

    tile = pl.BlockSpec((tm, D), lambda i: (i, 0))
    row = pl.BlockSpec((1, D), lambda i: (0, 0))
    rowshape = jax.ShapeDtypeStruct((1, D), F32)
    return pl.pallas_call(
        body, name=name, grid=(T // tm,),
        in_specs=[tile, row, tile, tile, row],
        out_specs=[pl.BlockSpec((8, LANES), lambda i: (0, 0)), tile, row, tile, row],
        out_shape=[jax.ShapeDtypeStruct((8, LANES), F32), jax.ShapeDtypeStruct((T, D), F32), rowshape,
                   jax.ShapeDtypeStruct((T, D), BF16), rowshape],
        compiler_params=_params("arbitrary"),
    )(x, g, target, y_prev, gate_prev)


def _split3(v):
    hi = v.astype(BF16)
    r1 = v - hi.astype(F32)
    mid = r1.astype(BF16)
    lo = (r1 - mid.astype(F32)).astype(BF16)
    return hi, mid, lo


def _tri_dot(tri, v):
    hi, mid, lo = _split3(v)
    dot = functools.partial(jnp.dot, preferred_element_type=F32)
    return dot(tri, hi) + dot(tri, mid) + dot(tri, lo)


def _fox_gate(flog, bias, name):
    T = flog.shape[0]
    tb = _pick(T, 256, BF16_ROWS)

    def body(f_ref, b_ref, o_ref, carry_ref):
        @pl.when(pl.program_id(0) == 0)
        def _():
            carry_ref[...] = jnp.zeros_like(carry_ref)

        xv = f_ref[...] + b_ref[...]
        lf = jnp.minimum(xv, 0.0) - jnp.log(1.0 + jnp.exp(-jnp.abs(xv)))
        r = lax.broadcasted_iota(jnp.int32, (tb, tb), 0)
        cidx = lax.broadcasted_iota(jnp.int32, (tb, tb), 1)
        tri = jnp.where(r >= cidx, 1.0, 0.0).astype(BF16)
        out = _tri_dot(tri, lf) + carry_ref[...]
        o_ref[...] = out
        carry_ref[...] = out[tb - 1:tb, :]

    return pl.pallas_call(
        body, name=name, grid=(T // tb,),
        in_specs=[pl.BlockSpec((tb, LANES), lambda i: (i, 0)), pl.BlockSpec((1, LANES), lambda i: (0, 0))],
        out_specs=pl.BlockSpec((tb, LANES), lambda i: (i, 0)),
        out_shape=jax.ShapeDtypeStruct((T, LANES), F32),
        scratch_shapes=[pltpu.VMEM((1, LANES), F32)],
        compiler_params=_params("arbitrary"),
    )(flog, bias)


def _fox_gate_bwd(dfq, dfk, flog, bias, name):
    T = flog.shape[0]
    tb = _pick(T, 256, BF16_ROWS)
    nb = T // tb

    def body(dq_ref, dk_ref, f_ref, b_ref, o_ref, db_ref, carry_ref):
        first = pl.program_id(0) == 0

        @pl.when(first)
        def _():
            carry_ref[...] = jnp.zeros_like(carry_ref)

        dv = dq_ref[...] + dk_ref[...]
        r = lax.broadcasted_iota(jnp.int32, (tb, tb), 0)
        cidx = lax.broadcasted_iota(jnp.int32, (tb, tb), 1)
        tri = jnp.where(cidx >= r, 1.0, 0.0).astype(BF16)
        dl = _tri_dot(tri, dv) + carry_ref[...]
        carry_ref[...] = dl[0:1, :]
        xv = f_ref[...] + b_ref[...]
        dfl = dl / (1.0 + jnp.exp(xv))
        o_ref[...] = dfl.astype(BF16)
        val = _colsum(dfl)

        @pl.when(first)
        def _():
            db_ref[...] = val

        @pl.when(jnp.logical_not(first))
        def _():
            db_ref[...] += val

    rev = pl.BlockSpec((tb, LANES), lambda i: (nb - 1 - i, 0))
    row = pl.BlockSpec((1, LANES), lambda i: (0, 0))
    return pl.pallas_call(
        body, name=name, grid=(nb,), in_specs=[rev, rev, rev, row], out_specs=[rev, row],
        out_shape=[jax.ShapeDtypeStruct((T, LANES), BF16), jax.ShapeDtypeStruct((1, LANES), F32)],
        scratch_shapes=[pltpu.VMEM((1, LANES), F32)],
        compiler_params=_params("arbitrary"),
    )(dfq, dfk, flog, bias)


_NT = _DIMS["nt"]
_TN = _DIMS["tn"]


ATTN_TQ, ATTN_TK = 1024, 1024
ATTN_TKB, ATTN_TQI = 1024, 1024


def _visible(tq, tk, shift):
    r = lax.broadcasted_iota(jnp.int32, (tq, tk), 0)
    c = lax.broadcasted_iota(jnp.int32, (tq, tk), 1)
    return c + shift <= r


def _attn_fwd(qkv, frow, n_heads, name, rides=()):
    T = qkv.shape[0]
    H = n_heads
    tq = _pick(T, ATTN_TQ, LANES)
    tk = _pick(tq, ATTN_TK, LANES)
    per = tq // tk
    scale = HEAD_DIM ** -0.5

    def body(q_ref, k_ref, v_ref, fr_ref, o_ref, lse_ref):
        i = pl.program_id(1)
        q = q_ref[...]

        def step(j, carry, shift):
            m, l, acc = carry
            ks = pl.multiple_of(j * tk, tk)
            kb = k_ref[pl.ds(ks, tk), :]
            vb = v_ref[pl.ds(ks, tk), :]
            s = lax.dot_general(q, kb, _NT, preferred_element_type=F32) * scale - fr_ref[:, pl.ds(ks, tk)]
            if shift is not None:
                s = jnp.where(_visible(tq, tk, shift), s, NEG_BIG)
            m_new = jnp.maximum(m, jnp.max(s, axis=1, keepdims=True))
            alpha = jnp.exp(m - m_new)
            p = jnp.exp(s - m_new)
            l = alpha * l + jnp.sum(p, axis=1, keepdims=True)
            acc = alpha * acc + jnp.dot(p.astype(BF16), vb, preferred_element_type=F32)
            return m_new, l, acc

        carry = (jnp.full((tq, 1), NEG_BIG, F32), jnp.zeros((tq, 1), F32), jnp.zeros((tq, HEAD_DIM), F32))
        carry = lax.fori_loop(0, i * per, lambda j, c: step(j, c, None), carry)
        for d in range(per):
            carry = step(i * per + d, carry, d * tk)
        m, l, acc = carry
        o_ref[...] = (acc / l).astype(BF16)
        lse_ref[...] = jnp.broadcast_to(m + jnp.log(l), (tq, LANES))

    outs, carried = _call(
        body, name=name, grid=(H, T // tq),
        in_specs=[pl.BlockSpec((tq, HEAD_DIM), lambda h, i: (i, h)),
                  pl.BlockSpec((T, HEAD_DIM), lambda h, i: (0, H + h)),
                  pl.BlockSpec((T, HEAD_DIM), lambda h, i: (0, 2 * H + h)),
                  pl.BlockSpec((None, 1, T), lambda h, i: (h, 0, 0))],
        out_specs=[pl.BlockSpec((tq, HEAD_DIM), lambda h, i: (i, h)),
                   pl.BlockSpec((None, tq, LANES), lambda h, i: (h, i, 0))],
        out_shape=[jax.ShapeDtypeStruct((T, H * HEAD_DIM), BF16), jax.ShapeDtypeStruct((H, T, LANES), F32)],
        scratch_shapes=[], args=(qkv, qkv, qkv, frow), rides=rides)
    return (outs, carried) if rides else outs


def _attn_dq(qkv, o, do, lse, frow, n_heads, name):
    T = qkv.shape[0]
    H = n_heads
    tq = _pick(T, ATTN_TQ, LANES)
    tk = _pick(tq, ATTN_TK, LANES)
    per = tq // tk
    scale = HEAD_DIM ** -0.5

    def body(q_ref, k_ref, v_ref, o_ref, do_ref, lse_ref, fr_ref, dq_ref, df_ref):
        i = pl.program_id(1)
        q = q_ref[...]
        dob = do_ref[...]
        delta = jnp.sum(dob.astype(F32) * o_ref[...].astype(F32), axis=1, keepdims=True)
        lse_i = lse_ref[:, 0:1]

        def step(j, carry, shift):
            dq, rsum = carry
            ks = pl.multiple_of(j * tk, tk)
            kb = k_ref[pl.ds(ks, tk), :]
            vb = v_ref[pl.ds(ks, tk), :]
            s = lax.dot_general(q, kb, _NT, preferred_element_type=F32) * scale - fr_ref[:, pl.ds(ks, tk)]
            p = jnp.exp(s - lse_i)
            if shift is not None:
                p = jnp.where(_visible(tq, tk, shift), p, 0.0)
            dp = lax.dot_general(dob, vb, _NT, preferred_element_type=F32)
            ds = p * (dp - delta)
            return (dq + jnp.dot(ds.astype(BF16), kb, preferred_element_type=F32),
                    rsum + jnp.sum(ds, axis=1, keepdims=True))

        carry = (jnp.zeros((tq, HEAD_DIM), F32), jnp.zeros((tq, 1), F32))
        carry = lax.fori_loop(0, i * per, lambda j, c: step(j, c, None), carry)
        for d in range(per):
            carry = step(i * per + d, carry, d * tk)
        dq, rsum = carry
        dq_ref[...] = (dq * scale).astype(BF16)
        df_ref[...] = jnp.transpose(jnp.broadcast_to(rsum, (tq, LANES)))[0:1, :]

    qtile = lambda off: pl.BlockSpec((tq, HEAD_DIM), lambda h, i: (i, off + h))
    return pl.pallas_call(
        body, name=name, grid=(H, T // tq),
        in_specs=[qtile(0),
                  pl.BlockSpec((T, HEAD_DIM), lambda h, i: (0, H + h)),
                  pl.BlockSpec((T, HEAD_DIM), lambda h, i: (0, 2 * H + h)),
                  qtile(0), qtile(0),
                  pl.BlockSpec((None, tq, LANES), lambda h, i: (h, i, 0)),
                  pl.BlockSpec((None, 1, T), lambda h, i: (h, 0, 0))],
        out_specs=[qtile(0), pl.BlockSpec((None, 1, tq), lambda h, i: (h, 0, i))],
        out_shape=[jax.ShapeDtypeStruct((T, H * HEAD_DIM), BF16), jax.ShapeDtypeStruct((H, 1, T), F32)],
        compiler_params=_params("parallel", "arbitrary"),
    )(qkv, qkv, qkv, o, do, lse, frow)


def _attn_dkv(qkv, o, do, lse, frow, n_heads, name, rides=()):
    T = qkv.shape[0]
    H = n_heads
    tk = _pick(T, ATTN_TKB, LANES)
    tq = _pick(tk, ATTN_TQI, LANES)
    per = tk // tq
    nq = T // tq
    scale = HEAD_DIM ** -0.5

    def body(k_ref, v_ref, q_ref, o_ref, do_ref, lse_ref, fr_ref, dk_ref, dv_ref, df_ref):
        j = pl.program_id(1)
        kb = k_ref[...]
        vb = v_ref[...]
        fj = fr_ref[...]

        def step(i, carry, shift):
            dk, dv, df = carry
            qs = pl.multiple_of(i * tq, tq)
            qb = q_ref[pl.ds(qs, tq), :]
            dob = do_ref[pl.ds(qs, tq), :]
            delta = jnp.sum(dob.astype(F32) * o_ref[pl.ds(qs, tq), :].astype(F32), axis=1, keepdims=True)
            lse_i = lse_ref[pl.ds(qs, tq), 0:1]
            s = lax.dot_general(qb, kb, _NT, preferred_element_type=F32) * scale - fj
            p = jnp.exp(s - lse_i)
            if shift is not None:
                p = jnp.where(_visible(tq, tk, shift), p, 0.0)
            dv = dv + lax.dot_general(p.astype(BF16), dob, _TN, preferred_element_type=F32)
            dp = lax.dot_general(dob, vb, _NT, preferred_element_type=F32)
            ds = p * (dp - delta)
            dk = dk + lax.dot_general(ds.astype(BF16), qb, _TN, preferred_element_type=F32)
            df = df - _colsum(ds)
            return dk, dv, df

        carry = (jnp.zeros((tk, HEAD_DIM), F32), jnp.zeros((tk, HEAD_DIM), F32), jnp.zeros((1, tk), F32))
        for d in range(per):
            carry = step(j * per + d, carry, -d * tq)
        dk, dv, df = lax.fori_loop((j + 1) * per, nq, lambda i, c: step(i, c, None), carry)
        dk_ref[...] = (dk * scale).astype(BF16)
        dv_ref[...] = dv.astype(BF16)
        df_ref[...] = df

    full = lambda off: pl.BlockSpec((T, HEAD_DIM), lambda h, j: (0, off + h))
    ktile = lambda off: pl.BlockSpec((tk, HEAD_DIM), lambda h, j: (j, off + h))
    outs, carried = _call(
        body, name=name, grid=(H, T // tk),
        in_specs=[ktile(H), ktile(2 * H), full(0), full(0), full(0),
                  pl.BlockSpec((None, T, LANES), lambda h, j: (h, 0, 0)),
                  pl.BlockSpec((None, 1, tk), lambda h, j: (h, 0, j))],
        out_specs=[ktile(0), ktile(0), pl.BlockSpec((None, 1, tk), lambda h, j: (h, 0, j))],
        out_shape=[jax.ShapeDtypeStruct((T, H * HEAD_DIM), BF16)] * 2 + [jax.ShapeDtypeStruct((H, 1, T), F32)],
        scratch_shapes=[], args=(qkv, qkv, qkv, o, do, lse, frow), rides=rides)
    return (outs, carried) if rides else outs


_GELU_C = math.sqrt(2.0 / math.pi)
_GELU_A = 0.044715


def _gelu(p):
    t = jnp.tanh(_GELU_C * (p + _GELU_A * p * p * p))
    return 0.5 * p * (1.0 + t), t


def _gelu_grad(p, t):
    return 0.5 * (1.0 + t) + 0.5 * p * (1.0 - t * t) * _GELU_C * (1.0 + 3.0 * _GELU_A * p * p)


def _tril(shape, upper=False):
    r = lax.broadcasted_iota(jnp.int32, shape, 0)
    c = lax.broadcasted_iota(jnp.int32, shape, 1)
    return (c >= r) if upper else (r >= c)


def _gm_fwd(p, vg, ws, bt, name):
    T, D2 = p.shape
    D = D2 // 2
    NG = D // GROUP
    tm = _pick(T, 2 * CHUNK, CHUNK)

    def body(p_ref, vg_ref, ws_ref, bt_ref, o_ref):
        z, _ = _gelu(p_ref[...].astype(F32))
        u, v = z[:, :D], z[:, D:]
        rstd = lax.rsqrt(jnp.mean(v * v, axis=-1, keepdims=True) + EPS)
        vn = (v * rstd * vg_ref[...]).astype(BF16)
        low = _tril((CHUNK, CHUNK))
        for g in range(NG):
            wm = jnp.where(low, ws_ref[g], 0.0).astype(BF16)
            bcol = bt_ref[:, g:g + 1]
            cs = slice(g * GROUP, (g + 1) * GROUP)
            for r in range(tm // CHUNK):
                rs = slice(r * CHUNK, (r + 1) * CHUNK)
                sv = jnp.dot(wm, vn[rs, cs], preferred_element_type=F32) + bcol
                o_ref[rs, cs] = (u[rs, cs] * sv).astype(BF16)

    return pl.pallas_call(
        body, name=name, grid=(T // tm,),
        in_specs=[pl.BlockSpec((tm, D2), lambda i: (i, 0)), pl.BlockSpec((1, D), lambda i: (0, 0)),
                  pl.BlockSpec((NG, CHUNK, CHUNK), lambda i: (0, 0, 0)),
                  pl.BlockSpec((CHUNK, LANES), lambda i: (0, 0))],
        out_specs=pl.BlockSpec((tm, D), lambda i: (i, 0)),
        out_shape=jax.ShapeDtypeStruct((T, D), BF16),
        compiler_params=_params("parallel"),
    )(p, vg, ws, bt)


def _gm_bwd(p, dgated, vg, ws, wst, bt, name):
    T, D2 = p.shape
    D = D2 // 2
    NG = D // GROUP
    tm = _pick(T, 2 * CHUNK, CHUNK)
    nb = T // tm

    def body(p_ref, dg_ref, vg_ref, ws_ref, wst_ref, bt_ref, dp_ref, dws_ref, dbt_ref, dvg_ref, du_s, dvn_s):
        step = pl.program_id(0)

        @pl.when(step == 0)
        def _():
            dws_ref[...] = jnp.zeros_like(dws_ref)
            dbt_ref[...] = jnp.zeros_like(dbt_ref)
            dvg_ref[...] = jnp.zeros_like(dvg_ref)

        pv = p_ref[...].astype(F32)
        z, t = _gelu(pv)
        u, v = z[:, :D], z[:, D:]
        rstd = lax.rsqrt(jnp.mean(v * v, axis=-1, keepdims=True) + EPS)
        vhat = v * rstd
        vgv = vg_ref[...]
        vn = (vhat * vgv).astype(BF16)
        dgt = dg_ref[...].astype(F32)
        low = _tril((CHUNK, CHUNK))
        up = _tril((CHUNK, CHUNK), upper=True)
        lane = lax.broadcasted_iota(jnp.int32, (CHUNK, LANES), 1)
        dbt = jnp.zeros((CHUNK, LANES), F32)
        for g in range(NG):
            wm = jnp.where(low, ws_ref[g], 0.0).astype(BF16)
            wmt = jnp.where(up, wst_ref[g], 0.0).astype(BF16)
            bcol = bt_ref[:, g:g + 1]
            cs = slice(g * GROUP, (g + 1) * GROUP)
            dw = jnp.zeros((CHUNK, CHUNK), F32)
            for r in range(tm // CHUNK):
                rs = slice(r * CHUNK, (r + 1) * CHUNK)
                blk = vn[rs, cs]
                sv = jnp.dot(wm, blk, preferred_element_type=F32) + bcol
                dgb = dgt[rs, cs]
                du_s[rs, cs] = dgb * sv
                dsv = dgb * u[rs, cs]
                dsvb = dsv.astype(BF16)
                dvn_s[rs, cs] = jnp.dot(wmt, dsvb, preferred_element_type=F32)
                dw = dw + lax.dot_general(dsvb, blk, _NT, preferred_element_type=F32)
                dbt = dbt + jnp.where(lane == g, jnp.sum(dsv, axis=1, keepdims=True), 0.0)
            dws_ref[g] += dw
        dbt_ref[...] += dbt
        dvn = dvn_s[...]
        dvg_ref[...] += _colsum(dvn * vhat)
        dvhat = dvn * vgv
        dv = rstd * (dvhat - vhat * jnp.mean(dvhat * vhat, axis=-1, keepdims=True))
        gg = _gelu_grad(pv, t)
        dp_ref[:, :D] = (du_s[...] * gg[:, :D]).astype(BF16)
        dp_ref[:, D:] = (dv * gg[:, D:]).astype(BF16)

        @pl.when(step == nb - 1)
        def _():
            for g in range(NG):
                dws_ref[g] = jnp.where(low, dws_ref[g], 0.0)

    const3 = pl.BlockSpec((NG, CHUNK, CHUNK), lambda i: (0, 0, 0))
    return pl.pallas_call(
        body, name=name, grid=(nb,),
        in_specs=[pl.BlockSpec((tm, D2), lambda i: (i, 0)), pl.BlockSpec((tm, D), lambda i: (i, 0)),
                  pl.BlockSpec((1, D), lambda i: (0, 0)), const3, const3,
                  pl.BlockSpec((CHUNK, LANES), lambda i: (0, 0))],
        out_specs=[pl.BlockSpec((tm, D2), lambda i: (i, 0)), const3,
                   pl.BlockSpec((CHUNK, LANES), lambda i: (0, 0)), pl.BlockSpec((1, D), lambda i: (0, 0))],
        out_shape=[jax.ShapeDtypeStruct((T, D2), BF16), jax.ShapeDtypeStruct((NG, CHUNK, CHUNK), F32),
                   jax.ShapeDtypeStruct((CHUNK, LANES), F32), jax.ShapeDtypeStruct((1, D), F32)],
        scratch_shapes=[pltpu.VMEM((tm, D), F32), pltpu.VMEM((tm, D), F32)],
        compiler_params=_params("arbitrary"),
    )(p, dgated, vg, ws, wst, bt)


HALO = BF16_ROWS


def _ffn_act(a, cw, cb, name):
    T, F2 = a.shape
    Fh = F2 // 2
    tm = _pick(T, 256, HALO)
    per = tm // HALO
    nch = Fh // LANES

    def body(a_ref, h_ref, cw_ref, cb_ref, o_ref):
        i = pl.program_id(0)
        row = lax.broadcasted_iota(jnp.int32, (tm, LANES), 0)
        keep = jnp.where(i > 0, 1.0, 0.0)

        def conv(off):
            af = a_ref[:, pl.ds(off, LANES)].astype(F32)
            hf = h_ref[:, pl.ds(off, LANES)].astype(F32) * keep
            r1 = jnp.where(row == 0, hf[HALO - 1:HALO], pltpu.roll(af, 1, 0))
            r2 = jnp.where(row == 0, hf[HALO - 2:HALO - 1], jnp.where(row == 1, hf[HALO - 1:HALO], pltpu.roll(af, 2, 0)))
            w = cw_ref[:, pl.ds(off, LANES)]
            return w[0:1] * r2 + w[1:2] * r1 + w[2:3] * af + cb_ref[:, pl.ds(off, LANES)]

        def chunk(c, carry):
            off = pl.multiple_of(c * LANES, LANES)
            gt = conv(off)
            upv = conv(pl.multiple_of(off + Fh, LANES))
            o_ref[:, pl.ds(off, LANES)] = (gt / (1.0 + jnp.exp(-gt)) * upv).astype(BF16)
            return carry

        lax.fori_loop(0, nch, chunk, 0)

    return pl.pallas_call(
        body, name=name, grid=(T // tm,),
        in_specs=[pl.BlockSpec((tm, F2), lambda i: (i, 0)),
                  pl.BlockSpec((HALO, F2), lambda i: (jnp.maximum(i * per - 1, 0), 0)),
                  pl.BlockSpec((3, F2), lambda i: (0, 0)), pl.BlockSpec((1, F2), lambda i: (0, 0))],
        out_specs=pl.BlockSpec((tm, Fh), lambda i: (i, 0)),
        out_shape=jax.ShapeDtypeStruct((T, Fh), BF16),
        compiler_params=_params("parallel"),
    )(a, a, cw, cb)


def _ffn_act_bwd(a, dhm, cw, cb, name, rides=()):
    T, F2 = a.shape
    Fh = F2 // 2
    tm = _pick(T, 256, HALO)
    per = tm // HALO
    nb = T // tm
    last_halo = T // HALO - 1
    nch = Fh // LANES
    ext = tm + 2 * HALO

    def body(a_ref, hp_ref, hn_ref, d_ref, dn_ref, cw_ref, cb_ref, da_ref, dcw_ref, dcb_ref):
        i = pl.program_id(0)

        @pl.when(i == 0)
        def _():
            dcw_ref[...] = jnp.zeros_like(dcw_ref)
            dcb_ref[...] = jnp.zeros_like(dcb_ref)

        keep_p = jnp.where(i > 0, 1.0, 0.0)
        keep_n = jnp.where(i < nb - 1, 1.0, 0.0)
        row = lax.broadcasted_iota(jnp.int32, (ext, LANES), 0)
        main = jnp.logical_and(row >= HALO, row < HALO + tm)

        def conv(off):
            e = jnp.concatenate([hp_ref[:, pl.ds(off, LANES)].astype(F32) * keep_p,
                                 a_ref[:, pl.ds(off, LANES)].astype(F32),
                                 hn_ref[:, pl.ds(off, LANES)].astype(F32) * keep_n], axis=0)
            r1 = pltpu.roll(e, 1, 0)
            r2 = pltpu.roll(e, 2, 0)
            w = cw_ref[:, pl.ds(off, LANES)]
            return w, e, r1, r2, w[0:1] * r2 + w[1:2] * r1 + w[2:3] * e + cb_ref[:, pl.ds(off, LANES)]

        def back(off, w, e, r1, r2, dc):
            dcm = jnp.where(main, dc, 0.0)
            for tap, shifted in enumerate((r2, r1, e)):
                dcw_ref[tap:tap + 1, pl.ds(off, LANES)] += _colsum(dcm * shifted)
            dcb_ref[:, pl.ds(off, LANES)] += _colsum(dcm)
            da = w[2:3] * dc + w[1:2] * pltpu.roll(dc, ext - 1, 0) + w[0:1] * pltpu.roll(dc, ext - 2, 0)
            da_ref[:, pl.ds(off, LANES)] = da[HALO:HALO + tm].astype(BF16)

        def chunk(c, carry):
            off = pl.multiple_of(c * LANES, LANES)
            off_u = pl.multiple_of(off + Fh, LANES)
            dh = jnp.concatenate([jnp.zeros((HALO, LANES), F32), d_ref[:, pl.ds(off, LANES)].astype(F32),
                                  dn_ref[:, pl.ds(off, LANES)].astype(F32) * keep_n], axis=0)
            wg, eg, r1g, r2g, gt = conv(off)
            wu, eu, r1u, r2u, upv = conv(off_u)
            sg = 1.0 / (1.0 + jnp.exp(-gt))
            back(off, wg, eg, r1g, r2g, dh * upv * (sg * (1.0 + gt * (1.0 - sg))))
            back(off_u, wu, eu, r1u, r2u, dh * (gt * sg))
            return carry

        lax.fori_loop(0, nch, chunk, 0)

    prev = lambda i: (jnp.maximum(i * per - 1, 0), 0)
    nxt = lambda i: (jnp.minimum((i + 1) * per, last_halo), 0)
    outs, carried = _call(
        body, name=name, grid=(nb,),
        in_specs=[pl.BlockSpec((tm, F2), lambda i: (i, 0)), pl.BlockSpec((HALO, F2), prev),
                  pl.BlockSpec((HALO, F2), nxt), pl.BlockSpec((tm, Fh), lambda i: (i, 0)),
                  pl.BlockSpec((HALO, Fh), nxt),
                  pl.BlockSpec((3, F2), lambda i: (0, 0)), pl.BlockSpec((1, F2), lambda i: (0, 0))],
        out_specs=[pl.BlockSpec((tm, F2), lambda i: (i, 0)), pl.BlockSpec((3, F2), lambda i: (0, 0)),
                   pl.BlockSpec((1, F2), lambda i: (0, 0))],
        out_shape=[jax.ShapeDtypeStruct((T, F2), BF16), jax.ShapeDtypeStruct((3, F2), F32),
                   jax.ShapeDtypeStruct((1, F2), F32)],
        scratch_shapes=[], args=(a, a, a, dhm, dhm, cw, cb), rides=rides)
    return (outs, carried) if rides else outs


def _round_up(n, k):
    return -(-n // k) * k


def _pad_axis(a, axis, size):
    pad = [(0, 0)] * a.ndim
    pad[axis] = (0, size - a.shape[axis])
    return jnp.pad(a, pad)


def _pad_blocks(a, axis, nblk, to):
    shp = a.shape
    blk = shp[axis] // nblk
    a = a.reshape(shp[:axis] + (nblk, blk) + shp[axis + 1:])
    a = _pad_axis(a, axis + 1, to)
    return a.reshape(shp[:axis] + (nblk * to,) + shp[axis + 1:])


def _unpad_blocks(a, axis, nblk, blk):
    shp = a.shape
    to = shp[axis] // nblk
    a = a.reshape(shp[:axis] + (nblk, to) + shp[axis + 1:])
    a = lax.slice_in_dim(a, 0, blk, axis=axis + 1)
    return a.reshape(shp[:axis] + (nblk * blk,) + shp[axis + 1:])


SIB_TILE_BYTES = 2 * 1024 * 1024


def _sib_stream(x, core, name, gather):
    if gather:
        B, R, D = x.shape
        out_shape = (B, 2, R, D)
    else:
        B, _, R, D = x.shape
        out_shape = (B, R, D)
    tr = _pick(R, max(BF16_ROWS, SIB_TILE_BYTES // (D * x.dtype.itemsize)), BF16_ROWS)
    nr = R // tr
    if gather:
        in_spec = pl.BlockSpec((tr, D), lambda b, i, core_ref: (b * nr + i, 0))
    else:
        in_spec = pl.BlockSpec((tr, D), lambda b, i, core_ref: ((2 * b + 1 - core_ref[0]) * nr + i, 0))

    def body(core_ref, x_ref, o_ref, send_sem, recv_sem, local_sem):
        b, i = pl.program_id(0), pl.program_id(1)
        me = lax.axis_index("c")
        sibling = (lax.axis_index("x"), lax.axis_index("y"), 1 - me)
        rows = pl.ds(pl.multiple_of(i * tr, tr), tr)
        dst = o_ref.at[b, me, rows] if gather else o_ref.at[b, rows]
        push = pltpu.make_async_remote_copy(src_ref=x_ref, dst_ref=dst, send_sem=send_sem, recv_sem=recv_sem,
                                            device_id=sibling, device_id_type=MESH)
        push.start()
        if gather:
            keep = pltpu.make_async_copy(x_ref, dst, local_sem)
            keep.start()
            keep.wait()
        push.wait_send()

        @pl.when(jnp.logical_and(b == B - 1, i == nr - 1))
        def _():
            landed = o_ref.at[:, 0] if gather else o_ref
            pltpu.make_async_remote_copy(src_ref=landed, dst_ref=landed, send_sem=send_sem, recv_sem=recv_sem,
                                         device_id=sibling, device_id_type=MESH).wait_recv()

    return pl.pallas_call(
        body, name=name,
        grid_spec=pltpu.PrefetchScalarGridSpec(
            num_scalar_prefetch=1, grid=(B, nr), in_specs=[in_spec],
            out_specs=pl.BlockSpec(memory_space=pl.ANY),
            scratch_shapes=[pltpu.SemaphoreType.DMA(()), pltpu.SemaphoreType.DMA(()), pltpu.SemaphoreType.DMA(())]),
        out_shape=jax.ShapeDtypeStruct(out_shape, x.dtype),
        compiler_params=_params("arbitrary", "arbitrary"),
    )(core, x.reshape(-1, D))


def _pair_sum(g, recv, core, name):
    B, _, R, D = g.shape
    tr = _pick(R, 256, BF16_ROWS)

    def body(core_ref, g_ref, r_ref, o_ref):
        o_ref[...] = (g_ref[...].astype(F32) + r_ref[...].astype(F32)).astype(BF16)

    tile = pl.BlockSpec((None, tr, D), lambda b, i, core_ref: (b, i, 0))
    return pl.pallas_call(
        body, name=name,
        grid_spec=pltpu.PrefetchScalarGridSpec(
            num_scalar_prefetch=1, grid=(B, R // tr),
            in_specs=[pl.BlockSpec((None, None, tr, D), lambda b, i, core_ref: (b, core_ref[0], i, 0)), tile],
            out_specs=tile),
        out_shape=jax.ShapeDtypeStruct((B, R, D), BF16),
        compiler_params=_params("parallel", "parallel"),
    )(core, g, recv)


def _all_reduce_small(v, name):
    pair = _sum_slots(_all_gather(v, ("c",), 0, name + "_ag_c")[None], F32, name + "_sum_c")[0]
    return _sum_slots(_all_gather(pair, ("x", "y"), 0, name + "_ag_xy")[None], F32, name + "_sum_xy")[0]


def _pack(arrs):
    flat = jnp.concatenate([a.reshape(-1) for a in arrs])
    rows = _round_up(-(-flat.shape[0] // SMALL_PACK_COLS), BF16_ROWS)
    return _pad_axis(flat, 0, rows * SMALL_PACK_COLS).reshape(rows, SMALL_PACK_COLS)


def _unpack(buf, like):
    flat = buf.reshape(-1)
    out, off = [], 0
    for a in like:
        out.append(flat[off:off + a.size].reshape(a.shape))
        off += a.size
    return out


def kernel(x, c, mod_w, mod_b, mix_norm_g, ffn_norm_g, attn_w_in, attn_b_f, attn_w_o, gm_w_in, gm_v_g, gm_w_s, gm_b_s, gm_w_o, ffn_w_in, ffn_conv_w, ffn_conv_b, ffn_w_out, final_g, loss_target, m_mod_w, m_mod_b, m_mix_norm_g, m_ffn_norm_g, m_attn_w_in, m_attn_b_f, m_attn_w_o, m_gm_w_in, m_gm_v_g, m_gm_w_s, m_gm_b_s, m_gm_w_o, m_ffn_w_in, m_ffn_conv_w, m_ffn_conv_b, m_ffn_w_out, m_final_g, v_mod_w, v_mod_b, v_mix_norm_g, v_ffn_norm_g, v_attn_w_in, v_attn_b_f, v_attn_w_o, v_gm_w_in, v_gm_v_g, v_gm_w_s, v_gm_b_s, v_gm_w_o, v_ffn_w_in, v_ffn_conv_w, v_ffn_conv_b, v_ffn_w_out, v_final_g):
    _, T, D = x.shape
    L = mod_w.shape[0]
    NA, NB = attn_w_in.shape[0], gm_w_in.shape[0]
    H = D // HEAD_DIM
    NG = D // GROUP
    Fq = ffn_w_out.shape[1]
    Fh = 4 * Fq
    Fqp = _round_up(Fq, LANES)
    Fp = 4 * Fqp
    n_attn = attn_w_in.shape[2]
    n_attn_p = _round_up(n_attn, 2 * BF16_ROWS)
    nmod = mod_w.shape[2]

    xi, yi, ci = lax.axis_index("x"), lax.axis_index("y"), lax.axis_index("c")
    chip = 2 * xi + yi
    example = 2 * chip + ci
    core = ci.astype(jnp.int32).reshape(1)

    halves = {
        "attn_in": _cast_half(_pad_axis(jnp.swapaxes(attn_w_in, 1, 2), 1, n_attn_p), core, "w_attn_in_cast"),
        "attn_o": _cast_half(attn_w_o, core, "w_attn_o_cast"),
        "gm_in": _cast_half(jnp.swapaxes(gm_w_in, 1, 2), core, "w_gm_in_cast"),
        "gm_o": _cast_half(gm_w_o, core, "w_gm_o_cast"),
        "ffn_in": _cast_half(jnp.swapaxes(ffn_w_in, 1, 2), core, "w_ffn_in_cast", Fq, Fqp),
        "ffn_out": _cast_half(ffn_w_out, core, "w_ffn_out_cast", Fq, Fqp),
    }
    weights = {}

    def mixer_kinds(i):
        return [("attn_in", i // 2), ("attn_o", i // 2)] if i % 2 == 0 else [("gm_in", i // 2), ("gm_o", i // 2)]

    def land(kind, l, gathered):
        n2 = gathered.shape[1]
        full = _sib_stream(gathered, core, "w_" + kind + "_ag_c", True).reshape(8 * n2, D)
        if kind == "attn_in":
            full = full.reshape(4, n_attn_p, D)[:, :n_attn].reshape(4 * n_attn, D)
            weights[("qkv", l)] = full[:3 * D]
            weights[("f", l)] = _pad_axis(full[3 * D:], 0, LANES)
        else:
            weights[(kind, l)] = full

    def fetching(host, kinds):
        if not kinds:
            return host(())
        result, carried = host([_Exchange(halves[k], ("x", "y"), 0, True, sel=l) for k, l in kinds])
        for (k, l), g in zip(kinds, carried):
            land(k, l, g)
        return result

    land("attn_in", 0, _all_gather(halves["attn_in"][0], ("x", "y"), 0, "w_attn_in_ag_xy"))
    cw_all = _all_gather(ffn_conv_w, ("x", "y"), 0, "conv_w_ag")
    conv_w_full = jnp.moveaxis(cw_all, 0, 2).reshape(L, 3, 2 * Fh)
    conv_w = _pad_blocks(conv_w_full, 2, 8, Fqp)
    conv_b = _pad_blocks(ffn_conv_b, 1, 8, Fqp)
    vg_all = _all_gather(gm_v_g, ("x", "y"), 0, "gm_vg_ag")
    vg_full = jnp.moveaxis(vg_all, 0, 1).reshape(NB, D)

    c_all = _all_gather(c, MESH_AXES, 0, "c_ag").reshape(8, D)
    mod_b_mine = lax.dynamic_slice_in_dim(mod_b, chip * nmod, nmod, axis=1)
    c_act, mod_part = _mod_fwd(c_all, mod_w, mod_b_mine, "mod_fwd")
    mod_all = _all_gather(mod_part, ("x", "y"), 0, "mod_ag")
    mod = lax.dynamic_index_in_dim(mod_all, example, axis=2, keepdims=False)
    mod = jnp.swapaxes(mod, 0, 1).reshape(L, 6, 1, D)

    xs = x.reshape(T, D)
    target = loss_target.reshape(T, D)
    saved = []
    for i in range(L):
        j = i // 2
        sh1, sc1, g1, sh2, sc2, g2 = (mod[i, k] for k in range(6))
        h1 = _norm_mod(xs, mix_norm_g[i:i + 1], sc1, sh1, "norm_mod")
        nxt = i + 1 < L
        if i % 2 == 0:
            qkv = fetching(lambda r: _matmul(h1, weights[("qkv", j)], "nt", "attn_qkv", tm=1024, tn=1024, rides=r),
                           ([("attn_o", j)] if i == 0 else []) + (mixer_kinds(i + 1) if nxt else []))
            flog = _matmul(h1, weights[("f", j)], "nt", "attn_f", out_dtype=F32, tm=1024)
            bias = _pad_axis(attn_b_f[j:j + 1], 1, LANES)
            fcol = _fox_gate(flog, bias, "fox_gate")
            frow = jnp.transpose(fcol)[:H].reshape(H, 1, T)
            o, lse = fetching(lambda r: _attn_fwd(qkv, frow, H, "attn_fwd", rides=r),
                              [("ffn_in", i), ("ffn_out", i)])
            y, x1 = _matmul(o, weights[("attn_o", j)], "nn", "attn_out", out_dtype=F32, tm=1024, tn=1024,
                            res=xs, gate=g1, emit_y=True)
            mix = ("fox", qkv, flog, bias, frow, o, lse)
            ride_in = [("ffn_in", i + 1)] if nxt else []
            ride_out = [("ffn_out", i + 1)] if nxt else []
        else:
            p = _matmul(h1, weights[("gm_in", j)], "nt", "gm_in", tm=1024, tn=1024)
            bt = _pad_axis(jnp.transpose(gm_b_s[j]), 1, LANES)
            gated = _gm_fwd(p, vg_full[j:j + 1], gm_w_s[j], bt, "gm_fwd")
            y, x1 = _matmul(gated, weights[("gm_o", j)], "nn", "gm_out", out_dtype=F32, tm=1024, tn=1024,
                            res=xs, gate=g1, emit_y=True)
            mix = ("gm", p, bt, gated)
            ride_in = mixer_kinds(i + 1) if nxt else []
            ride_out = []
        h2 = _norm_mod(x1, ffn_norm_g[i:i + 1], sc2, sh2, "norm_mod")
        a = fetching(lambda r: _matmul(h2, weights[("ffn_in", i)], "nt", "ffn_in", tm=1024, tn=1024, rides=r), ride_in)
        hmid = _ffn_act(a, conv_w[i], conv_b[i:i + 1], "ffn_act")
        y2, x2 = fetching(lambda r: _matmul(hmid, weights[("ffn_out", i)], "nn", "ffn_out", out_dtype=F32, tm=512,
                                            tn=1024, res=x1, gate=g2, emit_y=True, rides=r), ride_out)
        saved.append((xs, h1, mix, y, x1, h2, a, hmid, y2))
        xs = x2

    loss_blk, dx, g_final, dy2, dg2 = _final_loss_bwd(
        xs, final_g.reshape(1, D), target, saved[L - 1][8], mod[L - 1, 5], "final_loss_bwd")
    g_mix, g_ffn_n, g_cw, g_cb, dmod = [None] * L, [None] * L, [None] * L, [None] * L, [None] * L
    g_bf, g_vg, g_ws, g_bs = [None] * NA, [None] * NB, [None] * NB, [None] * NB
    reduced = {}

    def pair(kind, gw):
        g = gw.reshape(4, 2, gw.shape[0] // 8, D)
        return _pair_sum(g, _sib_stream(g, core, "g_" + kind + "_swap_c", False), core, "g_" + kind + "_sum_c")

    def sending(host, items):
        if not items:
            return host(())
        result, carried = host([_Exchange(ps, ("x", "y"), 0, False) for _, ps in items])
        for (key, _), got in zip(items, carried):
            reduced[key] = got
        return result

    late_act = []
    for i in reversed(range(L)):
        j = i // 2
        xin, h1, mix, y, x1, h2, a, hmid, y2 = saved[i]
        sh1, sc1, g1, sh2, sc2, g2 = (mod[i, k] for k in range(6))
        dhm = _matmul(dy2, weights[("ffn_out", i)], "nt", "ffn_out_dx", tm=1024, tn=512)
        p_out = pair("ffn_out", _matmul(hmid, dy2, "tn", "ffn_out_dw", tm=512, tn=1024))
        da, dcw, dcb = sending(lambda r: _ffn_act_bwd(a, dhm, conv_w[i], conv_b[i:i + 1], "ffn_act_bwd", rides=r),
                               [(("ffn_out", i), p_out)] + late_act)
        dh2 = _matmul(da, weights[("ffn_in", i)], "nn", "ffn_in_dx", out_dtype=F32, tm=512, tn=1024, tk=2816)
        late_act = []
        p_in = pair("ffn_in", _matmul(da, h2, "tn", "ffn_in_dw", tm=512, tn=1024))
        g_cw[i] = _unpad_blocks(dcw, 1, 8, Fq)
        g_cb[i] = _unpad_blocks(dcb, 1, 8, Fq)[0]
        dx1, dsc2, dsh2, dgn, dy, dg1 = _norm_bwd(
            x1, ffn_norm_g[i:i + 1], sc2, sh2, dh2, dx, (y, g1), "norm_bwd_prev")
        g_ffn_n[i] = dgn[0]
        if mix[0] == "fox":
            _, qkv, flog, bias, frow, o, lse = mix
            do = _matmul(dy, weights[("attn_o", j)], "nt", "attn_out_dx", tm=1024, tn=1024)
            p_o = pair("attn_o", _matmul(o, dy, "tn", "attn_out_dw", tm=512, tn=1024))
            dq, dfq = _attn_dq(qkv, o, do, lse, frow, H, "attn_dq")
            dk, dv, dfk = sending(lambda r: _attn_dkv(qkv, o, do, lse, frow, H, "attn_dkv", rides=r),
                                  [(("ffn_in", i), p_in), (("attn_o", j), p_o)])
            heads_on_lanes = lambda t: _pad_axis(jnp.transpose(t.reshape(H, T)), 1, LANES)
            dflog, dbf = _fox_gate_bwd(heads_on_lanes(dfq), heads_on_lanes(dfk), flog, bias, "fox_gate_bwd")
            dproj = jnp.concatenate([dq, dk, dv], axis=1)
            gwt_qkv = _matmul(dproj, h1, "tn", "attn_qkv_dw", tm=512, tn=1024)
            gwt_f = _matmul(dflog, h1, "tn", "attn_f_dw", tm=128, tn=1024)
            gwt = jnp.concatenate([gwt_qkv, gwt_f[:H]], axis=0).reshape(4, n_attn, D)
            p_a = pair("attn_in", _pad_axis(gwt, 1, n_attn_p).reshape(4 * n_attn_p, D))
            dh_f = _matmul(dflog, weights[("f", j)], "nn", "attn_f_dx", out_dtype=F32, tm=1024, tn=1024)
            dh1 = sending(lambda r: _matmul(dproj, weights[("qkv", j)], "nn", "attn_qkv_dx", out_dtype=F32, tm=512,
                                            tn=1024, tk=2048, res=dh_f, rides=r), [(("attn_in", j), p_a)])
            g_bf[j] = dbf[0, :H]
        else:
            _, p, bt, gated = mix
            late_act = [(("ffn_in", i), p_in)]
            dgated = _matmul(dy, weights[("gm_o", j)], "nt", "gm_out_dx", tm=1024, tn=1024)
            p_o = pair("gm_o", _matmul(gated, dy, "tn", "gm_out_dw", tm=512, tn=1024))
            dp, dws, dbt, dvg = _gm_bwd(p, dgated, vg_full[j:j + 1], gm_w_s[j],
                                        jnp.swapaxes(gm_w_s[j], 1, 2), bt, "gm_bwd")
            p_a = pair("gm_in", _matmul(dp, h1, "tn", "gm_in_dw", tm=512, tn=1024))
            dh1 = sending(lambda r: _matmul(dp, weights[("gm_in", j)], "nn", "gm_in_dx", out_dtype=F32, tm=512,
                                            tn=1024, tk=2048, rides=r), [(("gm_in", j), p_a), (("gm_o", j), p_o)])
            g_ws[j] = dws
            g_bs[j] = jnp.transpose(dbt)[:NG]
            g_vg[j] = dvg[0]
        if i > 0:
            dx, dsc1, dsh1, dgn, dy2, dg2_prev = _norm_bwd(
                xin, mix_norm_g[i:i + 1], sc1, sh1, dh1, dx1, (saved[i - 1][8], mod[i - 1, 5]), "norm_bwd_prev")
        else:
            dx, dsc1, dsh1, dgn = _norm_bwd(xin, mix_norm_g[i:i + 1], sc1, sh1, dh1, dx1, None, "norm_bwd_first")
            dg2_prev = None
        g_mix[i] = dgn[0]
        dmod[i] = jnp.concatenate([dsh1, dsc1, dg1, dsh2, dsc2, dg2], axis=1)[0]
        dg2 = dg2_prev
    grad_x = dx.reshape(1, T, D)
    loss = lax.psum(loss_blk[0, 0], MESH_AXES)

    small =[jnp.stack(g_mix), jnp.stack(g_ffn_n), jnp.stack(g_bf), jnp.stack(g_vg), jnp.stack(g_ws),
             jnp.stack(g_bs), jnp.stack(g_cw), jnp.stack(g_cb), g_final[0]]
    (grad_mix_norm_g, grad_ffn_norm_g, grad_attn_b_f, g_vg_full, grad_gm_w_s, grad_gm_b_s, g_cw_full,
     grad_ffn_conv_b, grad_final_g) = _unpack(_all_reduce_small(_pack(small), "g_small"), small)
    grad_gm_v_g = lax.dynamic_slice_in_dim(g_vg_full, chip * (D // 4), D // 4, axis=1)
    grad_ffn_conv_w = lax.dynamic_slice_in_dim(g_cw_full, chip * 2 * Fq, 2 * Fq, axis=2)

    dmod_all = _all_gather(jnp.stack(dmod), MESH_AXES, 0, "dmod_ag")
    grad_mod_b = _sum_slots(dmod_all[None], F32, "dmod_sum")[0]
    dmod_cols = jnp.swapaxes(lax.dynamic_slice_in_dim(dmod_all, chip * nmod, nmod, axis=2), 0, 1)

    grad_mod_w, delta_mod_w, new_m_mod_w, new_v_mod_w = _modw_adamw(
        jnp.transpose(c_act), dmod_cols, mod_w, m_mod_w, v_mod_w, "mod_w_adamw")

    def total(kind, n_layers):
        got = jnp.stack([reduced[(kind, l)] for l in range(n_layers)])
        tot = _sum_slots(got, F32, "g_" + kind + "_sum_xy")
        return _sib_stream(tot, core, "g_" + kind + "_ag_c", True).reshape(n_layers, -1, D)

    gt_attn_in = total("attn_in", NA)[:, :n_attn]
    grad_attn_w_in = jnp.swapaxes(gt_attn_in, 1, 2)
    grad_attn_w_o = total("attn_o", NA)
    grad_gm_w_in = jnp.swapaxes(total("gm_in", NB), 1, 2)
    grad_gm_w_o = total("gm_o", NB)
    gt_ffn_in = _unpad_blocks(total("ffn_in", L), 1, 2, Fq)
    grad_ffn_w_in = jnp.swapaxes(gt_ffn_in, 1, 2)
    grad_ffn_w_out = total("ffn_out", L)[:, :Fq]

    def big(w, g, m, v, name):
        shp = w.shape
        flat = lambda t: t.reshape(-1, shp[-1])
        return [t.reshape(shp) for t in _adamw(flat(w), flat(g), flat(m), flat(v), name)]

    def big_t(w, gt, m, v, name):
        t = lambda a: jnp.swapaxes(a, 1, 2)
        return [t(u) for u in big(t(w), gt, t(m), t(v), name)]

    upd_attn_in = big_t(attn_w_in, gt_attn_in, m_attn_w_in, v_attn_w_in, "adamw_attn_in")
    upd_attn_o = big(attn_w_o, grad_attn_w_o, m_attn_w_o, v_attn_w_o, "adamw_attn_o")
    upd_gm_in = big(gm_w_in, grad_gm_w_in, m_gm_w_in, v_gm_w_in, "adamw_gm_in")
    upd_gm_o = big(gm_w_o, grad_gm_w_o, m_gm_w_o, v_gm_w_o, "adamw_gm_o")
    upd_ffn_in = big_t(ffn_w_in, gt_ffn_in, m_ffn_w_in, v_ffn_w_in, "adamw_ffn_in")
    upd_ffn_out = big(ffn_w_out, grad_ffn_w_out, m_ffn_w_out, v_ffn_w_out, "adamw_ffn_out")

    sm_w = [mod_b, mix_norm_g, ffn_norm_g, attn_b_f, gm_v_g, gm_w_s, gm_b_s, ffn_conv_w, ffn_conv_b, final_g]
    sm_g = [grad_mod_b, grad_mix_norm_g, grad_ffn_norm_g, grad_attn_b_f, grad_gm_v_g, grad_gm_w_s, grad_gm_b_s,
            grad_ffn_conv_w, grad_ffn_conv_b, grad_final_g]
    sm_m = [m_mod_b, m_mix_norm_g, m_ffn_norm_g, m_attn_b_f, m_gm_v_g, m_gm_w_s, m_gm_b_s, m_ffn_conv_w,
            m_ffn_conv_b, m_final_g]
    sm_v = [v_mod_b, v_mix_norm_g, v_ffn_norm_g, v_attn_b_f, v_gm_v_g, v_gm_w_s, v_gm_b_s, v_ffn_conv_w,
            v_ffn_conv_b, v_final_g]
    sm_d, sm_nm, sm_nv = (_unpack(t, sm_w) for t in _adamw(_pack(sm_w), _pack(sm_g), _pack(sm_m), _pack(sm_v),
                                                             "adamw_small"))

    def ordered(mod_w_item, sm, k):
        return [mod_w_item, sm[0], sm[1], sm[2], upd_attn_in[k], sm[3], upd_attn_o[k], upd_gm_in[k], sm[4], sm[5],
                sm[6], upd_gm_o[k], upd_ffn_in[k], sm[7], sm[8], upd_ffn_out[k], sm[9]]

    grads = [grad_mod_w, grad_mod_b, grad_mix_norm_g, grad_ffn_norm_g, grad_attn_w_in, grad_attn_b_f, grad_attn_w_o,
             grad_gm_w_in, grad_gm_v_g, grad_gm_w_s, grad_gm_b_s, grad_gm_w_o, grad_ffn_w_in, grad_ffn_conv_w,
             grad_ffn_conv_b, grad_ffn_w_out, grad_final_g]
    return (loss, grad_x, *grads, *ordered(delta_mod_w, sm_d, 0), *ordered(new_m_mod_w, sm_nm, 1),
            *ordered(new_v_mod_w, sm_nv, 2))
```

```python
import functools
import math

import jax
import jax.numpy as jnp
from jax import lax
from jax.experimental import pallas as pl
from jax.experimental.pallas import tpu as pltpu

F32 = jnp.float32
BF16 = jnp.bfloat16

EPS = 1e-6
HEAD_DIM = 128
CHUNK = 128
GROUP = 128
LANES = 128
BF16_ROWS = 16
VMEM_LIMIT_BYTES = 52 * 1024 * 1024
NEG_BIG = -1e30
SMALL_PACK_COLS = 1024

ADAM_LR, ADAM_B1, ADAM_B2, ADAM_EPS, ADAM_WD, ADAM_STEP = 0.001, 0.9, 0.999, 1e-08, 0.01, 10

MESH_AXES = ("x", "y", "c")
MESH = pl.DeviceIdType.MESH


def _pick(n, pref, align):
    t = (min(pref, n) // align) * align
    while t >= align:
        if n % t == 0:
            return t
        t -= align
    return n


def _params(*sem):
    return pltpu.CompilerParams(dimension_semantics=sem, vmem_limit_bytes=VMEM_LIMIT_BYTES)


def _group(axes):
    pos = {ax: lax.axis_index(ax) for ax in MESH_AXES}
    rank = 0
    for ax in axes:
        rank = rank * 2 + pos[ax]
    peers = []
    for mask in range(1, 2 ** len(axes)):
        peer = dict(pos)
        for bit, ax in enumerate(reversed(axes)):
            if (mask >> bit) & 1:
                peer[ax] = 1 - pos[ax]
        peers.append((mask, tuple(peer[ax] for ax in MESH_AXES)))
    return rank, peers


def _slot(ref, gaxis, idx):
    return ref.at[(slice(None),) * gaxis + (idx,)]


D2D_STREAMS = 32
ICI_STREAMS = 8


def _pieces(shape, want):
    idx = [()]
    d = 0
    while d < len(shape) - 2 and len(idx) * shape[d] <= want:
        idx = [i + (k,) for i in idx for k in range(shape[d])]
        d += 1
    if d == len(shape) - 2:
        rows = shape[-2]
        r = max(1, want // len(idx))
        while r > 1 and not (rows % r == 0 and (rows // r) % BF16_ROWS == 0):
            r -= 1
        if r > 1:
            idx = [i + (pl.ds(k * (rows // r), rows // r),) for i in idx for k in range(r)]
    return idx


class _Exchange:
    def __init__(self, x, axes, gaxis, gather, sel=None):
        self.x, self.axes, self.gaxis, self.gather, self.sel = x, axes, gaxis, gather, sel
        n = 2 ** len(axes)
        shape = x.shape if sel is None else x.shape[1:]
        if gather:
            out_shape = shape[:gaxis] + (n,) + shape[gaxis:]
            slab = shape
        else:
            assert shape[gaxis] == n
            out_shape = shape
            slab = shape[:gaxis] + shape[gaxis + 1:]
        self.out_shape = jax.ShapeDtypeStruct(out_shape, x.dtype)
        self.pieces = _pieces(slab, D2D_STREAMS if axes == ("c",) else ICI_STREAMS)
        self.scratch = [pltpu.SemaphoreType.DMA((n - 1,)), pltpu.SemaphoreType.DMA((n - 1,)),
                        pltpu.SemaphoreType.DMA(())]

    def _views(self, x_ref, o_ref):
        if self.sel is not None:
            x_ref = x_ref.at[self.sel]
        src = (lambda member: x_ref) if self.gather else (lambda member: _slot(x_ref, self.gaxis, member))
        dst = lambda member: _slot(o_ref, self.gaxis, member)
        return src, dst

    @staticmethod
    def _remote(mask, peer, s, d, send_sems, recv_sems):
        return pltpu.make_async_remote_copy(
            src_ref=s, dst_ref=d, send_sem=send_sems.at[mask - 1], recv_sem=recv_sems.at[mask - 1],
            device_id=peer, device_id_type=MESH)

    def start(self, x_ref, o_ref, send_sems, recv_sems, local_sem):
        rank, peers = _group(self.axes)
        src, dst = self._views(x_ref, o_ref)
        cut = lambda ref, pc: ref.at[pc] if pc else ref
        for pc in self.pieces:
            pltpu.make_async_copy(cut(src(rank), pc), cut(dst(rank), pc), local_sem).start()
            for mask, peer in peers:
                self._remote(mask, peer, cut(src(rank ^ mask), pc), cut(dst(rank), pc), send_sems, recv_sems).start()

    def finish(self, x_ref, o_ref, send_sems, recv_sems, local_sem):
        rank, peers = _group(self.axes)
        src, dst = self._views(x_ref, o_ref)
        for mask, peer in peers:
            whole = self._remote(mask, peer, src(rank), dst(rank ^ mask), send_sems, recv_sems)
            whole.wait_recv()
            whole.wait_send()
        pltpu.make_async_copy(src(rank), dst(rank), local_sem).wait()


ANY_SPEC = pl.BlockSpec(memory_space=pl.ANY)


def _call(body, *, name, grid, in_specs, out_specs, out_shape, scratch_shapes, args, rides=()):
    out_specs, out_shape = list(out_specs), list(out_shape)
    n_in, n_out, n_s, nr = len(in_specs), len(out_specs), len(scratch_shapes), len(rides)
    if not rides:
        sem = ("arbitrary",) * len(grid)
        outs = pl.pallas_call(body, name=name, grid=grid, in_specs=in_specs, out_specs=out_specs,
                              out_shape=out_shape, scratch_shapes=scratch_shapes,
                              compiler_params=_params(*sem))(*args)
        return list(outs), []

    def carried(*refs):
        k = 0
        parts = []
        for cnt in (n_in, nr, n_out, nr, n_s, 3 * nr):
            parts.append(refs[k:k + cnt])
            k += cnt
        h_in, r_in, h_out, r_out, h_s, r_s = parts
        ids = [pl.program_id(d) for d in range(len(grid))]
        first = functools.reduce(jnp.logical_and, [i == 0 for i in ids])
        last = functools.reduce(jnp.logical_and, [i == g - 1 for i, g in zip(ids, grid)])

        @pl.when(first)
        def _():
            for r, ride in enumerate(rides):
                ride.start(r_in[r], r_out[r], *r_s[3 * r:3 * r + 3])

        body(*h_in, *h_out, *h_s)

        @pl.when(last)
        def _():
            for r, ride in enumerate(rides):
                ride.finish(r_in[r], r_out[r], *r_s[3 * r:3 * r + 3])

    outs = pl.pallas_call(
        carried, name=name, grid=grid,
        in_specs=list(in_specs) + [ANY_SPEC] * nr,
        out_specs=out_specs + [ANY_SPEC] * nr,
        out_shape=out_shape + [r.out_shape for r in rides],
        scratch_shapes=list(scratch_shapes) + [s for r in rides for s in r.scratch],
        compiler_params=_params(*(("arbitrary",) * len(grid))),
    )(*args, *[r.x for r in rides])
    return list(outs[:n_out]), list(outs[n_out:])


def _exchange(x, axes, gaxis, name, gather):
    ex = _Exchange(x, axes, gaxis, gather)

    def body(x_ref, o_ref, send_sems, recv_sems, local_sem):
        ex.start(x_ref, o_ref, send_sems, recv_sems, local_sem)
        ex.finish(x_ref, o_ref, send_sems, recv_sems, local_sem)

    return pl.pallas_call(body, name=name, out_shape=ex.out_shape, in_specs=[ANY_SPEC], out_specs=ANY_SPEC,
                          scratch_shapes=ex.scratch)(x)


def _all_gather(x, axes, gaxis, name):
    return _exchange(x, axes, gaxis, name, True)


def _all_to_all(x, axes, gaxis, name):
    return _exchange(x, axes, gaxis, name, False)


def _cast_half(w, core, name, blk=None, blk_p=None):
    L, n_src, D = w.shape
    if blk is None:
        blk = blk_p = n_src
    n2 = (n_src // blk) * blk_p // 2
    tr = _pick(math.gcd(math.gcd(blk, blk_p), n2), 256, BF16_ROWS)
    nb, src_per, dst_per = n2 // tr, blk // tr, blk_p // tr

    def body(core_ref, w_ref, o_ref):
        inside = (core_ref[0] * nb + pl.program_id(1)) % dst_per < src_per
        o_ref[...] = jnp.where(inside, w_ref[...], 0.0).astype(BF16)

    def src_block(l, i, core_ref):
        at = core_ref[0] * nb + i
        return l, (at // dst_per) * src_per + jnp.minimum(at % dst_per, src_per - 1), 0

    return pl.pallas_call(
        body, name=name,
        grid_spec=pltpu.PrefetchScalarGridSpec(
            num_scalar_prefetch=1, grid=(L, nb),
            in_specs=[pl.BlockSpec((None, tr, D), src_block)],
            out_specs=pl.BlockSpec((None, tr, D), lambda l, i, core_ref: (l, i, 0))),
        out_shape=jax.ShapeDtypeStruct((L, n2, D), BF16),
        compiler_params=_params("parallel", "parallel"),
    )(core, w)


def _sum_slots(x, out_dtype, name):
    A, G, R, C = x.shape
    budget = (2 * 1024 * 1024) // (G * C * x.dtype.itemsize)
    tr = _pick(R, max(budget, BF16_ROWS), BF16_ROWS)

    def body(x_ref, o_ref):
        acc = x_ref[0].astype(F32)
        for g in range(1, G):
            acc = acc + x_ref[g].astype(F32)
        o_ref[...] = acc.astype(out_dtype)

    return pl.pallas_call(
        body, name=name, grid=(A, R // tr),
        in_specs=[pl.BlockSpec((None, G, tr, C), lambda a, i: (a, 0, i, 0))],
        out_specs=pl.BlockSpec((None, tr, C), lambda a, i: (a, i, 0)),
        out_shape=jax.ShapeDtypeStruct((A, R, C), out_dtype),
        compiler_params=_params("parallel", "parallel"),
    )(x)


def _adamw_math(w, g, m, v):
    m = ADAM_B1 * m + (1.0 - ADAM_B1) * g
    v = ADAM_B2 * v + (1.0 - ADAM_B2) * (g * g)
    m_hat = m / (1.0 - ADAM_B1 ** ADAM_STEP)
    v_hat = v / (1.0 - ADAM_B2 ** ADAM_STEP)
    delta = -ADAM_LR * (m_hat / (jnp.sqrt(v_hat) + ADAM_EPS) + ADAM_WD * w)
    return delta, m, v


def _adamw(w, g, m, v, name):
    R, C = w.shape
    budget = (768 * 1024) // (C * 4)
    tr = _pick(R, max(budget, 8), 8)

    def body(w_ref, g_ref, m_ref, v_ref, d_ref, nm_ref, nv_ref):
        d, nm, nv = _adamw_math(w_ref[...], g_ref[...], m_ref[...], v_ref[...])
        d_ref[...] = d
        nm_ref[...] = nm
        nv_ref[...] = nv

    spec = pl.BlockSpec((tr, C), lambda i: (i, 0))
    return pl.pallas_call(
        body, name=name, grid=(R // tr,),
        in_specs=[spec] * 4, out_specs=[spec] * 3,
        out_shape=[jax.ShapeDtypeStruct((R, C), F32)] * 3,
        compiler_params=_params("parallel"),
    )(w, g, m, v)


def _modw_adamw(ct, dm, w, m, v, name, rides=()):
    L, D, N = w.shape
    E = ct.shape[1]
    tr = _pick(D, 64, 8)

    def body(ct_ref, dm_ref, w_ref, m_ref, v_ref, g_ref, d_ref, nm_ref, nv_ref):
        g = ct_ref[:, 0:1] * dm_ref[0:1, :]
        for e in range(1, E):
            g = g + ct_ref[:, e:e + 1] * dm_ref[e:e + 1, :]
        d, nm, nv = _adamw_math(w_ref[...], g, m_ref[...], v_ref[...])
        g_ref[...] = g
        d_ref[...] = d
        nm_ref[...] = nm
        nv_ref[...] = nv

    spec = pl.BlockSpec((None, tr, N), lambda l, i: (l, i, 0))
    outs, carried = _call(
        body, name=name, grid=(L, D // tr),
        in_specs=[pl.BlockSpec((tr, E), lambda l, i: (i, 0)),
                  pl.BlockSpec((None, E, N), lambda l, i: (l, 0, 0)), spec, spec, spec],
        out_specs=[spec] * 4,
        out_shape=[jax.ShapeDtypeStruct((L, D, N), F32)] * 4,
        scratch_shapes=[], args=(ct, dm, w, m, v), rides=rides)
    return (outs, carried) if rides else outs


def _mod_fwd(c_all, w, b, name):
    L, D, N = w.shape
    E = c_all.shape[0]
    tn = _pick(N, 512, LANES)

    def body(c_ref, w_ref, b_ref, act_ref, o_ref):
        cv = c_ref[...]
        act = cv / (1.0 + jnp.exp(-cv))
        act_ref[...] = act
        o_ref[...] = jnp.dot(act.astype(BF16), w_ref[...].astype(BF16),
                             preferred_element_type=F32) + b_ref[...]

    return pl.pallas_call(
        body, name=name, grid=(L, N // tn),
        in_specs=[pl.BlockSpec((E, D), lambda l, j: (0, 0)),
                  pl.BlockSpec((None, D, tn), lambda l, j: (l, 0, j)),
                  pl.BlockSpec((None, 1, tn), lambda l, j: (l, 0, j))],
        out_specs=[pl.BlockSpec((E, D), lambda l, j: (0, 0)),
                   pl.BlockSpec((None, E, tn), lambda l, j: (l, 0, j))],
        out_shape=[jax.ShapeDtypeStruct((E, D), F32), jax.ShapeDtypeStruct((L, E, N), F32)],
        compiler_params=_params("arbitrary", "arbitrary"),
    )(c_all, w, b.reshape(L, 1, N))


_DIMS = {"nn": (((1,), (0,)), ((), ())), "nt": (((1,), (1,)), ((), ())), "tn": (((0,), (0,)), ((), ()))}


def _matmul(a, b, form, name, *, out_dtype=BF16, tm=512, tn=512, tk=None, a_resident=False,
            res=None, gate=None, emit_y=False, rides=()):
    if form == "nn":
        (M, K), N = a.shape, b.shape[1]
    elif form == "nt":
        (M, K), N = a.shape, b.shape[0]
    else:
        (K, M), N = a.shape, b.shape[1]
    tm = _pick(M, tm, LANES if form == "tn" else BF16_ROWS)
    tn = _pick(N, tn, LANES)
    tk = K if (tk is None or form == "tn") else _pick(K, tk, LANES)
    nk = K // tk

    def ij(g0, g1):
        return (g0, g1) if a_resident else (g1, g0)

    grid = (M // tm, N // tn, nk) if a_resident else (N // tn, M // tm, nk)
    if form == "tn":
        a_spec = pl.BlockSpec((K, tm), lambda g0, g1, k: (0, ij(g0, g1)[0]))
        b_spec = pl.BlockSpec((K, tn), lambda g0, g1, k: (0, ij(g0, g1)[1]))
    else:
        a_spec = pl.BlockSpec((tm, tk), lambda g0, g1, k: (ij(g0, g1)[0], k))
        if form == "nn":
            b_spec = pl.BlockSpec((tk, tn), lambda g0, g1, k: (k, ij(g0, g1)[1]))
        else:
            b_spec = pl.BlockSpec((tn, tk), lambda g0, g1, k: (ij(g0, g1)[1], k))
    tile = pl.BlockSpec((tm, tn), lambda g0, g1, k: ij(g0, g1))
    row = pl.BlockSpec((1, tn), lambda g0, g1, k: (0, ij(g0, g1)[1]))
    in_specs, args = [a_spec, b_spec], [a, b]
    if res is not None:
        in_specs.append(tile)
        args.append(res)
    if gate is not None:
        in_specs.append(row)
        args.append(gate)
    if emit_y:
        out_specs = [tile, tile]
        out_shape = [jax.ShapeDtypeStruct((M, N), BF16), jax.ShapeDtypeStruct((M, N), out_dtype)]
    else:
        out_specs = tile
        out_shape = jax.ShapeDtypeStruct((M, N), out_dtype)
    n_in = len(args)

    def body(*refs):
        a_ref, b_ref = refs[0], refs[1]
        res_ref = refs[2] if res is not None else None
        gate_ref = refs[n_in - 1] if gate is not None else None
        outs = refs[n_in:n_in + (2 if emit_y else 1)]

        def finish(acc):
            val = acc
            if gate_ref is not None:
                val = gate_ref[...] * val
            if res_ref is not None:
                val = res_ref[...] + val
            if emit_y:
                outs[0][...] = acc.astype(BF16)
            outs[-1][...] = val.astype(out_dtype)

        part = lax.dot_general(a_ref[...], b_ref[...], _DIMS[form], preferred_element_type=F32)
        if nk == 1:
            finish(part)
        else:
            acc_ref = refs[-1]
            k = pl.program_id(2)

            @pl.when(k == 0)
            def _():
                acc_ref[...] = part

            @pl.when(k > 0)
            def _():
                acc_ref[...] += part

            @pl.when(k == nk - 1)
            def _():
                finish(acc_ref[...])

    outs, carried = _call(
        body, name=name, grid=grid, in_specs=in_specs, out_specs=out_specs if emit_y else [out_specs],
        out_shape=out_shape if emit_y else [out_shape],
        scratch_shapes=[pltpu.VMEM((tm, tn), F32)] if nk > 1 else [], args=args, rides=rides)
    result = outs if emit_y else outs[0]
    return (result, carried) if rides else result


def _norm_mod(x, g, sc, sh, name):
    T, D = x.shape
    tm = _pick(T, 256, BF16_ROWS)

    def body(x_ref, g_ref, sc_ref, sh_ref, h_ref):
        xv = x_ref[...]
        rstd = lax.rsqrt(jnp.mean(xv * xv, axis=-1, keepdims=True) + EPS)
        y = xv * rstd * g_ref[...]
        h_ref[...] = (y * (1.0 + sc_ref[...]) + sh_ref[...]).astype(BF16)

    tile = pl.BlockSpec((tm, D), lambda i: (i, 0))
    row = pl.BlockSpec((1, D), lambda i: (0, 0))
    return pl.pallas_call(
        body, name=name, grid=(T // tm,), in_specs=[tile, row, row, row], out_specs=tile,
        out_shape=jax.ShapeDtypeStruct((T, D), BF16), compiler_params=_params("parallel"),
    )(x, g, sc, sh)


def _colsum(v):
    return jnp.sum(v, axis=0, keepdims=True)


def _norm_bwd(x, g, sc, sh, dh, dres, prev, name):
    T, D = x.shape
    tm = _pick(T, 256, BF16_ROWS)
    has_prev = prev is not None

    def body(*refs):
        x_ref, g_ref, sc_ref, sh_ref, dh_ref, dres_ref = refs[:6]
        k = 6
        if has_prev:
            y_ref, gate_ref = refs[6:8]
            k = 8
        dx_ref, dsc_ref, dsh_ref, dg_ref = refs[k:k + 4]
        first = pl.program_id(0) == 0
        xv = x_ref[...]
        rstd = lax.rsqrt(jnp.mean(xv * xv, axis=-1, keepdims=True) + EPS)
        xhat = xv * rstd
        gv = g_ref[...]
        dhv = dh_ref[...].astype(F32)
        dyv = dhv * (1.0 + sc_ref[...])
        dxhat = dyv * gv
        dx = rstd * (dxhat - xhat * jnp.mean(dxhat * xhat, axis=-1, keepdims=True))
        dxt = dres_ref[...] + dx
        dx_ref[...] = dxt
        sums = [(dsc_ref, _colsum(dhv * (xhat * gv))), (dsh_ref, _colsum(dhv)), (dg_ref, _colsum(dyv * xhat))]
        if has_prev:
            dy_ref, dgate_ref = refs[k + 4:k + 6]
            dy_ref[...] = (dxt * gate_ref[...]).astype(BF16)
            sums.append((dgate_ref, _colsum(dxt * y_ref[...].astype(F32))))
        for ref, val in sums:
            @pl.when(first)
            def _(ref=ref, val=val):
                ref[...] = val

            @pl.when(jnp.logical_not(first))
            def _(ref=ref, val=val):
                ref[...] += val

    tile = pl.BlockSpec((tm, D), lambda i: (i, 0))
    row = pl.BlockSpec((1, D), lambda i: (0, 0))
    rowshape = jax.ShapeDtypeStruct((1, D), F32)
    in_specs = [tile, row, row, row, tile, tile]
    args = [x, g, sc, sh, dh, dres]
    out_specs = [tile, row, row, row]
    out_shape = [jax.ShapeDtypeStruct((T, D), F32), rowshape, rowshape, rowshape]
    if has_prev:
        in_specs += [tile, row]
        args += list(prev)
        out_specs += [tile, row]
        out_shape += [jax.ShapeDtypeStruct((T, D), BF16), rowshape]
    return pl.pallas_call(
        body, name=name, grid=(T // tm,), in_specs=in_specs, out_specs=out_specs, out_shape=out_shape,
        compiler_params=_params("arbitrary"),
    )(*args)


def _final_loss_bwd(x, g, target, y_prev, gate_prev, name):
    T, D = x.shape
    tm = _pick(T, 256, BF16_ROWS)

    def body(x_ref, g_ref, t_ref, y_ref, gate_ref, loss_ref, dx_ref, dg_ref, dy_ref, dgate_ref):
        first = pl.program_id(0) == 0
        xv = x_ref[...]
        rstd = lax.rsqrt(jnp.mean(xv * xv, axis=-1, keepdims=True) + EPS)
        xhat = xv * rstd
        gv = g_ref[...]
        err = xhat * gv - t_ref[...]
        part = 0.5 * jnp.sum(jnp.mean(err * err, axis=-1, keepdims=True))
        dyf = err * (1.0 / D)
        dxhat = dyf * gv
        dx = rstd * (dxhat - xhat * jnp.mean(dxhat * xhat, axis=-1, keepdims=True))
        dx_ref[...] = dx
        dy_ref[...] = (dx * gate_ref[...]).astype(BF16)
        sums = [(dg_ref, _colsum(dyf * xhat)), (dgate_ref, _colsum(dx * y_ref[...].astype(F32))),
                (loss_ref, jnp.full((8, LANES), part, F32))]
        for ref, val in sums:
            @pl.when(first)
            def _(ref=ref, val=val):
                ref[...] = val

            @pl.when(jnp.logical_not(first))
            def _(ref=ref, val=val):
                ref[...] += val

    tile = pl.BlockSpec((tm, D), lambda i: (i, 0))
    row = pl.BlockSpec((1, D), lambda i: (0, 0))
    rowshape = jax.ShapeDtypeStruct((1, D), F32)
    return pl.pallas_call(
        body, name=name, grid=(T // tm,),
        in_specs=[tile, row, tile, tile, row],
        out_specs=[pl.BlockSpec((8, LANES), lambda i: (0, 0)), tile, row, tile, row],
        out_shape=[jax.ShapeDtypeStruct((8, LANES), F32), jax.ShapeDtypeStruct((T, D), F32), rowshape,
                   jax.ShapeDtypeStruct((T, D), BF16), rowshape],
        compiler_params=_params("arbitrary"),
    )(x, g, target, y_prev, gate_prev)


def _split3(v):
    hi = v.astype(BF16)
    r1 = v - hi.astype(F32)
    mid = r1.astype(BF16)
    lo = (r1 - mid.astype(F32)).astype(BF16)
    return hi, mid, lo


def _tri_dot(tri, v):
    hi, mid, lo = _split3(v)
    dot = functools.partial(jnp.dot, preferred_element_type=F32)
    return dot(tri, hi) + dot(tri, mid) + dot(tri, lo)


def _fox_gate(flog, bias, name):
    T = flog.shape[0]
    tb = _pick(T, 256, BF16_ROWS)

    def body(f_ref, b_ref, o_ref, carry_ref):
        @pl.when(pl.program_id(0) == 0)
        def _():
            carry_ref[...] = jnp.zeros_like(carry_ref)

        xv = f_ref[...] + b_ref[...]
        lf = jnp.minimum(xv, 0.0) - jnp.log(1.0 + jnp.exp(-jnp.abs(xv)))
        r = lax.broadcasted_iota(jnp.int32, (tb, tb), 0)
        cidx = lax.broadcasted_iota(jnp.int32, (tb, tb), 1)
        tri = jnp.where(r >= cidx, 1.0, 0.0).astype(BF16)
        out = _tri_dot(tri, lf) + carry_ref[...]
        o_ref[...] = out
        carry_ref[...] = out[tb - 1:tb, :]

    return pl.pallas_call(
        body, name=name, grid=(T // tb,),
        in_specs=[pl.BlockSpec((tb, LANES), lambda i: (i, 0)), pl.BlockSpec((1, LANES), lambda i: (0, 0))],
        out_specs=pl.BlockSpec((tb, LANES), lambda i: (i, 0)),
        out_shape=jax.ShapeDtypeStruct((T, LANES), F32),
        scratch_shapes=[pltpu.VMEM((1, LANES), F32)],
        compiler_params=_params("arbitrary"),
    )(flog, bias)


def _fox_gate_bwd(dfq, dfk, flog, bias, name):
    T = flog.shape[0]
    tb = _pick(T, 256, BF16_ROWS)
    nb = T // tb

    def body(dq_ref, dk_ref, f_ref, b_ref, o_ref, db_ref, carry_ref):
        first = pl.program_id(0) == 0

        @pl.when(first)
        def _():
            carry_ref[...] = jnp.zeros_like(carry_ref)

        dv = dq_ref[...] + dk_ref[...]
        r = lax.broadcasted_iota(jnp.int32, (tb, tb), 0)
        cidx = lax.broadcasted_iota(jnp.int32, (tb, tb), 1)
        tri = jnp.where(cidx >= r, 1.0, 0.0).astype(BF16)
        dl = _tri_dot(tri, dv) + carry_ref[...]
        carry_ref[...] = dl[0:1, :]
        xv = f_ref[...] + b_ref[...]
        dfl = dl / (1.0 + jnp.exp(xv))
        o_ref[...] = dfl.astype(BF16)
        val = _colsum(dfl)

        @pl.when(first)
        def _():
            db_ref[...] = val

        @pl.when(jnp.logical_not(first))
        def _():
            db_ref[...] += val

    rev = pl.BlockSpec((tb, LANES), lambda i: (nb - 1 - i, 0))
    row = pl.BlockSpec((1, LANES), lambda i: (0, 0))
    return pl.pallas_call(
        body, name=name, grid=(nb,), in_specs=[rev, rev, rev, row], out_specs=[rev, row],
        out_shape=[jax.ShapeDtypeStruct((T, LANES), BF16), jax.ShapeDtypeStruct((1, LANES), F32)],
        scratch_shapes=[pltpu.VMEM((1, LANES), F32)],
        compiler_params=_params("arbitrary"),
    )(dfq, dfk, flog, bias)


_NT = _DIMS["nt"]
_TN = _DIMS["tn"]


ATTN_TQ, ATTN_TK = 1024, 1024
ATTN_TKB, ATTN_TQI = 1024, 1024


def _visible(tq, tk, shift):
    r = lax.broadcasted_iota(jnp.int32, (tq, tk), 0)
    c = lax.broadcasted_iota(jnp.int32, (tq, tk), 1)
    return c + shift <= r


def _attn_fwd(qkv, frow, n_heads, name, rides=()):
    T = qkv.shape[0]
    H = n_heads
    tq = _pick(T, ATTN_TQ, LANES)
    tk = _pick(tq, ATTN_TK, LANES)
    per = tq // tk
    scale = HEAD_DIM ** -0.5

    def body(q_ref, k_ref, v_ref, fr_ref, o_ref, lse_ref):
        i = pl.program_id(1)
        q = q_ref[...]

        def step(j, carry, shift):
            m, l, acc = carry
            ks = pl.multiple_of(j * tk, tk)
            kb = k_ref[pl.ds(ks, tk), :]
            vb = v_ref[pl.ds(ks, tk), :]
            s = lax.dot_general(q, kb, _NT, preferred_element_type=F32) * scale - fr_ref[:, pl.ds(ks, tk)]
            if shift is not None:
                s = jnp.where(_visible(tq, tk, shift), s, NEG_BIG)
            m_new = jnp.maximum(m, jnp.max(s, axis=1, keepdims=True))
            alpha = jnp.exp(m - m_new)
            p = jnp.exp(s - m_new)
            l = alpha * l + jnp.sum(p, axis=1, keepdims=True)
            acc = alpha * acc + jnp.dot(p.astype(BF16), vb, preferred_element_type=F32)
            return m_new, l, acc

        carry = (jnp.full((tq, 1), NEG_BIG, F32), jnp.zeros((tq, 1), F32), jnp.zeros((tq, HEAD_DIM), F32))
        carry = lax.fori_loop(0, i * per, lambda j, c: step(j, c, None), carry)
        for d in range(per):
            carry = step(i * per + d, carry, d * tk)
        m, l, acc = carry
        o_ref[...] = (acc / l).astype(BF16)
        lse_ref[...] = jnp.broadcast_to(m + jnp.log(l), (tq, LANES))

    outs, carried = _call(
        body, name=name, grid=(H, T // tq),
        in_specs=[pl.BlockSpec((tq, HEAD_DIM), lambda h, i: (i, h)),
                  pl.BlockSpec((T, HEAD_DIM), lambda h, i: (0, H + h)),
                  pl.BlockSpec((T, HEAD_DIM), lambda h, i: (0, 2 * H + h)),
                  pl.BlockSpec((None, 1, T), lambda h, i: (h, 0, 0))],
        out_specs=[pl.BlockSpec((tq, HEAD_DIM), lambda h, i: (i, h)),
                   pl.BlockSpec((None, tq, LANES), lambda h, i: (h, i, 0))],
        out_shape=[jax.ShapeDtypeStruct((T, H * HEAD_DIM), BF16), jax.ShapeDtypeStruct((H, T, LANES), F32)],
        scratch_shapes=[], args=(qkv, qkv, qkv, frow), rides=rides)
    return (outs, carried) if rides else outs


def _attn_dq(qkv, o, do, lse, frow, n_heads, name):
    T = qkv.shape[0]
    H = n_heads
    tq = _pick(T, ATTN_TQ, LANES)
    tk = _pick(tq, ATTN_TK, LANES)
    per = tq // tk
    scale = HEAD_DIM ** -0.5

    def body(q_ref, k_ref, v_ref, o_ref, do_ref, lse_ref, fr_ref, dq_ref, df_ref):
        i = pl.program_id(1)
        q = q_ref[...]
        dob = do_ref[...]
        delta = jnp.sum(dob.astype(F32) * o_ref[...].astype(F32), axis=1, keepdims=True)
        lse_i = lse_ref[:, 0:1]

        def step(j, carry, shift):
            dq, rsum = carry
            ks = pl.multiple_of(j * tk, tk)
            kb = k_ref[pl.ds(ks, tk), :]
            vb = v_ref[pl.ds(ks, tk), :]
            s = lax.dot_general(q, kb, _NT, preferred_element_type=F32) * scale - fr_ref[:, pl.ds(ks, tk)]
            p = jnp.exp(s - lse_i)
            if shift is not None:
                p = jnp.where(_visible(tq, tk, shift), p, 0.0)
            dp = lax.dot_general(dob, vb, _NT, preferred_element_type=F32)
            ds = p * (dp - delta)
            return (dq + jnp.dot(ds.astype(BF16), kb, preferred_element_type=F32),
                    rsum + jnp.sum(ds, axis=1, keepdims=True))

        carry = (jnp.zeros((tq, HEAD_DIM), F32), jnp.zeros((tq, 1), F32))
        carry = lax.fori_loop(0, i * per, lambda j, c: step(j, c, None), carry)
        for d in range(per):
            carry = step(i * per + d, carry, d * tk)
        dq, rsum = carry
        dq_ref[...] = (dq * scale).astype(BF16)
        df_ref[...] = jnp.transpose(jnp.broadcast_to(rsum, (tq, LANES)))[0:1, :]

    qtile = lambda off: pl.BlockSpec((tq, HEAD_DIM), lambda h, i: (i, off + h))
    return pl.pallas_call(
        body, name=name, grid=(H, T // tq),
        in_specs=[qtile(0),
                  pl.BlockSpec((T, HEAD_DIM), lambda h, i: (0, H + h)),
                  pl.BlockSpec((T, HEAD_DIM), lambda h, i: (0, 2 * H + h)),
                  qtile(0), qtile(0),
                  pl.BlockSpec((None, tq, LANES), lambda h, i: (h, i, 0)),
                  pl.BlockSpec((None, 1, T), lambda h, i: (h, 0, 0))],
        out_specs=[qtile(0), pl.BlockSpec((None, 1, tq), lambda h, i: (h, 0, i))],
        out_shape=[jax.ShapeDtypeStruct((T, H * HEAD_DIM), BF16), jax.ShapeDtypeStruct((H, 1, T), F32)],
        compiler_params=_params("parallel", "arbitrary"),
    )(qkv, qkv, qkv, o, do, lse, frow)


def _attn_dkv(qkv, o, do, lse, frow, n_heads, name, rides=()):
    T = qkv.shape[0]
    H = n_heads
    tk = _pick(T, ATTN_TKB, LANES)
    tq = _pick(tk, ATTN_TQI, LANES)
    per = tk // tq
    nq = T // tq
    scale = HEAD_DIM ** -0.5

    def body(k_ref, v_ref, q_ref, o_ref, do_ref, lse_ref, fr_ref, dk_ref, dv_ref, df_ref):
        j = pl.program_id(1)
        kb = k_ref[...]
        vb = v_ref[...]
        fj = fr_ref[...]

        def step(i, carry, shift):
            dk, dv, df = carry
            qs = pl.multiple_of(i * tq, tq)
            qb = q_ref[pl.ds(qs, tq), :]
            dob = do_ref[pl.ds(qs, tq), :]
            delta = jnp.sum(dob.astype(F32) * o_ref[pl.ds(qs, tq), :].astype(F32), axis=1, keepdims=True)
            lse_i = lse_ref[pl.ds(qs, tq), 0:1]
            s = lax.dot_general(qb, kb, _NT, preferred_element_type=F32) * scale - fj
            p = jnp.exp(s - lse_i)
            if shift is not None:
                p = jnp.where(_visible(tq, tk, shift), p, 0.0)
            dv = dv + lax.dot_general(p.astype(BF16), dob, _TN, preferred_element_type=F32)
            dp = lax.dot_general(dob, vb, _NT, preferred_element_type=F32)
            ds = p * (dp - delta)
            dk = dk + lax.dot_general(ds.astype(BF16), qb, _TN, preferred_element_type=F32)
            df = df - _colsum(ds)
            return dk, dv, df

        carry = (jnp.zeros((tk, HEAD_DIM), F32), jnp.zeros((tk, HEAD_DIM), F32), jnp.zeros((1, tk), F32))
        for d in range(per):
            carry = step(j * per + d, carry, -d * tq)
        dk, dv, df = lax.fori_loop((j + 1) * per, nq, lambda i, c: step(i, c, None), carry)
        dk_ref[...] = (dk * scale).astype(BF16)
        dv_ref[...] = dv.astype(BF16)
        df_ref[...] = df

    full = lambda off: pl.BlockSpec((T, HEAD_DIM), lambda h, j: (0, off + h))
    ktile = lambda off: pl.BlockSpec((tk, HEAD_DIM), lambda h, j: (j, off + h))
    outs, carried = _call(
        body, name=name, grid=(H, T // tk),
        in_specs=[ktile(H), ktile(2 * H), full(0), full(0), full(0),
                  pl.BlockSpec((None, T, LANES), lambda h, j: (h, 0, 0)),
                  pl.BlockSpec((None, 1, tk), lambda h, j: (h, 0, j))],
        out_specs=[ktile(0), ktile(0), pl.BlockSpec((None, 1, tk), lambda h, j: (h, 0, j))],
        out_shape=[jax.ShapeDtypeStruct((T, H * HEAD_DIM), BF16)] * 2 + [jax.ShapeDtypeStruct((H, 1, T), F32)],
        scratch_shapes=[], args=(qkv, qkv, qkv, o, do, lse, frow), rides=rides)
    return (outs, carried) if rides else outs


_GELU_C = math.sqrt(2.0 / math.pi)
_GELU_A = 0.044715


def _gelu(p):
    t = jnp.tanh(_GELU_C * (p + _GELU_A * p * p * p))
    return 0.5 * p * (1.0 + t), t


def _gelu_grad(p, t):
    return 0.5 * (1.0 + t) + 0.5 * p * (1.0 - t * t) * _GELU_C * (1.0 + 3.0 * _GELU_A * p * p)


def _tril(shape, upper=False):
    r = lax.broadcasted_iota(jnp.int32, shape, 0)
    c = lax.broadcasted_iota(jnp.int32, shape, 1)
    return (c >= r) if upper else (r >= c)


def _gm_fwd(p, vg, ws, bt, name):
    T, D2 = p.shape
    D = D2 // 2
    NG = D // GROUP
    tm = _pick(T, 2 * CHUNK, CHUNK)

    def body(p_ref, vg_ref, ws_ref, bt_ref, o_ref):
        z, _ = _gelu(p_ref[...].astype(F32))
        u, v = z[:, :D], z[:, D:]
        rstd = lax.rsqrt(jnp.mean(v * v, axis=-1, keepdims=True) + EPS)
        vn = (v * rstd * vg_ref[...]).astype(BF16)
        low = _tril((CHUNK, CHUNK))
        for g in range(NG):
            wm = jnp.where(low, ws_ref[g], 0.0).astype(BF16)
            bcol = bt_ref[:, g:g + 1]
            cs = slice(g * GROUP, (g + 1) * GROUP)
            for r in range(tm // CHUNK):
                rs = slice(r * CHUNK, (r + 1) * CHUNK)
                sv = jnp.dot(wm, vn[rs, cs], preferred_element_type=F32) + bcol
                o_ref[rs, cs] = (u[rs, cs] * sv).astype(BF16)

    return pl.pallas_call(
        body, name=name, grid=(T // tm,),
        in_specs=[pl.BlockSpec((tm, D2), lambda i: (i, 0)), pl.BlockSpec((1, D), lambda i: (0, 0)),
                  pl.BlockSpec((NG, CHUNK, CHUNK), lambda i: (0, 0, 0)),
                  pl.BlockSpec((CHUNK, LANES), lambda i: (0, 0))],
        out_specs=pl.BlockSpec((tm, D), lambda i: (i, 0)),
        out_shape=jax.ShapeDtypeStruct((T, D), BF16),
        compiler_params=_params("parallel"),
    )(p, vg, ws, bt)


def _gm_bwd(p, dgated, vg, ws, wst, bt, name):
    T, D2 = p.shape
    D = D2 // 2
    NG = D // GROUP
    tm = _pick(T, 2 * CHUNK, CHUNK)
    nb = T // tm

    def body(p_ref, dg_ref, vg_ref, ws_ref, wst_ref, bt_ref, dp_ref, dws_ref, dbt_ref, dvg_ref, du_s, dvn_s):
        step = pl.program_id(0)

        @pl.when(step == 0)
        def _():
            dws_ref[...] = jnp.zeros_like(dws_ref)
            dbt_ref[...] = jnp.zeros_like(dbt_ref)
            dvg_ref[...] = jnp.zeros_like(dvg_ref)

        pv = p_ref[...].astype(F32)
        z, t = _gelu(pv)
        u, v = z[:, :D], z[:, D:]
        rstd = lax.rsqrt(jnp.mean(v * v, axis=-1, keepdims=True) + EPS)
        vhat = v * rstd
        vgv = vg_ref[...]
        vn = (vhat * vgv).astype(BF16)
        dgt = dg_ref[...].astype(F32)
        low = _tril((CHUNK, CHUNK))
        up = _tril((CHUNK, CHUNK), upper=True)
        lane = lax.broadcasted_iota(jnp.int32, (CHUNK, LANES), 1)
        dbt = jnp.zeros((CHUNK, LANES), F32)
        for g in range(NG):
            wm = jnp.where(low, ws_ref[g], 0.0).astype(BF16)
            wmt = jnp.where(up, wst_ref[g], 0.0).astype(BF16)
            bcol = bt_ref[:, g:g + 1]
            cs = slice(g * GROUP, (g + 1) * GROUP)
            dw = jnp.zeros((CHUNK, CHUNK), F32)
            for r in range(tm // CHUNK):
                rs = slice(r * CHUNK, (r + 1) * CHUNK)
                blk = vn[rs, cs]
                sv = jnp.dot(wm, blk, preferred_element_type=F32) + bcol
                dgb = dgt[rs, cs]
                du_s[rs, cs] = dgb * sv
                dsv = dgb * u[rs, cs]
                dsvb = dsv.astype(BF16)
                dvn_s[rs, cs] = jnp.dot(wmt, dsvb, preferred_element_type=F32)
                dw = dw + lax.dot_general(dsvb, blk, _NT, preferred_element_type=F32)
                dbt = dbt + jnp.where(lane == g, jnp.sum(dsv, axis=1, keepdims=True), 0.0)
            dws_ref[g] += dw
        dbt_ref[...] += dbt
        dvn = dvn_s[...]
        dvg_ref[...] += _colsum(dvn * vhat)
        dvhat = dvn * vgv
        dv = rstd * (dvhat - vhat * jnp.mean(dvhat * vhat, axis=-1, keepdims=True))
        gg = _gelu_grad(pv, t)
        dp_ref[:, :D] = (du_s[...] * gg[:, :D]).astype(BF16)
        dp_ref[:, D:] = (dv * gg[:, D:]).astype(BF16)

        @pl.when(step == nb - 1)
        def _():
            for g in range(NG):
                dws_ref[g] = jnp.where(low, dws_ref[g], 0.0)

    const3 = pl.BlockSpec((NG, CHUNK, CHUNK), lambda i: (0, 0, 0))
    return pl.pallas_call(
        body, name=name, grid=(nb,),
        in_specs=[pl.BlockSpec((tm, D2), lambda i: (i, 0)), pl.BlockSpec((tm, D), lambda i: (i, 0)),
                  pl.BlockSpec((1, D), lambda i: (0, 0)), const3, const3,
                  pl.BlockSpec((CHUNK, LANES), lambda i: (0, 0))],
        out_specs=[pl.BlockSpec((tm, D2), lambda i: (i, 0)), const3,
                   pl.BlockSpec((CHUNK, LANES), lambda i: (0, 0)), pl.BlockSpec((1, D), lambda i: (0, 0))],
        out_shape=[jax.ShapeDtypeStruct((T, D2), BF16), jax.ShapeDtypeStruct((NG, CHUNK, CHUNK), F32),
                   jax.ShapeDtypeStruct((CHUNK, LANES), F32), jax.ShapeDtypeStruct((1, D), F32)],
        scratch_shapes=[pltpu.VMEM((tm, D), F32), pltpu.VMEM((tm, D), F32)],
        compiler_params=_params("arbitrary"),
    )(p, dgated, vg, ws, wst, bt)


HALO = BF16_ROWS


def _ffn_act(a, cw, cb, name):
    T, F2 = a.shape
    Fh = F2 // 2
    tm = _pick(T, 256, HALO)
    per = tm // HALO
    nch = Fh // LANES

    def body(a_ref, h_ref, cw_ref, cb_ref, o_ref):
        i = pl.program_id(0)
        row = lax.broadcasted_iota(jnp.int32, (tm, LANES), 0)
        keep = jnp.where(i > 0, 1.0, 0.0)

        def conv(off):
            af = a_ref[:, pl.ds(off, LANES)].astype(F32)
            hf = h_ref[:, pl.ds(off, LANES)].astype(F32) * keep
            r1 = jnp.where(row == 0, hf[HALO - 1:HALO], pltpu.roll(af, 1, 0))
            r2 = jnp.where(row == 0, hf[HALO - 2:HALO - 1], jnp.where(row == 1, hf[HALO - 1:HALO], pltpu.roll(af, 2, 0)))
            w = cw_ref[:, pl.ds(off, LANES)]
            return w[0:1] * r2 + w[1:2] * r1 + w[2:3] * af + cb_ref[:, pl.ds(off, LANES)]

        def chunk(c, carry):
            off = pl.multiple_of(c * LANES, LANES)
            gt = conv(off)
            upv = conv(pl.multiple_of(off + Fh, LANES))
            o_ref[:, pl.ds(off, LANES)] = (gt / (1.0 + jnp.exp(-gt)) * upv).astype(BF16)
            return carry

        lax.fori_loop(0, nch, chunk, 0)

    return pl.pallas_call(
        body, name=name, grid=(T // tm,),
        in_specs=[pl.BlockSpec((tm, F2), lambda i: (i, 0)),
                  pl.BlockSpec((HALO, F2), lambda i: (jnp.maximum(i * per - 1, 0), 0)),
                  pl.BlockSpec((3, F2), lambda i: (0, 0)), pl.BlockSpec((1, F2), lambda i: (0, 0))],
        out_specs=pl.BlockSpec((tm, Fh), lambda i: (i, 0)),
        out_shape=jax.ShapeDtypeStruct((T, Fh), BF16),
        compiler_params=_params("parallel"),
    )(a, a, cw, cb)


def _ffn_act_bwd(a, dhm, cw, cb, name, rides=()):
    T, F2 = a.shape
    Fh = F2 // 2
    tm = _pick(T, 256, HALO)
    per = tm // HALO
    nb = T // tm
    last_halo = T // HALO - 1
    nch = Fh // LANES
    ext = tm + 2 * HALO

    def body(a_ref, hp_ref, hn_ref, d_ref, dn_ref, cw_ref, cb_ref, da_ref, dcw_ref, dcb_ref):
        i = pl.program_id(0)

        @pl.when(i == 0)
        def _():
            dcw_ref[...] = jnp.zeros_like(dcw_ref)
            dcb_ref[...] = jnp.zeros_like(dcb_ref)

        keep_p = jnp.where(i > 0, 1.0, 0.0)
        keep_n = jnp.where(i < nb - 1, 1.0, 0.0)
        row = lax.broadcasted_iota(jnp.int32, (ext, LANES), 0)
        main = jnp.logical_and(row >= HALO, row < HALO + tm)

        def conv(off):
            e = jnp.concatenate([hp_ref[:, pl.ds(off, LANES)].astype(F32) * keep_p,
                                 a_ref[:, pl.ds(off, LANES)].astype(F32),
                                 hn_ref[:, pl.ds(off, LANES)].astype(F32) * keep_n], axis=0)
            r1 = pltpu.roll(e, 1, 0)
            r2 = pltpu.roll(e, 2, 0)
            w = cw_ref[:, pl.ds(off, LANES)]
            return w, e, r1, r2, w[0:1] * r2 + w[1:2] * r1 + w[2:3] * e + cb_ref[:, pl.ds(off, LANES)]

        def back(off, w, e, r1, r2, dc):
            dcm = jnp.where(main, dc, 0.0)
            for tap, shifted in enumerate((r2, r1, e)):
                dcw_ref[tap:tap + 1, pl.ds(off, LANES)] += _colsum(dcm * shifted)
            dcb_ref[:, pl.ds(off, LANES)] += _colsum(dcm)
            da = w[2:3] * dc + w[1:2] * pltpu.roll(dc, ext - 1, 0) + w[0:1] * pltpu.roll(dc, ext - 2, 0)
            da_ref[:, pl.ds(off, LANES)] = da[HALO:HALO + tm].astype(BF16)

        def chunk(c, carry):
            off = pl.multiple_of(c * LANES, LANES)
            off_u = pl.multiple_of(off + Fh, LANES)
            dh = jnp.concatenate([jnp.zeros((HALO, LANES), F32), d_ref[:, pl.ds(off, LANES)].astype(F32),
                                  dn_ref[:, pl.ds(off, LANES)].astype(F32) * keep_n], axis=0)
            wg, eg, r1g, r2g, gt = conv(off)
            wu, eu, r1u, r2u, upv = conv(off_u)
            sg = 1.0 / (1.0 + jnp.exp(-gt))
            back(off, wg, eg, r1g, r2g, dh * upv * (sg * (1.0 + gt * (1.0 - sg))))
            back(off_u, wu, eu, r1u, r2u, dh * (gt * sg))
            return carry

        lax.fori_loop(0, nch, chunk, 0)

    prev = lambda i: (jnp.maximum(i * per - 1, 0), 0)
    nxt = lambda i: (jnp.minimum((i + 1) * per, last_halo), 0)
    outs, carried = _call(
        body, name=name, grid=(nb,),
        in_specs=[pl.BlockSpec((tm, F2), lambda i: (i, 0)), pl.BlockSpec((HALO, F2), prev),
                  pl.BlockSpec((HALO, F2), nxt), pl.BlockSpec((tm, Fh), lambda i: (i, 0)),
                  pl.BlockSpec((HALO, Fh), nxt),
                  pl.BlockSpec((3, F2), lambda i: (0, 0)), pl.BlockSpec((1, F2), lambda i: (0, 0))],
        out_specs=[pl.BlockSpec((tm, F2), lambda i: (i, 0)), pl.BlockSpec((3, F2), lambda i: (0, 0)),
                   pl.BlockSpec((1, F2), lambda i: (0, 0))],
        out_shape=[jax.ShapeDtypeStruct((T, F2), BF16), jax.ShapeDtypeStruct((3, F2), F32),
                   jax.ShapeDtypeStruct((1, F2), F32)],
        scratch_shapes=[], args=(a, a, a, dhm, dhm, cw, cb), rides=rides)
    return (outs, carried) if rides else outs


def _round_up(n, k):
    return -(-n // k) * k


def _pad_axis(a, axis, size):
    pad = [(0, 0)] * a.ndim
    pad[axis] = (0, size - a.shape[axis])
    return jnp.pad(a, pad)


def _pad_blocks(a, axis, nblk, to):
    shp = a.shape
    blk = shp[axis] // nblk
    a = a.reshape(shp[:axis] + (nblk, blk) + shp[axis + 1:])
    a = _pad_axis(a, axis + 1, to)
    return a.reshape(shp[:axis] + (nblk * to,) + shp[axis + 1:])


def _unpad_blocks(a, axis, nblk, blk):
    shp = a.shape
    to = shp[axis] // nblk
    a = a.reshape(shp[:axis] + (nblk, to) + shp[axis + 1:])
    a = lax.slice_in_dim(a, 0, blk, axis=axis + 1)
    return a.reshape(shp[:axis] + (nblk * blk,) + shp[axis + 1:])


SIB_TILE_BYTES = 2 * 1024 * 1024


def _sib_stream(x, core, name, gather):
    if gather:
        B, R, D = x.shape
        out_shape = (B, 2, R, D)
    else:
        B, _, R, D = x.shape
        out_shape = (B, R, D)
    tr = _pick(R, max(BF16_ROWS, SIB_TILE_BYTES // (D * x.dtype.itemsize)), BF16_ROWS)
    nr = R // tr
    if gather:
        in_spec = pl.BlockSpec((tr, D), lambda b, i, core_ref: (b * nr + i, 0))
    else:
        in_spec = pl.BlockSpec((tr, D), lambda b, i, core_ref: ((2 * b + 1 - core_ref[0]) * nr + i, 0))

    def body(core_ref, x_ref, o_ref, send_sem, recv_sem, local_sem):
        b, i = pl.program_id(0), pl.program_id(1)
        me = lax.axis_index("c")
        sibling = (lax.axis_index("x"), lax.axis_index("y"), 1 - me)
        rows = pl.ds(pl.multiple_of(i * tr, tr), tr)
        dst = o_ref.at[b, me, rows] if gather else o_ref.at[b, rows]
        push = pltpu.make_async_remote_copy(src_ref=x_ref, dst_ref=dst, send_sem=send_sem, recv_sem=recv_sem,
                                            device_id=sibling, device_id_type=MESH)
        push.start()
        if gather:
            keep = pltpu.make_async_copy(x_ref, dst, local_sem)
            keep.start()
            keep.wait()
        push.wait_send()

        @pl.when(jnp.logical_and(b == B - 1, i == nr - 1))
        def _():
            landed = o_ref.at[:, 0] if gather else o_ref
            pltpu.make_async_remote_copy(src_ref=landed, dst_ref=landed, send_sem=send_sem, recv_sem=recv_sem,
                                         device_id=sibling, device_id_type=MESH).wait_recv()

    return pl.pallas_call(
        body, name=name,
        grid_spec=pltpu.PrefetchScalarGridSpec(
            num_scalar_prefetch=1, grid=(B, nr), in_specs=[in_spec],
            out_specs=pl.BlockSpec(memory_space=pl.ANY),
            scratch_shapes=[pltpu.SemaphoreType.DMA(()), pltpu.SemaphoreType.DMA(()), pltpu.SemaphoreType.DMA(())]),
        out_shape=jax.ShapeDtypeStruct(out_shape, x.dtype),
        compiler_params=_params("arbitrary", "arbitrary"),
    )(core, x.reshape(-1, D))


def _pair_sum(g, recv, core, name):
    B, _, R, D = g.shape
    tr = _pick(R, 256, BF16_ROWS)

    def body(core_ref, g_ref, r_ref, o_ref):
        o_ref[...] = (g_ref[...].astype(F32) + r_ref[...].astype(F32)).astype(BF16)

    tile = pl.BlockSpec((None, tr, D), lambda b, i, core_ref: (b, i, 0))
    return pl.pallas_call(
        body, name=name,
        grid_spec=pltpu.PrefetchScalarGridSpec(
            num_scalar_prefetch=1, grid=(B, R // tr),
            in_specs=[pl.BlockSpec((None, None, tr, D), lambda b, i, core_ref: (b, core_ref[0], i, 0)), tile],
            out_specs=tile),
        out_shape=jax.ShapeDtypeStruct((B, R, D), BF16),
        compiler_params=_params("parallel", "parallel"),
    )(core, g, recv)


def _all_reduce_small(v, name):
    pair = _sum_slots(_all_gather(v, ("c",), 0, name + "_ag_c")[None], F32, name + "_sum_c")[0]
    return _sum_slots(_all_gather(pair, ("x", "y"), 0, name + "_ag_xy")[None], F32, name + "_sum_xy")[0]


def _pack(arrs):
    flat = jnp.concatenate([a.reshape(-1) for a in arrs])
    rows = _round_up(-(-flat.shape[0] // SMALL_PACK_COLS), BF16_ROWS)
    return _pad_axis(flat, 0, rows * SMALL_PACK_COLS).reshape(rows, SMALL_PACK_COLS)


def _unpack(buf, like):
    flat = buf.reshape(-1)
    out, off = [], 0
    for a in like:
        out.append(flat[off:off + a.size].reshape(a.shape))
        off += a.size
    return out


def kernel(x, c, mod_w, mod_b, mix_norm_g, ffn_norm_g, attn_w_in, attn_b_f, attn_w_o, gm_w_in, gm_v_g, gm_w_s, gm_b_s, gm_w_o, ffn_w_in, ffn_conv_w, ffn_conv_b, ffn_w_out, final_g, loss_target, m_mod_w, m_mod_b, m_mix_norm_g, m_ffn_norm_g, m_attn_w_in, m_attn_b_f, m_attn_w_o, m_gm_w_in, m_gm_v_g, m_gm_w_s, m_gm_b_s, m_gm_w_o, m_ffn_w_in, m_ffn_conv_w, m_ffn_conv_b, m_ffn_w_out, m_final_g, v_mod_w, v_mod_b, v_mix_norm_g, v_ffn_norm_g, v_attn_w_in, v_attn_b_f, v_attn_w_o, v_gm_w_in, v_gm_v_g, v_gm_w_s, v_gm_b_s, v_gm_w_o, v_ffn_w_in, v_ffn_conv_w, v_ffn_conv_b, v_ffn_w_out, v_final_g):
    _, T, D = x.shape
    L = mod_w.shape[0]
    NA, NB = attn_w_in.shape[0], gm_w_in.shape[0]
    H = D // HEAD_DIM
    NG = D // GROUP
    Fq = ffn_w_out.shape[1]
    Fh = 4 * Fq
    Fqp = _round_up(Fq, LANES)
    Fp = 4 * Fqp
    n_attn = attn_w_in.shape[2]
    n_attn_p = _round_up(n_attn, 2 * BF16_ROWS)
    nmod = mod_w.shape[2]

    xi, yi, ci = lax.axis_index("x"), lax.axis_index("y"), lax.axis_index("c")
    chip = 2 * xi + yi
    example = 2 * chip + ci
    core = ci.astype(jnp.int32).reshape(1)

    halves = {
        "attn_in": _cast_half(_pad_axis(jnp.swapaxes(attn_w_in, 1, 2), 1, n_attn_p), core, "w_attn_in_cast"),
        "attn_o": _cast_half(attn_w_o, core, "w_attn_o_cast"),
        "gm_in": _cast_half(jnp.swapaxes(gm_w_in, 1, 2), core, "w_gm_in_cast"),
        "gm_o": _cast_half(gm_w_o, core, "w_gm_o_cast"),
        "ffn_in": _cast_half(jnp.swapaxes(ffn_w_in, 1, 2), core, "w_ffn_in_cast", Fq, Fqp),
        "ffn_out": _cast_half(ffn_w_out, core, "w_ffn_out_cast", Fq, Fqp),
    }
    weights = {}

    def mixer_kinds(i):
        return [("attn_in", i // 2), ("attn_o", i // 2)] if i % 2 == 0 else [("gm_in", i // 2), ("gm_o", i // 2)]

    def land(kind, l, gathered):
        n2 = gathered.shape[1]
        full = _sib_stream(gathered, core, "w_" + kind + "_ag_c", True).reshape(8 * n2, D)
        if kind == "attn_in":
            full = full.reshape(4, n_attn_p, D)[:, :n_attn].reshape(4 * n_attn, D)
            weights[("qkv", l)] = full[:3 * D]
            weights[("f", l)] = _pad_axis(full[3 * D:], 0, LANES)
        else:
            weights[(kind, l)] = full

    def fetching(host, kinds):
        if not kinds:
            return host(())
        result, carried = host([_Exchange(halves[k], ("x", "y"), 0, True, sel=l) for k, l in kinds])
        for (k, l), g in zip(kinds, carried):
            land(k, l, g)
        return result

    land("attn_in", 0, _all_gather(halves["attn_in"][0], ("x", "y"), 0, "w_attn_in_ag_xy"))
    cw_all = _all_gather(ffn_conv_w, ("x", "y"), 0, "conv_w_ag")
    conv_w_full = jnp.moveaxis(cw_all, 0, 2).reshape(L, 3, 2 * Fh)
    conv_w = _pad_blocks(conv_w_full, 2, 8, Fqp)
    conv_b = _pad_blocks(ffn_conv_b, 1, 8, Fqp)
    vg_all = _all_gather(gm_v_g, ("x", "y"), 0, "gm_vg_ag")
    vg_full = jnp.moveaxis(vg_all, 0, 1).reshape(NB, D)

    c_all = _all_gather(c, MESH_AXES, 0, "c_ag").reshape(8, D)
    mod_b_mine = lax.dynamic_slice_in_dim(mod_b, chip * nmod, nmod, axis=1)
    c_act, mod_part = _mod_fwd(c_all, mod_w, mod_b_mine, "mod_fwd")
    mod_all = _all_gather(mod_part, ("x", "y"), 0, "mod_ag")
    mod = lax.dynamic_index_in_dim(mod_all, example, axis=2, keepdims=False)
    mod = jnp.swapaxes(mod, 0, 1).reshape(L, 6, 1, D)

    xs = x.reshape(T, D)
    target = loss_target.reshape(T, D)
    saved = []
    for i in range(L):
        j = i // 2
        sh1, sc1, g1, sh2, sc2, g2 = (mod[i, k] for k in range(6))
        h1 = _norm_mod(xs, mix_norm_g[i:i + 1], sc1, sh1, "norm_mod")
        nxt = i + 1 < L
        if i % 2 == 0:
            qkv = fetching(lambda r: _matmul(h1, weights[("qkv", j)], "nt", "attn_qkv", tm=1024, tn=1024, rides=r),
                           ([("attn_o", j)] if i == 0 else []) + (mixer_kinds(i + 1)[:1] if nxt else []))
            flog = _matmul(h1, weights[("f", j)], "nt", "attn_f", out_dtype=F32, tm=1024)
            bias = _pad_axis(attn_b_f[j:j + 1], 1, LANES)
            fcol = _fox_gate(flog, bias, "fox_gate")
            frow = jnp.transpose(fcol)[:H].reshape(H, 1, T)
            o, lse = fetching(lambda r: _attn_fwd(qkv, frow, H, "attn_fwd", rides=r),
                              [("ffn_in", i), ("ffn_out", i)])
            y, x1 = fetching(lambda r: _matmul(o, weights[("attn_o", j)], "nn", "attn_out", out_dtype=F32, tm=1024,
                                               tn=1024, res=xs, gate=g1, emit_y=True, rides=r),
                             mixer_kinds(i + 1)[1:] if nxt else [])
            mix = ("fox", qkv, flog, bias, frow, o, lse)
            ride_in = [("ffn_in", i + 1)] if nxt else []
            ride_out = [("ffn_out", i + 1)] if nxt else []
        else:
            p = _matmul(h1, weights[("gm_in", j)], "nt", "gm_in", tm=1024, tn=1024)
            bt = _pad_axis(jnp.transpose(gm_b_s[j]), 1, LANES)
            gated = _gm_fwd(p, vg_full[j:j + 1], gm_w_s[j], bt, "gm_fwd")
            y, x1 = _matmul(gated, weights[("gm_o", j)], "nn", "gm_out", out_dtype=F32, tm=1024, tn=1024,
                            res=xs, gate=g1, emit_y=True)
            mix = ("gm", p, bt, gated)
            ride_in = mixer_kinds(i + 1) if nxt else []
            ride_out = []
        h2 = _norm_mod(x1, ffn_norm_g[i:i + 1], sc2, sh2, "norm_mod")
        a = fetching(lambda r: _matmul(h2, weights[("ffn_in", i)], "nt", "ffn_in", tm=1024, tn=1024, rides=r), ride_in)
        hmid = _ffn_act(a, conv_w[i], conv_b[i:i + 1], "ffn_act")
        y2, x2 = fetching(lambda r: _matmul(hmid, weights[("ffn_out", i)], "nn", "ffn_out", out_dtype=F32, tm=512,
                                            tn=1024, res=x1, gate=g2, emit_y=True, rides=r), ride_out)
        saved.append((xs, h1, mix, y, x1, h2, a, hmid, y2))
        xs = x2

    loss_blk, dx, g_final, dy2, dg2 = _final_loss_bwd(
        xs, final_g.reshape(1, D), target, saved[L - 1][8], mod[L - 1, 5], "final_loss_bwd")
    g_mix, g_ffn_n, g_cw, g_cb, dmod = [None] * L, [None] * L, [None] * L, [None] * L, [None] * L
    g_bf, g_vg, g_ws, g_bs = [None] * NA, [None] * NB, [None] * NB, [None] * NB
    reduced = {}

    def pair(kind, gw):
        g = gw.reshape(4, 2, gw.shape[0] // 8, D)
        return _pair_sum(g, _sib_stream(g, core, "g_" + kind + "_swap_c", False), core, "g_" + kind + "_sum_c")

    def sending(host, items):
        if not items:
            return host(())
        result, carried = host([_Exchange(ps, ("x", "y"), 0, False) for _, ps in items])
        for (key, _), got in zip(items, carried):
            reduced[key] = got
        return result

    late_act = []
    for i in reversed(range(L)):
        j = i // 2
        xin, h1, mix, y, x1, h2, a, hmid, y2 = saved[i]
        sh1, sc1, g1, sh2, sc2, g2 = (mod[i, k] for k in range(6))
        dhm = _matmul(dy2, weights[("ffn_out", i)], "nt", "ffn_out_dx", tm=1024, tn=512)
        p_out = pair("ffn_out", _matmul(hmid, dy2, "tn", "ffn_out_dw", tm=512, tn=1024))
        da, dcw, dcb = sending(lambda r: _ffn_act_bwd(a, dhm, conv_w[i], conv_b[i:i + 1], "ffn_act_bwd", rides=r),
                               [(("ffn_out", i), p_out)] + late_act)
        ffn_in_dx = lambda r: _matmul(da, weights[("ffn_in", i)], "nn", "ffn_in_dx", out_dtype=F32, tm=512, tn=1024,
                                      tk=2816, rides=r)
        if i == 0 and L > 1:
            early = [jnp.stack(g_mix[1:]), jnp.stack(g_ffn_n[1:]), jnp.stack(g_vg), jnp.stack(g_ws), jnp.stack(g_bs),
                     jnp.stack(g_cw[1:]), jnp.stack(g_cb[1:]), g_final[0]] + ([jnp.stack(g_bf[1:])] if NA > 1 else [])
            dh2, (both,) = ffn_in_dx([_Exchange(_pack(early), ("c",), 0, True)])
            early_pair = _sum_slots(both[None], F32, "g_small_sum_c")[0]
        else:
            dh2 = ffn_in_dx(())
        late_act = []
        p_in = pair("ffn_in", _matmul(da, h2, "tn", "ffn_in_dw", tm=512, tn=1024))
        g_cw[i] = _unpad_blocks(dcw, 1, 8, Fq)
        g_cb[i] = _unpad_blocks(dcb, 1, 8, Fq)[0]
        dx1, dsc2, dsh2, dgn, dy, dg1 = _norm_bwd(
            x1, ffn_norm_g[i:i + 1], sc2, sh2, dh2, dx, (y, g1), "norm_bwd_prev")
        g_ffn_n[i] = dgn[0]
        if mix[0] == "fox":
            _, qkv, flog, bias, frow, o, lse = mix
            do = _matmul(dy, weights[("attn_o", j)], "nt", "attn_out_dx", tm=1024, tn=1024)
            p_o = pair("attn_o", _matmul(o, dy, "tn", "attn_out_dw", tm=512, tn=1024))
            dq, dfq = _attn_dq(qkv, o, do, lse, frow, H, "attn_dq")
            rides = [_Exchange(ps, ("x", "y"), 0, False) for ps in (p_in, p_o)]
            if i == 0 and L > 1:
                rides.append(_Exchange(early_pair, ("x", "y"), 0, True))
            (dk, dv, dfk), carried = _attn_dkv(qkv, o, do, lse, frow, H, "attn_dkv", rides=rides)
            reduced[("ffn_in", i)], reduced[("attn_o", j)] = carried[:2]
            if i == 0 and L > 1:
                early_sum = _sum_slots(carried[2][None], F32, "g_small_sum_xy")[0]
            heads_on_lanes = lambda t: _pad_axis(jnp.transpose(t.reshape(H, T)), 1, LANES)
            dflog, dbf = _fox_gate_bwd(heads_on_lanes(dfq), heads_on_lanes(dfk), flog, bias, "fox_gate_bwd")
            dproj = jnp.concatenate([dq, dk, dv], axis=1)
            gwt_qkv = _matmul(dproj, h1, "tn", "attn_qkv_dw", tm=512, tn=1024)
            gwt_f = _matmul(dflog, h1, "tn", "attn_f_dw", tm=128, tn=1024)
            gwt = jnp.concatenate([gwt_qkv, gwt_f[:H]], axis=0).reshape(4, n_attn, D)
            p_a = pair("attn_in", _pad_axis(gwt, 1, n_attn_p).reshape(4 * n_attn_p, D))
            dh_f = _matmul(dflog, weights[("f", j)], "nn", "attn_f_dx", out_dtype=F32, tm=1024, tn=1024)
            dh1 = sending(lambda r: _matmul(dproj, weights[("qkv", j)], "nn", "attn_qkv_dx", out_dtype=F32, tm=512,
                                            tn=1024, tk=2048, res=dh_f, rides=r), [(("attn_in", j), p_a)])
            g_bf[j] = dbf[0, :H]
        else:
            _, p, bt, gated = mix
            late_act = [(("ffn_in", i), p_in)]
            dgated = _matmul(dy, weights[("gm_o", j)], "nt", "gm_out_dx", tm=1024, tn=1024)
            p_o = pair("gm_o", _matmul(gated, dy, "tn", "gm_out_dw", tm=512, tn=1024))
            dp, dws, dbt, dvg = _gm_bwd(p, dgated, vg_full[j:j + 1], gm_w_s[j],
                                        jnp.swapaxes(gm_w_s[j], 1, 2), bt, "gm_bwd")
            p_a = pair("gm_in", _matmul(dp, h1, "tn", "gm_in_dw", tm=512, tn=1024))
            dh1 = sending(lambda r: _matmul(dp, weights[("gm_in", j)], "nn", "gm_in_dx", out_dtype=F32, tm=512,
                                            tn=1024, tk=2048, rides=r), [(("gm_in", j), p_a), (("gm_o", j), p_o)])
            g_ws[j] = dws
            g_bs[j] = jnp.transpose(dbt)[:NG]
            g_vg[j] = dvg[0]
        if i > 0:
            dx, dsc1, dsh1, dgn, dy2, dg2_prev = _norm_bwd(
                xin, mix_norm_g[i:i + 1], sc1, sh1, dh1, dx1, (saved[i - 1][8], mod[i - 1, 5]), "norm_bwd_prev")
        else:
            dx, dsc1, dsh1, dgn = _norm_bwd(xin, mix_norm_g[i:i + 1], sc1, sh1, dh1, dx1, None, "norm_bwd_first")
            dg2_prev = None
        g_mix[i] = dgn[0]
        dmod[i] = jnp.concatenate([dsh1, dsc1, dg1, dsh2, dsc2, dg2], axis=1)[0]
        dg2 = dg2_prev
    grad_x = dx.reshape(1, T, D)
    loss = lax.psum(loss_blk[0, 0], MESH_AXES)

    if L > 1:
        late = [g_mix[0][None], g_ffn_n[0][None], g_cw[0][None], g_cb[0][None], g_bf[0][None]]
        l_mix, l_ffn, l_cw, l_cb, l_bf = _unpack(_all_reduce_small(_pack(late), "g_small_late"), late)
        got = _unpack(early_sum, early)
        e_mix, e_ffn, g_vg_full, grad_gm_w_s, grad_gm_b_s, e_cw, e_cb, grad_final_g = got[:8]
        grad_mix_norm_g = jnp.concatenate([l_mix, e_mix])
        grad_ffn_norm_g = jnp.concatenate([l_ffn, e_ffn])
        g_cw_full = jnp.concatenate([l_cw, e_cw])
        grad_ffn_conv_b = jnp.concatenate([l_cb, e_cb])
        grad_attn_b_f = jnp.concatenate([l_bf, got[8]]) if NA > 1 else l_bf
    else:
        small = [jnp.stack(g_mix), jnp.stack(g_ffn_n), jnp.stack(g_bf), jnp.stack(g_vg), jnp.stack(g_ws),
                 jnp.stack(g_bs), jnp.stack(g_cw), jnp.stack(g_cb), g_final[0]]
        (grad_mix_norm_g, grad_ffn_norm_g, grad_attn_b_f, g_vg_full, grad_gm_w_s, grad_gm_b_s, g_cw_full,
         grad_ffn_conv_b, grad_final_g) = _unpack(_all_reduce_small(_pack(small), "g_small"), small)
    grad_gm_v_g = lax.dynamic_slice_in_dim(g_vg_full, chip * (D // 4), D // 4, axis=1)
    grad_ffn_conv_w = lax.dynamic_slice_in_dim(g_cw_full, chip * 2 * Fq, 2 * Fq, axis=2)

    dmod_all = _all_gather(jnp.stack(dmod), MESH_AXES, 0, "dmod_ag")
    grad_mod_b = _sum_slots(dmod_all[None], F32, "dmod_sum")[0]
    dmod_cols = jnp.swapaxes(lax.dynamic_slice_in_dim(dmod_all, chip * nmod, nmod, axis=2), 0, 1)

    grad_mod_w, delta_mod_w, new_m_mod_w, new_v_mod_w = _modw_adamw(
        jnp.transpose(c_act), dmod_cols, mod_w, m_mod_w, v_mod_w, "mod_w_adamw")

    def total(kind, n_layers):
        got = jnp.stack([reduced[(kind, l)] for l in range(n_layers)])
        tot = _sum_slots(got, F32, "g_" + kind + "_sum_xy")
        return _sib_stream(tot, core, "g_" + kind + "_ag_c", True).reshape(n_layers, -1, D)

    gt_attn_in = total("attn_in", NA)[:, :n_attn]
    grad_attn_w_in = jnp.swapaxes(gt_attn_in, 1, 2)
    grad_attn_w_o = total("attn_o", NA)
    grad_gm_w_in = jnp.swapaxes(total("gm_in", NB), 1, 2)
    grad_gm_w_o = total("gm_o", NB)
    gt_ffn_in = _unpad_blocks(total("ffn_in", L), 1, 2, Fq)
    grad_ffn_w_in = jnp.swapaxes(gt_ffn_in, 1, 2)
    grad_ffn_w_out = total("ffn_out", L)[:, :Fq]

    def big(w, g, m, v, name):
        shp = w.shape
        flat = lambda t: t.reshape(-1, shp[-1])
        return [t.reshape(shp) for t in _adamw(flat(w), flat(g), flat(m), flat(v), name)]

    def big_t(w, gt, m, v, name):
        t = lambda a: jnp.swapaxes(a, 1, 2)
        return [t(u) for u in big(t(w), gt, t(m), t(v), name)]

    upd_attn_in = big_t(attn_w_in, gt_attn_in, m_attn_w_in, v_attn_w_in, "adamw_attn_in")
    upd_attn_o = big(attn_w_o, grad_attn_w_o, m_attn_w_o, v_attn_w_o, "adamw_attn_o")
    upd_gm_in = big(gm_w_in, grad_gm_w_in, m_gm_w_in, v_gm_w_in, "adamw_gm_in")
    upd_gm_o = big(gm_w_o, grad_gm_w_o, m_gm_w_o, v_gm_w_o, "adamw_gm_o")
    upd_ffn_in = big_t(ffn_w_in, gt_ffn_in, m_ffn_w_in, v_ffn_w_in, "adamw_ffn_in")
    upd_ffn_out = big(ffn_w_out, grad_ffn_w_out, m_ffn_w_out, v_ffn_w_out, "adamw_ffn_out")

    sm_w = [mod_b, mix_norm_g, ffn_norm_g, attn_b_f, gm_v_g, gm_w_s, gm_b_s, ffn_conv_w, ffn_conv_b, final_g]
    sm_g = [grad_mod_b, grad_mix_norm_g, grad_ffn_norm_g, grad_attn_b_f, grad_gm_v_g, grad_gm_w_s, grad_gm_b_s,
            grad_ffn_conv_w, grad_ffn_conv_b, grad_final_g]
    sm_m = [m_mod_b, m_mix_norm_g, m_ffn_norm_g, m_attn_b_f, m_gm_v_g, m_gm_w_s, m_gm_b_s, m_ffn_conv_w,
            m_ffn_conv_b, m_final_g]
    sm_v = [v_mod_b, v_mix_norm_g, v_ffn_norm_g, v_attn_b_f, v_gm_v_g, v_gm_w_s, v_gm_b_s, v_ffn_conv_w,
            v_ffn_conv_b, v_final_g]
    sm_d, sm_nm, sm_nv = (_unpack(t, sm_w) for t in _adamw(_pack(sm_w), _pack(sm_g), _pack(sm_m), _pack(sm_v),
                                                             "adamw_small"))

    def ordered(mod_w_item, sm, k):
        return [mod_w_item, sm[0], sm[1], sm[2], upd_attn_in[k], sm[3], upd_attn_o[k], upd_gm_in[k], sm[4], sm[5],
                sm[6], upd_gm_o[k], upd_ffn_in[k], sm[7], sm[8], upd_ffn_out[k], sm[9]]

    grads = [grad_mod_w, grad_mod_b, grad_mix_norm_g, grad_ffn_norm_g, grad_attn_w_in, grad_attn_b_f, grad_attn_w_o,
             grad_gm_w_in, grad_gm_v_g, grad_gm_w_s, grad_gm_b_s, grad_gm_w_o, grad_ffn_w_in, grad_ffn_conv_w,
             grad_ffn_conv_b, grad_ffn_w_out, grad_final_g]
    return (loss, grad_x, *grads, *ordered(delta_mod_w, sm_d, 0), *ordered(new_m_mod_w, sm_nm, 1),
            *ordered(new_v_mod_w, sm_nv, 2))
```

```python
import functools
import math

import jax
import jax.numpy as jnp
from jax import lax
from jax.experimental import pallas as pl
from jax.experimental.pallas import tpu as pltpu

F32 = jnp.float32
BF16 = jnp.bfloat16

EPS = 1e-6
HEAD_DIM = 128
CHUNK = 128
GROUP = 128
LANES = 128
BF16_ROWS = 16
VMEM_LIMIT_BYTES = 52 * 1024 * 1024
NEG_BIG = -1e30
SMALL_PACK_COLS = 1024

ADAM_LR, ADAM_B1, ADAM_B2, ADAM_EPS, ADAM_WD, ADAM_STEP = 0.001, 0.9, 0.999, 1e-08, 0.01, 10

MESH_AXES = ("x", "y", "c")
MESH = pl.DeviceIdType.MESH


def _pick(n, pref, align):
    t = (min(pref, n) // align) * align
    while t >= align:
        if n % t == 0:
            return t
        t -= align
    return n


def _params(*sem):
    return pltpu.CompilerParams(dimension_semantics=sem, vmem_limit_bytes=VMEM_LIMIT_BYTES)


def _group(axes):
    pos = {ax: lax.axis_index(ax) for ax in MESH_AXES}
    rank = 0
    for ax in axes:
        rank = rank * 2 + pos[ax]
    peers = []
    for mask in range(1, 2 ** len(axes)):
        peer = dict(pos)
        for bit, ax in enumerate(reversed(axes)):
            if (mask >> bit) & 1:
                peer[ax] = 1 - pos[ax]
        peers.append((mask, tuple(peer[ax] for ax in MESH_AXES)))
    return rank, peers


def _slot(ref, gaxis, idx):
    return ref.at[(slice(None),) * gaxis + (idx,)]


D2D_STREAMS = 32
ICI_STREAMS = 8


def _pieces(shape, want):
    idx = [()]
    d = 0
    while d < len(shape) - 2 and len(idx) * shape[d] <= want:
        idx = [i + (k,) for i in idx for k in range(shape[d])]
        d += 1
    if d == len(shape) - 2:
        rows = shape[-2]
        r = max(1, want // len(idx))
        while r > 1 and not (rows % r == 0 and (rows // r) % BF16_ROWS == 0):
            r -= 1
        if r > 1:
            idx = [i + (pl.ds(k * (rows // r), rows // r),) for i in idx for k in range(r)]
    return idx


class _Exchange:
    def __init__(self, x, axes, gaxis, gather, sel=None):
        self.x, self.axes, self.gaxis, self.gather, self.sel = x, axes, gaxis, gather, sel
        n = 2 ** len(axes)
        shape = x.shape if sel is None else x.shape[1:]
        if gather:
            out_shape = shape[:gaxis] + (n,) + shape[gaxis:]
            slab = shape
        else:
            assert shape[gaxis] == n
            out_shape = shape
            slab = shape[:gaxis] + shape[gaxis + 1:]
        self.out_shape = jax.ShapeDtypeStruct(out_shape, x.dtype)
        self.pieces = _pieces(slab, D2D_STREAMS if axes == ("c",) else ICI_STREAMS)
        self.scratch = [pltpu.SemaphoreType.DMA((n - 1,)), pltpu.SemaphoreType.DMA((n - 1,)),
                        pltpu.SemaphoreType.DMA(())]

    def _views(self, x_ref, o_ref):
        if self.sel is not None:
            x_ref = x_ref.at[self.sel]
        src = (lambda member: x_ref) if self.gather else (lambda member: _slot(x_ref, self.gaxis, member))
        dst = lambda member: _slot(o_ref, self.gaxis, member)
        return src, dst

    @staticmethod
    def _remote(mask, peer, s, d, send_sems, recv_sems):
        return pltpu.make_async_remote_copy(
            src_ref=s, dst_ref=d, send_sem=send_sems.at[mask - 1], recv_sem=recv_sems.at[mask - 1],
            device_id=peer, device_id_type=MESH)

    def start(self, x_ref, o_ref, send_sems, recv_sems, local_sem):
        rank, peers = _group(self.axes)
        src, dst = self._views(x_ref, o_ref)
        cut = lambda ref, pc: ref.at[pc] if pc else ref
        for pc in self.pieces:
            pltpu.make_async_copy(cut(src(rank), pc), cut(dst(rank), pc), local_sem).start()
            for mask, peer in peers:
                self._remote(mask, peer, cut(src(rank ^ mask), pc), cut(dst(rank), pc), send_sems, recv_sems).start()

    def finish(self, x_ref, o_ref, send_sems, recv_sems, local_sem):
        rank, peers = _group(self.axes)
        src, dst = self._views(x_ref, o_ref)
        for mask, peer in peers:
            whole = self._remote(mask, peer, src(rank), dst(rank ^ mask), send_sems, recv_sems)
            whole.wait_recv()
            whole.wait_send()
        pltpu.make_async_copy(src(rank), dst(rank), local_sem).wait()


ANY_SPEC = pl.BlockSpec(memory_space=pl.ANY)


def _call(body, *, name, grid, in_specs, out_specs, out_shape, scratch_shapes, args, rides=()):
    out_specs, out_shape = list(out_specs), list(out_shape)
    n_in, n_out, n_s, nr = len(in_specs), len(out_specs), len(scratch_shapes), len(rides)
    if not rides:
        sem = ("arbitrary",) * len(grid)
        outs = pl.pallas_call(body, name=name, grid=grid, in_specs=in_specs, out_specs=out_specs,
                              out_shape=out_shape, scratch_shapes=scratch_shapes,
                              compiler_params=_params(*sem))(*args)
        return list(outs), []

    def carried(*refs):
        k = 0
        parts = []
        for cnt in (n_in, nr, n_out, nr, n_s, 3 * nr):
            parts.append(refs[k:k + cnt])
            k += cnt
        h_in, r_in, h_out, r_out, h_s, r_s = parts
        ids = [pl.program_id(d) for d in range(len(grid))]
        first = functools.reduce(jnp.logical_and, [i == 0 for i in ids])
        last = functools.reduce(jnp.logical_and, [i == g - 1 for i, g in zip(ids, grid)])

        @pl.when(first)
        def _():
            for r, ride in enumerate(rides):
                ride.start(r_in[r], r_out[r], *r_s[3 * r:3 * r + 3])

        body(*h_in, *h_out, *h_s)

        @pl.when(last)
        def _():
            for r, ride in enumerate(rides):
                ride.finish(r_in[r], r_out[r], *r_s[3 * r:3 * r + 3])

    outs = pl.pallas_call(
        carried, name=name, grid=grid,
        in_specs=list(in_specs) + [ANY_SPEC] * nr,
        out_specs=out_specs + [ANY_SPEC] * nr,
        out_shape=out_shape + [r.out_shape for r in rides],
        scratch_shapes=list(scratch_shapes) + [s for r in rides for s in r.scratch],
        compiler_params=_params(*(("arbitrary",) * len(grid))),
    )(*args, *[r.x for r in rides])
    return list(outs[:n_out]), list(outs[n_out:])


def _exchange(x, axes, gaxis, name, gather):
    ex = _Exchange(x, axes, gaxis, gather)

    def body(x_ref, o_ref, send_sems, recv_sems, local_sem):
        ex.start(x_ref, o_ref, send_sems, recv_sems, local_sem)
        ex.finish(x_ref, o_ref, send_sems, recv_sems, local_sem)

    return pl.pallas_call(body, name=name, out_shape=ex.out_shape, in_specs=[ANY_SPEC], out_specs=ANY_SPEC,
                          scratch_shapes=ex.scratch)(x)


def _all_gather(x, axes, gaxis, name):
    return _exchange(x, axes, gaxis, name, True)


def _all_to_all(x, axes, gaxis, name):
    return _exchange(x, axes, gaxis, name, False)


def _cast_half(w, core, name, blk=None, blk_p=None):
    L, n_src, D = w.shape
    if blk is None:
        blk = blk_p = n_src
    n2 = (n_src // blk) * blk_p // 2
    tr = _pick(math.gcd(math.gcd(blk, blk_p), n2), 256, BF16_ROWS)
    nb, src_per, dst_per = n2 // tr, blk // tr, blk_p // tr

    def body(core_ref, w_ref, o_ref):
        inside = (core_ref[0] * nb + pl.program_id(1)) % dst_per < src_per
        o_ref[...] = jnp.where(inside, w_ref[...], 0.0).astype(BF16)

    def src_block(l, i, core_ref):
        at = core_ref[0] * nb + i
        return l, (at // dst_per) * src_per + jnp.minimum(at % dst_per, src_per - 1), 0

    return pl.pallas_call(
        body, name=name,
        grid_spec=pltpu.PrefetchScalarGridSpec(
            num_scalar_prefetch=1, grid=(L, nb),
            in_specs=[pl.BlockSpec((None, tr, D), src_block)],
            out_specs=pl.BlockSpec((None, tr, D), lambda l, i, core_ref: (l, i, 0))),
        out_shape=jax.ShapeDtypeStruct((L, n2, D), BF16),
        compiler_params=_params("parallel", "parallel"),
    )(core, w)


def _sum_slots(x, out_dtype, name):
    A, G, R, C = x.shape
    budget = (2 * 1024 * 1024) // (G * C * x.dtype.itemsize)
    tr = _pick(R, max(budget, BF16_ROWS), BF16_ROWS)

    def body(x_ref, o_ref):
        acc = x_ref[0].astype(F32)
        for g in range(1, G):
            acc = acc + x_ref[g].astype(F32)
        o_ref[...] = acc.astype(out_dtype)

    return pl.pallas_call(
        body, name=name, grid=(A, R // tr),
        in_specs=[pl.BlockSpec((None, G, tr, C), lambda a, i: (a, 0, i, 0))],
        out_specs=pl.BlockSpec((None, tr, C), lambda a, i: (a, i, 0)),
        out_shape=jax.ShapeDtypeStruct((A, R, C), out_dtype),
        compiler_params=_params("parallel", "parallel"),
    )(x)


def _adamw_math(w, g, m, v):
    m = ADAM_B1 * m + (1.0 - ADAM_B1) * g
    v = ADAM_B2 * v + (1.0 - ADAM_B2) * (g * g)
    m_hat = m / (1.0 - ADAM_B1 ** ADAM_STEP)
    v_hat = v / (1.0 - ADAM_B2 ** ADAM_STEP)
    delta = -ADAM_LR * (m_hat / (jnp.sqrt(v_hat) + ADAM_EPS) + ADAM_WD * w)
    return delta, m, v


def _adamw(w, g, m, v, name):
    R, C = w.shape
    budget = (768 * 1024) // (C * 4)
    tr = _pick(R, max(budget, 8), 8)

    def body(w_ref, g_ref, m_ref, v_ref, d_ref, nm_ref, nv_ref):
        d, nm, nv = _adamw_math(w_ref[...], g_ref[...], m_ref[...], v_ref[...])
        d_ref[...] = d
        nm_ref[...] = nm
        nv_ref[...] = nv

    spec = pl.BlockSpec((tr, C), lambda i: (i, 0))
    return pl.pallas_call(
        body, name=name, grid=(R // tr,),
        in_specs=[spec] * 4, out_specs=[spec] * 3,
        out_shape=[jax.ShapeDtypeStruct((R, C), F32)] * 3,
        compiler_params=_params("parallel"),
    )(w, g, m, v)


def _modw_adamw(ct, dm, w, m, v, name, rides=()):
    L, D, N = w.shape
    E = ct.shape[1]
    tr = _pick(D, 64, 8)

    def body(ct_ref, dm_ref, w_ref, m_ref, v_ref, g_ref, d_ref, nm_ref, nv_ref):
        g = ct_ref[:, 0:1] * dm_ref[0:1, :]
        for e in range(1, E):
            g = g + ct_ref[:, e:e + 1] * dm_ref[e:e + 1, :]
        d, nm, nv = _adamw_math(w_ref[...], g, m_ref[...], v_ref[...])
        g_ref[...] = g
        d_ref[...] = d
        nm_ref[...] = nm
        nv_ref[...] = nv

    spec = pl.BlockSpec((None, tr, N), lambda l, i: (l, i, 0))
    outs, carried = _call(
        body, name=name, grid=(L, D // tr),
        in_specs=[pl.BlockSpec((tr, E), lambda l, i: (i, 0)),
                  pl.BlockSpec((None, E, N), lambda l, i: (l, 0, 0)), spec, spec, spec],
        out_specs=[spec] * 4,
        out_shape=[jax.ShapeDtypeStruct((L, D, N), F32)] * 4,
        scratch_shapes=[], args=(ct, dm, w, m, v), rides=rides)
    return (outs, carried) if rides else outs


def _mod_fwd(c_all, w, b, name):
    L, D, N = w.shape
    E = c_all.shape[0]
    tn = _pick(N, 512, LANES)

    def body(c_ref, w_ref, b_ref, act_ref, o_ref):
        cv = c_ref[...]
        act = cv / (1.0 + jnp.exp(-cv))
        act_ref[...] = act
        o_ref[...] = jnp.dot(act.astype(BF16), w_ref[...].astype(BF16),
                             preferred_element_type=F32) + b_ref[...]

    return pl.pallas_call(
        body, name=name, grid=(L, N // tn),
        in_specs=[pl.BlockSpec((E, D), lambda l, j: (0, 0)),
                  pl.BlockSpec((None, D, tn), lambda l, j: (l, 0, j)),
                  pl.BlockSpec((None, 1, tn), lambda l, j: (l, 0, j))],
        out_specs=[pl.BlockSpec((E, D), lambda l, j: (0, 0)),
                   pl.BlockSpec((None, E, tn), lambda l, j: (l, 0, j))],
        out_shape=[jax.ShapeDtypeStruct((E, D), F32), jax.ShapeDtypeStruct((L, E, N), F32)],
        compiler_params=_params("arbitrary", "arbitrary"),
    )(c_all, w, b.reshape(L, 1, N))


_DIMS = {"nn": (((1,), (0,)), ((), ())), "nt": (((1,), (1,)), ((), ())), "tn": (((0,), (0,)), ((), ()))}


def _matmul(a, b, form, name, *, out_dtype=BF16, tm=512, tn=512, tk=None, a_resident=False,
            res=None, gate=None, emit_y=False, rides=()):
    if form == "nn":
        (M, K), N = a.shape, b.shape[1]
    elif form == "nt":
        (M, K), N = a.shape, b.shape[0]
    else:
        (K, M), N = a.shape, b.shape[1]
    tm = _pick(M, tm, LANES if form == "tn" else BF16_ROWS)
    tn = _pick(N, tn, LANES)
    tk = K if (tk is None or form == "tn") else _pick(K, tk, LANES)
    nk = K // tk

    def ij(g0, g1):
        return (g0, g1) if a_resident else (g1, g0)

    grid = (M // tm, N // tn, nk) if a_resident else (N // tn, M // tm, nk)
    if form == "tn":
        a_spec = pl.BlockSpec((K, tm), lambda g0, g1, k: (0, ij(g0, g1)[0]))
        b_spec = pl.BlockSpec((K, tn), lambda g0, g1, k: (0, ij(g0, g1)[1]))
    else:
        a_spec = pl.BlockSpec((tm, tk), lambda g0, g1, k: (ij(g0, g1)[0], k))
        if form == "nn":
            b_spec = pl.BlockSpec((tk, tn), lambda g0, g1, k: (k, ij(g0, g1)[1]))
        else:
            b_spec = pl.BlockSpec((tn, tk), lambda g0, g1, k: (ij(g0, g1)[1], k))
    tile = pl.BlockSpec((tm, tn), lambda g0, g1, k: ij(g0, g1))
    row = pl.BlockSpec((1, tn), lambda g0, g1, k: (0, ij(g0, g1)[1]))
    in_specs, args = [a_spec, b_spec], [a, b]
    if res is not None:
        in_specs.append(tile)
        args.append(res)
    if gate is not None:
        in_specs.append(row)
        args.append(gate)
    if emit_y:
        out_specs = [tile, tile]
        out_shape = [jax.ShapeDtypeStruct((M, N), BF16), jax.ShapeDtypeStruct((M, N), out_dtype)]
    else:
        out_specs = tile
        out_shape = jax.ShapeDtypeStruct((M, N), out_dtype)
    n_in = len(args)

    def body(*refs):
        a_ref, b_ref = refs[0], refs[1]
        res_ref = refs[2] if res is not None else None
        gate_ref = refs[n_in - 1] if gate is not None else None
        outs = refs[n_in:n_in + (2 if emit_y else 1)]

        def finish(acc):
            val = acc
            if gate_ref is not None:
                val = gate_ref[...] * val
            if res_ref is not None:
                val = res_ref[...] + val
            if emit_y:
                outs[0][...] = acc.astype(BF16)
            outs[-1][...] = val.astype(out_dtype)

        part = lax.dot_general(a_ref[...], b_ref[...], _DIMS[form], preferred_element_type=F32)
        if nk == 1:
            finish(part)
        else:
            acc_ref = refs[-1]
            k = pl.program_id(2)

            @pl.when(k == 0)
            def _():
                acc_ref[...] = part

            @pl.when(k > 0)
            def _():
                acc_ref[...] += part

            @pl.when(k == nk - 1)
            def _():
                finish(acc_ref[...])

    outs, carried = _call(
        body, name=name, grid=grid, in_specs=in_specs, out_specs=out_specs if emit_y else [out_specs],
        out_shape=out_shape if emit_y else [out_shape],
        scratch_shapes=[pltpu.VMEM((tm, tn), F32)] if nk > 1 else [], args=args, rides=rides)
    result = outs if emit_y else outs[0]
    return (result, carried) if rides else result


def _norm_mod(x, g, sc, sh, name):
    T, D = x.shape
    tm = _pick(T, 256, BF16_ROWS)

    def body(x_ref, g_ref, sc_ref, sh_ref, h_ref):
        xv = x_ref[...]
        rstd = lax.rsqrt(jnp.mean(xv * xv, axis=-1, keepdims=True) + EPS)
        y = xv * rstd * g_ref[...]
        h_ref[...] = (y * (1.0 + sc_ref[...]) + sh_ref[...]).astype(BF16)

    tile = pl.BlockSpec((tm, D), lambda i: (i, 0))
    row = pl.BlockSpec((1, D), lambda i: (0, 0))
    return pl.pallas_call(
        body, name=name, grid=(T // tm,), in_specs=[tile, row, row, row], out_specs=tile,
        out_shape=jax.ShapeDtypeStruct((T, D), BF16), compiler_params=_params("parallel"),
    )(x, g, sc, sh)


def _colsum(v):
    return jnp.sum(v, axis=0, keepdims=True)


def _norm_bwd(x, g, sc, sh, dh, dres, prev, name):
    T, D = x.shape
    tm = _pick(T, 256, BF16_ROWS)
    has_prev = prev is not None

    def body(*refs):
        x_ref, g_ref, sc_ref, sh_ref, dh_ref, dres_ref = refs[:6]
        k = 6
        if has_prev:
            y_ref, gate_ref = refs[6:8]
            k = 8
        dx_ref, dsc_ref, dsh_ref, dg_ref = refs[k:k + 4]
        first = pl.program_id(0) == 0
        xv = x_ref[...]
        rstd = lax.rsqrt(jnp.mean(xv * xv, axis=-1, keepdims=True) + EPS)
        xhat = xv * rstd
        gv = g_ref[...]
        dhv = dh_ref[...].astype(F32)
        dyv = dhv * (1.0 + sc_ref[...])
        dxhat = dyv * gv
        dx = rstd * (dxhat - xhat * jnp.mean(dxhat * xhat, axis=-1, keepdims=True))
        dxt = dres_ref[...] + dx
        dx_ref[...] = dxt
        sums = [(dsc_ref, _colsum(dhv * (xhat * gv))), (dsh_ref, _colsum(dhv)), (dg_ref, _colsum(dyv * xhat))]
        if has_prev:
            dy_ref, dgate_ref = refs[k + 4:k + 6]
            dy_ref[...] = (dxt * gate_ref[...]).astype(BF16)
            sums.append((dgate_ref, _colsum(dxt * y_ref[...].astype(F32))))
        for ref, val in sums:
            @pl.when(first)
            def _(ref=ref, val=val):
                ref[...] = val

            @pl.when(jnp.logical_not(first))
            def _(ref=ref, val=val):
                ref[...] += val

    tile = pl.BlockSpec((tm, D), lambda i: (i, 0))
    row = pl.BlockSpec((1, D), lambda i: (0, 0))
    rowshape = jax.ShapeDtypeStruct((1, D), F32)
    in_specs = [tile, row, row, row, tile, tile]
    args = [x, g, sc, sh, dh, dres]
    out_specs = [tile, row, row, row]
    out_shape = [jax.ShapeDtypeStruct((T, D), F32), rowshape, rowshape, rowshape]
    if has_prev:
        in_specs += [tile, row]
        args += list(prev)
        out_specs += [tile, row]
        out_shape += [jax.ShapeDtypeStruct((T, D), BF16), rowshape]
    return pl.pallas_call(
        body, name=name, grid=(T // tm,), in_specs=in_specs, out_specs=out_specs, out_shape=out_shape,
        compiler_params=_params("arbitrary"),
    )(*args)


def _final_loss_bwd(x, g, target, y_prev, gate_prev, name):
    T, D = x.shape
    tm = _pick(T, 256, BF16_ROWS)

    def body(x_ref, g_ref, t_ref, y_ref, gate_ref, loss_ref, dx_ref, dg_ref, dy_ref, dgate_ref):
        first = pl.program_id(0) == 0
        xv = x_ref[...]
        rstd = lax.rsqrt(jnp.mean(xv * xv, axis=-1, keepdims=True) + EPS)
        xhat = xv * rstd
        gv = g_ref[...]
        err = xhat * gv - t_ref[...]
        part = 0.5 * jnp.sum(jnp.mean(err * err, axis=-1, keepdims=True))
        dyf = err * (1.0 / D)
        dxhat = dyf * gv
        dx = rstd * (dxhat - xhat * jnp.mean(dxhat * xhat, axis=-1, keepdims=True))
        dx_ref[...] = dx
        dy_ref[...] = (dx * gate_ref[...]).astype(BF16)
        sums = [(dg_ref, _colsum(dyf * xhat)), (dgate_ref, _colsum(dx * y_ref[...].astype(F32))),
                (loss_ref, jnp.full((8, LANES), part, F32))]
        for ref, val in sums:
            @pl.when(first)
            def _(ref=ref, val=val):
                ref[...] = val

            @pl.when(jnp.logical_not(first))
            def _(ref=ref, val=val):
                ref[...] += val

    tile = pl.BlockSpec((tm, D), lambda i: (i, 0))
    row = pl.BlockSpec((1, D), lambda i: (0, 0))
    rowshape = jax.ShapeDtypeStruct((1, D), F32)
    return pl.pallas_call(
        body, name=name, grid=(T // tm,),
        in_specs=[tile, row, tile, tile, row],
        out_specs=[pl.BlockSpec((8, LANES), lambda i: (0, 0)), tile, row, tile, row],
        out_shape=[jax.ShapeDtypeStruct((8, LANES), F32), jax.ShapeDtypeStruct((T, D), F32), rowshape,
                   jax.ShapeDtypeStruct((T, D), BF16), rowshape],
        compiler_params=_params("arbitrary"),
    )(x, g, target, y_prev, gate_prev)


def _split3(v):
    hi = v.astype(BF16)
    r1 = v - hi.astype(F32)
    mid = r1.astype(BF16)
    lo = (r1 - mid.astype(F32)).astype(BF16)
    return hi, mid, lo


def _tri_dot(tri, v):
    hi, mid, lo = _split3(v)
    dot = functools.partial(jnp.dot, preferred_element_type=F32)
    return dot(tri, hi) + dot(tri, mid) + dot(tri, lo)


def _fox_gate(flog, bias, name):
    T = flog.shape[0]
    tb = _pick(T, 256, BF16_ROWS)

    def body(f_ref, b_ref, o_ref, carry_ref):
        @pl.when(pl.program_id(0) == 0)
        def _():
            carry_ref[...] = jnp.zeros_like(carry_ref)

        xv = f_ref[...] + b_ref[...]
        lf = jnp.minimum(xv, 0.0) - jnp.log(1.0 + jnp.exp(-jnp.abs(xv)))
        r = lax.broadcasted_iota(jnp.int32, (tb, tb), 0)
        cidx = lax.broadcasted_iota(jnp.int32, (tb, tb), 1)
        tri = jnp.where(r >= cidx, 1.0, 0.0).astype(BF16)
        out = _tri_dot(tri, lf) + carry_ref[...]
        o_ref[...] = out
        carry_ref[...] = out[tb - 1:tb, :]

    return pl.pallas_call(
        body, name=name, grid=(T // tb,),
        in_specs=[pl.BlockSpec((tb, LANES), lambda i: (i, 0)), pl.BlockSpec((1, LANES), lambda i: (0, 0))],
        out_specs=pl.BlockSpec((tb, LANES), lambda i: (i, 0)),
        out_shape=jax.ShapeDtypeStruct((T, LANES), F32),
        scratch_shapes=[pltpu.VMEM((1, LANES), F32)],
        compiler_params=_params("arbitrary"),
    )(flog, bias)


def _fox_gate_bwd(dfq, dfk, flog, bias, name):
    T = flog.shape[0]
    tb = _pick(T, 256, BF16_ROWS)
    nb = T // tb

    def body(dq_ref, dk_ref, f_ref, b_ref, o_ref, db_ref, carry_ref):
        first = pl.program_id(0) == 0

        @pl.when(first)
        def _():
            carry_ref[...] = jnp.zeros_like(carry_ref)

        dv = dq_ref[...] + dk_ref[...]
        r = lax.broadcasted_iota(jnp.int32, (tb, tb), 0)
        cidx = lax.broadcasted_iota(jnp.int32, (tb, tb), 1)
        tri = jnp.where(cidx >= r, 1.0, 0.0).astype(BF16)
        dl = _tri_dot(tri, dv) + carry_ref[...]
        carry_ref[...] = dl[0:1, :]
        xv = f_ref[...] + b_ref[...]
        dfl = dl / (1.0 + jnp.exp(xv))
        o_ref[...] = dfl.astype(BF16)
        val = _colsum(dfl)

        @pl.when(first)
        def _():
            db_ref[...] = val

        @pl.when(jnp.logical_not(first))
        def _():
            db_ref[...] += val

    rev = pl.BlockSpec((tb, LANES), lambda i: (nb - 1 - i, 0))
    row = pl.BlockSpec((1, LANES), lambda i: (0, 0))
    return pl.pallas_call(
        body, name=name, grid=(nb,), in_specs=[rev, rev, rev, row], out_specs=[rev, row],
        out_shape=[jax.ShapeDtypeStruct((T, LANES), BF16), jax.ShapeDtypeStruct((1, LANES), F32)],
        scratch_shapes=[pltpu.VMEM((1, LANES), F32)],
        compiler_params=_params("arbitrary"),
    )(dfq, dfk, flog, bias)


_NT = _DIMS["nt"]
_TN = _DIMS["tn"]


ATTN_TQ, ATTN_TK = 1024, 1024
ATTN_TKB, ATTN_TQI = 1024, 1024


def _visible(tq, tk, shift):
    r = lax.broadcasted_iota(jnp.int32, (tq, tk), 0)
    c = lax.broadcasted_iota(jnp.int32, (tq, tk), 1)
    return c + shift <= r


def _attn_fwd(qkv, frow, n_heads, name, rides=()):
    T = qkv.shape[0]
    H = n_heads
    tq = _pick(T, ATTN_TQ, LANES)
    tk = _pick(tq, ATTN_TK, LANES)
    per = tq // tk
    scale = HEAD_DIM ** -0.5

    def body(q_ref, k_ref, v_ref, fr_ref, o_ref, lse_ref):
        i = pl.program_id(1)
        q = q_ref[...]

        def step(j, carry, shift):
            m, l, acc = carry
            ks = pl.multiple_of(j * tk, tk)
            kb = k_ref[pl.ds(ks, tk), :]
            vb = v_ref[pl.ds(ks, tk), :]
            s = lax.dot_general(q, kb, _NT, preferred_element_type=F32) * scale - fr_ref[:, pl.ds(ks, tk)]
            if shift is not None:
                s = jnp.where(_visible(tq, tk, shift), s, NEG_BIG)
            m_new = jnp.maximum(m, jnp.max(s, axis=1, keepdims=True))
            alpha = jnp.exp(m - m_new)
            p = jnp.exp(s - m_new)
            l = alpha * l + jnp.sum(p, axis=1, keepdims=True)
            acc = alpha * acc + jnp.dot(p.astype(BF16), vb, preferred_element_type=F32)
            return m_new, l, acc

        carry = (jnp.full((tq, 1), NEG_BIG, F32), jnp.zeros((tq, 1), F32), jnp.zeros((tq, HEAD_DIM), F32))
        carry = lax.fori_loop(0, i * per, lambda j, c: step(j, c, None), carry)
        for d in range(per):
            carry = step(i * per + d, carry, d * tk)
        m, l, acc = carry
        o_ref[...] = (acc / l).astype(BF16)
        lse_ref[...] = jnp.broadcast_to(m + jnp.log(l), (tq, LANES))

    outs, carried = _call(
        body, name=name, grid=(H, T // tq),
        in_specs=[pl.BlockSpec((tq, HEAD_DIM), lambda h, i: (i, h)),
                  pl.BlockSpec((T, HEAD_DIM), lambda h, i: (0, H + h)),
                  pl.BlockSpec((T, HEAD_DIM), lambda h, i: (0, 2 * H + h)),
                  pl.BlockSpec((None, 1, T), lambda h, i: (h, 0, 0))],
        out_specs=[pl.BlockSpec((tq, HEAD_DIM), lambda h, i: (i, h)),
                   pl.BlockSpec((None, tq, LANES), lambda h, i: (h, i, 0))],
        out_shape=[jax.ShapeDtypeStruct((T, H * HEAD_DIM), BF16), jax.ShapeDtypeStruct((H, T, LANES), F32)],
        scratch_shapes=[], args=(qkv, qkv, qkv, frow), rides=rides)
    return (outs, carried) if rides else outs


def _attn_dq(qkv, o, do, lse, frow, n_heads, name):
    T = qkv.shape[0]
    H = n_heads
    tq = _pick(T, ATTN_TQ, LANES)
    tk = _pick(tq, ATTN_TK, LANES)
    per = tq // tk
    scale = HEAD_DIM ** -0.5

    def body(q_ref, k_ref, v_ref, o_ref, do_ref, lse_ref, fr_ref, dq_ref, df_ref):
        i = pl.program_id(1)
        q = q_ref[...]
        dob = do_ref[...]
        delta = jnp.sum(dob.astype(F32) * o_ref[...].astype(F32), axis=1, keepdims=True)
        lse_i = lse_ref[:, 0:1]

        def step(j, carry, shift):
            dq, rsum = carry
            ks = pl.multiple_of(j * tk, tk)
            kb = k_ref[pl.ds(ks, tk), :]
            vb = v_ref[pl.ds(ks, tk), :]
            s = lax.dot_general(q, kb, _NT, preferred_element_type=F32) * scale - fr_ref[:, pl.ds(ks, tk)]
            p = jnp.exp(s - lse_i)
            if shift is not None:
                p = jnp.where(_visible(tq, tk, shift), p, 0.0)
            dp = lax.dot_general(dob, vb, _NT, preferred_element_type=F32)
            ds = p * (dp - delta)
            return (dq + jnp.dot(ds.astype(BF16), kb, preferred_element_type=F32),
                    rsum + jnp.sum(ds, axis=1, keepdims=True))

        carry = (jnp.zeros((tq, HEAD_DIM), F32), jnp.zeros((tq, 1), F32))
        carry = lax.fori_loop(0, i * per, lambda j, c: step(j, c, None), carry)
        for d in range(per):
            carry = step(i * per + d, carry, d * tk)
        dq, rsum = carry
        dq_ref[...] = (dq * scale).astype(BF16)
        df_ref[...] = jnp.transpose(jnp.broadcast_to(rsum, (tq, LANES)))[0:1, :]

    qtile = lambda off: pl.BlockSpec((tq, HEAD_DIM), lambda h, i: (i, off + h))
    return pl.pallas_call(
        body, name=name, grid=(H, T // tq),
        in_specs=[qtile(0),
                  pl.BlockSpec((T, HEAD_DIM), lambda h, i: (0, H + h)),
                  pl.BlockSpec((T, HEAD_DIM), lambda h, i: (0, 2 * H + h)),
                  qtile(0), qtile(0),
                  pl.BlockSpec((None, tq, LANES), lambda h, i: (h, i, 0)),
                  pl.BlockSpec((None, 1, T), lambda h, i: (h, 0, 0))],
        out_specs=[qtile(0), pl.BlockSpec((None, 1, tq), lambda h, i: (h, 0, i))],
        out_shape=[jax.ShapeDtypeStruct((T, H * HEAD_DIM), BF16), jax.ShapeDtypeStruct((H, 1, T), F32)],
        compiler_params=_params("parallel", "arbitrary"),
    )(qkv, qkv, qkv, o, do, lse, frow)


def _attn_dkv(qkv, o, do, lse, frow, n_heads, name, rides=()):
    T = qkv.shape[0]
    H = n_heads
    tk = _pick(T, ATTN_TKB, LANES)
    tq = _pick(tk, ATTN_TQI, LANES)
    per = tk // tq
    nq = T // tq
    scale = HEAD_DIM ** -0.5

    def body(k_ref, v_ref, q_ref, o_ref, do_ref, lse_ref, fr_ref, dk_ref, dv_ref, df_ref):
        j = pl.program_id(1)
        kb = k_ref[...]
        vb = v_ref[...]
        fj = fr_ref[...]

        def step(i, carry, shift):
            dk, dv, df = carry
            qs = pl.multiple_of(i * tq, tq)
            qb = q_ref[pl.ds(qs, tq), :]
            dob = do_ref[pl.ds(qs, tq), :]
            delta = jnp.sum(dob.astype(F32) * o_ref[pl.ds(qs, tq), :].astype(F32), axis=1, keepdims=True)
            lse_i = lse_ref[pl.ds(qs, tq), 0:1]
            s = lax.dot_general(qb, kb, _NT, preferred_element_type=F32) * scale - fj
            p = jnp.exp(s - lse_i)
            if shift is not None:
                p = jnp.where(_visible(tq, tk, shift), p, 0.0)
            dv = dv + lax.dot_general(p.astype(BF16), dob, _TN, preferred_element_type=F32)
            dp = lax.dot_general(dob, vb, _NT, preferred_element_type=F32)
            ds = p * (dp - delta)
            dk = dk + lax.dot_general(ds.astype(BF16), qb, _TN, preferred_element_type=F32)
            df = df - _colsum(ds)
            return dk, dv, df

        carry = (jnp.zeros((tk, HEAD_DIM), F32), jnp.zeros((tk, HEAD_DIM), F32), jnp.zeros((1, tk), F32))
        for d in range(per):
            carry = step(j * per + d, carry, -d * tq)
        dk, dv, df = lax.fori_loop((j + 1) * per, nq, lambda i, c: step(i, c, None), carry)
        dk_ref[...] = (dk * scale).astype(BF16)
        dv_ref[...] = dv.astype(BF16)
        df_ref[...] = df

    full = lambda off: pl.BlockSpec((T, HEAD_DIM), lambda h, j: (0, off + h))
    ktile = lambda off: pl.BlockSpec((tk, HEAD_DIM), lambda h, j: (j, off + h))
    outs, carried = _call(
        body, name=name, grid=(H, T // tk),
        in_specs=[ktile(H), ktile(2 * H), full(0), full(0), full(0),
                  pl.BlockSpec((None, T, LANES), lambda h, j: (h, 0, 0)),
                  pl.BlockSpec((None, 1, tk), lambda h, j: (h, 0, j))],
        out_specs=[ktile(0), ktile(0), pl.BlockSpec((None, 1, tk), lambda h, j: (h, 0, j))],
        out_shape=[jax.ShapeDtypeStruct((T, H * HEAD_DIM), BF16)] * 2 + [jax.ShapeDtypeStruct((H, 1, T), F32)],
        scratch_shapes=[], args=(qkv, qkv, qkv, o, do, lse, frow), rides=rides)
    return (outs, carried) if rides else outs


_GELU_C = math.sqrt(2.0 / math.pi)
_GELU_A = 0.044715


def _gelu(p):
    t = jnp.tanh(_GELU_C * (p + _GELU_A * p * p * p))
    return 0.5 * p * (1.0 + t), t


def _gelu_grad(p, t):
    return 0.5 * (1.0 + t) + 0.5 * p * (1.0 - t * t) * _GELU_C * (1.0 + 3.0 * _GELU_A * p * p)


def _tril(shape, upper=False):
    r = lax.broadcasted_iota(jnp.int32, shape, 0)
    c = lax.broadcasted_iota(jnp.int32, shape, 1)
    return (c >= r) if upper else (r >= c)


def _gm_fwd(p, vg, ws, bt, name):
    T, D2 = p.shape
    D = D2 // 2
    NG = D // GROUP
    tm = _pick(T, 2 * CHUNK, CHUNK)

    def body(p_ref, vg_ref, ws_ref, bt_ref, o_ref):
        z, _ = _gelu(p_ref[...].astype(F32))
        u, v = z[:, :D], z[:, D:]
        rstd = lax.rsqrt(jnp.mean(v * v, axis=-1, keepdims=True) + EPS)
        vn = (v * rstd * vg_ref[...]).astype(BF16)
        low = _tril((CHUNK, CHUNK))
        for g in range(NG):
            wm = jnp.where(low, ws_ref[g], 0.0).astype(BF16)
            bcol = bt_ref[:, g:g + 1]
            cs = slice(g * GROUP, (g + 1) * GROUP)
            for r in range(tm // CHUNK):
                rs = slice(r * CHUNK, (r + 1) * CHUNK)
                sv = jnp.dot(wm, vn[rs, cs], preferred_element_type=F32) + bcol
                o_ref[rs, cs] = (u[rs, cs] * sv).astype(BF16)

    return pl.pallas_call(
        body, name=name, grid=(T // tm,),
        in_specs=[pl.BlockSpec((tm, D2), lambda i: (i, 0)), pl.BlockSpec((1, D), lambda i: (0, 0)),
                  pl.BlockSpec((NG, CHUNK, CHUNK), lambda i: (0, 0, 0)),
                  pl.BlockSpec((CHUNK, LANES), lambda i: (0, 0))],
        out_specs=pl.BlockSpec((tm, D), lambda i: (i, 0)),
        out_shape=jax.ShapeDtypeStruct((T, D), BF16),
        compiler_params=_params("parallel"),
    )(p, vg, ws, bt)


def _gm_bwd(p, dgated, vg, ws, wst, bt, name):
    T, D2 = p.shape
    D = D2 // 2
    NG = D // GROUP
    tm = _pick(T, 2 * CHUNK, CHUNK)
    nb = T // tm

    def body(p_ref, dg_ref, vg_ref, ws_ref, wst_ref, bt_ref, dp_ref, dws_ref, dbt_ref, dvg_ref, du_s, dvn_s):
        step = pl.program_id(0)

        @pl.when(step == 0)
        def _():
            dws_ref[...] = jnp.zeros_like(dws_ref)
            dbt_ref[...] = jnp.zeros_like(dbt_ref)
            dvg_ref[...] = jnp.zeros_like(dvg_ref)

        pv = p_ref[...].astype(F32)
        z, t = _gelu(pv)
        u, v = z[:, :D], z[:, D:]
        rstd = lax.rsqrt(jnp.mean(v * v, axis=-1, keepdims=True) + EPS)
        vhat = v * rstd
        vgv = vg_ref[...]
        vn = (vhat * vgv).astype(BF16)
        dgt = dg_ref[...].astype(F32)
        low = _tril((CHUNK, CHUNK))
        up = _tril((CHUNK, CHUNK), upper=True)
        lane = lax.broadcasted_iota(jnp.int32, (CHUNK, LANES), 1)
        dbt = jnp.zeros((CHUNK, LANES), F32)
        for g in range(NG):
            wm = jnp.where(low, ws_ref[g], 0.0).astype(BF16)
            wmt = jnp.where(up, wst_ref[g], 0.0).astype(BF16)
            bcol = bt_ref[:, g:g + 1]
            cs = slice(g * GROUP, (g + 1) * GROUP)
            dw = jnp.zeros((CHUNK, CHUNK), F32)
            for r in range(tm // CHUNK):
                rs = slice(r * CHUNK, (r + 1) * CHUNK)
                blk = vn[rs, cs]
                sv = jnp.dot(wm, blk, preferred_element_type=F32) + bcol
                dgb = dgt[rs, cs]
                du_s[rs, cs] = dgb * sv
                dsv = dgb * u[rs, cs]
                dsvb = dsv.astype(BF16)
                dvn_s[rs, cs] = jnp.dot(wmt, dsvb, preferred_element_type=F32)
                dw = dw + lax.dot_general(dsvb, blk, _NT, preferred_element_type=F32)
                dbt = dbt + jnp.where(lane == g, jnp.sum(dsv, axis=1, keepdims=True), 0.0)
            dws_ref[g] += dw
        dbt_ref[...] += dbt
        dvn = dvn_s[...]
        dvg_ref[...] += _colsum(dvn * vhat)
        dvhat = dvn * vgv
        dv = rstd * (dvhat - vhat * jnp.mean(dvhat * vhat, axis=-1, keepdims=True))
        gg = _gelu_grad(pv, t)
        dp_ref[:, :D] = (du_s[...] * gg[:, :D]).astype(BF16)
        dp_ref[:, D:] = (dv * gg[:, D:]).astype(BF16)

        @pl.when(step == nb - 1)
        def _():
            for g in range(NG):
                dws_ref[g] = jnp.where(low, dws_ref[g], 0.0)

    const3 = pl.BlockSpec((NG, CHUNK, CHUNK), lambda i: (0, 0, 0))
    return pl.pallas_call(
        body, name=name, grid=(nb,),
        in_specs=[pl.BlockSpec((tm, D2), lambda i: (i, 0)), pl.BlockSpec((tm, D), lambda i: (i, 0)),
                  pl.BlockSpec((1, D), lambda i: (0, 0)), const3, const3,
                  pl.BlockSpec((CHUNK, LANES), lambda i: (0, 0))],
        out_specs=[pl.BlockSpec((tm, D2), lambda i: (i, 0)), const3,
                   pl.BlockSpec((CHUNK, LANES), lambda i: (0, 0)), pl.BlockSpec((1, D), lambda i: (0, 0))],
        out_shape=[jax.ShapeDtypeStruct((T, D2), BF16), jax.ShapeDtypeStruct((NG, CHUNK, CHUNK), F32),
                   jax.ShapeDtypeStruct((CHUNK, LANES), F32), jax.ShapeDtypeStruct((1, D), F32)],
        scratch_shapes=[pltpu.VMEM((tm, D), F32), pltpu.VMEM((tm, D), F32)],
        compiler_params=_params("arbitrary"),
    )(p, dgated, vg, ws, wst, bt)


HALO = BF16_ROWS


def _ffn_act(a, cw, cb, name):
    T, F2 = a.shape
    Fh = F2 // 2
    tm = _pick(T, 256, HALO)
    per = tm // HALO
    nch = Fh // LANES

    def body(a_ref, h_ref, cw_ref, cb_ref, o_ref):
        i = pl.program_id(0)
        row = lax.broadcasted_iota(jnp.int32, (tm, LANES), 0)
        keep = jnp.where(i > 0, 1.0, 0.0)

        def conv(off):
            af = a_ref[:, pl.ds(off, LANES)].astype(F32)
            hf = h_ref[:, pl.ds(off, LANES)].astype(F32) * keep
            r1 = jnp.where(row == 0, hf[HALO - 1:HALO], pltpu.roll(af, 1, 0))
            r2 = jnp.where(row == 0, hf[HALO - 2:HALO - 1], jnp.where(row == 1, hf[HALO - 1:HALO], pltpu.roll(af, 2, 0)))
            w = cw_ref[:, pl.ds(off, LANES)]
            return w[0:1] * r2 + w[1:2] * r1 + w[2:3] * af + cb_ref[:, pl.ds(off, LANES)]

        def chunk(c, carry):
            off = pl.multiple_of(c * LANES, LANES)
            gt = conv(off)
            upv = conv(pl.multiple_of(off + Fh, LANES))
            o_ref[:, pl.ds(off, LANES)] = (gt / (1.0 + jnp.exp(-gt)) * upv).astype(BF16)
            return carry

        lax.fori_loop(0, nch, chunk, 0)

    return pl.pallas_call(
        body, name=name, grid=(T // tm,),
        in_specs=[pl.BlockSpec((tm, F2), lambda i: (i, 0)),
                  pl.BlockSpec((HALO, F2), lambda i: (jnp.maximum(i * per - 1, 0), 0)),
                  pl.BlockSpec((3, F2), lambda i: (0, 0)), pl.BlockSpec((1, F2), lambda i: (0, 0))],
        out_specs=pl.BlockSpec((tm, Fh), lambda i: (i, 0)),
        out_shape=jax.ShapeDtypeStruct((T, Fh), BF16),
        compiler_params=_params("parallel"),
    )(a, a, cw, cb)


def _ffn_act_bwd(a, dhm, cw, cb, name, rides=()):
    T, F2 = a.shape
    Fh = F2 // 2
    tm = _pick(T, 256, HALO)
    per = tm // HALO
    nb = T // tm
    last_halo = T // HALO - 1
    nch = Fh // LANES
    ext = tm + 2 * HALO

    def body(a_ref, hp_ref, hn_ref, d_ref, dn_ref, cw_ref, cb_ref, da_ref, dcw_ref, dcb_ref):
        i = pl.program_id(0)

        @pl.when(i == 0)
        def _():
            dcw_ref[...] = jnp.zeros_like(dcw_ref)
            dcb_ref[...] = jnp.zeros_like(dcb_ref)

        keep_p = jnp.where(i > 0, 1.0, 0.0)
        keep_n = jnp.where(i < nb - 1, 1.0, 0.0)
        row = lax.broadcasted_iota(jnp.int32, (ext, LANES), 0)
        main = jnp.logical_and(row >= HALO, row < HALO + tm)

        def conv(off):
            e = jnp.concatenate([hp_ref[:, pl.ds(off, LANES)].astype(F32) * keep_p,
                                 a_ref[:, pl.ds(off, LANES)].astype(F32),
                                 hn_ref[:, pl.ds(off, LANES)].astype(F32) * keep_n], axis=0)
            r1 = pltpu.roll(e, 1, 0)
            r2 = pltpu.roll(e, 2, 0)
            w = cw_ref[:, pl.ds(off, LANES)]
            return w, e, r1, r2, w[0:1] * r2 + w[1:2] * r1 + w[2:3] * e + cb_ref[:, pl.ds(off, LANES)]

        def back(off, w, e, r1, r2, dc):
            dcm = jnp.where(main, dc, 0.0)
            for tap, shifted in enumerate((r2, r1, e)):
                dcw_ref[tap:tap + 1, pl.ds(off, LANES)] += _colsum(dcm * shifted)
            dcb_ref[:, pl.ds(off, LANES)] += _colsum(dcm)
            da = w[2:3] * dc + w[1:2] * pltpu.roll(dc, ext - 1, 0) + w[0:1] * pltpu.roll(dc, ext - 2, 0)
            da_ref[:, pl.ds(off, LANES)] = da[HALO:HALO + tm].astype(BF16)

        def chunk(c, carry):
            off = pl.multiple_of(c * LANES, LANES)
            off_u = pl.multiple_of(off + Fh, LANES)
            dh = jnp.concatenate([jnp.zeros((HALO, LANES), F32), d_ref[:, pl.ds(off, LANES)].astype(F32),
                                  dn_ref[:, pl.ds(off, LANES)].astype(F32) * keep_n], axis=0)
            wg, eg, r1g, r2g, gt = conv(off)
            wu, eu, r1u, r2u, upv = conv(off_u)
            sg = 1.0 / (1.0 + jnp.exp(-gt))
            back(off, wg, eg, r1g, r2g, dh * upv * (sg * (1.0 + gt * (1.0 - sg))))
            back(off_u, wu, eu, r1u, r2u, dh * (gt * sg))
            return carry

        lax.fori_loop(0, nch, chunk, 0)

    prev = lambda i: (jnp.maximum(i * per - 1, 0), 0)
    nxt = lambda i: (jnp.minimum((i + 1) * per, last_halo), 0)
    outs, carried = _call(
        body, name=name, grid=(nb,),
        in_specs=[pl.BlockSpec((tm, F2), lambda i: (i, 0)), pl.BlockSpec((HALO, F2), prev),
                  pl.BlockSpec((HALO, F2), nxt), pl.BlockSpec((tm, Fh), lambda i: (i, 0)),
                  pl.BlockSpec((HALO, Fh), nxt),
                  pl.BlockSpec((3, F2), lambda i: (0, 0)), pl.BlockSpec((1, F2), lambda i: (0, 0))],
        out_specs=[pl.BlockSpec((tm, F2), lambda i: (i, 0)), pl.BlockSpec((3, F2), lambda i: (0, 0)),
                   pl.BlockSpec((1, F2), lambda i: (0, 0))],
        out_shape=[jax.ShapeDtypeStruct((T, F2), BF16), jax.ShapeDtypeStruct((3, F2), F32),
                   jax.ShapeDtypeStruct((1, F2), F32)],
        scratch_shapes=[], args=(a, a, a, dhm, dhm, cw, cb), rides=rides)
    return (outs, carried) if rides else outs


def _round_up(n, k):
    return -(-n // k) * k


def _pad_axis(a, axis, size):
    pad = [(0, 0)] * a.ndim
    pad[axis] = (0, size - a.shape[axis])
    return jnp.pad(a, pad)


def _pad_blocks(a, axis, nblk, to):
    shp = a.shape
    blk = shp[axis] // nblk
    a = a.reshape(shp[:axis] + (nblk, blk) + shp[axis + 1:])
    a = _pad_axis(a, axis + 1, to)
    return a.reshape(shp[:axis] + (nblk * to,) + shp[axis + 1:])


def _unpad_blocks(a, axis, nblk, blk):
    shp = a.shape
    to = shp[axis] // nblk
    a = a.reshape(shp[:axis] + (nblk, to) + shp[axis + 1:])
    a = lax.slice_in_dim(a, 0, blk, axis=axis + 1)
    return a.reshape(shp[:axis] + (nblk * blk,) + shp[axis + 1:])


SIB_TILE_BYTES = 2 * 1024 * 1024


def _sib_stream(x, core, name, gather):
    if gather:
        B, R, D = x.shape
        out_shape = (B, 2, R, D)
    else:
        B, _, R, D = x.shape
        out_shape = (B, R, D)
    tr = _pick(R, max(BF16_ROWS, SIB_TILE_BYTES // (D * x.dtype.itemsize)), BF16_ROWS)
    nr = R // tr
    if gather:
        in_spec = pl.BlockSpec((tr, D), lambda b, i, core_ref: (b * nr + i, 0))
    else:
        in_spec = pl.BlockSpec((tr, D), lambda b, i, core_ref: ((2 * b + 1 - core_ref[0]) * nr + i, 0))

    def body(core_ref, x_ref, o_ref, send_sem, recv_sem, local_sem):
        b, i = pl.program_id(0), pl.program_id(1)
        me = lax.axis_index("c")
        sibling = (lax.axis_index("x"), lax.axis_index("y"), 1 - me)
        rows = pl.ds(pl.multiple_of(i * tr, tr), tr)
        dst = o_ref.at[b, me, rows] if gather else o_ref.at[b, rows]
        push = pltpu.make_async_remote_copy(src_ref=x_ref, dst_ref=dst, send_sem=send_sem, recv_sem=recv_sem,
                                            device_id=sibling, device_id_type=MESH)
        push.start()
        if gather:
            keep = pltpu.make_async_copy(x_ref, dst, local_sem)
            keep.start()
            keep.wait()
        push.wait_send()

        @pl.when(jnp.logical_and(b == B - 1, i == nr - 1))
        def _():
            landed = o_ref.at[:, 0] if gather else o_ref
            pltpu.make_async_remote_copy(src_ref=landed, dst_ref=landed, send_sem=send_sem, recv_sem=recv_sem,
                                         device_id=sibling, device_id_type=MESH).wait_recv()

    return pl.pallas_call(
        body, name=name,
        grid_spec=pltpu.PrefetchScalarGridSpec(
            num_scalar_prefetch=1, grid=(B, nr), in_specs=[in_spec],
            out_specs=pl.BlockSpec(memory_space=pl.ANY),
            scratch_shapes=[pltpu.SemaphoreType.DMA(()), pltpu.SemaphoreType.DMA(()), pltpu.SemaphoreType.DMA(())]),
        out_shape=jax.ShapeDtypeStruct(out_shape, x.dtype),
        compiler_params=_params("arbitrary", "arbitrary"),
    )(core, x.reshape(-1, D))


def _pair_sum(g, recv, core, name):
    B, _, R, D = g.shape
    tr = _pick(R, 256, BF16_ROWS)

    def body(core_ref, g_ref, r_ref, o_ref):
        o_ref[...] = (g_ref[...].astype(F32) + r_ref[...].astype(F32)).astype(BF16)

    tile = pl.BlockSpec((None, tr, D), lambda b, i, core_ref: (b, i, 0))
    return pl.pallas_call(
        body, name=name,
        grid_spec=pltpu.PrefetchScalarGridSpec(
            num_scalar_prefetch=1, grid=(B, R // tr),
            in_specs=[pl.BlockSpec((None, None, tr, D), lambda b, i, core_ref: (b, core_ref[0], i, 0)), tile],
            out_specs=tile),
        out_shape=jax.ShapeDtypeStruct((B, R, D), BF16),
        compiler_params=_params("parallel", "parallel"),
    )(core, g, recv)


def _all_reduce_small(v, name):
    pair = _sum_slots(_all_gather(v, ("c",), 0, name + "_ag_c")[None], F32, name + "_sum_c")[0]
    return _sum_slots(_all_gather(pair, ("x", "y"), 0, name + "_ag_xy")[None], F32, name + "_sum_xy")[0]


def _pack(arrs):
    flat = jnp.concatenate([a.reshape(-1) for a in arrs])
    rows = _round_up(-(-flat.shape[0] // SMALL_PACK_COLS), BF16_ROWS)
    return _pad_axis(flat, 0, rows * SMALL_PACK_COLS).reshape(rows, SMALL_PACK_COLS)


def _unpack(buf, like):
    flat = buf.reshape(-1)
    out, off = [], 0
    for a in like:
        out.append(flat[off:off + a.size].reshape(a.shape))
        off += a.size
    return out


def kernel(x, c, mod_w, mod_b, mix_norm_g, ffn_norm_g, attn_w_in, attn_b_f, attn_w_o, gm_w_in, gm_v_g, gm_w_s, gm_b_s, gm_w_o, ffn_w_in, ffn_conv_w, ffn_conv_b, ffn_w_out, final_g, loss_target, m_mod_w, m_mod_b, m_mix_norm_g, m_ffn_norm_g, m_attn_w_in, m_attn_b_f, m_attn_w_o, m_gm_w_in, m_gm_v_g, m_gm_w_s, m_gm_b_s, m_gm_w_o, m_ffn_w_in, m_ffn_conv_w, m_ffn_conv_b, m_ffn_w_out, m_final_g, v_mod_w, v_mod_b, v_mix_norm_g, v_ffn_norm_g, v_attn_w_in, v_attn_b_f, v_attn_w_o, v_gm_w_in, v_gm_v_g, v_gm_w_s, v_gm_b_s, v_gm_w_o, v_ffn_w_in, v_ffn_conv_w, v_ffn_conv_b, v_ffn_w_out, v_final_g):
    _, T, D = x.shape
    L = mod_w.shape[0]
    NA, NB = attn_w_in.shape[0], gm_w_in.shape[0]
    H = D // HEAD_DIM
    NG = D // GROUP
    Fq = ffn_w_out.shape[1]
    Fh = 4 * Fq
    Fqp = _round_up(Fq, LANES)
    Fp = 4 * Fqp
    n_attn = attn_w_in.shape[2]
    n_attn_p = _round_up(n_attn, 2 * BF16_ROWS)
    nmod = mod_w.shape[2]

    xi, yi, ci = lax.axis_index("x"), lax.axis_index("y"), lax.axis_index("c")
    chip = 2 * xi + yi
    example = 2 * chip + ci
    core = ci.astype(jnp.int32).reshape(1)

    halves = {
        "attn_in": _cast_half(_pad_axis(jnp.swapaxes(attn_w_in, 1, 2), 1, n_attn_p), core, "w_attn_in_cast"),
        "attn_o": _cast_half(attn_w_o, core, "w_attn_o_cast"),
        "gm_in": _cast_half(jnp.swapaxes(gm_w_in, 1, 2), core, "w_gm_in_cast"),
        "gm_o": _cast_half(gm_w_o, core, "w_gm_o_cast"),
        "ffn_in": _cast_half(jnp.swapaxes(ffn_w_in, 1, 2), core, "w_ffn_in_cast", Fq, Fqp),
        "ffn_out": _cast_half(ffn_w_out, core, "w_ffn_out_cast", Fq, Fqp),
    }
    weights = {}

    def mixer_kinds(i):
        return [("attn_in", i // 2), ("attn_o", i // 2)] if i % 2 == 0 else [("gm_in", i // 2), ("gm_o", i // 2)]

    def land(kind, l, gathered):
        n2 = gathered.shape[1]
        full = _sib_stream(gathered, core, "w_" + kind + "_ag_c", True).reshape(8 * n2, D)
        if kind == "attn_in":
            full = full.reshape(4, n_attn_p, D)[:, :n_attn].reshape(4 * n_attn, D)
            weights[("qkv", l)] = full[:3 * D]
            weights[("f", l)] = _pad_axis(full[3 * D:], 0, LANES)
        else:
            weights[(kind, l)] = full

    def fetching(host, kinds):
        if not kinds:
            return host(())
        result, carried = host([_Exchange(halves[k], ("x", "y"), 0, True, sel=l) for k, l in kinds])
        for (k, l), g in zip(kinds, carried):
            land(k, l, g)
        return result

    land("attn_in", 0, _all_gather(halves["attn_in"][0], ("x", "y"), 0, "w_attn_in_ag_xy"))
    cw_all = _all_gather(ffn_conv_w, ("x", "y"), 0, "conv_w_ag")
    conv_w_full = jnp.moveaxis(cw_all, 0, 2).reshape(L, 3, 2 * Fh)
    conv_w = _pad_blocks(conv_w_full, 2, 8, Fqp)
    conv_b = _pad_blocks(ffn_conv_b, 1, 8, Fqp)
    vg_all = _all_gather(gm_v_g, ("x", "y"), 0, "gm_vg_ag")
    vg_full = jnp.moveaxis(vg_all, 0, 1).reshape(NB, D)

    c_all = _all_gather(c, MESH_AXES, 0, "c_ag").reshape(8, D)
    mod_b_mine = lax.dynamic_slice_in_dim(mod_b, chip * nmod, nmod, axis=1)
    c_act, mod_part = _mod_fwd(c_all, mod_w, mod_b_mine, "mod_fwd")
    mod_all = _all_gather(mod_part, ("x", "y"), 0, "mod_ag")
    mod = lax.dynamic_index_in_dim(mod_all, example, axis=2, keepdims=False)
    mod = jnp.swapaxes(mod, 0, 1).reshape(L, 6, 1, D)

    xs = x.reshape(T, D)
    target = loss_target.reshape(T, D)
    saved = []
    for i in range(L):
        j = i // 2
        sh1, sc1, g1, sh2, sc2, g2 = (mod[i, k] for k in range(6))
        h1 = _norm_mod(xs, mix_norm_g[i:i + 1], sc1, sh1, "norm_mod")
        nxt = i + 1 < L
        if i % 2 == 0:
            qkv = fetching(lambda r: _matmul(h1, weights[("qkv", j)], "nt", "attn_qkv", tm=1024, tn=1024, rides=r),
                           ([("attn_o", j)] if i == 0 else []) + (mixer_kinds(i + 1)[:1] if nxt else []))
            flog = _matmul(h1, weights[("f", j)], "nt", "attn_f", out_dtype=F32, tm=1024)
            bias = _pad_axis(attn_b_f[j:j + 1], 1, LANES)
            fcol = _fox_gate(flog, bias, "fox_gate")
            frow = jnp.transpose(fcol)[:H].reshape(H, 1, T)
            o, lse = fetching(lambda r: _attn_fwd(qkv, frow, H, "attn_fwd", rides=r),
                              [("ffn_in", i), ("ffn_out", i)])
            y, x1 = fetching(lambda r: _matmul(o, weights[("attn_o", j)], "nn", "attn_out", out_dtype=F32, tm=1024,
                                               tn=1024, res=xs, gate=g1, emit_y=True, rides=r),
                             mixer_kinds(i + 1)[1:] if nxt else [])
            mix = ("fox", qkv, flog, bias, frow, o, lse)
            ride_in = [("ffn_in", i + 1)] if nxt else []
            ride_out = [("ffn_out", i + 1)] if nxt else []
        else:
            p = _matmul(h1, weights[("gm_in", j)], "nt", "gm_in", tm=1024, tn=1024)
            bt = _pad_axis(jnp.transpose(gm_b_s[j]), 1, LANES)
            gated = _gm_fwd(p, vg_full[j:j + 1], gm_w_s[j], bt, "gm_fwd")
            y, x1 = _matmul(gated, weights[("gm_o", j)], "nn", "gm_out", out_dtype=F32, tm=1024, tn=1024,
                            res=xs, gate=g1, emit_y=True)
            mix = ("gm", p, bt, gated)
            ride_in = mixer_kinds(i + 1) if nxt else []
            ride_out = []
        h2 = _norm_mod(x1, ffn_norm_g[i:i + 1], sc2, sh2, "norm_mod")
        a = fetching(lambda r: _matmul(h2, weights[("ffn_in", i)], "nt", "ffn_in", tm=1024, tn=1024, rides=r), ride_in)
        hmid = _ffn_act(a, conv_w[i], conv_b[i:i + 1], "ffn_act")
        y2, x2 = fetching(lambda r: _matmul(hmid, weights[("ffn_out", i)], "nn", "ffn_out", out_dtype=F32, tm=512,
                                            tn=1024, res=x1, gate=g2, emit_y=True, rides=r), ride_out)
        saved.append((xs, h1, mix, y, x1, h2, a, hmid, y2))
        xs = x2

    loss_blk, dx, g_final, dy2, dg2 = _final_loss_bwd(
        xs, final_g.reshape(1, D), target, saved[L - 1][8], mod[L - 1, 5], "final_loss_bwd")
    g_mix, g_ffn_n, g_cw, g_cb, dmod = [None] * L, [None] * L, [None] * L, [None] * L, [None] * L
    g_bf, g_vg, g_ws, g_bs = [None] * NA, [None] * NB, [None] * NB, [None] * NB
    reduced = {}

    def pair(kind, gw):
        g = gw.reshape(4, 2, gw.shape[0] // 8, D)
        return _pair_sum(g, _sib_stream(g, core, "g_" + kind + "_swap_c", False), core, "g_" + kind + "_sum_c")

    def sending(host, items):
        if not items:
            return host(())
        result, carried = host([_Exchange(ps, ("x", "y"), 0, False) for _, ps in items])
        for (key, _), got in zip(items, carried):
            reduced[key] = got
        return result

    late_act = []
    for i in reversed(range(L)):
        j = i // 2
        xin, h1, mix, y, x1, h2, a, hmid, y2 = saved[i]
        sh1, sc1, g1, sh2, sc2, g2 = (mod[i, k] for k in range(6))
        dhm = _matmul(dy2, weights[("ffn_out", i)], "nt", "ffn_out_dx", tm=1024, tn=512)
        p_out = pair("ffn_out", _matmul(hmid, dy2, "tn", "ffn_out_dw", tm=512, tn=1024))
        da, dcw, dcb = sending(lambda r: _ffn_act_bwd(a, dhm, conv_w[i], conv_b[i:i + 1], "ffn_act_bwd", rides=r),
                               [(("ffn_out", i), p_out)])
        ffn_in_dx = lambda r: _matmul(da, weights[("ffn_in", i)], "nn", "ffn_in_dx", out_dtype=F32, tm=512, tn=1024,
                                      tk=2816, rides=r)
        if i == 0 and L > 1:
            early = [jnp.stack(g_mix[1:]), jnp.stack(g_ffn_n[1:]), jnp.stack(g_vg), jnp.stack(g_ws), jnp.stack(g_bs),
                     jnp.stack(g_cw[1:]), jnp.stack(g_cb[1:]), g_final[0]] + ([jnp.stack(g_bf[1:])] if NA > 1 else [])
            extra = [_Exchange(_pack(early), ("c",), 0, True)]
        else:
            extra = []
        rides = [_Exchange(ps, ("x", "y"), 0, False) for _, ps in late_act] + extra
        if rides:
            dh2, carried = ffn_in_dx(rides)
            for (key, _), got in zip(late_act, carried):
                reduced[key] = got
            if extra:
                early_pair = _sum_slots(carried[-1][None], F32, "g_small_sum_c")[0]
        else:
            dh2 = ffn_in_dx(())
        late_act = []
        p_in = pair("ffn_in", _matmul(da, h2, "tn", "ffn_in_dw", tm=512, tn=1024))
        g_cw[i] = _unpad_blocks(dcw, 1, 8, Fq)
        g_cb[i] = _unpad_blocks(dcb, 1, 8, Fq)[0]
        dx1, dsc2, dsh2, dgn, dy, dg1 = _norm_bwd(
            x1, ffn_norm_g[i:i + 1], sc2, sh2, dh2, dx, (y, g1), "norm_bwd_prev")
        g_ffn_n[i] = dgn[0]
        if mix[0] == "fox":
            _, qkv, flog, bias, frow, o, lse = mix
            do = _matmul(dy, weights[("attn_o", j)], "nt", "attn_out_dx", tm=1024, tn=1024)
            p_o = pair("attn_o", _matmul(o, dy, "tn", "attn_out_dw", tm=512, tn=1024))
            dq, dfq = _attn_dq(qkv, o, do, lse, frow, H, "attn_dq")
            rides = [_Exchange(ps, ("x", "y"), 0, False) for ps in (p_in, p_o)]
            if i == 0 and L > 1:
                rides.append(_Exchange(early_pair, ("x", "y"), 0, True))
            (dk, dv, dfk), carried = _attn_dkv(qkv, o, do, lse, frow, H, "attn_dkv", rides=rides)
            reduced[("ffn_in", i)], reduced[("attn_o", j)] = carried[:2]
            if i == 0 and L > 1:
                early_sum = _sum_slots(carried[2][None], F32, "g_small_sum_xy")[0]
            heads_on_lanes = lambda t: _pad_axis(jnp.transpose(t.reshape(H, T)), 1, LANES)
            dflog, dbf = _fox_gate_bwd(heads_on_lanes(dfq), heads_on_lanes(dfk), flog, bias, "fox_gate_bwd")
            dproj = jnp.concatenate([dq, dk, dv], axis=1)
            gwt_qkv = _matmul(dproj, h1, "tn", "attn_qkv_dw", tm=512, tn=1024)
            gwt_f = _matmul(dflog, h1, "tn", "attn_f_dw", tm=128, tn=1024)
            gwt = jnp.concatenate([gwt_qkv, gwt_f[:H]], axis=0).reshape(4, n_attn, D)
            p_a = pair("attn_in", _pad_axis(gwt, 1, n_attn_p).reshape(4 * n_attn_p, D))
            dh_f = _matmul(dflog, weights[("f", j)], "nn", "attn_f_dx", out_dtype=F32, tm=1024, tn=1024)
            dh1 = sending(lambda r: _matmul(dproj, weights[("qkv", j)], "nn", "attn_qkv_dx", out_dtype=F32, tm=512,
                                            tn=1024, tk=2048, res=dh_f, rides=r), [(("attn_in", j), p_a)])
            g_bf[j] = dbf[0, :H]
        else:
            _, p, bt, gated = mix
            late_act = [(("ffn_in", i), p_in)]
            dgated = _matmul(dy, weights[("gm_o", j)], "nt", "gm_out_dx", tm=1024, tn=1024)
            p_o = pair("gm_o", _matmul(gated, dy, "tn", "gm_out_dw", tm=512, tn=1024))
            dp, dws, dbt, dvg = _gm_bwd(p, dgated, vg_full[j:j + 1], gm_w_s[j],
                                        jnp.swapaxes(gm_w_s[j], 1, 2), bt, "gm_bwd")
            p_a = pair("gm_in", _matmul(dp, h1, "tn", "gm_in_dw", tm=512, tn=1024))
            dh1 = sending(lambda r: _matmul(dp, weights[("gm_in", j)], "nn", "gm_in_dx", out_dtype=F32, tm=512,
                                            tn=1024, tk=2048, rides=r), [(("gm_in", j), p_a), (("gm_o", j), p_o)])
            g_ws[j] = dws
            g_bs[j] = jnp.transpose(dbt)[:NG]
            g_vg[j] = dvg[0]
        if i > 0:
            dx, dsc1, dsh1, dgn, dy2, dg2_prev = _norm_bwd(
                xin, mix_norm_g[i:i + 1], sc1, sh1, dh1, dx1, (saved[i - 1][8], mod[i - 1, 5]), "norm_bwd_prev")
        else:
            dx, dsc1, dsh1, dgn = _norm_bwd(xin, mix_norm_g[i:i + 1], sc1, sh1, dh1, dx1, None, "norm_bwd_first")
            dg2_prev = None
        g_mix[i] = dgn[0]
        dmod[i] = jnp.concatenate([dsh1, dsc1, dg1, dsh2, dsc2, dg2], axis=1)[0]
        dg2 = dg2_prev
    grad_x = dx.reshape(1, T, D)
    loss = lax.psum(loss_blk[0, 0], MESH_AXES)

    if L > 1:
        late = [g_mix[0][None], g_ffn_n[0][None], g_cw[0][None], g_cb[0][None], g_bf[0][None]]
        l_mix, l_ffn, l_cw, l_cb, l_bf = _unpack(_all_reduce_small(_pack(late), "g_small_late"), late)
        got = _unpack(early_sum, early)
        e_mix, e_ffn, g_vg_full, grad_gm_w_s, grad_gm_b_s, e_cw, e_cb, grad_final_g = got[:8]
        grad_mix_norm_g = jnp.concatenate([l_mix, e_mix])
        grad_ffn_norm_g = jnp.concatenate([l_ffn, e_ffn])
        g_cw_full = jnp.concatenate([l_cw, e_cw])
        grad_ffn_conv_b = jnp.concatenate([l_cb, e_cb])
        grad_attn_b_f = jnp.concatenate([l_bf, got[8]]) if NA > 1 else l_bf
    else:
        small = [jnp.stack(g_mix), jnp.stack(g_ffn_n), jnp.stack(g_bf), jnp.stack(g_vg), jnp.stack(g_ws),
                 jnp.stack(g_bs), jnp.stack(g_cw), jnp.stack(g_cb), g_final[0]]
        (grad_mix_norm_g, grad_ffn_norm_g, grad_attn_b_f, g_vg_full, grad_gm_w_s, grad_gm_b_s, g_cw_full,
         grad_ffn_conv_b, grad_final_g) = _unpack(_all_reduce_small(_pack(small), "g_small"), small)
    grad_gm_v_g = lax.dynamic_slice_in_dim(g_vg_full, chip * (D // 4), D // 4, axis=1)
    grad_ffn_conv_w = lax.dynamic_slice_in_dim(g_cw_full, chip * 2 * Fq, 2 * Fq, axis=2)

    dmod_all = _all_gather(jnp.stack(dmod), MESH_AXES, 0, "dmod_ag")
    grad_mod_b = _sum_slots(dmod_all[None], F32, "dmod_sum")[0]
    dmod_cols = jnp.swapaxes(lax.dynamic_slice_in_dim(dmod_all, chip * nmod, nmod, axis=2), 0, 1)

    grad_mod_w, delta_mod_w, new_m_mod_w, new_v_mod_w = _modw_adamw(
        jnp.transpose(c_act), dmod_cols, mod_w, m_mod_w, v_mod_w, "mod_w_adamw")

    def total(kind, n_layers):
        got = jnp.stack([reduced[(kind, l)] for l in range(n_layers)])
        tot = _sum_slots(got, F32, "g_" + kind + "_sum_xy")
        return _sib_stream(tot, core, "g_" + kind + "_ag_c", True).reshape(n_layers, -1, D)

    gt_attn_in = total("attn_in", NA)[:, :n_attn]
    grad_attn_w_in = jnp.swapaxes(gt_attn_in, 1, 2)
    grad_attn_w_o = total("attn_o", NA)
    grad_gm_w_in = jnp.swapaxes(total("gm_in", NB), 1, 2)
    grad_gm_w_o = total("gm_o", NB)
    gt_ffn_in = _unpad_blocks(total("ffn_in", L), 1, 2, Fq)
    grad_ffn_w_in = jnp.swapaxes(gt_ffn_in, 1, 2)
    grad_ffn_w_out = total("ffn_out", L)[:, :Fq]

    def big(w, g, m, v, name):
        shp = w.shape
        flat = lambda t: t.reshape(-1, shp[-1])
        return [t.reshape(shp) for t in _adamw(flat(w), flat(g), flat(m), flat(v), name)]

    def big_t(w, gt, m, v, name):
        t = lambda a: jnp.swapaxes(a, 1, 2)
        return [t(u) for u in big(t(w), gt, t(m), t(v), name)]

    upd_attn_in = big_t(attn_w_in, gt_attn_in, m_attn_w_in, v_attn_w_in, "adamw_attn_in")
    upd_attn_o = big(attn_w_o, grad_attn_w_o, m_attn_w_o, v_attn_w_o, "adamw_attn_o")
    upd_gm_in = big(gm_w_in, grad_gm_w_in, m_gm_w_in, v_gm_w_in, "adamw_gm_in")
    upd_gm_o = big(gm_w_o, grad_gm_w_o, m_gm_w_o, v_gm_w_o, "adamw_gm_o")
    upd_ffn_in = big_t(ffn_w_in, gt_ffn_in, m_ffn_w_in, v_ffn_w_in, "adamw_ffn_in")
    upd_ffn_out = big(ffn_w_out, grad_ffn_w_out, m_ffn_w_out, v_ffn_w_out, "adamw_ffn_out")

    sm_w = [mod_b, mix_norm_g, ffn_norm_g, attn_b_f, gm_v_g, gm_w_s, gm_b_s, ffn_conv_w, ffn_conv_b, final_g]
    sm_g = [grad_mod_b, grad_mix_norm_g, grad_ffn_norm_g, grad_attn_b_f, grad_gm_v_g, grad_gm_w_s, grad_gm_b_s,
            grad_ffn_conv_w, grad_ffn_conv_b, grad_final_g]
    sm_m = [m_mod_b, m_mix_norm_g, m_ffn_norm_g, m_attn_b_f, m_gm_v_g, m_gm_w_s, m_gm_b_s, m_ffn_conv_w,
            m_ffn_conv_b, m_final_g]
    sm_v = [v_mod_b, v_mix_norm_g, v_ffn_norm_g, v_attn_b_f, v_gm_v_g, v_gm_w_s, v_gm_b_s, v_ffn_conv_w,
            v_ffn_conv_b, v_final_g]
    sm_d, sm_nm, sm_nv = (_unpack(t, sm_w) for t in _adamw(_pack(sm_w), _pack(sm_g), _pack(sm_m), _pack(sm_v),
                                                             "adamw_small"))

    def ordered(mod_w_item, sm, k):
        return [mod_w_item, sm[0], sm[1], sm[2], upd_attn_in[k], sm[3], upd_attn_o[k], upd_gm_in[k], sm[4], sm[5],
                sm[6], upd_gm_o[k], upd_ffn_in[k], sm[7], sm[8], upd_ffn_out[k], sm[9]]

    grads = [grad_mod_w, grad_mod_b, grad_mix_norm_g, grad_ffn_norm_g, grad_attn_w_in, grad_attn_b_f, grad_attn_w_o,
             grad_gm_w_in, grad_gm_v_g, grad_gm_w_s, grad_gm_b_s, grad_gm_w_o, grad_ffn_w_in, grad_ffn_conv_w,
             grad_ffn_conv_b, grad_ffn_w_out, grad_final_g]
    return (loss, grad_x, *grads, *ordered(delta_mod_w, sm_d, 0), *ordered(new_m_mod_w, sm_nm, 1),
            *ordered(new_v_mod_w, sm_nv, 2))
```
